```python
import math
import jax, jax.numpy as jnp
from jax import lax
import numpy as np

D_MODEL = 1024
BATCH = 8
SEQ = 4096
DEPTH = 4

HEAD_DIM = 64
N_HEADS = 8
N_KV_HEADS = 2
WINDOW = 128
ATTN_WIDTH = N_HEADS * HEAD_DIM
KV_WIDTH = N_KV_HEADS * HEAD_DIM
CONV_CH = D_MODEL // 2
CONV_WIDTH = 31
MIX_WIDTH = ATTN_WIDTH + CONV_CH
IN_WIDTH = ATTN_WIDTH + 2 * KV_WIDTH + 2 * CONV_CH
D_FF = 2816
FFN_RESIDUAL_WEIGHT = 0.5
EPS = 1e-6
NEG_INF = -1e30

kernel_name = "hybrid_swa_sink_alibi_conformer_conv_macaron"


def rms_norm(x, g):
    xf = x.astype(jnp.float32)
    y = xf * lax.rsqrt(jnp.mean(xf * xf, axis=-1, keepdims=True) + EPS)
    return (y * g.astype(jnp.float32)).astype(x.dtype)


def swiglu_ffn(h, w_in, w_out):
    gu = h @ w_in
    gate, up = jnp.split(gu, 2, axis=-1)
    return (jax.nn.silu(gate) * up) @ w_out


def alibi_slopes(n_heads):
    return jnp.exp2(-8.0 * jnp.arange(1, n_heads + 1, dtype=jnp.float32) / n_heads)


def sliding_window_sink_attention(q, k, v, sinks):
    B, S, H, hd = q.shape
    nb = S // WINDOW
    G = H // N_KV_HEADS
    qb = q.reshape(B, nb, WINDOW, N_KV_HEADS, G, hd).astype(jnp.float32)

    def band(t):
        cur = t.reshape(B, nb, WINDOW, N_KV_HEADS, hd)
        prev = jnp.pad(cur, ((0, 0), (1, 0), (0, 0), (0, 0), (0, 0)))[:, :-1]
        return jnp.concatenate([prev, cur], axis=2).astype(jnp.float32)

    kb, vb = band(k), band(v)
    scores = jnp.einsum('bnqkgd,bnskd->bkgnqs', qb, kb) * (1.0 / math.sqrt(hd))

    t_loc = jnp.arange(WINDOW)[:, None]
    s_loc = jnp.arange(2 * WINDOW)[None, :]
    dist = t_loc + WINDOW - s_loc
    in_window = (dist >= 0) & (dist < WINDOW)
    blk = jnp.arange(nb)[:, None, None]
    valid = in_window[None] & ((blk > 0) | (s_loc >= WINDOW)[None])

    slopes = alibi_slopes(H).reshape(N_KV_HEADS, G)
    bias = -slopes[:, :, None, None] * jnp.abs(dist).astype(jnp.float32)[None, None]
    scores = jnp.where(valid[None, None, None], scores + bias[:, :, None], NEG_INF)

    sink = sinks.astype(jnp.float32).reshape(N_KV_HEADS, G)[None, :, :, None, None]
    m = jnp.maximum(jnp.max(scores, axis=-1), sink)
    p = jnp.exp(scores - m[..., None])
    denom = jnp.sum(p, axis=-1) + jnp.exp(sink - m)
    p = p / denom[..., None]
    out = jnp.einsum('bkgnqs,bnskd->bnqkgd', p, vb)
    return out.reshape(B, S, H * hd).astype(q.dtype)


def conformer_conv(u, w_dw, b_dw, ln_g, ln_b):
    a, gate = jnp.split(u, 2, axis=-1)
    z = a * jax.nn.sigmoid(gate)
    C = z.shape[-1]
    y = lax.conv_general_dilated(
        z, w_dw.astype(z.dtype)[:, None, :],
        window_strides=(1,), padding=[(CONV_WIDTH - 1, 0)],
        dimension_numbers=('NWC', 'WIO', 'NWC'), feature_group_count=C)
    y = (y + b_dw).astype(jnp.float32)
    mu = jnp.mean(y, axis=-1, keepdims=True)
    var = jnp.mean(jnp.square(y - mu), axis=-1, keepdims=True)
    y = (y - mu) * lax.rsqrt(var + EPS) * ln_g.astype(jnp.float32) + ln_b.astype(jnp.float32)
    return jax.nn.silu(y).astype(u.dtype)


def _fwd_setup_inputs(seed: int = 0) -> dict:
    key = jax.random.key(seed)
    ks = jax.random.split(key, 20)
    f32 = jnp.float32

    def nrm(k, shape, scale):
        return jax.random.normal(k, shape, f32) * scale

    def gain(k, shape):
        return 1.0 + 0.05 * jax.random.normal(k, shape, f32)

    return {
        "x": jax.random.normal(ks[0], (BATCH, SEQ, D_MODEL), f32),
        "norm_ffn1": gain(ks[1], (DEPTH, D_MODEL)),
        "w_ffn1_in": nrm(ks[2], (DEPTH, D_MODEL, 2 * D_FF), D_MODEL ** -0.5),
        "w_ffn1_out": nrm(ks[3], (DEPTH, D_FF, D_MODEL), D_FF ** -0.5),
        "norm_mix": gain(ks[4], (DEPTH, D_MODEL)),
        "w_in": nrm(ks[5], (DEPTH, D_MODEL, IN_WIDTH), D_MODEL ** -0.5),
        "sinks": nrm(ks[6], (DEPTH, N_HEADS), 1.0),
        "w_dw": nrm(ks[7], (DEPTH, CONV_WIDTH, CONV_CH), CONV_WIDTH ** -0.5),
        "b_dw": nrm(ks[8], (DEPTH, CONV_CH), 0.02),
        "conv_ln_g": gain(ks[9], (DEPTH, CONV_CH)),
        "conv_ln_b": nrm(ks[10], (DEPTH, CONV_CH), 0.02),
        "w_out": nrm(ks[11], (DEPTH, MIX_WIDTH, D_MODEL), MIX_WIDTH ** -0.5),
        "norm_ffn2": gain(ks[12], (DEPTH, D_MODEL)),
        "w_ffn2_in": nrm(ks[13], (DEPTH, D_MODEL, 2 * D_FF), D_MODEL ** -0.5),
        "w_ffn2_out": nrm(ks[14], (DEPTH, D_FF, D_MODEL), D_FF ** -0.5),
        "final_norm": gain(ks[15], (D_MODEL,)),
    }


def _fwd_reference(x, norm_ffn1, w_ffn1_in, w_ffn1_out, norm_mix, w_in, sinks, w_dw, b_dw,
              conv_ln_g, conv_ln_b, w_out, norm_ffn2, w_ffn2_in, w_ffn2_out, final_norm):
    B, S, _ = x.shape
    split_pts = [ATTN_WIDTH, ATTN_WIDTH + KV_WIDTH, ATTN_WIDTH + 2 * KV_WIDTH]
    for l in range(DEPTH):
        h = rms_norm(x, norm_ffn1[l])
        x = x + FFN_RESIDUAL_WEIGHT * swiglu_ffn(h, w_ffn1_in[l], w_ffn1_out[l])

        h = rms_norm(x, norm_mix[l])
        proj = h @ w_in[l]
        q, k, v, u = jnp.split(proj, split_pts, axis=-1)
        attn = sliding_window_sink_attention(
            q.reshape(B, S, N_HEADS, HEAD_DIM),
            k.reshape(B, S, N_KV_HEADS, HEAD_DIM),
            v.reshape(B, S, N_KV_HEADS, HEAD_DIM),
            sinks[l])
        conv = conformer_conv(u, w_dw[l], b_dw[l], conv_ln_g[l], conv_ln_b[l])
        x = x + jnp.concatenate([attn, conv], axis=-1) @ w_out[l]

        h = rms_norm(x, norm_ffn2[l])
        x = x + FFN_RESIDUAL_WEIGHT * swiglu_ffn(h, w_ffn2_in[l], w_ffn2_out[l])
    return rms_norm(x, final_norm)


import jax as _jax
import jax.numpy as _jnp

TWIN_FORMAT = 'train_step'
FWD_PARAMS = ['x', 'norm_ffn1', 'w_ffn1_in', 'w_ffn1_out', 'norm_mix', 'w_in', 'sinks', 'w_dw', 'b_dw', 'conv_ln_g', 'conv_ln_b', 'w_out', 'norm_ffn2', 'w_ffn2_in', 'w_ffn2_out', 'final_norm']
TWIN_WEIGHTS = ['norm_ffn1', 'w_ffn1_in', 'w_ffn1_out', 'norm_mix', 'w_in', 'sinks', 'w_dw', 'b_dw', 'conv_ln_g', 'conv_ln_b', 'w_out', 'norm_ffn2', 'w_ffn2_in', 'w_ffn2_out', 'final_norm']
TWIN_DIFF_INPUT = 'x'
TWIN_INPUTS = ['x', 'norm_ffn1', 'w_ffn1_in', 'w_ffn1_out', 'norm_mix', 'w_in', 'sinks', 'w_dw', 'b_dw', 'conv_ln_g', 'conv_ln_b', 'w_out', 'norm_ffn2', 'w_ffn2_in', 'w_ffn2_out', 'final_norm', 'loss_target', 'm_norm_ffn1', 'm_w_ffn1_in', 'm_w_ffn1_out', 'm_norm_mix', 'm_w_in', 'm_sinks', 'm_w_dw', 'm_b_dw', 'm_conv_ln_g', 'm_conv_ln_b', 'm_w_out', 'm_norm_ffn2', 'm_w_ffn2_in', 'm_w_ffn2_out', 'm_final_norm', 'v_norm_ffn1', 'v_w_ffn1_in', 'v_w_ffn1_out', 'v_norm_mix', 'v_w_in', 'v_sinks', 'v_w_dw', 'v_b_dw', 'v_conv_ln_g', 'v_conv_ln_b', 'v_w_out', 'v_norm_ffn2', 'v_w_ffn2_in', 'v_w_ffn2_out', 'v_final_norm']
TWIN_OUTPUTS = ['loss', 'grad_x', 'grad_norm_ffn1', 'grad_w_ffn1_in', 'grad_w_ffn1_out', 'grad_norm_mix', 'grad_w_in', 'grad_sinks', 'grad_w_dw', 'grad_b_dw', 'grad_conv_ln_g', 'grad_conv_ln_b', 'grad_w_out', 'grad_norm_ffn2', 'grad_w_ffn2_in', 'grad_w_ffn2_out', 'grad_final_norm', 'delta_norm_ffn1', 'delta_w_ffn1_in', 'delta_w_ffn1_out', 'delta_norm_mix', 'delta_w_in', 'delta_sinks', 'delta_w_dw', 'delta_b_dw', 'delta_conv_ln_g', 'delta_conv_ln_b', 'delta_w_out', 'delta_norm_ffn2', 'delta_w_ffn2_in', 'delta_w_ffn2_out', 'delta_final_norm', 'new_m_norm_ffn1', 'new_m_w_ffn1_in', 'new_m_w_ffn1_out', 'new_m_norm_mix', 'new_m_w_in', 'new_m_sinks', 'new_m_w_dw', 'new_m_b_dw', 'new_m_conv_ln_g', 'new_m_conv_ln_b', 'new_m_w_out', 'new_m_norm_ffn2', 'new_m_w_ffn2_in', 'new_m_w_ffn2_out', 'new_m_final_norm', 'new_v_norm_ffn1', 'new_v_w_ffn1_in', 'new_v_w_ffn1_out', 'new_v_norm_mix', 'new_v_w_in', 'new_v_sinks', 'new_v_w_dw', 'new_v_b_dw', 'new_v_conv_ln_g', 'new_v_conv_ln_b', 'new_v_w_out', 'new_v_norm_ffn2', 'new_v_w_ffn2_in', 'new_v_w_ffn2_out', 'new_v_final_norm']
TWIN_LEAF_KINDS = {'loss': 'loss', 'grad_x': 'grad_x', 'grad_norm_ffn1': 'grad_w', 'grad_w_ffn1_in': 'grad_w', 'grad_w_ffn1_out': 'grad_w', 'grad_norm_mix': 'grad_w', 'grad_w_in': 'grad_w', 'grad_sinks': 'grad_w', 'grad_w_dw': 'grad_w', 'grad_b_dw': 'grad_w', 'grad_conv_ln_g': 'grad_w', 'grad_conv_ln_b': 'grad_w', 'grad_w_out': 'grad_w', 'grad_norm_ffn2': 'grad_w', 'grad_w_ffn2_in': 'grad_w', 'grad_w_ffn2_out': 'grad_w', 'grad_final_norm': 'grad_w', 'delta_norm_ffn1': 'delta_w', 'delta_w_ffn1_in': 'delta_w', 'delta_w_ffn1_out': 'delta_w', 'delta_norm_mix': 'delta_w', 'delta_w_in': 'delta_w', 'delta_sinks': 'delta_w', 'delta_w_dw': 'delta_w', 'delta_b_dw': 'delta_w', 'delta_conv_ln_g': 'delta_w', 'delta_conv_ln_b': 'delta_w', 'delta_w_out': 'delta_w', 'delta_norm_ffn2': 'delta_w', 'delta_w_ffn2_in': 'delta_w', 'delta_w_ffn2_out': 'delta_w', 'delta_final_norm': 'delta_w', 'new_m_norm_ffn1': 'new_m', 'new_m_w_ffn1_in': 'new_m', 'new_m_w_ffn1_out': 'new_m', 'new_m_norm_mix': 'new_m', 'new_m_w_in': 'new_m', 'new_m_sinks': 'new_m', 'new_m_w_dw': 'new_m', 'new_m_b_dw': 'new_m', 'new_m_conv_ln_g': 'new_m', 'new_m_conv_ln_b': 'new_m', 'new_m_w_out': 'new_m', 'new_m_norm_ffn2': 'new_m', 'new_m_w_ffn2_in': 'new_m', 'new_m_w_ffn2_out': 'new_m', 'new_m_final_norm': 'new_m', 'new_v_norm_ffn1': 'new_v', 'new_v_w_ffn1_in': 'new_v', 'new_v_w_ffn1_out': 'new_v', 'new_v_norm_mix': 'new_v', 'new_v_w_in': 'new_v', 'new_v_sinks': 'new_v', 'new_v_w_dw': 'new_v', 'new_v_b_dw': 'new_v', 'new_v_conv_ln_g': 'new_v', 'new_v_conv_ln_b': 'new_v', 'new_v_w_out': 'new_v', 'new_v_norm_ffn2': 'new_v', 'new_v_w_ffn2_in': 'new_v', 'new_v_w_ffn2_out': 'new_v', 'new_v_final_norm': 'new_v'}


def _forward(args):
    return _fwd_reference(*[args[k] for k in FWD_PARAMS])


def _output_shape():
    out = _jax.eval_shape(lambda: _forward(_fwd_setup_inputs(0)))
    return out.shape, out.dtype

N_MICROBATCH = 1
ADAM_LR = 0.001
ADAM_B1 = 0.9
ADAM_B2 = 0.999
ADAM_EPS = 1e-08
ADAM_WD = 0.01
ADAM_STEP = 10
PER_EXAMPLE_BATCH_AXIS = {'x': 0, 'loss_target': 0}
SHARED_INPUTS = []
_WEIGHT_DTYPES = {'norm_ffn1': _jnp.float32, 'w_ffn1_in': _jnp.float32, 'w_ffn1_out': _jnp.float32, 'norm_mix': _jnp.float32, 'w_in': _jnp.float32, 'sinks': _jnp.float32, 'w_dw': _jnp.float32, 'b_dw': _jnp.float32, 'conv_ln_g': _jnp.float32, 'conv_ln_b': _jnp.float32, 'w_out': _jnp.float32, 'norm_ffn2': _jnp.float32, 'w_ffn2_in': _jnp.float32, 'w_ffn2_out': _jnp.float32, 'final_norm': _jnp.float32}
MOMENT_SCALE = {'norm_ffn1': 7.335649e-02, 'w_ffn1_in': 3.094111e-02, 'w_ffn1_out': 5.066606e-02, 'norm_mix': 8.784608e-02, 'w_in': 6.739898e-02, 'sinks': 1.753116e-01, 'w_dw': 9.713023e-02, 'b_dw': 3.077114e-01, 'conv_ln_g': 1.470596e-01, 'conv_ln_b': 1.881729e-01, 'w_out': 9.024283e-02, 'norm_ffn2': 6.504324e-02, 'w_ffn2_in': 2.680039e-02, 'w_ffn2_out': 4.390243e-02, 'final_norm': 3.210151e+01}


def _to_microbatches(a, axis):
    t = _jnp.moveaxis(a, axis, 0)
    t = t.reshape((N_MICROBATCH, t.shape[0] // N_MICROBATCH) + t.shape[1:])
    return _jnp.moveaxis(t, 1, axis + 1)


def setup_inputs(seed: int = 0) -> dict:
    inp = _fwd_setup_inputs(seed)
    key = _jax.random.fold_in(_jax.random.key(seed), 7919)
    shape, _ = _output_shape()
    out = dict(inp)
    out["loss_target"] = _jax.random.normal(_jax.random.fold_in(key, 0), shape, _jnp.float32)
    for i, name in enumerate(TWIN_WEIGHTS):
        w = inp[name].astype(_jnp.float32)
        if MOMENT_SCALE is None:
            s = _jnp.sqrt(_jnp.mean(_jnp.square(w)) + 1e-30)
        else:
            s = MOMENT_SCALE[name]
        km, kv = _jax.random.split(_jax.random.fold_in(key, i + 1))
        out[name] = w
        out["m_" + name] = s * _jax.random.normal(km, w.shape, _jnp.float32)
        out["v_" + name] = (s * s) * _jax.random.uniform(kv, w.shape, _jnp.float32, 0.5, 1.5)
    if N_MICROBATCH > 1:
        for name, axis in PER_EXAMPLE_BATCH_AXIS.items():
            out[name] = _to_microbatches(out[name], axis)
    return {'x': out['x'], 'norm_ffn1': out['norm_ffn1'], 'w_ffn1_in': out['w_ffn1_in'], 'w_ffn1_out': out['w_ffn1_out'], 'norm_mix': out['norm_mix'], 'w_in': out['w_in'], 'sinks': out['sinks'], 'w_dw': out['w_dw'], 'b_dw': out['b_dw'], 'conv_ln_g': out['conv_ln_g'], 'conv_ln_b': out['conv_ln_b'], 'w_out': out['w_out'], 'norm_ffn2': out['norm_ffn2'], 'w_ffn2_in': out['w_ffn2_in'], 'w_ffn2_out': out['w_ffn2_out'], 'final_norm': out['final_norm'], 'loss_target': out['loss_target'], 'm_norm_ffn1': out['m_norm_ffn1'], 'm_w_ffn1_in': out['m_w_ffn1_in'], 'm_w_ffn1_out': out['m_w_ffn1_out'], 'm_norm_mix': out['m_norm_mix'], 'm_w_in': out['m_w_in'], 'm_sinks': out['m_sinks'], 'm_w_dw': out['m_w_dw'], 'm_b_dw': out['m_b_dw'], 'm_conv_ln_g': out['m_conv_ln_g'], 'm_conv_ln_b': out['m_conv_ln_b'], 'm_w_out': out['m_w_out'], 'm_norm_ffn2': out['m_norm_ffn2'], 'm_w_ffn2_in': out['m_w_ffn2_in'], 'm_w_ffn2_out': out['m_w_ffn2_out'], 'm_final_norm': out['m_final_norm'], 'v_norm_ffn1': out['v_norm_ffn1'], 'v_w_ffn1_in': out['v_w_ffn1_in'], 'v_w_ffn1_out': out['v_w_ffn1_out'], 'v_norm_mix': out['v_norm_mix'], 'v_w_in': out['v_w_in'], 'v_sinks': out['v_sinks'], 'v_w_dw': out['v_w_dw'], 'v_b_dw': out['v_b_dw'], 'v_conv_ln_g': out['v_conv_ln_g'], 'v_conv_ln_b': out['v_conv_ln_b'], 'v_w_out': out['v_w_out'], 'v_norm_ffn2': out['v_norm_ffn2'], 'v_w_ffn2_in': out['v_w_ffn2_in'], 'v_w_ffn2_out': out['v_w_ffn2_out'], 'v_final_norm': out['v_final_norm']}


def _loss(weights, diff, rest, loss_target):
    with _jax.named_scope("forward"):
        args = {**rest, TWIN_DIFF_INPUT: diff, **{k: w.astype(_WEIGHT_DTYPES[k]) for k, w in weights.items()}}
        y = _forward(args)
    with _jax.named_scope("loss_head"):
        err = _jnp.square(y.astype(_jnp.float32) - loss_target)
        return 0.5 * _jnp.sum(_jnp.mean(err, axis=-1)) if err.ndim else 0.5 * err


def _adamw(w, g, m, v):
    m = ADAM_B1 * m + (1.0 - ADAM_B1) * g
    v = ADAM_B2 * v + (1.0 - ADAM_B2) * _jnp.square(g)
    m_hat = m / (1.0 - ADAM_B1 ** ADAM_STEP)
    v_hat = v / (1.0 - ADAM_B2 ** ADAM_STEP)
    delta = -ADAM_LR * (m_hat / (_jnp.sqrt(v_hat) + ADAM_EPS) + ADAM_WD * w)
    return delta, m, v


def reference(x, norm_ffn1, w_ffn1_in, w_ffn1_out, norm_mix, w_in, sinks, w_dw, b_dw, conv_ln_g, conv_ln_b, w_out, norm_ffn2, w_ffn2_in, w_ffn2_out, final_norm, loss_target, m_norm_ffn1, m_w_ffn1_in, m_w_ffn1_out, m_norm_mix, m_w_in, m_sinks, m_w_dw, m_b_dw, m_conv_ln_g, m_conv_ln_b, m_w_out, m_norm_ffn2, m_w_ffn2_in, m_w_ffn2_out, m_final_norm, v_norm_ffn1, v_w_ffn1_in, v_w_ffn1_out, v_norm_mix, v_w_in, v_sinks, v_w_dw, v_b_dw, v_conv_ln_g, v_conv_ln_b, v_w_out, v_norm_ffn2, v_w_ffn2_in, v_w_ffn2_out, v_final_norm):
    given = dict(x=x, norm_ffn1=norm_ffn1, w_ffn1_in=w_ffn1_in, w_ffn1_out=w_ffn1_out, norm_mix=norm_mix, w_in=w_in, sinks=sinks, w_dw=w_dw, b_dw=b_dw, conv_ln_g=conv_ln_g, conv_ln_b=conv_ln_b, w_out=w_out, norm_ffn2=norm_ffn2, w_ffn2_in=w_ffn2_in, w_ffn2_out=w_ffn2_out, final_norm=final_norm, loss_target=loss_target, m_norm_ffn1=m_norm_ffn1, m_w_ffn1_in=m_w_ffn1_in, m_w_ffn1_out=m_w_ffn1_out, m_norm_mix=m_norm_mix, m_w_in=m_w_in, m_sinks=m_sinks, m_w_dw=m_w_dw, m_b_dw=m_b_dw, m_conv_ln_g=m_conv_ln_g, m_conv_ln_b=m_conv_ln_b, m_w_out=m_w_out, m_norm_ffn2=m_norm_ffn2, m_w_ffn2_in=m_w_ffn2_in, m_w_ffn2_out=m_w_ffn2_out, m_final_norm=m_final_norm, v_norm_ffn1=v_norm_ffn1, v_w_ffn1_in=v_w_ffn1_in, v_w_ffn1_out=v_w_ffn1_out, v_norm_mix=v_norm_mix, v_w_in=v_w_in, v_sinks=v_sinks, v_w_dw=v_w_dw, v_b_dw=v_b_dw, v_conv_ln_g=v_conv_ln_g, v_conv_ln_b=v_conv_ln_b, v_w_out=v_w_out, v_norm_ffn2=v_norm_ffn2, v_w_ffn2_in=v_w_ffn2_in, v_w_ffn2_out=v_w_ffn2_out, v_final_norm=v_final_norm)
    weights = {n: given[n] for n in TWIN_WEIGHTS}
    shared = {n: given[n] for n in SHARED_INPUTS}
    per_example = {n: given[n] for n in ['x']}
    grad_fn = _jax.value_and_grad(_loss, argnums=(0, 1))

    def one_microbatch(ex, loss_target):
        ex = dict(ex)
        diff = ex.pop(TWIN_DIFF_INPUT)
        return grad_fn(weights, diff, {**shared, **ex}, loss_target)

    if N_MICROBATCH == 1:
        loss, (grad_w, grad_x) = one_microbatch(per_example, given["loss_target"])
    else:
        def body(carry, xs):
            loss_sum, grad_sum = carry
            l_k, (gw_k, gx_k) = one_microbatch(xs[0], xs[1])
            with _jax.named_scope("update"):
                return (loss_sum + l_k, _jax.tree.map(_jnp.add, grad_sum, gw_k)), gx_k

        init = (_jnp.zeros((), _jnp.float32), _jax.tree.map(_jnp.zeros_like, weights))
        (loss, grad_w), grad_x = _jax.lax.scan(body, init, (per_example, given["loss_target"]))
    with _jax.named_scope("update"):
        delta_w, new_m, new_v = {}, {}, {}
        for n in TWIN_WEIGHTS:
            delta_w[n], new_m[n], new_v[n] = _adamw(weights[n], grad_w[n], given["m_" + n], given["v_" + n])
    return (loss, grad_x, *[grad_w[n] for n in TWIN_WEIGHTS], *[delta_w[n] for n in TWIN_WEIGHTS],
            *[new_m[n] for n in TWIN_WEIGHTS], *[new_v[n] for n in TWIN_WEIGHTS])
```

```python
import functools
import math

import jax
import jax.numpy as jnp
from jax import lax
from jax.experimental import pallas as pl
from jax.experimental.pallas import tpu as pltpu

F32 = jnp.float32
BF16 = jnp.bfloat16
MESH = pl.DeviceIdType.MESH

N_DEV = 8
N_CHIP = 4
HEAD_DIM = 64
N_KV_HEADS = 2
WINDOW = 128
KV_DUP = 2 * HEAD_DIM * N_KV_HEADS
RMS_EPS = 1e-6
NEG_INF = -1e30
FFN_RES = 0.5
HALO = 32
ROWS = 32
LANES = 128
V7X_VMEM_LIMIT = 56 * 1024 * 1024

ADAM_LR = 0.001
ADAM_B1 = 0.9
ADAM_B2 = 0.999
ADAM_EPS = 1e-08
ADAM_WD = 0.01
ADAM_STEP = 10


def _pcall(body, **kw):
    return pl.pallas_call(body, **kw)


def _params(*sem):
    return pltpu.CompilerParams(dimension_semantics=sem, vmem_limit_bytes=V7X_VMEM_LIMIT)


def _dot(a, b):
    return jnp.dot(a, b, preferred_element_type=F32)


def _dot_nt(a, b):
    return lax.dot_general(a, b, (((1,), (1,)), ((), ())), preferred_element_type=F32)


def _dot_tn(a, b):
    return lax.dot_general(a, b, (((0,), (0,)), ((), ())), preferred_element_type=F32)


def _sigmoid(x):
    return 1.0 / (1.0 + jnp.exp(-x))


def _rms(x):
    r = lax.rsqrt(jnp.mean(x * x, axis=-1, keepdims=True) + RMS_EPS)
    return x * r, r


def _rms_bwd(dh, xh, r, g):
    dxh = dh * g
    dx = r * (dxh - xh * jnp.mean(dxh * xh, axis=-1, keepdims=True))
    return dx, jnp.sum(dh * xh, axis=0, keepdims=True)


def _sds(shape, dtype):
    return jax.ShapeDtypeStruct(shape, dtype)


def _ffn_fwd(x, gain, win, wout, tm):
    S, D = x.shape
    _, nj, _, NF = win.shape

    def body(x_ref, g_ref, w_ref, wo_ref, xo_ref, gu_ref, h_ref, acc_ref):
        j = pl.program_id(1)

        @pl.when(j == 0)
        def _():
            xh, _ = _rms(x_ref[...])
            h_ref[...] = (xh * g_ref[...]).astype(BF16)
            acc_ref[...] = jnp.zeros_like(acc_ref)

        h = h_ref[...]
        gb = _dot(h, w_ref[0]).astype(BF16)
        ub = _dot(h, w_ref[1]).astype(BF16)
        gu_ref[0] = gb
        gu_ref[1] = ub
        g = gb.astype(F32)
        a = (g * _sigmoid(g) * ub.astype(F32)).astype(BF16)
        acc_ref[...] += _dot(a, wo_ref[...])

        @pl.when(j == nj - 1)
        def _():
            xo_ref[...] = x_ref[...] + FFN_RES * acc_ref[...]

    return _pcall(
        body, grid=(S // tm, nj),
        in_specs=[pl.BlockSpec((tm, D), lambda i, j: (i, 0)),
                  pl.BlockSpec((1, D), lambda i, j: (0, 0)),
                  pl.BlockSpec((2, None, D, NF), lambda i, j: (0, j, 0, 0)),
                  pl.BlockSpec((NF, D), lambda i, j: (j, 0))],
        out_specs=[pl.BlockSpec((tm, D), lambda i, j: (i, 0)),
                   pl.BlockSpec((2, None, tm, NF), lambda i, j: (0, j, i, 0))],
        out_shape=[_sds((S, D), F32), _sds((2, nj, S, NF), BF16)],
        scratch_shapes=[pltpu.VMEM((tm, D), BF16), pltpu.VMEM((tm, D), F32)],
        compiler_params=_params("parallel", "arbitrary"), name="ffn_fwd",
    )(x, gain, win, wout)


def _ffn_bwd_a(x, gain, dxo, gu, win, wout, tm):
    S, D = x.shape
    _, nj, _, NF = win.shape

    def body(x_ref, g_ref, dxo_ref, gu_ref, w_ref, wo_ref, dx_ref, dgu_ref, dgain_ref, dys_ref, dh_ref):
        i, j = pl.program_id(0), pl.program_id(1)

        @pl.when(j == 0)
        def _():
            dys_ref[...] = (FFN_RES * dxo_ref[...]).astype(BF16)
            dh_ref[...] = jnp.zeros_like(dh_ref)

        dact = _dot_nt(dys_ref[...], wo_ref[...])
        g = gu_ref[0].astype(F32)
        u = gu_ref[1].astype(F32)
        s = _sigmoid(g)
        dgb = (dact * u * (s * (1.0 + g * (1.0 - s)))).astype(BF16)
        dub = (dact * (g * s)).astype(BF16)
        dgu_ref[0] = dgb
        dgu_ref[1] = dub
        dh_ref[...] += _dot_nt(dgb, w_ref[0]) + _dot_nt(dub, w_ref[1])

        @pl.when(j == nj - 1)
        def _():
            xh, r = _rms(x_ref[...])
            dxn, dgn = _rms_bwd(dh_ref[...], xh, r, g_ref[...])
            dx_ref[...] = dxo_ref[...] + dxn

            @pl.when(i == 0)
            def _():
                dgain_ref[...] = dgn

            @pl.when(i > 0)
            def _():
                dgain_ref[...] += dgn

    return _pcall(
        body, grid=(S // tm, nj),
        in_specs=[pl.BlockSpec((tm, D), lambda i, j: (i, 0)),
                  pl.BlockSpec((1, D), lambda i, j: (0, 0)),
                  pl.BlockSpec((tm, D), lambda i, j: (i, 0)),
                  pl.BlockSpec((2, None, tm, NF), lambda i, j: (0, j, i, 0)),
                  pl.BlockSpec((2, None, D, NF), lambda i, j: (0, j, 0, 0)),
                  pl.BlockSpec((NF, D), lambda i, j: (j, 0))],
        out_specs=[pl.BlockSpec((tm, D), lambda i, j: (i, 0)),
                   pl.BlockSpec((2, None, tm, NF), lambda i, j: (0, j, i, 0)),
                   pl.BlockSpec((1, D), lambda i, j: (0, 0))],
        out_shape=[_sds((S, D), F32), _sds((2, nj, S, NF), BF16), _sds((1, D), F32)],
        scratch_shapes=[pltpu.VMEM((tm, D), BF16), pltpu.VMEM((tm, D), F32)],
        compiler_params=_params("arbitrary", "arbitrary"), name="ffn_bwd_a",
    )(x, gain, dxo, gu, win, wout)


def _ffn_bwd_w(x, gain, dxo, gu, dgu, tk):
    S, D = x.shape
    _, nj, _, NF = gu.shape
    nk = S // tk

    def body(x_ref, g_ref, dxo_ref, gu_ref, dgu_ref, dw_ref, dwo_ref, accw_ref, acco_ref):
        k = pl.program_id(1)

        @pl.when(k == 0)
        def _():
            accw_ref[...] = jnp.zeros_like(accw_ref)
            acco_ref[...] = jnp.zeros_like(acco_ref)

        xh, _ = _rms(x_ref[...])
        h = (xh * g_ref[...]).astype(BF16)
        dys = (FFN_RES * dxo_ref[...]).astype(BF16)
        g = gu_ref[0].astype(F32)
        act = (g * _sigmoid(g) * gu_ref[1].astype(F32)).astype(BF16)
        accw_ref[0] += _dot_tn(h, dgu_ref[0])
        accw_ref[1] += _dot_tn(h, dgu_ref[1])
        acco_ref[...] += _dot_tn(act, dys)

        @pl.when(k == nk - 1)
        def _():
            dw_ref[...] = accw_ref[...].astype(BF16)
            dwo_ref[...] = acco_ref[...].astype(BF16)

    return _pcall(
        body, grid=(nj, nk),
        in_specs=[pl.BlockSpec((tk, D), lambda j, k: (k, 0)),
                  pl.BlockSpec((1, D), lambda j, k: (0, 0)),
                  pl.BlockSpec((tk, D), lambda j, k: (k, 0)),
                  pl.BlockSpec((2, None, tk, NF), lambda j, k: (0, j, k, 0)),
                  pl.BlockSpec((2, None, tk, NF), lambda j, k: (0, j, k, 0))],
        out_specs=[pl.BlockSpec((2, None, D, NF), lambda j, k: (0, j, 0, 0)),
                   pl.BlockSpec((NF, D), lambda j, k: (j, 0))],
        out_shape=[_sds((2, nj, D, NF), BF16), _sds((nj * NF, D), BF16)],
        scratch_shapes=[pltpu.VMEM((2, D, NF), F32), pltpu.VMEM((NF, D), F32)],
        compiler_params=_params("parallel", "arbitrary"), name="ffn_bwd_w",
    )(x, gain, dxo, gu, dgu)


def _mixin_fwd(x, gain, wext, tm):
    S, D = x.shape
    PW = wext.shape[1]

    def body(x_ref, g_ref, w_ref, p_ref):
        xh, _ = _rms(x_ref[...])
        p_ref[...] = _dot((xh * g_ref[...]).astype(BF16), w_ref[...]).astype(BF16)

    return _pcall(
        body, grid=(S // tm,),
        in_specs=[pl.BlockSpec((tm, D), lambda i: (i, 0)),
                  pl.BlockSpec((1, D), lambda i: (0, 0)),
                  pl.BlockSpec((D, PW), lambda i: (0, 0))],
        out_specs=pl.BlockSpec((tm, PW), lambda i: (i, 0)),
        out_shape=_sds((S, PW), BF16),
        compiler_params=_params("parallel"), name="mixin_fwd",
    )(x, gain, wext)


def _mixin_bwd(x, gain, dxo, dq, dkv, dag, wext, tm):
    S, D = x.shape
    PW = wext.shape[1]
    QW = dq.shape[1]
    o1, o2 = QW, QW + 2 * KV_DUP

    def body(x_ref, g_ref, dxo_ref, dq_ref, dkv_ref, dag_ref, w_ref, dx_ref, dgain_ref, dw_ref):
        i = pl.program_id(0)
        xh, r = _rms(x_ref[...])
        h = (xh * g_ref[...]).astype(BF16)
        dqv, dkvv, dagv = dq_ref[...], dkv_ref[...], dag_ref[...]
        dh = (_dot_nt(dqv, w_ref[:, 0:o1]) + _dot_nt(dkvv, w_ref[:, o1:o2]) + _dot_nt(dagv, w_ref[:, o2:PW]))
        dxn, dgn = _rms_bwd(dh, xh, r, g_ref[...])
        dx_ref[...] = dxo_ref[...] + dxn

        @pl.when(i == 0)
        def _():
            dgain_ref[...] = dgn
            dw_ref[:, 0:o1] = _dot_tn(h, dqv)
            dw_ref[:, o1:o2] = _dot_tn(h, dkvv)
            dw_ref[:, o2:PW] = _dot_tn(h, dagv)

        @pl.when(i > 0)
        def _():
            dgain_ref[...] += dgn
            dw_ref[:, 0:o1] += _dot_tn(h, dqv)
            dw_ref[:, o1:o2] += _dot_tn(h, dkvv)
            dw_ref[:, o2:PW] += _dot_tn(h, dagv)

    return _pcall(
        body, grid=(S // tm,),
        in_specs=[pl.BlockSpec((tm, D), lambda i: (i, 0)),
                  pl.BlockSpec((1, D), lambda i: (0, 0)),
                  pl.BlockSpec((tm, D), lambda i: (i, 0)),
                  pl.BlockSpec((tm, QW), lambda i: (i, 0)),
                  pl.BlockSpec((tm, 2 * KV_DUP), lambda i: (i, 0)),
                  pl.BlockSpec((tm, PW - o2), lambda i: (i, 0)),
                  pl.BlockSpec((D, PW), lambda i: (0, 0))],
        out_specs=[pl.BlockSpec((tm, D), lambda i: (i, 0)),
                   pl.BlockSpec((1, D), lambda i: (0, 0)),
                   pl.BlockSpec((D, PW), lambda i: (0, 0))],
        out_shape=[_sds((S, D), F32), _sds((1, D), F32), _sds((D, PW), F32)],
        compiler_params=_params("arbitrary"), name="mixin_bwd",
    )(x, gain, dxo, dq, dkv, dag, wext)


def _attn_consts(n_heads):
    slopes = [2.0 ** (-8.0 * (h + 1) / n_heads) for h in range(n_heads)]
    return slopes, 1.0 / math.sqrt(HEAD_DIM)


def _attn_mask(n):
    t = lax.broadcasted_iota(jnp.int32, (WINDOW, 2 * WINDOW), 0)
    s = lax.broadcasted_iota(jnp.int32, (WINDOW, 2 * WINDOW), 1)
    dist = t + WINDOW - s
    valid = (dist >= 0) & (dist < WINDOW) & jnp.logical_or(n > 0, s >= WINDOW)
    return valid, dist.astype(F32)


def _lane_halves():
    lo = lax.broadcasted_iota(jnp.int32, (WINDOW, LANES), 1) < HEAD_DIM
    return lo, [jnp.where(lo, 1.0, 0.0).astype(BF16), jnp.where(lo, 0.0, 1.0).astype(BF16)]


def _attn_probs(qm, kd, slope, scale, distf, valid, sk):
    sc = _dot_nt(qm, kd) * scale - slope * distf
    sc = jnp.where(valid, sc, NEG_INF)
    m = jnp.maximum(jnp.max(sc, axis=-1, keepdims=True), sk)
    p = jnp.exp(sc - m)
    es = jnp.exp(sk - m)
    inv = 1.0 / (jnp.sum(p, axis=-1, keepdims=True) + es)
    return p * inv, es * inv


def _attn_fwd(proj, sinks, n_heads):
    S = proj.shape[0]
    QW = n_heads * HEAD_DIM
    nb = S // WINDOW
    kblk, vblk = QW // KV_DUP, QW // KV_DUP + 1
    group = n_heads // N_KV_HEADS
    slopes, scale = _attn_consts(n_heads)

    def body(sink_ref, q_ref, kc_ref, kp_ref, vc_ref, vp_ref, o_ref):
        n = pl.program_id(0)
        valid, distf = _attn_mask(n)
        lo, halves = _lane_halves()
        for hp in range(n_heads // 2):
            kh = (2 * hp) // group
            ksl = slice(LANES * kh, LANES * (kh + 1))
            q2 = q_ref[:, LANES * hp:LANES * (hp + 1)]
            kd = jnp.concatenate([kp_ref[:, ksl], kc_ref[:, ksl]], axis=0)
            vd = jnp.concatenate([vp_ref[:, ksl], vc_ref[:, ksl]], axis=0)
            outs = []
            for e in range(2):
                h = 2 * hp + e
                qm = q2 * halves[e]
                pn, _ = _attn_probs(qm, kd, slopes[h], scale, distf, valid, sink_ref[h])
                outs.append(_dot(pn.astype(BF16), vd))
            o_ref[:, LANES * hp:LANES * (hp + 1)] = jnp.where(lo, outs[0], outs[1]).astype(BF16)

    cur = lambda b: (lambda n: (n, b))
    prev = lambda b: (lambda n: (jnp.maximum(n - 1, 0), b))
    return _pcall(
        body, grid=(nb,),
        in_specs=[pl.BlockSpec(memory_space=pltpu.SMEM),
                  pl.BlockSpec((WINDOW, QW), cur(0)),
                  pl.BlockSpec((WINDOW, KV_DUP), cur(kblk)), pl.BlockSpec((WINDOW, KV_DUP), prev(kblk)),
                  pl.BlockSpec((WINDOW, KV_DUP), cur(vblk)), pl.BlockSpec((WINDOW, KV_DUP), prev(vblk))],
        out_specs=pl.BlockSpec((WINDOW, QW), lambda n: (n, 0)),
        out_shape=_sds((S, QW), BF16),
        compiler_params=_params("parallel"), name="attn_fwd",
    )(sinks, proj, proj, proj, proj, proj)


def _attn_bwd(proj, dmix, sinks, n_heads):
    S = proj.shape[0]
    QW = n_heads * HEAD_DIM
    nb = S // WINDOW
    kblk, vblk = QW // KV_DUP, QW // KV_DUP + 1
    group = n_heads // N_KV_HEADS
    slopes, scale = _attn_consts(n_heads)

    def body(sink_ref, q_ref, kc_ref, kp_ref, vc_ref, vp_ref, do_ref, dq_ref, dkv_ref, dsink_ref, carry_ref):
        n = pl.program_id(0)

        @pl.when(n == 0)
        def _():
            carry_ref[...] = jnp.zeros_like(carry_ref)
            dsink_ref[...] = jnp.zeros_like(dsink_ref)

        @pl.when(n < nb)
        def _():
            valid, distf = _attn_mask(n)
            lo, halves = _lane_halves()
            lane1 = lax.broadcasted_iota(jnp.int32, (1, LANES), 1)
            dkd = [jnp.zeros((2 * WINDOW, LANES), F32) for _ in range(N_KV_HEADS)]
            dvd = [jnp.zeros((2 * WINDOW, LANES), F32) for _ in range(N_KV_HEADS)]
            dsink = jnp.zeros((1, LANES), F32)
            for hp in range(n_heads // 2):
                kh = (2 * hp) // group
                ksl = slice(LANES * kh, LANES * (kh + 1))
                hsl = slice(LANES * hp, LANES * (hp + 1))
                q2 = q_ref[:, hsl]
                do2 = do_ref[:, hsl]
                kd = jnp.concatenate([kp_ref[:, ksl], kc_ref[:, ksl]], axis=0)
                vd = jnp.concatenate([vp_ref[:, ksl], vc_ref[:, ksl]], axis=0)
                dqs = []
                for e in range(2):
                    h = 2 * hp + e
                    qm = q2 * halves[e]
                    dom = do2 * halves[e]
                    pn, psink = _attn_probs(qm, kd, slopes[h], scale, distf, valid, sink_ref[h])
                    dp = _dot_nt(dom, vd)
                    delta = jnp.sum(pn * dp, axis=-1, keepdims=True)
                    dsb = (pn * (dp - delta) * scale).astype(BF16)
                    dsink = dsink - jnp.where(lane1 == h, jnp.sum(psink * delta), 0.0)
                    dqs.append(_dot(dsb, kd))
                    dkd[kh] = dkd[kh] + _dot_tn(dsb, qm)
                    dvd[kh] = dvd[kh] + _dot_tn(pn.astype(BF16), dom)
                dq_ref[:, hsl] = jnp.where(lo, dqs[0], dqs[1]).astype(BF16)
            dsink_ref[...] += dsink
            both = jnp.concatenate(dkd + dvd, axis=1)
            dkv_ref[...] = (carry_ref[...] + both[0:WINDOW]).astype(BF16)
            carry_ref[...] = both[WINDOW:2 * WINDOW]

        @pl.when(n == nb)
        def _():
            dkv_ref[...] = carry_ref[...].astype(BF16)

    last = nb - 1
    cur = lambda b: (lambda n: (jnp.minimum(n, last), b))
    prev = lambda b: (lambda n: (jnp.clip(n - 1, 0, last), b))
    return _pcall(
        body, grid=(nb + 1,),
        in_specs=[pl.BlockSpec(memory_space=pltpu.SMEM),
                  pl.BlockSpec((WINDOW, QW), cur(0)),
                  pl.BlockSpec((WINDOW, KV_DUP), cur(kblk)), pl.BlockSpec((WINDOW, KV_DUP), prev(kblk)),
                  pl.BlockSpec((WINDOW, KV_DUP), cur(vblk)), pl.BlockSpec((WINDOW, KV_DUP), prev(vblk)),
                  pl.BlockSpec((WINDOW, QW), cur(0))],
        out_specs=[pl.BlockSpec((WINDOW, QW), cur(0)),
                   pl.BlockSpec((WINDOW, 2 * KV_DUP), prev(0)),
                   pl.BlockSpec((1, LANES), lambda n: (0, 0))],
        out_shape=[_sds((S, QW), BF16), _sds((S, 2 * KV_DUP), BF16), _sds((1, LANES), F32)],
        scratch_shapes=[pltpu.VMEM((WINDOW, 2 * KV_DUP), F32)],
        compiler_params=_params("arbitrary"), name="attn_bwd",
    )(sinks, proj, proj, proj, proj, proj, dmix)


def _glu_window(a_ref, g_ref, ap_ref, gp_ref, win_ref, first):
    tm = a_ref.shape[0]
    zp = ap_ref[...].astype(F32) * _sigmoid(gp_ref[...].astype(F32))
    win_ref[0:HALO, :] = jnp.where(first, jnp.zeros_like(zp), zp)
    win_ref[HALO:HALO + tm, :] = a_ref[...].astype(F32) * _sigmoid(g_ref[...].astype(F32))


def _conv_fwd(proj, wdw, bdw, lng, lnb, n_heads, tm):
    S = proj.shape[0]
    taps, C = wdw.shape
    ablk = (n_heads * HEAD_DIM + 2 * KV_DUP) // C
    hb = tm // HALO
    off = HALO - (taps - 1)

    def body(a_ref, g_ref, ap_ref, gp_ref, w_ref, b_ref, lg_ref, lb_ref, o_ref, y_ref, win_ref):
        _glu_window(a_ref, g_ref, ap_ref, gp_ref, win_ref, pl.program_id(0) == 0)
        for c in range(tm // ROWS):
            r0 = c * ROWS
            acc = jnp.zeros((ROWS, C), F32) + b_ref[...]
            for k in range(taps):
                acc = acc + w_ref[k:k + 1, :] * win_ref[r0 + off + k:r0 + off + k + ROWS, :]
            y_ref[r0:r0 + ROWS, :] = acc
        y = y_ref[...]
        mu = jnp.mean(y, axis=-1, keepdims=True)
        yc = y - mu
        yn = yc * lax.rsqrt(jnp.mean(yc * yc, axis=-1, keepdims=True) + RMS_EPS) * lg_ref[...] + lb_ref[...]
        o_ref[...] = (yn * _sigmoid(yn)).astype(BF16)

    vec = pl.BlockSpec((1, C), lambda i: (0, 0))
    halo = lambda b: pl.BlockSpec((HALO, C), lambda i: (jnp.maximum(i * hb - 1, 0), b))
    return _pcall(
        body, grid=(S // tm,),
        in_specs=[pl.BlockSpec((tm, C), lambda i: (i, ablk)), pl.BlockSpec((tm, C), lambda i: (i, ablk + 1)),
                  halo(ablk), halo(ablk + 1),
                  pl.BlockSpec((taps, C), lambda i: (0, 0)), vec, vec, vec],
        out_specs=[pl.BlockSpec((tm, C), lambda i: (i, 0)), pl.BlockSpec((tm, C), lambda i: (i, 0))],
        out_shape=[_sds((S, C), BF16), _sds((S, C), F32)],
        scratch_shapes=[pltpu.VMEM((tm + HALO, C), F32)],
        compiler_params=_params("parallel"), name="conv_fwd",
    )(proj, proj, proj, proj, wdw, bdw, lng, lnb)


def _conv_bwd(proj, dmix, ysave, wdw, lng, lnb, n_heads, tm):
    S = proj.shape[0]
    taps, C = wdw.shape
    QW = n_heads * HEAD_DIM
    ablk = (QW + 2 * KV_DUP) // C
    cblk = QW // C
    hb = tm // HALO
    nt = S // tm
    off = HALO - (taps - 1)

    def ln_bwd(dc, y, lg, lb):
        mu = jnp.mean(y, axis=-1, keepdims=True)
        yc = y - mu
        r = lax.rsqrt(jnp.mean(yc * yc, axis=-1, keepdims=True) + RMS_EPS)
        yh = yc * r
        yn = yh * lg + lb
        sg = _sigmoid(yn)
        dyn = dc * (sg * (1.0 + yn * (1.0 - sg)))
        dyh = dyn * lg
        dy = r * (dyh - jnp.mean(dyh, axis=-1, keepdims=True) - yh * jnp.mean(dyh * yh, axis=-1, keepdims=True))
        return dy, dyn, yh

    def body(dc_ref, dcn_ref, y_ref, yn_ref, a_ref, g_ref, ap_ref, gp_ref, w_ref, lg_ref, lb_ref,
             dag_ref, dw_ref, dvec_ref, zwin_ref, dyw_ref, dwacc_ref):
        i = pl.program_id(0)

        @pl.when(i == 0)
        def _():
            dwacc_ref[...] = jnp.zeros_like(dwacc_ref)
            dvec_ref[...] = jnp.zeros_like(dvec_ref)

        lg, lb = lg_ref[...], lb_ref[...]
        dy, dyn, yh = ln_bwd(dc_ref[...].astype(F32), y_ref[...], lg, lb)
        dy_next, _, _ = ln_bwd(dcn_ref[...].astype(F32), yn_ref[...], lg, lb)
        dyw_ref[0:tm, :] = dy
        dyw_ref[tm:tm + HALO, :] = jnp.where(i == nt - 1, jnp.zeros_like(dy_next), dy_next)
        dvec_ref[0:1, :] += jnp.sum(dy, axis=0, keepdims=True)
        dvec_ref[1:2, :] += jnp.sum(dyn * yh, axis=0, keepdims=True)
        dvec_ref[2:3, :] += jnp.sum(dyn, axis=0, keepdims=True)
        _glu_window(a_ref, g_ref, ap_ref, gp_ref, zwin_ref, i == 0)

        for c in range(tm // ROWS):
            r0 = c * ROWS
            dz = jnp.zeros((ROWS, C), F32)
            dyc = dyw_ref[r0:r0 + ROWS, :]
            for k in range(taps):
                dz = dz + w_ref[k:k + 1, :] * dyw_ref[r0 + taps - 1 - k:r0 + taps - 1 - k + ROWS, :]
                prod = dyc * zwin_ref[r0 + off + k:r0 + off + k + ROWS, :]
                dwacc_ref[k] += jnp.sum(prod.reshape(ROWS // 8, 8, C), axis=0)
            a = a_ref[r0:r0 + ROWS, :].astype(F32)
            s = _sigmoid(g_ref[r0:r0 + ROWS, :].astype(F32))
            dag_ref[r0:r0 + ROWS, 0:C] = (dz * s).astype(BF16)
            dag_ref[r0:r0 + ROWS, C:2 * C] = (dz * a * s * (1.0 - s)).astype(BF16)

        @pl.when(i == nt - 1)
        def _():
            dw_ref[...] = jnp.zeros_like(dw_ref)
            for k in range(taps):
                dw_ref[k:k + 1, :] = jnp.sum(dwacc_ref[k], axis=0, keepdims=True)

    vec = pl.BlockSpec((1, C), lambda i: (0, 0))
    tile = lambda b: pl.BlockSpec((tm, C), lambda i: (i, b))
    prev = lambda b: pl.BlockSpec((HALO, C), lambda i: (jnp.maximum(i * hb - 1, 0), b))
    nxt = lambda b: pl.BlockSpec((HALO, C), lambda i: (jnp.minimum((i + 1) * hb, S // HALO - 1), b))
    return _pcall(
        body, grid=(nt,),
        in_specs=[tile(cblk), nxt(cblk), tile(0), nxt(0), tile(ablk), tile(ablk + 1), prev(ablk), prev(ablk + 1),
                  pl.BlockSpec((taps, C), lambda i: (0, 0)), vec, vec],
        out_specs=[pl.BlockSpec((tm, 2 * C), lambda i: (i, 0)),
                   pl.BlockSpec((HALO, C), lambda i: (0, 0)),
                   pl.BlockSpec((8, C), lambda i: (0, 0))],
        out_shape=[_sds((S, 2 * C), BF16), _sds((HALO, C), F32), _sds((8, C), F32)],
        scratch_shapes=[pltpu.VMEM((tm + HALO, C), F32), pltpu.VMEM((tm + HALO, C), F32),
                        pltpu.VMEM((taps, 8, C), F32)],
        compiler_params=_params("arbitrary"), name="conv_bwd",
    )(dmix, dmix, ysave, ysave, proj, proj, proj, proj, wdw, lng, lnb)


def _mixout_fwd(x, attn, conv, wo, tm):
    S, D = x.shape
    QW, C = attn.shape[1], conv.shape[1]

    def body(x_ref, a_ref, c_ref, w_ref, o_ref):
        o_ref[...] = x_ref[...] + _dot(a_ref[...], w_ref[0:QW, :]) + _dot(c_ref[...], w_ref[QW:QW + C, :])

    return _pcall(
        body, grid=(S // tm,),
        in_specs=[pl.BlockSpec((tm, D), lambda i: (i, 0)),
                  pl.BlockSpec((tm, QW), lambda i: (i, 0)),
                  pl.BlockSpec((tm, C), lambda i: (i, 0)),
                  pl.BlockSpec((QW + C, D), lambda i: (0, 0))],
        out_specs=pl.BlockSpec((tm, D), lambda i: (i, 0)),
        out_shape=_sds((S, D), F32),
        compiler_params=_params("parallel"), name="mixout_fwd",
    )(x, attn, conv, wo)


def _mixout_bwd(dxo, attn, conv, wo, tm):
    S, D = dxo.shape
    QW, C = attn.shape[1], conv.shape[1]
    nt = S // tm

    def body(dx_ref, a_ref, c_ref, w_ref, dm_ref, dw_ref, acc_ref):
        i = pl.program_id(0)
        dxb = dx_ref[...].astype(BF16)
        dm_ref[...] = _dot_nt(dxb, w_ref[...]).astype(BF16)

        @pl.when(i == 0)
        def _():
            acc_ref[...] = jnp.zeros_like(acc_ref)

        acc_ref[0:QW, :] += _dot_tn(a_ref[...], dxb)
        acc_ref[QW:QW + C, :] += _dot_tn(c_ref[...], dxb)

        @pl.when(i == nt - 1)
        def _():
            dw_ref[...] = acc_ref[...].astype(BF16)

    return _pcall(
        body, grid=(nt,),
        in_specs=[pl.BlockSpec((tm, D), lambda i: (i, 0)),
                  pl.BlockSpec((tm, QW), lambda i: (i, 0)),
                  pl.BlockSpec((tm, C), lambda i: (i, 0)),
                  pl.BlockSpec((QW + C, D), lambda i: (0, 0))],
        out_specs=[pl.BlockSpec((tm, QW + C), lambda i: (i, 0)),
                   pl.BlockSpec((QW + C, D), lambda i: (0, 0))],
        out_shape=[_sds((S, QW + C), BF16), _sds((QW + C, D), BF16)],
        scratch_shapes=[pltpu.VMEM((QW + C, D), F32)],
        compiler_params=_params("arbitrary"), name="mixout_bwd",
    )(dxo, attn, conv, wo)


def _loss_head(x, gain, target, tm):
    S, D = x.shape
    nt = S // tm

    def body(x_ref, g_ref, t_ref, dx_ref, loss_ref, dgain_ref):
        i = pl.program_id(0)
        xh, r = _rms(x_ref[...])
        e = xh * g_ref[...] - t_ref[...]
        loss_ref[...] = jnp.zeros((1, LANES), F32) + 0.5 * jnp.sum(jnp.mean(e * e, axis=-1, keepdims=True))
        dxn, dgn = _rms_bwd(e * (1.0 / D), xh, r, g_ref[...])
        dx_ref[...] = dxn

        @pl.when(i == 0)
        def _():
            dgain_ref[...] = dgn

        @pl.when(i > 0)
        def _():
            dgain_ref[...] += dgn

    return _pcall(
        body, grid=(nt,),
        in_specs=[pl.BlockSpec((tm, D), lambda i: (i, 0)),
                  pl.BlockSpec((1, D), lambda i: (0, 0)),
                  pl.BlockSpec((tm, D), lambda i: (i, 0))],
        out_specs=[pl.BlockSpec((tm, D), lambda i: (i, 0)),
                   pl.BlockSpec((None, 1, LANES), lambda i: (i, 0, 0)),
                   pl.BlockSpec((1, D), lambda i: (0, 0))],
        out_shape=[_sds((S, D), F32), _sds((nt, 1, LANES), F32), _sds((1, D), F32)],
        compiler_params=_params("arbitrary"), name="loss_head",
    )(x, gain, target)


def _adam(w, m, v, parts, name):
    R, C = w.shape
    P = parts.shape[0]
    br = R if R <= 512 else 256
    c1 = 1.0 - ADAM_B1 ** ADAM_STEP
    c2 = 1.0 - ADAM_B2 ** ADAM_STEP

    def body(w_ref, m_ref, v_ref, p_ref, g_ref, d_ref, mo_ref, vo_ref):
        g = p_ref[0].astype(F32)
        for k in range(1, P):
            g = g + p_ref[k].astype(F32)
        mn = ADAM_B1 * m_ref[...] + (1.0 - ADAM_B1) * g
        vn = ADAM_B2 * v_ref[...] + (1.0 - ADAM_B2) * (g * g)
        g_ref[...] = g
        mo_ref[...] = mn
        vo_ref[...] = vn
        d_ref[...] = -ADAM_LR * ((mn / c1) / (jnp.sqrt(vn / c2) + ADAM_EPS) + ADAM_WD * w_ref[...])

    blk = pl.BlockSpec((br, C), lambda i: (i, 0))
    return _pcall(
        body, grid=(R // br,),
        in_specs=[blk, blk, blk, pl.BlockSpec((P, br, C), lambda i: (0, i, 0))],
        out_specs=[blk, blk, blk, blk],
        out_shape=[_sds((R, C), F32)] * 4,
        compiler_params=_params("parallel"), name=name,
    )(w, m, v, parts)


def _sum_parts(parts):
    P, R, C = parts.shape

    def body(p_ref, o_ref):
        g = p_ref[0]
        for k in range(1, P):
            g = g + p_ref[k]
        o_ref[...] = g

    return _pcall(body, out_shape=_sds((R, C), F32), name="sum_parts",
                  compiler_params=_params())(parts)


def _place():
    x, y, c = lax.axis_index("x"), lax.axis_index("y"), lax.axis_index("c")
    return x, y, c, [(1 - x, y), (x, 1 - y), (1 - x, 1 - y)]


ANY = pl.BlockSpec(memory_space=pl.ANY)


def _all_gather(shards, fulls, slot_of, name):
    n = len(shards)

    def body(*refs):
        srcs, outs = refs[:n], refs[n:2 * n]
        send_sems, recv_sems, local_sems = refs[2 * n:]
        x, y, c, chips = _place()
        dev = lambda px, py, pc: 4 * px + 2 * py + pc
        me, sibling = (x, y, c), (x, y, 1 - c)

        def copy(a, k, block, to, from_shard=False):
            dst = slot_of[a](outs[a], dev(*block))
            return pltpu.make_async_remote_copy(
                src_ref=srcs[a] if from_shard else dst, dst_ref=dst,
                send_sem=send_sems.at[a, k], recv_sem=recv_sems.at[a, k], device_id=to, device_id_type=MESH)

        mine = [pltpu.make_async_copy(srcs[a], slot_of[a](outs[a], dev(*me)), local_sems.at[a]) for a in range(n)]
        for cp in mine:
            cp.start()
        first = []
        for a in range(n):
            first.append(copy(a, 0, me, sibling, True))
            first += [copy(a, 1 + j, me, (*chip, c), True) for j, chip in enumerate(chips)]
        for cp in first:
            cp.start()
        passed = []
        for j, chip in enumerate(chips):
            for a in range(n):
                copy(a, 1 + j, (*chip, c), me).wait_recv()
                fwd = copy(a, 4 + j, (*chip, c), sibling)
                fwd.start()
                passed.append(fwd)
        for a in range(n):
            copy(a, 0, sibling, me).wait_recv()
            for j, chip in enumerate(chips):
                copy(a, 4 + j, (*chip, 1 - c), me).wait_recv()
        for cp in first + passed:
            cp.wait_send()
        for cp in mine:
            cp.wait()

    return _pcall(
        body, out_shape=[_sds(f.shape, f.dtype) for f in fulls],
        in_specs=[ANY] * n, out_specs=[ANY] * n,
        scratch_shapes=[pltpu.SemaphoreType.DMA((n, 7)), pltpu.SemaphoreType.DMA((n, 7)),
                        pltpu.SemaphoreType.DMA((n,))],
        name=name,
    )(*shards)


def _gather_small(v):
    R, C = v.shape

    def body(x_ref, out_ref, send_sems, recv_sems, local_sem):
        x, y, c, chips = _place()
        dev = lambda px, py, pc: 4 * px + 2 * py + pc
        me, sibling = (x, y, c), (x, y, 1 - c)

        def copy(k, block, to, from_shard=False):
            dst = out_ref.at[dev(*block)]
            return pltpu.make_async_remote_copy(
                src_ref=x_ref if from_shard else dst, dst_ref=dst,
                send_sem=send_sems.at[k], recv_sem=recv_sems.at[k], device_id=to, device_id_type=MESH)

        mine = pltpu.make_async_copy(x_ref, out_ref.at[dev(*me)], local_sem)
        mine.start()
        first = [copy(0, me, sibling, True)] + [copy(1 + j, me, (*chip, c), True) for j, chip in enumerate(chips)]
        for cp in first:
            cp.start()
        passed = [copy(4 + j, (*chip, c), sibling) for j, chip in enumerate(chips)]
        for j, chip in enumerate(chips):
            copy(1 + j, (*chip, c), me).wait_recv()
            passed[j].start()
        copy(0, sibling, me).wait_recv()
        for j, chip in enumerate(chips):
            copy(4 + j, (*chip, 1 - c), me).wait_recv()
        for cp in first + passed:
            cp.wait_send()
        mine.wait()

    return _pcall(
        body, out_shape=_sds((N_DEV, R, C), F32),
        in_specs=[pl.BlockSpec(memory_space=pltpu.VMEM)], out_specs=pl.BlockSpec(memory_space=pltpu.VMEM),
        scratch_shapes=[pltpu.SemaphoreType.DMA((7,)), pltpu.SemaphoreType.DMA((7,)), pltpu.SemaphoreType.DMA],
        compiler_params=pltpu.CompilerParams(vmem_limit_bytes=V7X_VMEM_LIMIT), name="gather_small",
    )(v)


def _swap_sibling(grads):
    n = len(grads)

    def body(*refs):
        srcs, outs = refs[:n], refs[n:2 * n]
        send_sems, recv_sems = refs[2 * n:]
        x, y, c, _ = _place()
        cps = [pltpu.make_async_remote_copy(
            src_ref=srcs[a].at[:, pl.ds(1 - c, 1)], dst_ref=outs[a],
            send_sem=send_sems.at[a], recv_sem=recv_sems.at[a], device_id=(x, y, 1 - c), device_id_type=MESH)
            for a in range(n)]
        for cp in cps:
            cp.start()
        for cp in cps:
            cp.wait()

    return _pcall(
        body, out_shape=[_sds((N_CHIP, 1) + g.shape[2:], g.dtype) for g in grads],
        in_specs=[ANY] * n, out_specs=[ANY] * n,
        scratch_shapes=[pltpu.SemaphoreType.DMA((n,)), pltpu.SemaphoreType.DMA((n,))],
        name="rs_swap_sibling",
    )(*grads)


def _add_sibling(core, g, r):
    _, _, R, C = g.shape
    br = R if R <= 512 else 256

    def body(c_ref, g_ref, r_ref, o_ref):
        o_ref[...] = (g_ref[...].astype(F32) + r_ref[...].astype(F32)).astype(BF16)

    return _pcall(
        body,
        grid_spec=pltpu.PrefetchScalarGridSpec(
            num_scalar_prefetch=1, grid=(N_CHIP, R // br),
            in_specs=[pl.BlockSpec((None, None, br, C), lambda k, i, c_ref: (k, c_ref[0], i, 0)),
                      pl.BlockSpec((None, None, br, C), lambda k, i, c_ref: (k, 0, i, 0))],
            out_specs=pl.BlockSpec((None, br, C), lambda k, i, c_ref: (k, i, 0))),
        out_shape=_sds((N_CHIP, R, C), BF16),
        compiler_params=_params("parallel", "parallel"), name="rs_add_sibling",
    )(core, g, r)


def _exchange_chips(parts):
    n = len(parts)

    def body(*refs):
        srcs, outs = refs[:n], refs[n:2 * n]
        send_sems, recv_sems, local_sems = refs[2 * n:]
        x, y, c, chips = _place()
        mine = 2 * x + y
        loc = [pltpu.make_async_copy(srcs[a].at[pl.ds(mine, 1)], outs[a].at[pl.ds(mine, 1)], local_sems.at[a])
               for a in range(n)]
        for cp in loc:
            cp.start()
        cps = []
        for a in range(n):
            for j, (px, py) in enumerate(chips):
                cps.append(pltpu.make_async_remote_copy(
                    src_ref=srcs[a].at[pl.ds(2 * px + py, 1)], dst_ref=outs[a].at[pl.ds(mine, 1)],
                    send_sem=send_sems.at[a, j], recv_sem=recv_sems.at[a, j],
                    device_id=(px, py, c), device_id_type=MESH))
        for cp in cps:
            cp.start()
        for cp in cps:
            cp.wait()
        for cp in loc:
            cp.wait()

    return _pcall(
        body, out_shape=[_sds(p.shape, p.dtype) for p in parts],
        in_specs=[ANY] * n, out_specs=[ANY] * n,
        scratch_shapes=[pltpu.SemaphoreType.DMA((n, 3)), pltpu.SemaphoreType.DMA((n, 3)),
                        pltpu.SemaphoreType.DMA((n,))],
        name="rs_exchange_chips",
    )(*parts)


def _reduce_scatter(core, grads):
    g4 = [g.reshape((N_CHIP, 2) + g.shape[1:]) for g in grads]
    got = _swap_sibling(g4)
    parts = [_add_sibling(core, g, r) for g, r in zip(g4, got)]
    return _exchange_chips(parts)


def _pack_rows(vecs):
    flat = jnp.concatenate([v.reshape(-1).astype(F32) for v in vecs])
    rows = -(-flat.shape[0] // (8 * LANES)) * 8
    return jnp.pad(flat, (0, rows * LANES - flat.shape[0])).reshape(rows, LANES)


def _unpack_rows(rows, shapes):
    flat = rows.reshape(-1)
    out, o = [], 0
    for s in shapes:
        n = math.prod(s)
        out.append(flat[o:o + n].reshape(s))
        o += n
    return out


def _adam_any(w, m, v, g, name):
    shape = w.shape
    cols = shape[-1]
    two = lambda t: t.reshape(-1, cols)
    g_, d_, m_, v_ = _adam(two(w), two(m), two(v), two(g)[None], name)
    return tuple(t.reshape(shape) for t in (g_, d_, m_, v_))


def kernel(x, norm_ffn1, w_ffn1_in, w_ffn1_out, norm_mix, w_in, sinks, w_dw, b_dw, conv_ln_g, conv_ln_b, w_out, norm_ffn2, w_ffn2_in, w_ffn2_out, final_norm, loss_target, m_norm_ffn1, m_w_ffn1_in, m_w_ffn1_out, m_norm_mix, m_w_in, m_sinks, m_w_dw, m_b_dw, m_conv_ln_g, m_conv_ln_b, m_w_out, m_norm_ffn2, m_w_ffn2_in, m_w_ffn2_out, m_final_norm, v_norm_ffn1, v_w_ffn1_in, v_w_ffn1_out, v_norm_mix, v_w_in, v_sinks, v_w_dw, v_b_dw, v_conv_ln_g, v_conv_ln_b, v_w_out, v_norm_ffn2, v_w_ffn2_in, v_w_ffn2_out, v_final_norm):
    _, S, D = x.shape
    L = norm_ffn1.shape[0]
    NF = w_ffn1_in.shape[2]
    RF = w_ffn1_out.shape[1]
    NW = w_in.shape[2]
    RO = w_out.shape[1]
    taps, CD = w_dw.shape[1], w_dw.shape[2]
    H = sinks.shape[1]
    C = N_DEV * CD
    QW = H * HEAD_DIM
    KVW = N_KV_HEADS * HEAD_DIM
    NJ = N_DEV // 2
    assert 2 * RF == NF and QW + C == N_DEV * RO and N_DEV * NW == QW + 2 * KVW + 2 * C
    tm = min(512, S)
    tc = min(256, S)
    core = lax.axis_index("c").astype(jnp.int32).reshape(1)

    x0 = x[0]
    target = loss_target[0]

    wdw_all = _gather_small(_pack_rows([w_dw]))
    wdw_full = jnp.stack(_unpack_rows_dev(wdw_all, (L, taps, CD)), axis=2).reshape(L, taps, C)

    def gather_layer(l):
        a_sh = jnp.stack([w_ffn1_in[l], w_ffn2_in[l]]).astype(BF16)[:, None]
        b_sh = jnp.stack([w_ffn1_out[l], w_ffn2_out[l]]).astype(BF16)
        wi_sh = w_in[l].astype(BF16)[None]
        wo_sh = w_out[l].astype(BF16)
        fulls = [_sds((2, N_DEV, D, NF), BF16), _sds((2, N_DEV * RF, D), BF16),
                 _sds((N_DEV, D, NW), BF16), _sds((N_DEV * RO, D), BF16)]
        slot_of = [lambda r, b: r.at[:, pl.ds(b, 1)],
                   lambda r, b: r.at[:, pl.ds(b * RF, RF)],
                   lambda r, b: r.at[pl.ds(b, 1)],
                   lambda r, b: r.at[pl.ds(b * RO, RO)]]
        a_f, b_f, wi_f, wo_f = _all_gather([a_sh, b_sh, wi_sh, wo_sh], fulls, slot_of, "ag_weights")
        wi = wi_f.transpose(1, 0, 2).reshape(D, N_DEV * NW)
        q, k, v, u = wi[:, :QW], wi[:, QW:QW + KVW], wi[:, QW + KVW:QW + 2 * KVW], wi[:, QW + 2 * KVW:]
        dup = lambda t: jnp.concatenate(
            [t[:, HEAD_DIM * (i // 2):HEAD_DIM * (i // 2 + 1)] for i in range(2 * N_KV_HEADS)], axis=1)
        wext = jnp.concatenate([q, dup(k), dup(v), u], axis=1)
        return dict(a1=a_f[0].reshape(2, NJ, D, NF), a2=a_f[1].reshape(2, NJ, D, NF),
                    b1=b_f[0], b2=b_f[1], wext=wext, wo=wo_f)

    saved = []
    xc = x0
    for l in range(L):
        W = gather_layer(l)
        g1, gm, g2 = norm_ffn1[l][None], norm_mix[l][None], norm_ffn2[l][None]
        x1, gu1 = _ffn_fwd(xc, g1, W["a1"], W["b1"], tm)
        proj = _mixin_fwd(x1, gm, W["wext"], tm)
        attn = _attn_fwd(proj, sinks[l], H)
        conv, ysave = _conv_fwd(proj, wdw_full[l], b_dw[l][None], conv_ln_g[l][None], conv_ln_b[l][None], H, tc)
        x2 = _mixout_fwd(x1, attn, conv, W["wo"], tm)
        x3, gu2 = _ffn_fwd(x2, g2, W["a2"], W["b2"], tm)
        saved.append(dict(W=W, x0=xc, x1=x1, x2=x2, gu1=gu1, gu2=gu2, proj=proj, attn=attn, conv=conv, ysave=ysave))
        xc = x3

    dx, loss_parts, dfinal = _loss_head(xc, final_norm[None], target, tm)
    loss = lax.psum(jnp.sum(loss_parts[:, 0, 0]), ("x", "y", "c"))

    small = [None] * L
    big = [None] * L
    for l in reversed(range(L)):
        sv = saved[l]
        W = sv["W"]
        g1, gm, g2 = norm_ffn1[l][None], norm_mix[l][None], norm_ffn2[l][None]
        dx2, dgu2, dg2 = _ffn_bwd_a(sv["x2"], g2, dx, sv["gu2"], W["a2"], W["b2"], tm)
        da2, db2 = _ffn_bwd_w(sv["x2"], g2, dx, sv["gu2"], dgu2, tm)
        dmix, dwo = _mixout_bwd(dx2, sv["attn"], sv["conv"], W["wo"], tm)
        dag, dwdw, dvec = _conv_bwd(sv["proj"], dmix, sv["ysave"], wdw_full[l], conv_ln_g[l][None],
                                    conv_ln_b[l][None], H, tc)
        dq, dkv, dsink = _attn_bwd(sv["proj"], dmix, sinks[l], H)
        dx1, dgm, dwext = _mixin_bwd(sv["x1"], gm, dx2, dq, dkv, dag, W["wext"], tm)
        dx0, dgu1, dg1 = _ffn_bwd_a(sv["x0"], g1, dx1, sv["gu1"], W["a1"], W["b1"], tm)
        da1, db1 = _ffn_bwd_w(sv["x0"], g1, dx1, sv["gu1"], dgu1, tm)
        dx = dx0

        fold = lambda t: jnp.concatenate(
            [t[:, 2 * HEAD_DIM * i:2 * HEAD_DIM * i + HEAD_DIM] + t[:, 2 * HEAD_DIM * i + HEAD_DIM:2 * HEAD_DIM * (i + 1)]
             for i in range(N_KV_HEADS)], axis=1)
        dwi = jnp.concatenate([dwext[:, :QW], fold(dwext[:, QW:QW + KV_DUP]),
                               fold(dwext[:, QW + KV_DUP:QW + 2 * KV_DUP]), dwext[:, QW + 2 * KV_DUP:]], axis=1)
        dwi = dwi.reshape(D, N_DEV, NW).transpose(1, 0, 2).astype(BF16)
        grads = [da1.reshape(N_DEV, D, NF), db1.reshape(N_DEV, RF, D), da2.reshape(N_DEV, D, NF),
                 db2.reshape(N_DEV, RF, D), dwi, dwo.reshape(N_DEV, RO, D)]
        big[l] = _reduce_scatter(core, grads)
        small[l] = [dg1[0], dgm[0], dsink[0, :H], dwdw[:taps], dvec[0], dvec[1], dvec[2], dg2[0]]

    grad_x = dx[None]

    small_shapes = [(D,), (D,), (H,), (taps, C), (C,), (C,), (C,), (D,)]
    packed = _pack_rows([t for l in range(L) for t in small[l]] + [dfinal[0]])
    total = _sum_parts(_gather_small(packed))
    flat = _unpack_rows(total, small_shapes * L + [(D,)])
    per = [jnp.stack([flat[l * len(small_shapes) + i] for l in range(L)]) for i in range(len(small_shapes))]
    g_nf1, g_nmix, g_sinks, g_wdw_full, g_bdw, g_lng, g_lnb, g_nf2 = per
    g_final = flat[-1]
    dev = 4 * lax.axis_index("x") + 2 * lax.axis_index("y") + lax.axis_index("c")
    g_wdw = lax.dynamic_slice_in_dim(g_wdw_full, dev * CD, CD, axis=2)

    res = {}
    res["norm_ffn1"] = _adam_any(norm_ffn1, m_norm_ffn1, v_norm_ffn1, g_nf1, "adam_small")
    res["norm_mix"] = _adam_any(norm_mix, m_norm_mix, v_norm_mix, g_nmix, "adam_small")
    res["sinks"] = _adam_any(sinks, m_sinks, v_sinks, g_sinks, "adam_small")
    res["w_dw"] = _adam_any(w_dw, m_w_dw, v_w_dw, g_wdw, "adam_small")
    res["b_dw"] = _adam_any(b_dw, m_b_dw, v_b_dw, g_bdw, "adam_small")
    res["conv_ln_g"] = _adam_any(conv_ln_g, m_conv_ln_g, v_conv_ln_g, g_lng, "adam_small")
    res["conv_ln_b"] = _adam_any(conv_ln_b, m_conv_ln_b, v_conv_ln_b, g_lnb, "adam_small")
    res["norm_ffn2"] = _adam_any(norm_ffn2, m_norm_ffn2, v_norm_ffn2, g_nf2, "adam_small")
    res["final_norm"] = _adam_any(final_norm[None], m_final_norm[None], v_final_norm[None], g_final[None], "adam_small")
    res["final_norm"] = tuple(t[0] for t in res["final_norm"])

    def adam_big(i, w, m, v, name):
        outs = [_adam(w[l], m[l], v[l], big[l][i], name) for l in range(L)]
        return tuple(jnp.stack([o[t] for o in outs]) for t in range(4))

    res["w_ffn1_in"] = adam_big(0, w_ffn1_in, m_w_ffn1_in, v_w_ffn1_in, "adam_ffn_in")
    res["w_ffn1_out"] = adam_big(1, w_ffn1_out, m_w_ffn1_out, v_w_ffn1_out, "adam_ffn_out")
    res["w_ffn2_in"] = adam_big(2, w_ffn2_in, m_w_ffn2_in, v_w_ffn2_in, "adam_ffn_in")
    res["w_ffn2_out"] = adam_big(3, w_ffn2_out, m_w_ffn2_out, v_w_ffn2_out, "adam_ffn_out")
    res["w_in"] = adam_big(4, w_in, m_w_in, v_w_in, "adam_w_in")
    res["w_out"] = adam_big(5, w_out, m_w_out, v_w_out, "adam_w_out")

    order = ["norm_ffn1", "w_ffn1_in", "w_ffn1_out", "norm_mix", "w_in", "sinks", "w_dw", "b_dw", "conv_ln_g",
             "conv_ln_b", "w_out", "norm_ffn2", "w_ffn2_in", "w_ffn2_out", "final_norm"]
    return (loss, grad_x, *[res[n][0] for n in order], *[res[n][1] for n in order],
            *[res[n][2] for n in order], *[res[n][3] for n in order])


def _unpack_rows_dev(rows_all, shape):
    n = math.prod(shape)
    return [rows_all[d].reshape(-1)[:n].reshape(shape) for d in range(N_DEV)]
```

```python
import functools
import math

import jax
import jax.numpy as jnp
from jax import lax
from jax.experimental import pallas as pl
from jax.experimental.pallas import tpu as pltpu

F32 = jnp.float32
BF16 = jnp.bfloat16
MESH = pl.DeviceIdType.MESH

N_DEV = 8
N_CHIP = 4
HEAD_DIM = 64
N_KV_HEADS = 2
WINDOW = 128
KV_DUP = 2 * HEAD_DIM * N_KV_HEADS
RMS_EPS = 1e-6
NEG_INF = -1e30
FFN_RES = 0.5
HALO = 32
ROWS = 32
LANES = 128
V7X_VMEM_LIMIT = 56 * 1024 * 1024

ADAM_LR = 0.001
ADAM_B1 = 0.9
ADAM_B2 = 0.999
ADAM_EPS = 1e-08
ADAM_WD = 0.01
ADAM_STEP = 10


def _raw_call(body, **kw):
    return pl.pallas_call(body, **kw)


class _Comm:
    def __init__(self, ins, outs, sems, start, finish):
        self.ins, self.outs, self.sems, self.start, self.finish = list(ins), list(outs), list(sems), start, finish


def _join(*comms):
    comms = [c for c in comms if c is not None]
    if not comms:
        return None

    def split(refs, attr):
        out, o = [], 0
        for c in comms:
            n = len(getattr(c, attr))
            out.append(refs[o:o + n])
            o += n
        return out

    def run(which):
        def go(ins, outs, sems):
            for c, i, o, m in zip(comms, split(ins, "ins"), split(outs, "outs"), split(sems, "sems")):
                getattr(c, which)(i, o, m)
        return go

    return _Comm(sum((c.ins for c in comms), []), sum((c.outs for c in comms), []),
                 sum((c.sems for c in comms), []), run("start"), run("finish"))


def _pcall(body, args, *, name, out_shape, grid=(), in_specs=None, out_specs=None, scratch_shapes=(), sem=(),
           comm=None, grid_spec=None):
    if grid_spec is not None:
        return _raw_call(body, grid_spec=grid_spec, out_shape=out_shape, name=name,
                         compiler_params=_params(*sem))(*args)
    if comm is None:
        return _raw_call(body, grid=grid, in_specs=in_specs, out_specs=out_specs, out_shape=out_shape,
                         scratch_shapes=list(scratch_shapes), name=name, compiler_params=_params(*sem))(*args)
    n_in, n_out, n_scr = len(in_specs), len(out_shape), len(scratch_shapes)
    ci, co = len(comm.ins), len(comm.outs)

    def fused(*refs):
        cuts = [n_in, ci, n_out, co, n_scr]
        parts, o = [], 0
        for n in cuts:
            parts.append(refs[o:o + n])
            o += n
        ins, cins, outs, couts, scr = parts
        csems = refs[o:]
        if not grid:
            comm.start(cins, couts, csems)
            body(*ins, *outs, *scr)
            comm.finish(cins, couts, csems)
            return
        ids = [pl.program_id(a) for a in range(len(grid))]
        first = functools.reduce(jnp.logical_and, [i == 0 for i in ids])
        last = functools.reduce(jnp.logical_and, [i == g - 1 for i, g in zip(ids, grid)])

        @pl.when(first)
        def _():
            comm.start(cins, couts, csems)

        body(*ins, *outs, *scr)

        @pl.when(last)
        def _():
            comm.finish(cins, couts, csems)

    return _raw_call(
        fused, grid=grid, in_specs=list(in_specs) + [ANY] * ci, out_specs=list(out_specs) + [ANY] * co,
        out_shape=list(out_shape) + comm.outs, scratch_shapes=list(scratch_shapes) + comm.sems, name=name,
        compiler_params=_params(*(["arbitrary"] * len(grid))))(*args, *comm.ins)


ANY = pl.BlockSpec(memory_space=pl.ANY)


def _params(*sem):
    return pltpu.CompilerParams(dimension_semantics=sem, vmem_limit_bytes=V7X_VMEM_LIMIT)


def _dot(a, b):
    return jnp.dot(a, b, preferred_element_type=F32)


def _dot_nt(a, b):
    return lax.dot_general(a, b, (((1,), (1,)), ((), ())), preferred_element_type=F32)


def _dot_tn(a, b):
    return lax.dot_general(a, b, (((0,), (0,)), ((), ())), preferred_element_type=F32)


def _sigmoid(x):
    return 1.0 / (1.0 + jnp.exp(-x))


def _rms(x):
    r = lax.rsqrt(jnp.mean(x * x, axis=-1, keepdims=True) + RMS_EPS)
    return x * r, r


def _rms_bwd(dh, xh, r, g):
    dxh = dh * g
    dx = r * (dxh - xh * jnp.mean(dxh * xh, axis=-1, keepdims=True))
    return dx, jnp.sum(dh * xh, axis=0, keepdims=True)


def _sds(shape, dtype):
    return jax.ShapeDtypeStruct(shape, dtype)


def _ffn_fwd(x, gain, win, wout, tm, comm=None):
    S, D = x.shape
    _, nj, _, NF = win.shape

    def body(x_ref, g_ref, w_ref, wo_ref, xo_ref, gu_ref, h_ref, acc_ref):
        j = pl.program_id(1)

        @pl.when(j == 0)
        def _():
            xh, _ = _rms(x_ref[...])
            h_ref[...] = (xh * g_ref[...]).astype(BF16)
            acc_ref[...] = jnp.zeros_like(acc_ref)

        h = h_ref[...]
        gb = _dot(h, w_ref[0]).astype(BF16)
        ub = _dot(h, w_ref[1]).astype(BF16)
        gu_ref[0] = gb
        gu_ref[1] = ub
        g = gb.astype(F32)
        a = (g * _sigmoid(g) * ub.astype(F32)).astype(BF16)
        acc_ref[...] += _dot(a, wo_ref[...])

        @pl.when(j == nj - 1)
        def _():
            xo_ref[...] = x_ref[...] + FFN_RES * acc_ref[...]

    return _pcall(
        body, (x, gain, win, wout,), grid=(S // tm, nj),
        in_specs=[pl.BlockSpec((tm, D), lambda i, j: (i, 0)),
                  pl.BlockSpec((1, D), lambda i, j: (0, 0)),
                  pl.BlockSpec((2, None, D, NF), lambda i, j: (0, j, 0, 0)),
                  pl.BlockSpec((NF, D), lambda i, j: (j, 0))],
        out_specs=[pl.BlockSpec((tm, D), lambda i, j: (i, 0)),
                   pl.BlockSpec((2, None, tm, NF), lambda i, j: (0, j, i, 0))],
        out_shape=[_sds((S, D), F32), _sds((2, nj, S, NF), BF16)],
        scratch_shapes=[pltpu.VMEM((tm, D), BF16), pltpu.VMEM((tm, D), F32)],
        sem=("parallel", "arbitrary",), name="ffn_fwd", comm=comm)


def _ffn_bwd_a(x, gain, dxo, gu, win, wout, tm, comm=None):
    S, D = x.shape
    _, nj, _, NF = win.shape

    def body(x_ref, g_ref, dxo_ref, gu_ref, w_ref, wo_ref, dx_ref, dgu_ref, dgain_ref, dys_ref, dh_ref):
        i, j = pl.program_id(0), pl.program_id(1)

        @pl.when(j == 0)
        def _():
            dys_ref[...] = (FFN_RES * dxo_ref[...]).astype(BF16)
            dh_ref[...] = jnp.zeros_like(dh_ref)

        dact = _dot_nt(dys_ref[...], wo_ref[...])
        g = gu_ref[0].astype(F32)
        u = gu_ref[1].astype(F32)
        s = _sigmoid(g)
        dgb = (dact * u * (s * (1.0 + g * (1.0 - s)))).astype(BF16)
        dub = (dact * (g * s)).astype(BF16)
        dgu_ref[0] = dgb
        dgu_ref[1] = dub
        dh_ref[...] += _dot_nt(dgb, w_ref[0]) + _dot_nt(dub, w_ref[1])

        @pl.when(j == nj - 1)
        def _():
            xh, r = _rms(x_ref[...])
            dxn, dgn = _rms_bwd(dh_ref[...], xh, r, g_ref[...])
            dx_ref[...] = dxo_ref[...] + dxn

            @pl.when(i == 0)
            def _():
                dgain_ref[...] = dgn

            @pl.when(i > 0)
            def _():
                dgain_ref[...] += dgn

    return _pcall(
        body, (x, gain, dxo, gu, win, wout,), grid=(S // tm, nj),
        in_specs=[pl.BlockSpec((tm, D), lambda i, j: (i, 0)),
                  pl.BlockSpec((1, D), lambda i, j: (0, 0)),
                  pl.BlockSpec((tm, D), lambda i, j: (i, 0)),
                  pl.BlockSpec((2, None, tm, NF), lambda i, j: (0, j, i, 0)),
                  pl.BlockSpec((2, None, D, NF), lambda i, j: (0, j, 0, 0)),
                  pl.BlockSpec((NF, D), lambda i, j: (j, 0))],
        out_specs=[pl.BlockSpec((tm, D), lambda i, j: (i, 0)),
                   pl.BlockSpec((2, None, tm, NF), lambda i, j: (0, j, i, 0)),
                   pl.BlockSpec((1, D), lambda i, j: (0, 0))],
        out_shape=[_sds((S, D), F32), _sds((2, nj, S, NF), BF16), _sds((1, D), F32)],
        scratch_shapes=[pltpu.VMEM((tm, D), BF16), pltpu.VMEM((tm, D), F32)],
        sem=("arbitrary", "arbitrary",), name="ffn_bwd_a", comm=comm)


def _ffn_bwd_w(x, gain, dxo, gu, dgu, tk, comm=None):
    S, D = x.shape
    _, nj, _, NF = gu.shape
    nk = S // tk

    def body(x_ref, g_ref, dxo_ref, gu_ref, dgu_ref, dw_ref, dwo_ref, accw_ref, acco_ref):
        k = pl.program_id(1)

        @pl.when(k == 0)
        def _():
            accw_ref[...] = jnp.zeros_like(accw_ref)
            acco_ref[...] = jnp.zeros_like(acco_ref)

        xh, _ = _rms(x_ref[...])
        h = (xh * g_ref[...]).astype(BF16)
        dys = (FFN_RES * dxo_ref[...]).astype(BF16)
        g = gu_ref[0].astype(F32)
        act = (g * _sigmoid(g) * gu_ref[1].astype(F32)).astype(BF16)
        accw_ref[0] += _dot_tn(h, dgu_ref[0])
        accw_ref[1] += _dot_tn(h, dgu_ref[1])
        acco_ref[...] += _dot_tn(act, dys)

        @pl.when(k == nk - 1)
        def _():
            dw_ref[...] = accw_ref[...].astype(BF16)
            dwo_ref[...] = acco_ref[...].astype(BF16)

    return _pcall(
        body, (x, gain, dxo, gu, dgu,), grid=(nj, nk),
        in_specs=[pl.BlockSpec((tk, D), lambda j, k: (k, 0)),
                  pl.BlockSpec((1, D), lambda j, k: (0, 0)),
                  pl.BlockSpec((tk, D), lambda j, k: (k, 0)),
                  pl.BlockSpec((2, None, tk, NF), lambda j, k: (0, j, k, 0)),
                  pl.BlockSpec((2, None, tk, NF), lambda j, k: (0, j, k, 0))],
        out_specs=[pl.BlockSpec((2, None, D, NF), lambda j, k: (0, j, 0, 0)),
                   pl.BlockSpec((NF, D), lambda j, k: (j, 0))],
        out_shape=[_sds((2, nj, D, NF), BF16), _sds((nj * NF, D), BF16)],
        scratch_shapes=[pltpu.VMEM((2, D, NF), F32), pltpu.VMEM((NF, D), F32)],
        sem=("parallel", "arbitrary",), name="ffn_bwd_w", comm=comm)


def _mixin_fwd(x, gain, wext, tm, comm=None):
    S, D = x.shape
    PW = wext.shape[1]

    def body(x_ref, g_ref, w_ref, p_ref):
        xh, _ = _rms(x_ref[...])
        p_ref[...] = _dot((xh * g_ref[...]).astype(BF16), w_ref[...]).astype(BF16)

    return _pcall(
        body, (x, gain, wext,), grid=(S // tm,),
        in_specs=[pl.BlockSpec((tm, D), lambda i: (i, 0)),
                  pl.BlockSpec((1, D), lambda i: (0, 0)),
                  pl.BlockSpec((D, PW), lambda i: (0, 0))],
        out_specs=[pl.BlockSpec((tm, PW), lambda i: (i, 0))],
        out_shape=[_sds((S, PW), BF16)],
        sem=("parallel",), name="mixin_fwd", comm=comm)


def _mixin_bwd(x, gain, dxo, dq, dkv, dag, wext, tm, comm=None):
    S, D = x.shape
    PW = wext.shape[1]
    QW = dq.shape[1]
    o1, o2 = QW, QW + 2 * KV_DUP

    def body(x_ref, g_ref, dxo_ref, dq_ref, dkv_ref, dag_ref, w_ref, dx_ref, dgain_ref, dw_ref):
        i = pl.program_id(0)
        xh, r = _rms(x_ref[...])
        h = (xh * g_ref[...]).astype(BF16)
        dqv, dkvv, dagv = dq_ref[...], dkv_ref[...], dag_ref[...]
        dh = (_dot_nt(dqv, w_ref[:, 0:o1]) + _dot_nt(dkvv, w_ref[:, o1:o2]) + _dot_nt(dagv, w_ref[:, o2:PW]))
        dxn, dgn = _rms_bwd(dh, xh, r, g_ref[...])
        dx_ref[...] = dxo_ref[...] + dxn

        @pl.when(i == 0)
        def _():
            dgain_ref[...] = dgn
            dw_ref[:, 0:o1] = _dot_tn(h, dqv)
            dw_ref[:, o1:o2] = _dot_tn(h, dkvv)
            dw_ref[:, o2:PW] = _dot_tn(h, dagv)

        @pl.when(i > 0)
        def _():
            dgain_ref[...] += dgn
            dw_ref[:, 0:o1] += _dot_tn(h, dqv)
            dw_ref[:, o1:o2] += _dot_tn(h, dkvv)
            dw_ref[:, o2:PW] += _dot_tn(h, dagv)

    return _pcall(
        body, (x, gain, dxo, dq, dkv, dag, wext,), grid=(S // tm,),
        in_specs=[pl.BlockSpec((tm, D), lambda i: (i, 0)),
                  pl.BlockSpec((1, D), lambda i: (0, 0)),
                  pl.BlockSpec((tm, D), lambda i: (i, 0)),
                  pl.BlockSpec((tm, QW), lambda i: (i, 0)),
                  pl.BlockSpec((tm, 2 * KV_DUP), lambda i: (i, 0)),
                  pl.BlockSpec((tm, PW - o2), lambda i: (i, 0)),
                  pl.BlockSpec((D, PW), lambda i: (0, 0))],
        out_specs=[pl.BlockSpec((tm, D), lambda i: (i, 0)),
                   pl.BlockSpec((1, D), lambda i: (0, 0)),
                   pl.BlockSpec((D, PW), lambda i: (0, 0))],
        out_shape=[_sds((S, D), F32), _sds((1, D), F32), _sds((D, PW), F32)],
        sem=("arbitrary",), name="mixin_bwd", comm=comm)


def _attn_consts(n_heads):
    slopes = [2.0 ** (-8.0 * (h + 1) / n_heads) for h in range(n_heads)]
    return slopes, 1.0 / math.sqrt(HEAD_DIM)


def _attn_mask(n):
    t = lax.broadcasted_iota(jnp.int32, (WINDOW, 2 * WINDOW), 0)
    s = lax.broadcasted_iota(jnp.int32, (WINDOW, 2 * WINDOW), 1)
    dist = t + WINDOW - s
    valid = (dist >= 0) & (dist < WINDOW) & jnp.logical_or(n > 0, s >= WINDOW)
    return valid, dist.astype(F32)


def _lane_halves():
    lo = lax.broadcasted_iota(jnp.int32, (WINDOW, LANES), 1) < HEAD_DIM
    return lo, [jnp.where(lo, 1.0, 0.0).astype(BF16), jnp.where(lo, 0.0, 1.0).astype(BF16)]


def _attn_probs(qm, kd, slope, scale, distf, valid, sk):
    sc = _dot_nt(qm, kd) * scale - slope * distf
    sc = jnp.where(valid, sc, NEG_INF)
    m = jnp.maximum(jnp.max(sc, axis=-1, keepdims=True), sk)
    p = jnp.exp(sc - m)
    es = jnp.exp(sk - m)
    inv = 1.0 / (jnp.sum(p, axis=-1, keepdims=True) + es)
    return p * inv, es * inv


def _attn_fwd(proj, sinks, n_heads, comm=None):
    S = proj.shape[0]
    QW = n_heads * HEAD_DIM
    nb = S // WINDOW
    kblk, vblk = QW // KV_DUP, QW // KV_DUP + 1
    group = n_heads // N_KV_HEADS
    slopes, scale = _attn_consts(n_heads)

    def body(sink_ref, q_ref, kc_ref, kp_ref, vc_ref, vp_ref, o_ref):
        n = pl.program_id(0)
        valid, distf = _attn_mask(n)
        lo, halves = _lane_halves()
        for hp in range(n_heads // 2):
            kh = (2 * hp) // group
            ksl = slice(LANES * kh, LANES * (kh + 1))
            q2 = q_ref[:, LANES * hp:LANES * (hp + 1)]
            kd = jnp.concatenate([kp_ref[:, ksl], kc_ref[:, ksl]], axis=0)
            vd = jnp.concatenate([vp_ref[:, ksl], vc_ref[:, ksl]], axis=0)
            outs = []
            for e in range(2):
                h = 2 * hp + e
                qm = q2 * halves[e]
                pn, _ = _attn_probs(qm, kd, slopes[h], scale, distf, valid, sink_ref[h])
                outs.append(_dot(pn.astype(BF16), vd))
            o_ref[:, LANES * hp:LANES * (hp + 1)] = jnp.where(lo, outs[0], outs[1]).astype(BF16)

    cur = lambda b: (lambda n: (n, b))
    prev = lambda b: (lambda n: (jnp.maximum(n - 1, 0), b))
    return _pcall(
        body, (sinks, proj, proj, proj, proj, proj,), grid=(nb,),
        in_specs=[pl.BlockSpec(memory_space=pltpu.SMEM),
                  pl.BlockSpec((WINDOW, QW), cur(0)),
                  pl.BlockSpec((WINDOW, KV_DUP), cur(kblk)), pl.BlockSpec((WINDOW, KV_DUP), prev(kblk)),
                  pl.BlockSpec((WINDOW, KV_DUP), cur(vblk)), pl.BlockSpec((WINDOW, KV_DUP), prev(vblk))],
        out_specs=[pl.BlockSpec((WINDOW, QW), lambda n: (n, 0))],
        out_shape=[_sds((S, QW), BF16)],
        sem=("parallel",), name="attn_fwd", comm=comm)


def _attn_bwd(proj, dmix, sinks, n_heads, comm=None):
    S = proj.shape[0]
    QW = n_heads * HEAD_DIM
    nb = S // WINDOW
    kblk, vblk = QW // KV_DUP, QW // KV_DUP + 1
    group = n_heads // N_KV_HEADS
    slopes, scale = _attn_consts(n_heads)

    def body(sink_ref, q_ref, kc_ref, kp_ref, vc_ref, vp_ref, do_ref, dq_ref, dkv_ref, dsink_ref, carry_ref):
        n = pl.program_id(0)

        @pl.when(n == 0)
        def _():
            carry_ref[...] = jnp.zeros_like(carry_ref)
            dsink_ref[...] = jnp.zeros_like(dsink_ref)

        @pl.when(n < nb)
        def _():
            valid, distf = _attn_mask(n)
            lo, halves = _lane_halves()
            lane1 = lax.broadcasted_iota(jnp.int32, (1, LANES), 1)
            dkd = [jnp.zeros((2 * WINDOW, LANES), F32) for _ in range(N_KV_HEADS)]
            dvd = [jnp.zeros((2 * WINDOW, LANES), F32) for _ in range(N_KV_HEADS)]
            dsink = jnp.zeros((1, LANES), F32)
            for hp in range(n_heads // 2):
                kh = (2 * hp) // group
                ksl = slice(LANES * kh, LANES * (kh + 1))
                hsl = slice(LANES * hp, LANES * (hp + 1))
                q2 = q_ref[:, hsl]
                do2 = do_ref[:, hsl]
                kd = jnp.concatenate([kp_ref[:, ksl], kc_ref[:, ksl]], axis=0)
                vd = jnp.concatenate([vp_ref[:, ksl], vc_ref[:, ksl]], axis=0)
                dqs = []
                for e in range(2):
                    h = 2 * hp + e
                    qm = q2 * halves[e]
                    dom = do2 * halves[e]
                    pn, psink = _attn_probs(qm, kd, slopes[h], scale, distf, valid, sink_ref[h])
                    dp = _dot_nt(dom, vd)
                    delta = jnp.sum(pn * dp, axis=-1, keepdims=True)
                    dsb = (pn * (dp - delta) * scale).astype(BF16)
                    dsink = dsink - jnp.where(lane1 == h, jnp.sum(psink * delta), 0.0)
                    dqs.append(_dot(dsb, kd))
                    dkd[kh] = dkd[kh] + _dot_tn(dsb, qm)
                    dvd[kh] = dvd[kh] + _dot_tn(pn.astype(BF16), dom)
                dq_ref[:, hsl] = jnp.where(lo, dqs[0], dqs[1]).astype(BF16)
            dsink_ref[...] += dsink
            both = jnp.concatenate(dkd + dvd, axis=1)
            dkv_ref[...] = (carry_ref[...] + both[0:WINDOW]).astype(BF16)
            carry_ref[...] = both[WINDOW:2 * WINDOW]

        @pl.when(n == nb)
        def _():
            dkv_ref[...] = carry_ref[...].astype(BF16)

    last = nb - 1
    cur = lambda b: (lambda n: (jnp.minimum(n, last), b))
    prev = lambda b: (lambda n: (jnp.clip(n - 1, 0, last), b))
    return _pcall(
        body, (sinks, proj, proj, proj, proj, proj, dmix,), grid=(nb + 1,),
        in_specs=[pl.BlockSpec(memory_space=pltpu.SMEM),
                  pl.BlockSpec((WINDOW, QW), cur(0)),
                  pl.BlockSpec((WINDOW, KV_DUP), cur(kblk)), pl.BlockSpec((WINDOW, KV_DUP), prev(kblk)),
                  pl.BlockSpec((WINDOW, KV_DUP), cur(vblk)), pl.BlockSpec((WINDOW, KV_DUP), prev(vblk)),
                  pl.BlockSpec((WINDOW, QW), cur(0))],
        out_specs=[pl.BlockSpec((WINDOW, QW), cur(0)),
                   pl.BlockSpec((WINDOW, 2 * KV_DUP), prev(0)),
                   pl.BlockSpec((1, LANES), lambda n: (0, 0))],
        out_shape=[_sds((S, QW), BF16), _sds((S, 2 * KV_DUP), BF16), _sds((1, LANES), F32)],
        scratch_shapes=[pltpu.VMEM((WINDOW, 2 * KV_DUP), F32)],
        sem=("arbitrary",), name="attn_bwd", comm=comm)


def _glu_window(a_ref, g_ref, ap_ref, gp_ref, win_ref, first):
    tm = a_ref.shape[0]
    zp = ap_ref[...].astype(F32) * _sigmoid(gp_ref[...].astype(F32))
    win_ref[0:HALO, :] = jnp.where(first, jnp.zeros_like(zp), zp)
    win_ref[HALO:HALO + tm, :] = a_ref[...].astype(F32) * _sigmoid(g_ref[...].astype(F32))


def _conv_fwd(proj, wdw, bdw, lng, lnb, n_heads, tm, comm=None):
    S = proj.shape[0]
    taps, C = wdw.shape
    ablk = (n_heads * HEAD_DIM + 2 * KV_DUP) // C
    hb = tm // HALO
    off = HALO - (taps - 1)

    def body(a_ref, g_ref, ap_ref, gp_ref, w_ref, b_ref, lg_ref, lb_ref, o_ref, y_ref, win_ref):
        _glu_window(a_ref, g_ref, ap_ref, gp_ref, win_ref, pl.program_id(0) == 0)
        for c in range(tm // ROWS):
            r0 = c * ROWS
            acc = jnp.zeros((ROWS, C), F32) + b_ref[...]
            for k in range(taps):
                acc = acc + w_ref[k:k + 1, :] * win_ref[r0 + off + k:r0 + off + k + ROWS, :]
            y_ref[r0:r0 + ROWS, :] = acc
        y = y_ref[...]
        mu = jnp.mean(y, axis=-1, keepdims=True)
        yc = y - mu
        yn = yc * lax.rsqrt(jnp.mean(yc * yc, axis=-1, keepdims=True) + RMS_EPS) * lg_ref[...] + lb_ref[...]
        o_ref[...] = (yn * _sigmoid(yn)).astype(BF16)

    vec = pl.BlockSpec((1, C), lambda i: (0, 0))
    halo = lambda b: pl.BlockSpec((HALO, C), lambda i: (jnp.maximum(i * hb - 1, 0), b))
    return _pcall(
        body, (proj, proj, proj, proj, wdw, bdw, lng, lnb,), grid=(S // tm,),
        in_specs=[pl.BlockSpec((tm, C), lambda i: (i, ablk)), pl.BlockSpec((tm, C), lambda i: (i, ablk + 1)),
                  halo(ablk), halo(ablk + 1),
                  pl.BlockSpec((taps, C), lambda i: (0, 0)), vec, vec, vec],
        out_specs=[pl.BlockSpec((tm, C), lambda i: (i, 0)), pl.BlockSpec((tm, C), lambda i: (i, 0))],
        out_shape=[_sds((S, C), BF16), _sds((S, C), F32)],
        scratch_shapes=[pltpu.VMEM((tm + HALO, C), F32)],
        sem=("parallel",), name="conv_fwd", comm=comm)


def _conv_bwd(proj, dmix, ysave, wdw, lng, lnb, n_heads, tm, comm=None):
    S = proj.shape[0]
    taps, C = wdw.shape
    QW = n_heads * HEAD_DIM
    ablk = (QW + 2 * KV_DUP) // C
    cblk = QW // C
    hb = tm // HALO
    nt = S // tm
    off = HALO - (taps - 1)

    def ln_bwd(dc, y, lg, lb):
        mu = jnp.mean(y, axis=-1, keepdims=True)
        yc = y - mu
        r = lax.rsqrt(jnp.mean(yc * yc, axis=-1, keepdims=True) + RMS_EPS)
        yh = yc * r
        yn = yh * lg + lb
        sg = _sigmoid(yn)
        dyn = dc * (sg * (1.0 + yn * (1.0 - sg)))
        dyh = dyn * lg
        dy = r * (dyh - jnp.mean(dyh, axis=-1, keepdims=True) - yh * jnp.mean(dyh * yh, axis=-1, keepdims=True))
        return dy, dyn, yh

    def body(dc_ref, dcn_ref, y_ref, yn_ref, a_ref, g_ref, ap_ref, gp_ref, w_ref, lg_ref, lb_ref,
             dag_ref, dw_ref, dvec_ref, zwin_ref, dyw_ref, dwacc_ref):
        i = pl.program_id(0)

        @pl.when(i == 0)
        def _():
            dwacc_ref[...] = jnp.zeros_like(dwacc_ref)
            dvec_ref[...] = jnp.zeros_like(dvec_ref)

        lg, lb = lg_ref[...], lb_ref[...]
        dy, dyn, yh = ln_bwd(dc_ref[...].astype(F32), y_ref[...], lg, lb)
        dy_next, _, _ = ln_bwd(dcn_ref[...].astype(F32), yn_ref[...], lg, lb)
        dyw_ref[0:tm, :] = dy
        dyw_ref[tm:tm + HALO, :] = jnp.where(i == nt - 1, jnp.zeros_like(dy_next), dy_next)
        dvec_ref[0:1, :] += jnp.sum(dy, axis=0, keepdims=True)
        dvec_ref[1:2, :] += jnp.sum(dyn * yh, axis=0, keepdims=True)
        dvec_ref[2:3, :] += jnp.sum(dyn, axis=0, keepdims=True)
        _glu_window(a_ref, g_ref, ap_ref, gp_ref, zwin_ref, i == 0)

        for c in range(tm // ROWS):
            r0 = c * ROWS
            dz = jnp.zeros((ROWS, C), F32)
            dyc = dyw_ref[r0:r0 + ROWS, :]
            for k in range(taps):
                dz = dz + w_ref[k:k + 1, :] * dyw_ref[r0 + taps - 1 - k:r0 + taps - 1 - k + ROWS, :]
                prod = dyc * zwin_ref[r0 + off + k:r0 + off + k + ROWS, :]
                dwacc_ref[k] += jnp.sum(prod.reshape(ROWS // 8, 8, C), axis=0)
            a = a_ref[r0:r0 + ROWS, :].astype(F32)
            s = _sigmoid(g_ref[r0:r0 + ROWS, :].astype(F32))
            dag_ref[r0:r0 + ROWS, 0:C] = (dz * s).astype(BF16)
            dag_ref[r0:r0 + ROWS, C:2 * C] = (dz * a * s * (1.0 - s)).astype(BF16)

        @pl.when(i == nt - 1)
        def _():
            dw_ref[...] = jnp.zeros_like(dw_ref)
            for k in range(taps):
                dw_ref[k:k + 1, :] = jnp.sum(dwacc_ref[k], axis=0, keepdims=True)

    vec = pl.BlockSpec((1, C), lambda i: (0, 0))
    tile = lambda b: pl.BlockSpec((tm, C), lambda i: (i, b))
    prev = lambda b: pl.BlockSpec((HALO, C), lambda i: (jnp.maximum(i * hb - 1, 0), b))
    nxt = lambda b: pl.BlockSpec((HALO, C), lambda i: (jnp.minimum((i + 1) * hb, S // HALO - 1), b))
    return _pcall(
        body, (dmix, dmix, ysave, ysave, proj, proj, proj, proj, wdw, lng, lnb,), grid=(nt,),
        in_specs=[tile(cblk), nxt(cblk), tile(0), nxt(0), tile(ablk), tile(ablk + 1), prev(ablk), prev(ablk + 1),
                  pl.BlockSpec((taps, C), lambda i: (0, 0)), vec, vec],
        out_specs=[pl.BlockSpec((tm, 2 * C), lambda i: (i, 0)),
                   pl.BlockSpec((HALO, C), lambda i: (0, 0)),
                   pl.BlockSpec((8, C), lambda i: (0, 0))],
        out_shape=[_sds((S, 2 * C), BF16), _sds((HALO, C), F32), _sds((8, C), F32)],
        scratch_shapes=[pltpu.VMEM((tm + HALO, C), F32), pltpu.VMEM((tm + HALO, C), F32),
                        pltpu.VMEM((taps, 8, C), F32)],
        sem=("arbitrary",), name="conv_bwd", comm=comm)


def _mixout_fwd(x, attn, conv, wo, tm, comm=None):
    S, D = x.shape
    QW, C = attn.shape[1], conv.shape[1]

    def body(x_ref, a_ref, c_ref, w_ref, o_ref):
        o_ref[...] = x_ref[...] + _dot(a_ref[...], w_ref[0:QW, :]) + _dot(c_ref[...], w_ref[QW:QW + C, :])

    return _pcall(
        body, (x, attn, conv, wo,), grid=(S // tm,),
        in_specs=[pl.BlockSpec((tm, D), lambda i: (i, 0)),
                  pl.BlockSpec((tm, QW), lambda i: (i, 0)),
                  pl.BlockSpec((tm, C), lambda i: (i, 0)),
                  pl.BlockSpec((QW + C, D), lambda i: (0, 0))],
        out_specs=[pl.BlockSpec((tm, D), lambda i: (i, 0))],
        out_shape=[_sds((S, D), F32)],
        sem=("parallel",), name="mixout_fwd", comm=comm)


def _mixout_bwd(dxo, attn, conv, wo, tm, comm=None):
    S, D = dxo.shape
    QW, C = attn.shape[1], conv.shape[1]
    nt = S // tm

    def body(dx_ref, a_ref, c_ref, w_ref, dm_ref, dw_ref, acc_ref):
        i = pl.program_id(0)
        dxb = dx_ref[...].astype(BF16)
        dm_ref[...] = _dot_nt(dxb, w_ref[...]).astype(BF16)

        @pl.when(i == 0)
        def _():
            acc_ref[...] = jnp.zeros_like(acc_ref)

        acc_ref[0:QW, :] += _dot_tn(a_ref[...], dxb)
        acc_ref[QW:QW + C, :] += _dot_tn(c_ref[...], dxb)

        @pl.when(i == nt - 1)
        def _():
            dw_ref[...] = acc_ref[...].astype(BF16)

    return _pcall(
        body, (dxo, attn, conv, wo,), grid=(nt,),
        in_specs=[pl.BlockSpec((tm, D), lambda i: (i, 0)),
                  pl.BlockSpec((tm, QW), lambda i: (i, 0)),
                  pl.BlockSpec((tm, C), lambda i: (i, 0)),
                  pl.BlockSpec((QW + C, D), lambda i: (0, 0))],
        out_specs=[pl.BlockSpec((tm, QW + C), lambda i: (i, 0)),
                   pl.BlockSpec((QW + C, D), lambda i: (0, 0))],
        out_shape=[_sds((S, QW + C), BF16), _sds((QW + C, D), BF16)],
        scratch_shapes=[pltpu.VMEM((QW + C, D), F32)],
        sem=("arbitrary",), name="mixout_bwd", comm=comm)


def _loss_head(x, gain, target, tm, comm=None):
    S, D = x.shape
    nt = S // tm

    def body(x_ref, g_ref, t_ref, dx_ref, loss_ref, dgain_ref):
        i = pl.program_id(0)
        xh, r = _rms(x_ref[...])
        e = xh * g_ref[...] - t_ref[...]
        loss_ref[...] = jnp.zeros((1, LANES), F32) + 0.5 * jnp.sum(jnp.mean(e * e, axis=-1, keepdims=True))
        dxn, dgn = _rms_bwd(e * (1.0 / D), xh, r, g_ref[...])
        dx_ref[...] = dxn

        @pl.when(i == 0)
        def _():
            dgain_ref[...] = dgn

        @pl.when(i > 0)
        def _():
            dgain_ref[...] += dgn

    return _pcall(
        body, (x, gain, target,), grid=(nt,),
        in_specs=[pl.BlockSpec((tm, D), lambda i: (i, 0)),
                  pl.BlockSpec((1, D), lambda i: (0, 0)),
                  pl.BlockSpec((tm, D), lambda i: (i, 0))],
        out_specs=[pl.BlockSpec((tm, D), lambda i: (i, 0)),
                   pl.BlockSpec((None, 1, LANES), lambda i: (i, 0, 0)),
                   pl.BlockSpec((1, D), lambda i: (0, 0))],
        out_shape=[_sds((S, D), F32), _sds((nt, 1, LANES), F32), _sds((1, D), F32)],
        sem=("arbitrary",), name="loss_head", comm=comm)


def _adam(w, m, v, parts, name):
    R, C = w.shape
    P = parts.shape[0]
    br = R if R <= 512 else 256
    c1 = 1.0 - ADAM_B1 ** ADAM_STEP
    c2 = 1.0 - ADAM_B2 ** ADAM_STEP

    def body(w_ref, m_ref, v_ref, p_ref, g_ref, d_ref, mo_ref, vo_ref):
        g = p_ref[0].astype(F32)
        for k in range(1, P):
            g = g + p_ref[k].astype(F32)
        mn = ADAM_B1 * m_ref[...] + (1.0 - ADAM_B1) * g
        vn = ADAM_B2 * v_ref[...] + (1.0 - ADAM_B2) * (g * g)
        g_ref[...] = g
        mo_ref[...] = mn
        vo_ref[...] = vn
        d_ref[...] = -ADAM_LR * ((mn / c1) / (jnp.sqrt(vn / c2) + ADAM_EPS) + ADAM_WD * w_ref[...])

    blk = pl.BlockSpec((br, C), lambda i: (i, 0))
    return _pcall(
        body, (w, m, v, parts), grid=(R // br,),
        in_specs=[blk, blk, blk, pl.BlockSpec((P, br, C), lambda i: (0, i, 0))],
        out_specs=[blk, blk, blk, blk],
        out_shape=[_sds((R, C), F32)] * 4,
        sem=("parallel",), name=name)


def _sum_parts(parts):
    P, R, C = parts.shape

    def body(p_ref, o_ref):
        g = p_ref[0]
        for k in range(1, P):
            g = g + p_ref[k]
        o_ref[...] = g

    vmem = pl.BlockSpec(memory_space=pltpu.VMEM)
    return _pcall(body, (parts,), in_specs=[vmem], out_specs=[vmem], out_shape=[_sds((R, C), F32)],
                  name="sum_parts")[0]


def _place():
    x, y, c = lax.axis_index("x"), lax.axis_index("y"), lax.axis_index("c")
    return x, y, c, [(1 - x, y), (x, 1 - y), (1 - x, 1 - y)]


def _dev(px, py, pc):
    return 4 * px + 2 * py + pc


def _gather_comm(shards, fulls, slot_of):
    n = len(shards)

    def copies(srcs, outs, send_sems, recv_sems):
        x, y, c, chips = _place()

        def copy(a, k, block, to, from_shard=False):
            dst = slot_of[a](outs[a], _dev(*block))
            return pltpu.make_async_remote_copy(
                src_ref=srcs[a] if from_shard else dst, dst_ref=dst,
                send_sem=send_sems.at[a, k], recv_sem=recv_sems.at[a, k], device_id=to, device_id_type=MESH)

        return copy, (x, y, c), (x, y, 1 - c), chips

    def local(srcs, outs, local_sems):
        x, y, c, _ = _place()
        return [pltpu.make_async_copy(srcs[a], slot_of[a](outs[a], _dev(x, y, c)), local_sems.at[a])
                for a in range(n)]

    def first_copies(copy, me, sibling, chips):
        out = []
        for a in range(n):
            out.append(copy(a, 0, me, sibling, True))
            out += [copy(a, 1 + j, me, (*chip, me[2]), True) for j, chip in enumerate(chips)]
        return out

    def start(srcs, outs, sems):
        send_sems, recv_sems, local_sems = sems
        copy, me, sibling, chips = copies(srcs, outs, send_sems, recv_sems)
        for cp in local(srcs, outs, local_sems):
            cp.start()
        for cp in first_copies(copy, me, sibling, chips):
            cp.start()

    def finish(srcs, outs, sems):
        send_sems, recv_sems, local_sems = sems
        copy, me, sibling, chips = copies(srcs, outs, send_sems, recv_sems)
        c = me[2]
        passed = []
        for j, chip in enumerate(chips):
            for a in range(n):
                copy(a, 1 + j, (*chip, c), me).wait_recv()
                fwd = copy(a, 4 + j, (*chip, c), sibling)
                fwd.start()
                passed.append(fwd)
        for a in range(n):
            copy(a, 0, sibling, me).wait_recv()
            for j, chip in enumerate(chips):
                copy(a, 4 + j, (*chip, 1 - c), me).wait_recv()
        for cp in first_copies(copy, me, sibling, chips) + passed:
            cp.wait_send()
        for cp in local(srcs, outs, local_sems):
            cp.wait()

    sems = [pltpu.SemaphoreType.DMA((n, 7)), pltpu.SemaphoreType.DMA((n, 7)), pltpu.SemaphoreType.DMA((n,))]
    return _Comm(shards, fulls, sems, start, finish)


def _swap_comm(grads):
    n = len(grads)

    def copies(srcs, outs, sems):
        x, y, c, _ = _place()
        return [pltpu.make_async_remote_copy(
            src_ref=srcs[a].at[:, pl.ds(1 - c, 1)], dst_ref=outs[a],
            send_sem=sems[0].at[a], recv_sem=sems[1].at[a], device_id=(x, y, 1 - c), device_id_type=MESH)
            for a in range(n)]

    def start(srcs, outs, sems):
        for cp in copies(srcs, outs, sems):
            cp.start()

    def finish(srcs, outs, sems):
        for cp in copies(srcs, outs, sems):
            cp.wait()

    return _Comm(grads, [_sds((N_CHIP, 1) + g.shape[2:], g.dtype) for g in grads],
                 [pltpu.SemaphoreType.DMA((n,)), pltpu.SemaphoreType.DMA((n,))], start, finish)


def _exchange_comm(parts):
    n = len(parts)

    def copies(srcs, outs, sems):
        x, y, c, chips = _place()
        mine = 2 * x + y
        loc = [pltpu.make_async_copy(srcs[a].at[pl.ds(mine, 1)], outs[a].at[pl.ds(mine, 1)], sems[2].at[a])
               for a in range(n)]
        rem = [pltpu.make_async_remote_copy(
            src_ref=srcs[a].at[pl.ds(2 * px + py, 1)], dst_ref=outs[a].at[pl.ds(mine, 1)],
            send_sem=sems[0].at[a, j], recv_sem=sems[1].at[a, j], device_id=(px, py, c), device_id_type=MESH)
            for a in range(n) for j, (px, py) in enumerate(chips)]
        return loc + rem

    def start(srcs, outs, sems):
        for cp in copies(srcs, outs, sems):
            cp.start()

    def finish(srcs, outs, sems):
        for cp in copies(srcs, outs, sems):
            cp.wait()

    return _Comm(parts, [_sds(p.shape, p.dtype) for p in parts],
                 [pltpu.SemaphoreType.DMA((n, 3)), pltpu.SemaphoreType.DMA((n, 3)), pltpu.SemaphoreType.DMA((n,))],
                 start, finish)


def _comm_only(comm, name):
    return _pcall(lambda: None, (), in_specs=[], out_specs=[], out_shape=[], name=name, comm=comm)


def _gather_small(v):
    R, C = v.shape

    def body(x_ref, out_ref, send_sems, recv_sems, local_sem):
        x, y, c, chips = _place()
        me, sibling = (x, y, c), (x, y, 1 - c)

        def copy(k, block, to, from_shard=False):
            dst = out_ref.at[_dev(*block)]
            return pltpu.make_async_remote_copy(
                src_ref=x_ref if from_shard else dst, dst_ref=dst,
                send_sem=send_sems.at[k], recv_sem=recv_sems.at[k], device_id=to, device_id_type=MESH)

        mine = pltpu.make_async_copy(x_ref, out_ref.at[_dev(*me)], local_sem)
        mine.start()
        first = [copy(0, me, sibling, True)] + [copy(1 + j, me, (*chip, c), True) for j, chip in enumerate(chips)]
        for cp in first:
            cp.start()
        passed = [copy(4 + j, (*chip, c), sibling) for j, chip in enumerate(chips)]
        for j, chip in enumerate(chips):
            copy(1 + j, (*chip, c), me).wait_recv()
            passed[j].start()
        copy(0, sibling, me).wait_recv()
        for j, chip in enumerate(chips):
            copy(4 + j, (*chip, 1 - c), me).wait_recv()
        for cp in first + passed:
            cp.wait_send()
        mine.wait()

    vmem = pl.BlockSpec(memory_space=pltpu.VMEM)
    return _pcall(
        body, (v,), in_specs=[vmem], out_specs=[vmem], out_shape=[_sds((N_DEV, R, C), F32)],
        scratch_shapes=[pltpu.SemaphoreType.DMA((7,)), pltpu.SemaphoreType.DMA((7,)), pltpu.SemaphoreType.DMA],
        name="gather_small")[0]


def _add_sibling(core, g, r):
    _, _, R, C = g.shape
    br = R if R <= 512 else 256

    def body(c_ref, g_ref, r_ref, o_ref):
        o_ref[...] = (g_ref[...].astype(F32) + r_ref[...].astype(F32)).astype(BF16)

    return _pcall(
        body, (core, g, r),
        grid_spec=pltpu.PrefetchScalarGridSpec(
            num_scalar_prefetch=1, grid=(N_CHIP, R // br),
            in_specs=[pl.BlockSpec((None, None, br, C), lambda k, i, c_ref: (k, c_ref[0], i, 0)),
                      pl.BlockSpec((None, None, br, C), lambda k, i, c_ref: (k, 0, i, 0))],
            out_specs=pl.BlockSpec((None, br, C), lambda k, i, c_ref: (k, i, 0))),
        out_shape=_sds((N_CHIP, R, C), BF16), sem=("parallel", "parallel"), name="rs_add_sibling")


def _by_chip(g):
    return g.reshape((N_CHIP, 2) + g.shape[1:])


def _pack_rows(vecs):
    flat = jnp.concatenate([v.reshape(-1).astype(F32) for v in vecs])
    rows = -(-flat.shape[0] // (8 * LANES)) * 8
    return jnp.pad(flat, (0, rows * LANES - flat.shape[0])).reshape(rows, LANES)


def _unpack_rows(rows, shapes):
    flat = rows.reshape(-1)
    out, o = [], 0
    for s in shapes:
        n = math.prod(s)
        out.append(flat[o:o + n].reshape(s))
        o += n
    return out


def _adam_any(w, m, v, g, name):
    shape = w.shape
    cols = shape[-1]
    two = lambda t: t.reshape(-1, cols)
    return tuple(t.reshape(shape) for t in _adam(two(w), two(m), two(v), two(g)[None], name))


def kernel(x, norm_ffn1, w_ffn1_in, w_ffn1_out, norm_mix, w_in, sinks, w_dw, b_dw, conv_ln_g, conv_ln_b, w_out, norm_ffn2, w_ffn2_in, w_ffn2_out, final_norm, loss_target, m_norm_ffn1, m_w_ffn1_in, m_w_ffn1_out, m_norm_mix, m_w_in, m_sinks, m_w_dw, m_b_dw, m_conv_ln_g, m_conv_ln_b, m_w_out, m_norm_ffn2, m_w_ffn2_in, m_w_ffn2_out, m_final_norm, v_norm_ffn1, v_w_ffn1_in, v_w_ffn1_out, v_norm_mix, v_w_in, v_sinks, v_w_dw, v_b_dw, v_conv_ln_g, v_conv_ln_b, v_w_out, v_norm_ffn2, v_w_ffn2_in, v_w_ffn2_out, v_final_norm):
    _, S, D = x.shape
    L = norm_ffn1.shape[0]
    NF = w_ffn1_in.shape[2]
    RF = w_ffn1_out.shape[1]
    NW = w_in.shape[2]
    RO = w_out.shape[1]
    taps, CD = w_dw.shape[1], w_dw.shape[2]
    H = sinks.shape[1]
    C = N_DEV * CD
    QW = H * HEAD_DIM
    KVW = N_KV_HEADS * HEAD_DIM
    NJ = N_DEV // 2
    assert 2 * RF == NF and QW + C == N_DEV * RO and N_DEV * NW == QW + 2 * KVW + 2 * C
    tm = min(512, S)
    tc = min(256, S)
    core = lax.axis_index("c").astype(jnp.int32).reshape(1)

    x0 = x[0]
    target = loss_target[0]

    wdw_all = _gather_small(_pack_rows([w_dw]))
    n_dw = L * taps * CD
    wdw_full = jnp.stack([wdw_all[d].reshape(-1)[:n_dw].reshape(L, taps, CD) for d in range(N_DEV)],
                         axis=2).reshape(L, taps, C)

    names = ["a1", "wi", "b1", "b2", "wo", "a2"]

    def shard(name, l):
        return {"a1": lambda: w_ffn1_in[l].astype(BF16)[None], "a2": lambda: w_ffn2_in[l].astype(BF16)[None],
                "b1": lambda: w_ffn1_out[l].astype(BF16), "b2": lambda: w_ffn2_out[l].astype(BF16),
                "wi": lambda: w_in[l].astype(BF16)[None], "wo": lambda: w_out[l].astype(BF16)}[name]()

    by_block = lambda r, b: r.at[pl.ds(b, 1)]
    full_of = {"a1": _sds((N_DEV, D, NF), BF16), "a2": _sds((N_DEV, D, NF), BF16),
               "b1": _sds((N_DEV * RF, D), BF16), "b2": _sds((N_DEV * RF, D), BF16),
               "wi": _sds((N_DEV, D, NW), BF16), "wo": _sds((N_DEV * RO, D), BF16)}
    slot_of = {"a1": by_block, "a2": by_block, "wi": by_block,
               "b1": lambda r, b: r.at[pl.ds(b * RF, RF)], "b2": lambda r, b: r.at[pl.ds(b * RF, RF)],
               "wo": lambda r, b: r.at[pl.ds(b * RO, RO)]}

    def gather(name_list, l):
        if l >= L:
            return None
        return _gather_comm([shard(n, l) for n in name_list], [full_of[n] for n in name_list],
                            [slot_of[n] for n in name_list])

    def weights(g):
        wi = g["wi"].transpose(1, 0, 2).reshape(D, N_DEV * NW)
        q, k, v, u = wi[:, :QW], wi[:, QW:QW + KVW], wi[:, QW + KVW:QW + 2 * KVW], wi[:, QW + 2 * KVW:]
        dup = lambda t: jnp.concatenate(
            [t[:, HEAD_DIM * (i // 2):HEAD_DIM * (i // 2 + 1)] for i in range(2 * N_KV_HEADS)], axis=1)
        return dict(a1=g["a1"].reshape(2, NJ, D, NF), a2=g["a2"].reshape(2, NJ, D, NF), b1=g["b1"], b2=g["b2"],
                    wext=jnp.concatenate([q, dup(k), dup(v), u], axis=1), wo=g["wo"])

    got = dict(zip(names, _comm_only(gather(names, 0), "ag_layer0")))

    saved = []
    xc = x0
    for l in range(L):
        W = weights(got)
        nxt = {}

        def take(res, n_own, name):
            if l + 1 < L:
                nxt[name] = res[n_own]
            return res[:n_own]

        g1, gm, g2 = norm_ffn1[l][None], norm_mix[l][None], norm_ffn2[l][None]
        x1, gu1 = take(_ffn_fwd(xc, g1, W["a1"], W["b1"], tm, gather(["a1"], l + 1)), 2, "a1")
        proj, = take(_mixin_fwd(x1, gm, W["wext"], tm, gather(["wi"], l + 1)), 1, "wi")
        attn, = take(_attn_fwd(proj, sinks[l], H, gather(["b1"], l + 1)), 1, "b1")
        conv, ysave = take(_conv_fwd(proj, wdw_full[l], b_dw[l][None], conv_ln_g[l][None], conv_ln_b[l][None], H, tc,
                                     gather(["b2"], l + 1)), 2, "b2")
        x2, = take(_mixout_fwd(x1, attn, conv, W["wo"], tm, gather(["wo"], l + 1)), 1, "wo")
        x3, gu2 = take(_ffn_fwd(x2, g2, W["a2"], W["b2"], tm, gather(["a2"], l + 1)), 2, "a2")
        saved.append(dict(W=W, x0=xc, x1=x1, x2=x2, gu1=gu1, gu2=gu2, proj=proj, attn=attn, conv=conv, ysave=ysave))
        xc = x3
        got = nxt

    dx, loss_parts, dfinal = _loss_head(xc, final_norm[None], target, tm)
    loss = lax.psum(jnp.sum(loss_parts[:, 0, 0]), ("x", "y", "c"))

    def swap(gs):
        return _swap_comm([_by_chip(g) for g in gs]) if gs else None

    def added(gs, gots):
        return [_add_sibling(core, _by_chip(g), r) for g, r in zip(gs, gots)]

    small = [None] * L
    big = [dict() for _ in range(L)]
    carry = None
    for l in reversed(range(L)):
        sv = saved[l]
        W = sv["W"]
        g1, gm, g2 = norm_ffn1[l][None], norm_mix[l][None], norm_ffn2[l][None]
        fold = lambda t: jnp.concatenate(
            [t[:, 2 * HEAD_DIM * i:2 * HEAD_DIM * i + HEAD_DIM] + t[:, 2 * HEAD_DIM * i + HEAD_DIM:2 * HEAD_DIM * (i + 1)]
             for i in range(N_KV_HEADS)], axis=1)

        r = _ffn_bwd_a(sv["x2"], g2, dx, sv["gu2"], W["a2"], W["b2"], tm, swap(carry[1]) if carry else None)
        dx2, dgu2, dg2 = r[:3]
        p_carry = added(carry[1], r[3:]) if carry else None
        r = _ffn_bwd_w(sv["x2"], g2, dx, sv["gu2"], dgu2, tm, _exchange_comm(p_carry) if carry else None)
        da2, db2 = r[0].reshape(N_DEV, D, NF), r[1].reshape(N_DEV, RF, D)
        if carry:
            big[carry[0]]["a1"], big[carry[0]]["b1"] = r[2:]
        r = _mixout_bwd(dx2, sv["attn"], sv["conv"], W["wo"], tm, swap([da2, db2]))
        dmix, dwo = r[0], r[1].reshape(N_DEV, RO, D)
        p_a2, p_b2 = added([da2, db2], r[2:])
        r = _conv_bwd(sv["proj"], dmix, sv["ysave"], wdw_full[l], conv_ln_g[l][None], conv_ln_b[l][None], H, tc,
                      _join(_exchange_comm([p_b2]), swap([dwo])))
        dag, dwdw, dvec = r[:3]
        big[l]["b2"] = r[3]
        p_wo, = added([dwo], r[4:])
        r = _attn_bwd(sv["proj"], dmix, sinks[l], H, _exchange_comm([p_a2]))
        dq, dkv, dsink = r[:3]
        big[l]["a2"] = r[3]
        r = _mixin_bwd(sv["x1"], gm, dx2, dq, dkv, dag, W["wext"], tm, _exchange_comm([p_wo]))
        dx1, dgm, dwext = r[:3]
        big[l]["wo"] = r[3]
        dwi = jnp.concatenate([dwext[:, :QW], fold(dwext[:, QW:QW + KV_DUP]),
                               fold(dwext[:, QW + KV_DUP:QW + 2 * KV_DUP]), dwext[:, QW + 2 * KV_DUP:]], axis=1)
        dwi = dwi.reshape(D, N_DEV, NW).transpose(1, 0, 2).astype(BF16)
        r = _ffn_bwd_a(sv["x0"], g1, dx1, sv["gu1"], W["a1"], W["b1"], tm, swap([dwi]))
        dx0, dgu1, dg1 = r[:3]
        p_wi, = added([dwi], r[3:])
        r = _ffn_bwd_w(sv["x0"], g1, dx1, sv["gu1"], dgu1, tm, _exchange_comm([p_wi]))
        carry = (l, [r[0].reshape(N_DEV, D, NF), r[1].reshape(N_DEV, RF, D)])
        big[l]["wi"] = r[2]
        dx = dx0
        small[l] = [dg1[0], dgm[0], dsink[0, :H], dwdw[:taps], dvec[0], dvec[1], dvec[2], dg2[0]]

    p_carry = added(carry[1], _comm_only(swap(carry[1]), "rs_swap_last"))
    big[carry[0]]["a1"], big[carry[0]]["b1"] = _comm_only(_exchange_comm(p_carry), "rs_exchange_last")
    grad_x = dx[None]

    small_shapes = [(D,), (D,), (H,), (taps, C), (C,), (C,), (C,), (D,)]
    packed = _pack_rows([t for l in range(L) for t in small[l]] + [dfinal[0]])
    total = _sum_parts(_gather_small(packed))
    flat = _unpack_rows(total, small_shapes * L + [(D,)])
    per = [jnp.stack([flat[l * len(small_shapes) + i] for l in range(L)]) for i in range(len(small_shapes))]
    g_nf1, g_nmix, g_sinks, g_wdw_full, g_bdw, g_lng, g_lnb, g_nf2 = per
    g_final = flat[-1]
    dev = _dev(lax.axis_index("x"), lax.axis_index("y"), lax.axis_index("c"))
    g_wdw = lax.dynamic_slice_in_dim(g_wdw_full, dev * CD, CD, axis=2)

    res = {}
    res["norm_ffn1"] = _adam_any(norm_ffn1, m_norm_ffn1, v_norm_ffn1, g_nf1, "adam_small")
    res["norm_mix"] = _adam_any(norm_mix, m_norm_mix, v_norm_mix, g_nmix, "adam_small")
    res["sinks"] = _adam_any(sinks, m_sinks, v_sinks, g_sinks, "adam_small")
    res["w_dw"] = _adam_any(w_dw, m_w_dw, v_w_dw, g_wdw, "adam_small")
    res["b_dw"] = _adam_any(b_dw, m_b_dw, v_b_dw, g_bdw, "adam_small")
    res["conv_ln_g"] = _adam_any(conv_ln_g, m_conv_ln_g, v_conv_ln_g, g_lng, "adam_small")
    res["conv_ln_b"] = _adam_any(conv_ln_b, m_conv_ln_b, v_conv_ln_b, g_lnb, "adam_small")
    res["norm_ffn2"] = _adam_any(norm_ffn2, m_norm_ffn2, v_norm_ffn2, g_nf2, "adam_small")
    res["final_norm"] = tuple(t[0] for t in _adam_any(final_norm[None], m_final_norm[None], v_final_norm[None],
                                                      g_final[None], "adam_small"))

    def adam_big(key, w, m, v, name):
        outs = [_adam(w[l], m[l], v[l], big[l][key], name) for l in range(L)]
        return tuple(jnp.stack([o[t] for o in outs]) for t in range(4))

    res["w_ffn1_in"] = adam_big("a1", w_ffn1_in, m_w_ffn1_in, v_w_ffn1_in, "adam_ffn_in")
    res["w_ffn1_out"] = adam_big("b1", w_ffn1_out, m_w_ffn1_out, v_w_ffn1_out, "adam_ffn_out")
    res["w_ffn2_in"] = adam_big("a2", w_ffn2_in, m_w_ffn2_in, v_w_ffn2_in, "adam_ffn_in")
    res["w_ffn2_out"] = adam_big("b2", w_ffn2_out, m_w_ffn2_out, v_w_ffn2_out, "adam_ffn_out")
    res["w_in"] = adam_big("wi", w_in, m_w_in, v_w_in, "adam_w_in")
    res["w_out"] = adam_big("wo", w_out, m_w_out, v_w_out, "adam_w_out")

    order = ["norm_ffn1", "w_ffn1_in", "w_ffn1_out", "norm_mix", "w_in", "sinks", "w_dw", "b_dw", "conv_ln_g",
             "conv_ln_b", "w_out", "norm_ffn2", "w_ffn2_in", "w_ffn2_out", "final_norm"]
    return (loss, grad_x, *[res[n][0] for n in order], *[res[n][1] for n in order],
            *[res[n][2] for n in order], *[res[n][3] for n in order])
```

```python
import functools
import math

import jax
import jax.numpy as jnp
from jax import lax
from jax.experimental import pallas as pl
from jax.experimental.pallas import tpu as pltpu

F32 = jnp.float32
BF16 = jnp.bfloat16
MESH = pl.DeviceIdType.MESH

N_DEV = 8
N_CHIP = 4
HEAD_DIM = 64
N_KV_HEADS = 2
WINDOW = 128
KV_DUP = 2 * HEAD_DIM * N_KV_HEADS
RMS_EPS = 1e-6
NEG_INF = -1e30
FFN_RES = 0.5
HALO = 32
ROWS = 32
FFN_CHUNK = 512
BWD_W_CHUNK = 384
LANES = 128
V7X_VMEM_LIMIT = 56 * 1024 * 1024

ADAM_LR = 0.001
ADAM_B1 = 0.9
ADAM_B2 = 0.999
ADAM_EPS = 1e-08
ADAM_WD = 0.01
ADAM_STEP = 10


def _raw_call(body, **kw):
    return pl.pallas_call(body, **kw)


class _Comm:
    def __init__(self, ins, outs, sems, start, finish):
        self.ins, self.outs, self.sems, self.start, self.finish = list(ins), list(outs), list(sems), start, finish


def _join(*comms):
    comms = [c for c in comms if c is not None]
    if not comms:
        return None

    def split(refs, attr):
        out, o = [], 0
        for c in comms:
            n = len(getattr(c, attr))
            out.append(refs[o:o + n])
            o += n
        return out

    def run(which):
        def go(ins, outs, sems):
            for c, i, o, m in zip(comms, split(ins, "ins"), split(outs, "outs"), split(sems, "sems")):
                getattr(c, which)(i, o, m)
        return go

    return _Comm(sum((c.ins for c in comms), []), sum((c.outs for c in comms), []),
                 sum((c.sems for c in comms), []), run("start"), run("finish"))


def _pcall(body, args, *, name, out_shape, grid=(), in_specs=None, out_specs=None, scratch_shapes=(), sem=(),
           comm=None, grid_spec=None):
    if grid_spec is not None:
        return _raw_call(body, grid_spec=grid_spec, out_shape=out_shape, name=name,
                         compiler_params=_params(*sem))(*args)
    if comm is None:
        return _raw_call(body, grid=grid, in_specs=in_specs, out_specs=out_specs, out_shape=out_shape,
                         scratch_shapes=list(scratch_shapes), name=name, compiler_params=_params(*sem))(*args)
    n_in, n_out, n_scr = len(in_specs), len(out_shape), len(scratch_shapes)
    ci, co = len(comm.ins), len(comm.outs)

    def fused(*refs):
        cuts = [n_in, ci, n_out, co, n_scr]
        parts, o = [], 0
        for n in cuts:
            parts.append(refs[o:o + n])
            o += n
        ins, cins, outs, couts, scr = parts
        csems = refs[o:]
        if not grid:
            comm.start(cins, couts, csems)
            body(*ins, *outs, *scr)
            comm.finish(cins, couts, csems)
            return
        ids = [pl.program_id(a) for a in range(len(grid))]
        first = functools.reduce(jnp.logical_and, [i == 0 for i in ids])
        last = functools.reduce(jnp.logical_and, [i == g - 1 for i, g in zip(ids, grid)])

        @pl.when(first)
        def _():
            comm.start(cins, couts, csems)

        body(*ins, *outs, *scr)

        @pl.when(last)
        def _():
            comm.finish(cins, couts, csems)

    return _raw_call(
        fused, grid=grid, in_specs=list(in_specs) + [ANY] * ci, out_specs=list(out_specs) + [ANY] * co,
        out_shape=list(out_shape) + comm.outs, scratch_shapes=list(scratch_shapes) + comm.sems, name=name,
        compiler_params=_params(*(["arbitrary"] * len(grid))))(*args, *comm.ins)


ANY = pl.BlockSpec(memory_space=pl.ANY)


def _params(*sem):
    return pltpu.CompilerParams(dimension_semantics=sem, vmem_limit_bytes=V7X_VMEM_LIMIT)


def _dot(a, b):
    return jnp.dot(a, b, preferred_element_type=F32)


def _dot_nt(a, b):
    return lax.dot_general(a, b, (((1,), (1,)), ((), ())), preferred_element_type=F32)


def _dot_tn(a, b):
    return lax.dot_general(a, b, (((0,), (0,)), ((), ())), preferred_element_type=F32)


def _sigmoid(x):
    return 1.0 / (1.0 + jnp.exp(-x))


def _rms(x):
    r = lax.rsqrt(jnp.mean(x * x, axis=-1, keepdims=True) + RMS_EPS)
    return x * r, r


def _rms_bwd(dh, xh, r, g):
    dxh = dh * g
    dx = r * (dxh - xh * jnp.mean(dxh * xh, axis=-1, keepdims=True))
    return dx, jnp.sum(dh * xh, axis=0, keepdims=True)


def _sds(shape, dtype):
    return jax.ShapeDtypeStruct(shape, dtype)


def _row_block(rows, limit=512):
    fits = [d for d in range(16, min(rows, limit) + 1, 16) if rows % d == 0]
    return fits[-1] if fits else rows


def _chunks(n, step):
    return [(o, min(step, n - o)) for o in range(0, n, step)]


def _resident(shape):
    return pl.BlockSpec(shape, lambda *_: (0,) * len(shape), pipeline_mode=pl.Buffered(1))


def _ffn_fwd(x, gain, wint, wout, tm, comm=None):
    S, D = x.shape
    F = wout.shape[0]

    def body(x_ref, g_ref, w_ref, wo_ref, xo_ref, gu_ref, a_ref):
        xh, _ = _rms(x_ref[...])
        h = (xh * g_ref[...]).astype(BF16)
        for o, n in _chunks(F, FFN_CHUNK):
            gb = _dot_nt(h, w_ref[o:o + n, :]).astype(BF16)
            ub = _dot_nt(h, w_ref[F + o:F + o + n, :]).astype(BF16)
            gu_ref[:, o:o + n] = gb
            gu_ref[:, F + o:F + o + n] = ub
            g = gb.astype(F32)
            a_ref[:, o:o + n] = (g * _sigmoid(g) * ub.astype(F32)).astype(BF16)
        xo_ref[...] = x_ref[...] + FFN_RES * _dot(a_ref[...], wo_ref[...])

    return _pcall(
        body, (x, gain, wint, wout), grid=(S // tm,),
        in_specs=[pl.BlockSpec((tm, D), lambda i: (i, 0)), _resident((1, D)),
                  _resident((2 * F, D)), _resident((F, D))],
        out_specs=[pl.BlockSpec((tm, D), lambda i: (i, 0)), pl.BlockSpec((tm, 2 * F), lambda i: (i, 0))],
        out_shape=[_sds((S, D), F32), _sds((S, 2 * F), BF16)],
        scratch_shapes=[pltpu.VMEM((tm, F), BF16)],
        sem=("parallel",), name="ffn_fwd", comm=comm)


def _ffn_bwd_a(x, gain, dxo, gu, wint, wout, tm, comm=None):
    S, D = x.shape
    F = wout.shape[0]

    def body(x_ref, g_ref, dxo_ref, gu_ref, w_ref, wo_ref, dx_ref, dgu_ref, dgain_ref, h_ref, dys_ref):
        i = pl.program_id(0)
        dys = (FFN_RES * dxo_ref[...]).astype(BF16)
        dys_ref[...] = dys
        for o, n in _chunks(F, FFN_CHUNK):
            dact = _dot_nt(dys, wo_ref[o:o + n, :])
            g = gu_ref[:, o:o + n].astype(F32)
            u = gu_ref[:, F + o:F + o + n].astype(F32)
            s = _sigmoid(g)
            dgu_ref[:, o:o + n] = (dact * u * (s * (1.0 + g * (1.0 - s)))).astype(BF16)
            dgu_ref[:, F + o:F + o + n] = (dact * (g * s)).astype(BF16)
        dh = _dot(dgu_ref[...], w_ref[...])
        xh, r = _rms(x_ref[...])
        h_ref[...] = (xh * g_ref[...]).astype(BF16)
        dxn, dgn = _rms_bwd(dh, xh, r, g_ref[...])
        dx_ref[...] = dxo_ref[...] + dxn

        @pl.when(i == 0)
        def _():
            dgain_ref[...] = dgn

        @pl.when(i > 0)
        def _():
            dgain_ref[...] += dgn

    tile = pl.BlockSpec((tm, D), lambda i: (i, 0))
    wide = pl.BlockSpec((tm, 2 * F), lambda i: (i, 0))
    return _pcall(
        body, (x, gain, dxo, gu, wint, wout), grid=(S // tm,),
        in_specs=[tile, _resident((1, D)), tile, wide, _resident((2 * F, D)), _resident((F, D))],
        out_specs=[tile, wide, pl.BlockSpec((1, D), lambda i: (0, 0)), tile, tile],
        out_shape=[_sds((S, D), F32), _sds((S, 2 * F), BF16), _sds((1, D), F32), _sds((S, D), BF16), _sds((S, D), BF16)],
        sem=("arbitrary",), name="ffn_bwd_a", comm=comm)


def _ffn_bwd_w(h, dys, gu, dgu, tk, comm=None):
    S, D = h.shape
    F = gu.shape[1] // 2
    FH = F // 2
    nk = S // tk

    def body(h_ref, dys_ref, gg_ref, gu_ref, dg_ref, du_ref, dw_ref, dwo_ref, accw_ref, acco_ref):
        k = pl.program_id(1)

        @pl.when(k == 0)
        def _():
            accw_ref[...] = jnp.zeros_like(accw_ref)
            acco_ref[...] = jnp.zeros_like(acco_ref)

        hv, dys = h_ref[...], dys_ref[...]
        for o, n in _chunks(FH, BWD_W_CHUNK):
            g = gg_ref[:, o:o + n].astype(F32)
            act = (g * _sigmoid(g) * gu_ref[:, o:o + n].astype(F32)).astype(BF16)
            accw_ref[0, o:o + n, :] += _dot_tn(dg_ref[:, o:o + n], hv)
            accw_ref[1, o:o + n, :] += _dot_tn(du_ref[:, o:o + n], hv)
            acco_ref[o:o + n, :] += _dot_tn(act, dys)

        @pl.when(k == nk - 1)
        def _():
            dw_ref[...] = accw_ref[...].astype(BF16)
            dwo_ref[...] = acco_ref[...].astype(BF16)

    tile = pl.BlockSpec((tk, D), lambda j, k: (k, 0))
    gate = pl.BlockSpec((tk, FH), lambda j, k: (k, j))
    up = pl.BlockSpec((tk, FH), lambda j, k: (k, j + 2))
    return _pcall(
        body, (h, dys, gu, gu, dgu, dgu), grid=(2, nk),
        in_specs=[tile, tile, gate, up, gate, up],
        out_specs=[pl.BlockSpec((2, FH, D), lambda j, k: (0, j, 0), pipeline_mode=pl.Buffered(1)),
                   pl.BlockSpec((FH, D), lambda j, k: (j, 0), pipeline_mode=pl.Buffered(1))],
        out_shape=[_sds((2, F, D), BF16), _sds((F, D), BF16)],
        scratch_shapes=[pltpu.VMEM((2, FH, D), F32), pltpu.VMEM((FH, D), F32)],
        sem=("parallel", "arbitrary"), name="ffn_bwd_w", comm=comm)


def _mixin_fwd(x, gain, wext, tm, comm=None):
    S, D = x.shape
    PW = wext.shape[0]

    def body(x_ref, g_ref, w_ref, p_ref):
        xh, _ = _rms(x_ref[...])
        p_ref[...] = _dot_nt((xh * g_ref[...]).astype(BF16), w_ref[...]).astype(BF16)

    return _pcall(
        body, (x, gain, wext), grid=(S // tm,),
        in_specs=[pl.BlockSpec((tm, D), lambda i: (i, 0)), _resident((1, D)), _resident((PW, D))],
        out_specs=[pl.BlockSpec((tm, PW), lambda i: (i, 0))],
        out_shape=[_sds((S, PW), BF16)],
        sem=("parallel",), name="mixin_fwd", comm=comm)


def _mixin_bwd(x, gain, dxo, dq, dkv, dag, wext, tm, comm=None):
    S, D = x.shape
    PW = wext.shape[0]
    QW = dq.shape[1]
    o1, o2 = QW, QW + 2 * KV_DUP

    def body(x_ref, g_ref, dxo_ref, dq_ref, dkv_ref, dag_ref, w_ref, dx_ref, dgain_ref, dw_ref):
        i = pl.program_id(0)
        xh, r = _rms(x_ref[...])
        h = (xh * g_ref[...]).astype(BF16)
        dqv, dkvv, dagv = dq_ref[...], dkv_ref[...], dag_ref[...]
        dh = _dot(dqv, w_ref[0:o1, :]) + _dot(dkvv, w_ref[o1:o2, :]) + _dot(dagv, w_ref[o2:PW, :])
        dxn, dgn = _rms_bwd(dh, xh, r, g_ref[...])
        dx_ref[...] = dxo_ref[...] + dxn

        @pl.when(i == 0)
        def _():
            dgain_ref[...] = dgn
            dw_ref[0:o1, :] = _dot_tn(dqv, h)
            dw_ref[o1:o2, :] = _dot_tn(dkvv, h)
            dw_ref[o2:PW, :] = _dot_tn(dagv, h)

        @pl.when(i > 0)
        def _():
            dgain_ref[...] += dgn
            dw_ref[0:o1, :] += _dot_tn(dqv, h)
            dw_ref[o1:o2, :] += _dot_tn(dkvv, h)
            dw_ref[o2:PW, :] += _dot_tn(dagv, h)

    return _pcall(
        body, (x, gain, dxo, dq, dkv, dag, wext), grid=(S // tm,),
        in_specs=[pl.BlockSpec((tm, D), lambda i: (i, 0)), _resident((1, D)),
                  pl.BlockSpec((tm, D), lambda i: (i, 0)),
                  pl.BlockSpec((tm, QW), lambda i: (i, 0)),
                  pl.BlockSpec((tm, 2 * KV_DUP), lambda i: (i, 0)),
                  pl.BlockSpec((tm, PW - o2), lambda i: (i, 0)),
                  _resident((PW, D))],
        out_specs=[pl.BlockSpec((tm, D), lambda i: (i, 0)),
                   pl.BlockSpec((1, D), lambda i: (0, 0)),
                   pl.BlockSpec((PW, D), lambda i: (0, 0))],
        out_shape=[_sds((S, D), F32), _sds((1, D), F32), _sds((PW, D), F32)],
        sem=("arbitrary",), name="mixin_bwd", comm=comm)


def _attn_consts(n_heads):
    slopes = [2.0 ** (-8.0 * (h + 1) / n_heads) for h in range(n_heads)]
    return slopes, 1.0 / math.sqrt(HEAD_DIM)


def _attn_mask(n):
    t = lax.broadcasted_iota(jnp.int32, (WINDOW, 2 * WINDOW), 0)
    s = lax.broadcasted_iota(jnp.int32, (WINDOW, 2 * WINDOW), 1)
    dist = t + WINDOW - s
    valid = (dist >= 0) & (dist < WINDOW) & jnp.logical_or(n > 0, s >= WINDOW)
    return valid, dist.astype(F32)


def _lane_halves():
    lo = lax.broadcasted_iota(jnp.int32, (WINDOW, LANES), 1) < HEAD_DIM
    return lo, [jnp.where(lo, 1.0, 0.0).astype(BF16), jnp.where(lo, 0.0, 1.0).astype(BF16)]


def _attn_probs(qm, kd, slope, scale, distf, valid, sk):
    sc = _dot_nt(qm, kd) * scale - slope * distf
    sc = jnp.where(valid, sc, NEG_INF)
    m = jnp.maximum(jnp.max(sc, axis=-1, keepdims=True), sk)
    p = jnp.exp(sc - m)
    es = jnp.exp(sk - m)
    inv = 1.0 / (jnp.sum(p, axis=-1, keepdims=True) + es)
    return p * inv, es * inv


def _attn_fwd(proj, sinks, n_heads, comm=None):
    S = proj.shape[0]
    QW = n_heads * HEAD_DIM
    nb = S // WINDOW
    kblk, vblk = QW // KV_DUP, QW // KV_DUP + 1
    group = n_heads // N_KV_HEADS
    slopes, scale = _attn_consts(n_heads)

    def body(sink_ref, q_ref, kc_ref, kp_ref, vc_ref, vp_ref, o_ref):
        n = pl.program_id(0)
        valid, distf = _attn_mask(n)
        lo, halves = _lane_halves()
        for hp in range(n_heads // 2):
            kh = (2 * hp) // group
            ksl = slice(LANES * kh, LANES * (kh + 1))
            q2 = q_ref[:, LANES * hp:LANES * (hp + 1)]
            kd = jnp.concatenate([kp_ref[:, ksl], kc_ref[:, ksl]], axis=0)
            vd = jnp.concatenate([vp_ref[:, ksl], vc_ref[:, ksl]], axis=0)
            outs = []
            for e in range(2):
                h = 2 * hp + e
                qm = q2 * halves[e]
                pn, _ = _attn_probs(qm, kd, slopes[h], scale, distf, valid, sink_ref[h])
                outs.append(_dot(pn.astype(BF16), vd))
            o_ref[:, LANES * hp:LANES * (hp + 1)] = jnp.where(lo, outs[0], outs[1]).astype(BF16)

    cur = lambda b: (lambda n: (n, b))
    prev = lambda b: (lambda n: (jnp.maximum(n - 1, 0), b))
    return _pcall(
        body, (sinks, proj, proj, proj, proj, proj,), grid=(nb,),
        in_specs=[pl.BlockSpec(memory_space=pltpu.SMEM),
                  pl.BlockSpec((WINDOW, QW), cur(0)),
                  pl.BlockSpec((WINDOW, KV_DUP), cur(kblk)), pl.BlockSpec((WINDOW, KV_DUP), prev(kblk)),
                  pl.BlockSpec((WINDOW, KV_DUP), cur(vblk)), pl.BlockSpec((WINDOW, KV_DUP), prev(vblk))],
        out_specs=[pl.BlockSpec((WINDOW, QW), lambda n: (n, 0))],
        out_shape=[_sds((S, QW), BF16)],
        sem=("parallel",), name="attn_fwd", comm=comm)


def _attn_bwd(proj, dmix, sinks, n_heads, comm=None):
    S = proj.shape[0]
    QW = n_heads * HEAD_DIM
    nb = S // WINDOW
    kblk, vblk = QW // KV_DUP, QW // KV_DUP + 1
    group = n_heads // N_KV_HEADS
    slopes, scale = _attn_consts(n_heads)

    def body(sink_ref, q_ref, kc_ref, kp_ref, vc_ref, vp_ref, do_ref, dq_ref, dkv_ref, dsink_ref, carry_ref):
        n = pl.program_id(0)

        @pl.when(n == 0)
        def _():
            carry_ref[...] = jnp.zeros_like(carry_ref)
            dsink_ref[...] = jnp.zeros_like(dsink_ref)

        @pl.when(n < nb)
        def _():
            valid, distf = _attn_mask(n)
            lo, halves = _lane_halves()
            lane1 = lax.broadcasted_iota(jnp.int32, (1, LANES), 1)
            dkd = [jnp.zeros((2 * WINDOW, LANES), F32) for _ in range(N_KV_HEADS)]
            dvd = [jnp.zeros((2 * WINDOW, LANES), F32) for _ in range(N_KV_HEADS)]
            dsink = jnp.zeros((1, LANES), F32)
            for hp in range(n_heads // 2):
                kh = (2 * hp) // group
                ksl = slice(LANES * kh, LANES * (kh + 1))
                hsl = slice(LANES * hp, LANES * (hp + 1))
                q2 = q_ref[:, hsl]
                do2 = do_ref[:, hsl]
                kd = jnp.concatenate([kp_ref[:, ksl], kc_ref[:, ksl]], axis=0)
                vd = jnp.concatenate([vp_ref[:, ksl], vc_ref[:, ksl]], axis=0)
                dqs = []
                for e in range(2):
                    h = 2 * hp + e
                    qm = q2 * halves[e]
                    dom = do2 * halves[e]
                    pn, psink = _attn_probs(qm, kd, slopes[h], scale, distf, valid, sink_ref[h])
                    dp = _dot_nt(dom, vd)
                    delta = jnp.sum(pn * dp, axis=-1, keepdims=True)
                    dsb = (pn * (dp - delta) * scale).astype(BF16)
                    dsink = dsink - jnp.where(lane1 == h, jnp.sum(psink * delta), 0.0)
                    dqs.append(_dot(dsb, kd))
                    dkd[kh] = dkd[kh] + _dot_tn(dsb, qm)
                    dvd[kh] = dvd[kh] + _dot_tn(pn.astype(BF16), dom)
                dq_ref[:, hsl] = jnp.where(lo, dqs[0], dqs[1]).astype(BF16)
            dsink_ref[...] += dsink
            both = jnp.concatenate(dkd + dvd, axis=1)
            dkv_ref[...] = (carry_ref[...] + both[0:WINDOW]).astype(BF16)
            carry_ref[...] = both[WINDOW:2 * WINDOW]

        @pl.when(n == nb)
        def _():
            dkv_ref[...] = carry_ref[...].astype(BF16)

    last = nb - 1
    cur = lambda b: (lambda n: (jnp.minimum(n, last), b))
    prev = lambda b: (lambda n: (jnp.clip(n - 1, 0, last), b))
    return _pcall(
        body, (sinks, proj, proj, proj, proj, proj, dmix,), grid=(nb + 1,),
        in_specs=[pl.BlockSpec(memory_space=pltpu.SMEM),
                  pl.BlockSpec((WINDOW, QW), cur(0)),
                  pl.BlockSpec((WINDOW, KV_DUP), cur(kblk)), pl.BlockSpec((WINDOW, KV_DUP), prev(kblk)),
                  pl.BlockSpec((WINDOW, KV_DUP), cur(vblk)), pl.BlockSpec((WINDOW, KV_DUP), prev(vblk)),
                  pl.BlockSpec((WINDOW, QW), cur(0))],
        out_specs=[pl.BlockSpec((WINDOW, QW), cur(0)),
                   pl.BlockSpec((WINDOW, 2 * KV_DUP), prev(0)),
                   pl.BlockSpec((1, LANES), lambda n: (0, 0))],
        out_shape=[_sds((S, QW), BF16), _sds((S, 2 * KV_DUP), BF16), _sds((1, LANES), F32)],
        scratch_shapes=[pltpu.VMEM((WINDOW, 2 * KV_DUP), F32)],
        sem=("arbitrary",), name="attn_bwd", comm=comm)


def _glu_window(a_ref, g_ref, ap_ref, gp_ref, win_ref, first):
    tm = a_ref.shape[0]
    zp = ap_ref[...].astype(F32) * _sigmoid(gp_ref[...].astype(F32))
    win_ref[0:HALO, :] = jnp.where(first, jnp.zeros_like(zp), zp)
    win_ref[HALO:HALO + tm, :] = a_ref[...].astype(F32) * _sigmoid(g_ref[...].astype(F32))


def _conv_fwd(proj, wdw, bdw, lng, lnb, n_heads, tm, comm=None):
    S = proj.shape[0]
    taps, C = wdw.shape
    ablk = (n_heads * HEAD_DIM + 2 * KV_DUP) // C
    hb = tm // HALO
    off = HALO - (taps - 1)

    def body(a_ref, g_ref, ap_ref, gp_ref, w_ref, b_ref, lg_ref, lb_ref, o_ref, y_ref, win_ref):
        _glu_window(a_ref, g_ref, ap_ref, gp_ref, win_ref, pl.program_id(0) == 0)
        for c in range(tm // ROWS):
            r0 = c * ROWS
            acc = jnp.zeros((ROWS, C), F32) + b_ref[...]
            for k in range(taps):
                acc = acc + w_ref[k:k + 1, :] * win_ref[r0 + off + k:r0 + off + k + ROWS, :]
            y_ref[r0:r0 + ROWS, :] = acc
        y = y_ref[...]
        mu = jnp.mean(y, axis=-1, keepdims=True)
        yc = y - mu
        yn = yc * lax.rsqrt(jnp.mean(yc * yc, axis=-1, keepdims=True) + RMS_EPS) * lg_ref[...] + lb_ref[...]
        o_ref[...] = (yn * _sigmoid(yn)).astype(BF16)

    vec = pl.BlockSpec((1, C), lambda i: (0, 0))
    halo = lambda b: pl.BlockSpec((HALO, C), lambda i: (jnp.maximum(i * hb - 1, 0), b))
    return _pcall(
        body, (proj, proj, proj, proj, wdw, bdw, lng, lnb,), grid=(S // tm,),
        in_specs=[pl.BlockSpec((tm, C), lambda i: (i, ablk)), pl.BlockSpec((tm, C), lambda i: (i, ablk + 1)),
                  halo(ablk), halo(ablk + 1),
                  pl.BlockSpec((taps, C), lambda i: (0, 0)), vec, vec, vec],
        out_specs=[pl.BlockSpec((tm, C), lambda i: (i, 0)), pl.BlockSpec((tm, C), lambda i: (i, 0))],
        out_shape=[_sds((S, C), BF16), _sds((S, C), F32)],
        scratch_shapes=[pltpu.VMEM((tm + HALO, C), F32)],
        sem=("parallel",), name="conv_fwd", comm=comm)


def _conv_bwd(proj, dmix, ysave, wdw, lng, lnb, n_heads, tm, comm=None):
    S = proj.shape[0]
    taps, C = wdw.shape
    QW = n_heads * HEAD_DIM
    ablk = (QW + 2 * KV_DUP) // C
    cblk = QW // C
    hb = tm // HALO
    nt = S // tm
    off = HALO - (taps - 1)

    def ln_bwd(dc, y, lg, lb):
        mu = jnp.mean(y, axis=-1, keepdims=True)
        yc = y - mu
        r = lax.rsqrt(jnp.mean(yc * yc, axis=-1, keepdims=True) + RMS_EPS)
        yh = yc * r
        yn = yh * lg + lb
        sg = _sigmoid(yn)
        dyn = dc * (sg * (1.0 + yn * (1.0 - sg)))
        dyh = dyn * lg
        dy = r * (dyh - jnp.mean(dyh, axis=-1, keepdims=True) - yh * jnp.mean(dyh * yh, axis=-1, keepdims=True))
        return dy, dyn, yh

    def body(dc_ref, dcn_ref, y_ref, yn_ref, a_ref, g_ref, ap_ref, gp_ref, w_ref, lg_ref, lb_ref,
             dag_ref, dw_ref, dvec_ref, zwin_ref, dyw_ref, dwacc_ref):
        i = pl.program_id(0)

        @pl.when(i == 0)
        def _():
            dwacc_ref[...] = jnp.zeros_like(dwacc_ref)
            dvec_ref[...] = jnp.zeros_like(dvec_ref)

        lg, lb = lg_ref[...], lb_ref[...]
        dy, dyn, yh = ln_bwd(dc_ref[...].astype(F32), y_ref[...], lg, lb)
        dy_next, _, _ = ln_bwd(dcn_ref[...].astype(F32), yn_ref[...], lg, lb)
        dyw_ref[0:tm, :] = dy
        dyw_ref[tm:tm + HALO, :] = jnp.where(i == nt - 1, jnp.zeros_like(dy_next), dy_next)
        dvec_ref[0:1, :] += jnp.sum(dy, axis=0, keepdims=True)
        dvec_ref[1:2, :] += jnp.sum(dyn * yh, axis=0, keepdims=True)
        dvec_ref[2:3, :] += jnp.sum(dyn, axis=0, keepdims=True)
        _glu_window(a_ref, g_ref, ap_ref, gp_ref, zwin_ref, i == 0)

        for c in range(tm // ROWS):
            r0 = c * ROWS
            dz = jnp.zeros((ROWS, C), F32)
            dyc = dyw_ref[r0:r0 + ROWS, :]
            for k in range(taps):
                dz = dz + w_ref[k:k + 1, :] * dyw_ref[r0 + taps - 1 - k:r0 + taps - 1 - k + ROWS, :]
                prod = dyc * zwin_ref[r0 + off + k:r0 + off + k + ROWS, :]
                dwacc_ref[k] += jnp.sum(prod.reshape(ROWS // 8, 8, C), axis=0)
            a = a_ref[r0:r0 + ROWS, :].astype(F32)
            s = _sigmoid(g_ref[r0:r0 + ROWS, :].astype(F32))
            dag_ref[r0:r0 + ROWS, 0:C] = (dz * s).astype(BF16)
            dag_ref[r0:r0 + ROWS, C:2 * C] = (dz * a * s * (1.0 - s)).astype(BF16)

        @pl.when(i == nt - 1)
        def _():
            dw_ref[...] = jnp.zeros_like(dw_ref)
            for k in range(taps):
                dw_ref[k:k + 1, :] = jnp.sum(dwacc_ref[k], axis=0, keepdims=True)

    vec = pl.BlockSpec((1, C), lambda i: (0, 0))
    tile = lambda b: pl.BlockSpec((tm, C), lambda i: (i, b))
    prev = lambda b: pl.BlockSpec((HALO, C), lambda i: (jnp.maximum(i * hb - 1, 0), b))
    nxt = lambda b: pl.BlockSpec((HALO, C), lambda i: (jnp.minimum((i + 1) * hb, S // HALO - 1), b))
    return _pcall(
        body, (dmix, dmix, ysave, ysave, proj, proj, proj, proj, wdw, lng, lnb,), grid=(nt,),
        in_specs=[tile(cblk), nxt(cblk), tile(0), nxt(0), tile(ablk), tile(ablk + 1), prev(ablk), prev(ablk + 1),
                  pl.BlockSpec((taps, C), lambda i: (0, 0)), vec, vec],
        out_specs=[pl.BlockSpec((tm, 2 * C), lambda i: (i, 0)),
                   pl.BlockSpec((HALO, C), lambda i: (0, 0)),
                   pl.BlockSpec((8, C), lambda i: (0, 0))],
        out_shape=[_sds((S, 2 * C), BF16), _sds((HALO, C), F32), _sds((8, C), F32)],
        scratch_shapes=[pltpu.VMEM((tm + HALO, C), F32), pltpu.VMEM((tm + HALO, C), F32),
                        pltpu.VMEM((taps, 8, C), F32)],
        sem=("arbitrary",), name="conv_bwd", comm=comm)


def _mixout_fwd(x, attn, conv, wo, tm, comm=None):
    S, D = x.shape
    QW, C = attn.shape[1], conv.shape[1]

    def body(x_ref, a_ref, c_ref, w_ref, o_ref):
        o_ref[...] = x_ref[...] + _dot(a_ref[...], w_ref[0:QW, :]) + _dot(c_ref[...], w_ref[QW:QW + C, :])

    return _pcall(
        body, (x, attn, conv, wo,), grid=(S // tm,),
        in_specs=[pl.BlockSpec((tm, D), lambda i: (i, 0)),
                  pl.BlockSpec((tm, QW), lambda i: (i, 0)),
                  pl.BlockSpec((tm, C), lambda i: (i, 0)),
                  pl.BlockSpec((QW + C, D), lambda i: (0, 0))],
        out_specs=[pl.BlockSpec((tm, D), lambda i: (i, 0))],
        out_shape=[_sds((S, D), F32)],
        sem=("parallel",), name="mixout_fwd", comm=comm)


def _mixout_bwd(dxo, attn, conv, wo, tm, comm=None):
    S, D = dxo.shape
    QW, C = attn.shape[1], conv.shape[1]
    nt = S // tm

    def body(dx_ref, a_ref, c_ref, w_ref, dm_ref, dw_ref, acc_ref):
        i = pl.program_id(0)
        dxb = dx_ref[...].astype(BF16)
        dm_ref[...] = _dot_nt(dxb, w_ref[...]).astype(BF16)

        @pl.when(i == 0)
        def _():
            acc_ref[...] = jnp.zeros_like(acc_ref)

        acc_ref[0:QW, :] += _dot_tn(a_ref[...], dxb)
        acc_ref[QW:QW + C, :] += _dot_tn(c_ref[...], dxb)

        @pl.when(i == nt - 1)
        def _():
            dw_ref[...] = acc_ref[...].astype(BF16)

    return _pcall(
        body, (dxo, attn, conv, wo,), grid=(nt,),
        in_specs=[pl.BlockSpec((tm, D), lambda i: (i, 0)),
                  pl.BlockSpec((tm, QW), lambda i: (i, 0)),
                  pl.BlockSpec((tm, C), lambda i: (i, 0)),
                  pl.BlockSpec((QW + C, D), lambda i: (0, 0))],
        out_specs=[pl.BlockSpec((tm, QW + C), lambda i: (i, 0)),
                   pl.BlockSpec((QW + C, D), lambda i: (0, 0))],
        out_shape=[_sds((S, QW + C), BF16), _sds((QW + C, D), BF16)],
        scratch_shapes=[pltpu.VMEM((QW + C, D), F32)],
        sem=("arbitrary",), name="mixout_bwd", comm=comm)


def _loss_head(x, gain, target, tm, comm=None):
    S, D = x.shape
    nt = S // tm

    def body(x_ref, g_ref, t_ref, dx_ref, loss_ref, dgain_ref):
        i = pl.program_id(0)
        xh, r = _rms(x_ref[...])
        e = xh * g_ref[...] - t_ref[...]
        loss_ref[...] = jnp.zeros((1, LANES), F32) + 0.5 * jnp.sum(jnp.mean(e * e, axis=-1, keepdims=True))
        dxn, dgn = _rms_bwd(e * (1.0 / D), xh, r, g_ref[...])
        dx_ref[...] = dxn

        @pl.when(i == 0)
        def _():
            dgain_ref[...] = dgn

        @pl.when(i > 0)
        def _():
            dgain_ref[...] += dgn

    return _pcall(
        body, (x, gain, target,), grid=(nt,),
        in_specs=[pl.BlockSpec((tm, D), lambda i: (i, 0)),
                  pl.BlockSpec((1, D), lambda i: (0, 0)),
                  pl.BlockSpec((tm, D), lambda i: (i, 0))],
        out_specs=[pl.BlockSpec((tm, D), lambda i: (i, 0)),
                   pl.BlockSpec((None, 1, LANES), lambda i: (i, 0, 0)),
                   pl.BlockSpec((1, D), lambda i: (0, 0))],
        out_shape=[_sds((S, D), F32), _sds((nt, 1, LANES), F32), _sds((1, D), F32)],
        sem=("arbitrary",), name="loss_head", comm=comm)


def _adam(w, m, v, parts, name):
    L, R, C = w.shape
    P = parts[0].shape[0]
    br = _row_block(R, 256)
    c1 = 1.0 - ADAM_B1 ** ADAM_STEP
    c2 = 1.0 - ADAM_B2 ** ADAM_STEP

    def body(w_ref, m_ref, v_ref, *rest):
        p_refs, (g_ref, d_ref, mo_ref, vo_ref) = rest[:L], rest[L:]
        layer = pl.program_id(0)

        def update(p_ref):
            g = p_ref[0].astype(F32)
            for k in range(1, P):
                g = g + p_ref[k].astype(F32)
            mn = ADAM_B1 * m_ref[...] + (1.0 - ADAM_B1) * g
            vn = ADAM_B2 * v_ref[...] + (1.0 - ADAM_B2) * (g * g)
            g_ref[...] = g
            mo_ref[...] = mn
            vo_ref[...] = vn
            d_ref[...] = -ADAM_LR * ((mn / c1) / (jnp.sqrt(vn / c2) + ADAM_EPS) + ADAM_WD * w_ref[...])

        for k in range(L):
            pl.when(layer == k)(functools.partial(update, p_refs[k]))

    blk = pl.BlockSpec((None, br, C), lambda l, i: (l, i, 0))
    part = lambda k: pl.BlockSpec((P, br, C), lambda l, i: (0, jnp.where(l == k, i, 0), 0))
    return _pcall(
        body, (w, m, v, *parts), grid=(L, R // br),
        in_specs=[blk, blk, blk] + [part(k) for k in range(L)],
        out_specs=[blk, blk, blk, blk],
        out_shape=[_sds((L, R, C), F32)] * 4,
        sem=("parallel", "parallel"), name=name)


def _sum_parts(parts):
    P, R, C = parts.shape

    def body(p_ref, o_ref):
        g = p_ref[0]
        for k in range(1, P):
            g = g + p_ref[k]
        o_ref[...] = g

    vmem = pl.BlockSpec(memory_space=pltpu.VMEM)
    return _pcall(body, (parts,), in_specs=[vmem], out_specs=[vmem], out_shape=[_sds((R, C), F32)],
                  name="sum_parts")[0]


def _place():
    x, y, c = lax.axis_index("x"), lax.axis_index("y"), lax.axis_index("c")
    return x, y, c, [(1 - x, y), (x, 1 - y), (1 - x, 1 - y)]


def _dev(px, py, pc):
    return 4 * px + 2 * py + pc


def _gather_comm(shards, fulls, slot_of):
    n = len(shards)

    def copies(srcs, outs, send_sems, recv_sems):
        x, y, c, chips = _place()

        def copy(a, k, block, to, from_shard=False):
            dst = slot_of[a](outs[a], _dev(*block))
            return pltpu.make_async_remote_copy(
                src_ref=srcs[a] if from_shard else dst, dst_ref=dst,
                send_sem=send_sems.at[a, k], recv_sem=recv_sems.at[a, k], device_id=to, device_id_type=MESH)

        return copy, (x, y, c), (x, y, 1 - c), chips

    def local(srcs, outs, local_sems):
        x, y, c, _ = _place()
        return [pltpu.make_async_copy(srcs[a], slot_of[a](outs[a], _dev(x, y, c)), local_sems.at[a])
                for a in range(n)]

    def first_copies(copy, me, sibling, chips):
        out = []
        for a in range(n):
            out.append(copy(a, 0, me, sibling, True))
            out += [copy(a, 1 + j, me, (*chip, me[2]), True) for j, chip in enumerate(chips)]
        return out

    def start(srcs, outs, sems):
        send_sems, recv_sems, local_sems = sems
        copy, me, sibling, chips = copies(srcs, outs, send_sems, recv_sems)
        for cp in local(srcs, outs, local_sems):
            cp.start()
        for cp in first_copies(copy, me, sibling, chips):
            cp.start()

    def finish(srcs, outs, sems):
        send_sems, recv_sems, local_sems = sems
        copy, me, sibling, chips = copies(srcs, outs, send_sems, recv_sems)
        c = me[2]
        passed = []
        for j, chip in enumerate(chips):
            for a in range(n):
                copy(a, 1 + j, (*chip, c), me).wait_recv()
                fwd = copy(a, 4 + j, (*chip, c), sibling)
                fwd.start()
                passed.append(fwd)
        for a in range(n):
            copy(a, 0, sibling, me).wait_recv()
            for j, chip in enumerate(chips):
                copy(a, 4 + j, (*chip, 1 - c), me).wait_recv()
        for cp in first_copies(copy, me, sibling, chips) + passed:
            cp.wait_send()
        for cp in local(srcs, outs, local_sems):
            cp.wait()

    sems = [pltpu.SemaphoreType.DMA((n, 7)), pltpu.SemaphoreType.DMA((n, 7)), pltpu.SemaphoreType.DMA((n,))]
    return _Comm(shards, fulls, sems, start, finish)


def _swap_comm(grads):
    n = len(grads)

    def copies(srcs, outs, sems):
        x, y, c, _ = _place()
        return [pltpu.make_async_remote_copy(
            src_ref=srcs[a].at[:, pl.ds(1 - c, 1)], dst_ref=outs[a],
            send_sem=sems[0].at[a], recv_sem=sems[1].at[a], device_id=(x, y, 1 - c), device_id_type=MESH)
            for a in range(n)]

    def start(srcs, outs, sems):
        for cp in copies(srcs, outs, sems):
            cp.start()

    def finish(srcs, outs, sems):
        for cp in copies(srcs, outs, sems):
            cp.wait()

    return _Comm(grads, [_sds((N_CHIP, 1) + g.shape[2:], g.dtype) for g in grads],
                 [pltpu.SemaphoreType.DMA((n,)), pltpu.SemaphoreType.DMA((n,))], start, finish)


def _exchange_comm(parts):
    n = len(parts)

    def copies(srcs, outs, sems):
        x, y, c, chips = _place()
        mine = 2 * x + y
        loc = [pltpu.make_async_copy(srcs[a].at[pl.ds(mine, 1)], outs[a].at[pl.ds(mine, 1)], sems[2].at[a])
               for a in range(n)]
        rem = [pltpu.make_async_remote_copy(
            src_ref=srcs[a].at[pl.ds(2 * px + py, 1)], dst_ref=outs[a].at[pl.ds(mine, 1)],
            send_sem=sems[0].at[a, j], recv_sem=sems[1].at[a, j], device_id=(px, py, c), device_id_type=MESH)
            for a in range(n) for j, (px, py) in enumerate(chips)]
        return loc + rem

    def start(srcs, outs, sems):
        for cp in copies(srcs, outs, sems):
            cp.start()

    def finish(srcs, outs, sems):
        for cp in copies(srcs, outs, sems):
            cp.wait()

    return _Comm(parts, [_sds(p.shape, p.dtype) for p in parts],
                 [pltpu.SemaphoreType.DMA((n, 3)), pltpu.SemaphoreType.DMA((n, 3)), pltpu.SemaphoreType.DMA((n,))],
                 start, finish)


def _comm_only(comm, name):
    return _pcall(lambda: None, (), in_specs=[], out_specs=[], out_shape=[], name=name, comm=comm)


def _gather_small(v):
    R, C = v.shape

    def body(x_ref, out_ref, send_sems, recv_sems, local_sem):
        x, y, c, chips = _place()
        me, sibling = (x, y, c), (x, y, 1 - c)

        def copy(k, block, to, from_shard=False):
            dst = out_ref.at[_dev(*block)]
            return pltpu.make_async_remote_copy(
                src_ref=x_ref if from_shard else dst, dst_ref=dst,
                send_sem=send_sems.at[k], recv_sem=recv_sems.at[k], device_id=to, device_id_type=MESH)

        mine = pltpu.make_async_copy(x_ref, out_ref.at[_dev(*me)], local_sem)
        mine.start()
        first = [copy(0, me, sibling, True)] + [copy(1 + j, me, (*chip, c), True) for j, chip in enumerate(chips)]
        for cp in first:
            cp.start()
        passed = [copy(4 + j, (*chip, c), sibling) for j, chip in enumerate(chips)]
        for j, chip in enumerate(chips):
            copy(1 + j, (*chip, c), me).wait_recv()
            passed[j].start()
        copy(0, sibling, me).wait_recv()
        for j, chip in enumerate(chips):
            copy(4 + j, (*chip, 1 - c), me).wait_recv()
        for cp in first + passed:
            cp.wait_send()
        mine.wait()

    vmem = pl.BlockSpec(memory_space=pltpu.VMEM)
    return _pcall(
        body, (v,), in_specs=[vmem], out_specs=[vmem], out_shape=[_sds((N_DEV, R, C), F32)],
        scratch_shapes=[pltpu.SemaphoreType.DMA((7,)), pltpu.SemaphoreType.DMA((7,)), pltpu.SemaphoreType.DMA],
        name="gather_small")[0]


def _add_sibling(core, g, r):
    _, _, R, C = g.shape
    br = _row_block(R)

    def body(c_ref, g_ref, r_ref, o_ref):
        o_ref[...] = (g_ref[...].astype(F32) + r_ref[...].astype(F32)).astype(BF16)

    return _pcall(
        body, (core, g, r),
        grid_spec=pltpu.PrefetchScalarGridSpec(
            num_scalar_prefetch=1, grid=(N_CHIP, R // br),
            in_specs=[pl.BlockSpec((None, None, br, C), lambda k, i, c_ref: (k, c_ref[0], i, 0)),
                      pl.BlockSpec((None, None, br, C), lambda k, i, c_ref: (k, 0, i, 0))],
            out_specs=pl.BlockSpec((None, br, C), lambda k, i, c_ref: (k, i, 0))),
        out_shape=_sds((N_CHIP, R, C), BF16), sem=("parallel", "parallel"), name="rs_add_sibling")


def _by_chip(g):
    return g.reshape((N_CHIP, 2) + g.shape[1:])


def _pack_rows(vecs):
    flat = jnp.concatenate([v.reshape(-1).astype(F32) for v in vecs])
    rows = -(-flat.shape[0] // (8 * LANES)) * 8
    return jnp.pad(flat, (0, rows * LANES - flat.shape[0])).reshape(rows, LANES)


def _unpack_rows(rows, shapes):
    flat = rows.reshape(-1)
    out, o = [], 0
    for s in shapes:
        n = math.prod(s)
        out.append(flat[o:o + n].reshape(s))
        o += n
    return out


def _adam_any(w, m, v, g, name):
    shape = w.shape
    one = lambda t: t.reshape(1, -1, shape[-1])
    return tuple(t.reshape(shape) for t in _adam(one(w), one(m), one(v), [one(g)], name))


def kernel(x, norm_ffn1, w_ffn1_in, w_ffn1_out, norm_mix, w_in, sinks, w_dw, b_dw, conv_ln_g, conv_ln_b, w_out, norm_ffn2, w_ffn2_in, w_ffn2_out, final_norm, loss_target, m_norm_ffn1, m_w_ffn1_in, m_w_ffn1_out, m_norm_mix, m_w_in, m_sinks, m_w_dw, m_b_dw, m_conv_ln_g, m_conv_ln_b, m_w_out, m_norm_ffn2, m_w_ffn2_in, m_w_ffn2_out, m_final_norm, v_norm_ffn1, v_w_ffn1_in, v_w_ffn1_out, v_norm_mix, v_w_in, v_sinks, v_w_dw, v_b_dw, v_conv_ln_g, v_conv_ln_b, v_w_out, v_norm_ffn2, v_w_ffn2_in, v_w_ffn2_out, v_final_norm):
    _, S, D = x.shape
    L = norm_ffn1.shape[0]
    NF = w_ffn1_in.shape[2]
    RF = w_ffn1_out.shape[1]
    NW = w_in.shape[2]
    RO = w_out.shape[1]
    taps, CD = w_dw.shape[1], w_dw.shape[2]
    H = sinks.shape[1]
    C = N_DEV * CD
    QW = H * HEAD_DIM
    KVW = N_KV_HEADS * HEAD_DIM
    assert 2 * RF == NF and QW + C == N_DEV * RO and N_DEV * NW == QW + 2 * KVW + 2 * C
    tm = min(512, S)
    ta = min(256, S)
    tc = min(256, S)
    core = lax.axis_index("c").astype(jnp.int32).reshape(1)

    x0 = x[0]
    target = loss_target[0]

    wdw_all = _gather_small(_pack_rows([w_dw]))
    n_dw = L * taps * CD
    wdw_full = jnp.stack([wdw_all[d].reshape(-1)[:n_dw].reshape(L, taps, CD) for d in range(N_DEV)],
                         axis=2).reshape(L, taps, C)

    names = ["a1", "wi", "b1", "b2", "wo", "a2"]

    def shard(name, l):
        return {"a1": lambda: w_ffn1_in[l].T.astype(BF16), "a2": lambda: w_ffn2_in[l].T.astype(BF16),
                "b1": lambda: w_ffn1_out[l].astype(BF16), "b2": lambda: w_ffn2_out[l].astype(BF16),
                "wi": lambda: w_in[l].T.astype(BF16), "wo": lambda: w_out[l].astype(BF16)}[name]()

    rows_of = {"a1": NF, "a2": NF, "b1": RF, "b2": RF, "wi": NW, "wo": RO}
    full_of = {n: _sds((N_DEV * r, D), BF16) for n, r in rows_of.items()}
    slot_of = {n: (lambda ref, b, r=r: ref.at[pl.ds(b * r, r)]) for n, r in rows_of.items()}

    def gather(name_list, l):
        if l >= L:
            return None
        return _gather_comm([shard(n, l) for n in name_list], [full_of[n] for n in name_list],
                            [slot_of[n] for n in name_list])

    def weights(g):
        wi = g["wi"]
        q, k, v, u = wi[:QW], wi[QW:QW + KVW], wi[QW + KVW:QW + 2 * KVW], wi[QW + 2 * KVW:]
        dup = lambda t: jnp.concatenate(
            [t[HEAD_DIM * (i // 2):HEAD_DIM * (i // 2 + 1)] for i in range(2 * N_KV_HEADS)], axis=0)
        return dict(g, wext=jnp.concatenate([q, dup(k), dup(v), u], axis=0))

    got = dict(zip(names, _comm_only(gather(names, 0), "ag_layer0")))

    saved = []
    xc = x0
    for l in range(L):
        W = weights(got)
        nxt = {}

        def take(res, n_own, name):
            if l + 1 < L:
                nxt[name] = res[n_own]
            return res[:n_own]

        g1, gm, g2 = norm_ffn1[l][None], norm_mix[l][None], norm_ffn2[l][None]
        x1, gu1 = take(_ffn_fwd(xc, g1, W["a1"], W["b1"], tm, gather(["a1"], l + 1)), 2, "a1")
        proj, = take(_mixin_fwd(x1, gm, W["wext"], tm, gather(["wi"], l + 1)), 1, "wi")
        attn, = take(_attn_fwd(proj, sinks[l], H, gather(["b1"], l + 1)), 1, "b1")
        conv, ysave = take(_conv_fwd(proj, wdw_full[l], b_dw[l][None], conv_ln_g[l][None], conv_ln_b[l][None], H, tc,
                                     gather(["b2"], l + 1)), 2, "b2")
        x2, = take(_mixout_fwd(x1, attn, conv, W["wo"], tm, gather(["wo"], l + 1)), 1, "wo")
        x3, gu2 = take(_ffn_fwd(x2, g2, W["a2"], W["b2"], tm, gather(["a2"], l + 1)), 2, "a2")
        saved.append(dict(W=W, x0=xc, x1=x1, x2=x2, gu1=gu1, gu2=gu2, proj=proj, attn=attn, conv=conv, ysave=ysave))
        xc = x3
        got = nxt

    dx, loss_parts, dfinal = _loss_head(xc, final_norm[None], target, tm)
    loss = lax.psum(jnp.sum(loss_parts[:, 0, 0]), ("x", "y", "c"))

    def swap(gs):
        return _swap_comm([_by_chip(g) for g in gs]) if gs else None

    def added(gs, gots):
        return [_add_sibling(core, _by_chip(g), r) for g, r in zip(gs, gots)]

    small = [None] * L
    big = [dict() for _ in range(L)]
    carry = None
    for l in reversed(range(L)):
        sv = saved[l]
        W = sv["W"]
        g1, gm, g2 = norm_ffn1[l][None], norm_mix[l][None], norm_ffn2[l][None]
        fold = lambda t: jnp.concatenate(
            [t[2 * HEAD_DIM * i:2 * HEAD_DIM * i + HEAD_DIM] + t[2 * HEAD_DIM * i + HEAD_DIM:2 * HEAD_DIM * (i + 1)]
             for i in range(N_KV_HEADS)], axis=0)

        r = _ffn_bwd_a(sv["x2"], g2, dx, sv["gu2"], W["a2"], W["b2"], ta, swap(carry[1]) if carry else None)
        dx2, dgu2, dg2, h2, dys2 = r[:5]
        p_carry = added(carry[1], r[5:]) if carry else None
        r = _ffn_bwd_w(h2, dys2, sv["gu2"], dgu2, tm, _exchange_comm(p_carry) if carry else None)
        da2, db2 = r[0].reshape(N_DEV, NF, D), r[1].reshape(N_DEV, RF, D)
        if carry:
            big[carry[0]]["a1"], big[carry[0]]["b1"] = r[2:]
        r = _mixout_bwd(dx2, sv["attn"], sv["conv"], W["wo"], tm, swap([da2, db2]))
        dmix, dwo = r[0], r[1].reshape(N_DEV, RO, D)
        p_a2, p_b2 = added([da2, db2], r[2:])
        r = _conv_bwd(sv["proj"], dmix, sv["ysave"], wdw_full[l], conv_ln_g[l][None], conv_ln_b[l][None], H, tc,
                      _join(_exchange_comm([p_b2]), swap([dwo])))
        dag, dwdw, dvec = r[:3]
        big[l]["b2"] = r[3]
        p_wo, = added([dwo], r[4:])
        r = _attn_bwd(sv["proj"], dmix, sinks[l], H, _exchange_comm([p_a2]))
        dq, dkv, dsink = r[:3]
        big[l]["a2"] = r[3]
        r = _mixin_bwd(sv["x1"], gm, dx2, dq, dkv, dag, W["wext"], tm, _exchange_comm([p_wo]))
        dx1, dgm, dwext = r[:3]
        big[l]["wo"] = r[3]
        dwi = jnp.concatenate([dwext[:QW], fold(dwext[QW:QW + KV_DUP]),
                               fold(dwext[QW + KV_DUP:QW + 2 * KV_DUP]), dwext[QW + 2 * KV_DUP:]], axis=0)
        dwi = dwi.astype(BF16).reshape(N_DEV, NW, D)
        r = _ffn_bwd_a(sv["x0"], g1, dx1, sv["gu1"], W["a1"], W["b1"], ta, swap([dwi]))
        dx0, dgu1, dg1, h1, dys1 = r[:5]
        p_wi, = added([dwi], r[5:])
        r = _ffn_bwd_w(h1, dys1, sv["gu1"], dgu1, tm, _exchange_comm([p_wi]))
        carry = (l, [r[0].reshape(N_DEV, NF, D), r[1].reshape(N_DEV, RF, D)])
        big[l]["wi"] = r[2]
        dx = dx0
        small[l] = [dg1[0], dgm[0], dsink[0, :H], dwdw[:taps], dvec[0], dvec[1], dvec[2], dg2[0]]

    p_carry = added(carry[1], _comm_only(swap(carry[1]), "rs_swap_last"))
    big[carry[0]]["a1"], big[carry[0]]["b1"] = _comm_only(_exchange_comm(p_carry), "rs_exchange_last")
    grad_x = dx[None]

    small_shapes = [(D,), (D,), (H,), (taps, C), (C,), (C,), (C,), (D,)]
    packed = _pack_rows([t for l in range(L) for t in small[l]] + [dfinal[0]])
    total = _sum_parts(_gather_small(packed))
    flat = _unpack_rows(total, small_shapes * L + [(D,)])
    per = [jnp.stack([flat[l * len(small_shapes) + i] for l in range(L)]) for i in range(len(small_shapes))]
    g_nf1, g_nmix, g_sinks, g_wdw_full, g_bdw, g_lng, g_lnb, g_nf2 = per
    g_final = flat[-1]
    dev = _dev(lax.axis_index("x"), lax.axis_index("y"), lax.axis_index("c"))
    g_wdw = lax.dynamic_slice_in_dim(g_wdw_full, dev * CD, CD, axis=2)

    res = {}
    res["norm_ffn1"] = _adam_any(norm_ffn1, m_norm_ffn1, v_norm_ffn1, g_nf1, "adam_small")
    res["norm_mix"] = _adam_any(norm_mix, m_norm_mix, v_norm_mix, g_nmix, "adam_small")
    res["sinks"] = _adam_any(sinks, m_sinks, v_sinks, g_sinks, "adam_small")
    res["w_dw"] = _adam_any(w_dw, m_w_dw, v_w_dw, g_wdw, "adam_small")
    res["b_dw"] = _adam_any(b_dw, m_b_dw, v_b_dw, g_bdw, "adam_small")
    res["conv_ln_g"] = _adam_any(conv_ln_g, m_conv_ln_g, v_conv_ln_g, g_lng, "adam_small")
    res["conv_ln_b"] = _adam_any(conv_ln_b, m_conv_ln_b, v_conv_ln_b, g_lnb, "adam_small")
    res["norm_ffn2"] = _adam_any(norm_ffn2, m_norm_ffn2, v_norm_ffn2, g_nf2, "adam_small")
    res["final_norm"] = tuple(t[0] for t in _adam_any(final_norm[None], m_final_norm[None], v_final_norm[None],
                                                      g_final[None], "adam_small"))

    def adam_big(key, w, m, v, name, transposed=False):
        parts = [jnp.transpose(big[l][key], (0, 2, 1)) if transposed else big[l][key] for l in range(L)]
        return tuple(_adam(w, m, v, parts, name))

    res["w_ffn1_in"] = adam_big("a1", w_ffn1_in, m_w_ffn1_in, v_w_ffn1_in, "adam_ffn_in", True)
    res["w_ffn1_out"] = adam_big("b1", w_ffn1_out, m_w_ffn1_out, v_w_ffn1_out, "adam_ffn_out")
    res["w_ffn2_in"] = adam_big("a2", w_ffn2_in, m_w_ffn2_in, v_w_ffn2_in, "adam_ffn_in", True)
    res["w_ffn2_out"] = adam_big("b2", w_ffn2_out, m_w_ffn2_out, v_w_ffn2_out, "adam_ffn_out")
    res["w_in"] = adam_big("wi", w_in, m_w_in, v_w_in, "adam_w_in", True)
    res["w_out"] = adam_big("wo", w_out, m_w_out, v_w_out, "adam_w_out")

    order = ["norm_ffn1", "w_ffn1_in", "w_ffn1_out", "norm_mix", "w_in", "sinks", "w_dw", "b_dw", "conv_ln_g",
             "conv_ln_b", "w_out", "norm_ffn2", "w_ffn2_in", "w_ffn2_out", "final_norm"]
    return (loss, grad_x, *[res[n][0] for n in order], *[res[n][1] for n in order],
            *[res[n][2] for n in order], *[res[n][3] for n in order])
```

```python
import functools
import math

import jax
import jax.numpy as jnp
from jax import lax
from jax.experimental import pallas as pl
from jax.experimental.pallas import tpu as pltpu

F32 = jnp.float32
BF16 = jnp.bfloat16
MESH = pl.DeviceIdType.MESH

N_DEV = 8
N_CHIP = 4
HEAD_DIM = 64
N_KV_HEADS = 2
WINDOW = 128
KV_DUP = 2 * HEAD_DIM * N_KV_HEADS
RMS_EPS = 1e-6
NEG_INF = -1e30
FFN_RES = 0.5
HALO = 32
ROWS = 32
FFN_CHUNK = 512
BWD_W_CHUNK = 384
LANES = 128
V7X_VMEM_LIMIT = 56 * 1024 * 1024

ADAM_LR = 0.001
ADAM_B1 = 0.9
ADAM_B2 = 0.999
ADAM_EPS = 1e-08
ADAM_WD = 0.01
ADAM_STEP = 10


def _raw_call(body, **kw):
    return pl.pallas_call(body, **kw)


class _Comm:
    def __init__(self, ins, outs, sems, start, finish):
        self.ins, self.outs, self.sems, self.start, self.finish = list(ins), list(outs), list(sems), start, finish


def _join(*comms):
    comms = [c for c in comms if c is not None]
    if not comms:
        return None

    def split(refs, attr):
        out, o = [], 0
        for c in comms:
            n = len(getattr(c, attr))
            out.append(refs[o:o + n])
            o += n
        return out

    def run(which):
        def go(ins, outs, sems):
            for c, i, o, m in zip(comms, split(ins, "ins"), split(outs, "outs"), split(sems, "sems")):
                getattr(c, which)(i, o, m)
        return go

    return _Comm(sum((c.ins for c in comms), []), sum((c.outs for c in comms), []),
                 sum((c.sems for c in comms), []), run("start"), run("finish"))


def _pcall(body, args, *, name, out_shape, grid=(), in_specs=None, out_specs=None, scratch_shapes=(), sem=(),
           comm=None, grid_spec=None):
    if grid_spec is not None:
        return _raw_call(body, grid_spec=grid_spec, out_shape=out_shape, name=name,
                         compiler_params=_params(*sem))(*args)
    if comm is None:
        return _raw_call(body, grid=grid, in_specs=in_specs, out_specs=out_specs, out_shape=out_shape,
                         scratch_shapes=list(scratch_shapes), name=name, compiler_params=_params(*sem))(*args)
    n_in, n_out, n_scr = len(in_specs), len(out_shape), len(scratch_shapes)
    ci, co = len(comm.ins), len(comm.outs)

    def fused(*refs):
        cuts = [n_in, ci, n_out, co, n_scr]
        parts, o = [], 0
        for n in cuts:
            parts.append(refs[o:o + n])
            o += n
        ins, cins, outs, couts, scr = parts
        csems = refs[o:]
        if not grid:
            comm.start(cins, couts, csems)
            body(*ins, *outs, *scr)
            comm.finish(cins, couts, csems)
            return
        ids = [pl.program_id(a) for a in range(len(grid))]
        first = functools.reduce(jnp.logical_and, [i == 0 for i in ids])
        last = functools.reduce(jnp.logical_and, [i == g - 1 for i, g in zip(ids, grid)])

        @pl.when(first)
        def _():
            comm.start(cins, couts, csems)

        body(*ins, *outs, *scr)

        @pl.when(last)
        def _():
            comm.finish(cins, couts, csems)

    return _raw_call(
        fused, grid=grid, in_specs=list(in_specs) + [ANY] * ci, out_specs=list(out_specs) + [ANY] * co,
        out_shape=list(out_shape) + comm.outs, scratch_shapes=list(scratch_shapes) + comm.sems, name=name,
        compiler_params=_params(*(["arbitrary"] * len(grid))))(*args, *comm.ins)


ANY = pl.BlockSpec(memory_space=pl.ANY)


def _params(*sem):
    return pltpu.CompilerParams(dimension_semantics=sem, vmem_limit_bytes=V7X_VMEM_LIMIT)


def _dot(a, b):
    return jnp.dot(a, b, preferred_element_type=F32)


def _dot_nt(a, b):
    return lax.dot_general(a, b, (((1,), (1,)), ((), ())), preferred_element_type=F32)


def _dot_tn(a, b):
    return lax.dot_general(a, b, (((0,), (0,)), ((), ())), preferred_element_type=F32)


def _sigmoid(x):
    return 1.0 / (1.0 + jnp.exp(-x))


def _rms(x):
    r = lax.rsqrt(jnp.mean(x * x, axis=-1, keepdims=True) + RMS_EPS)
    return x * r, r


def _rms_bwd(dh, xh, r, g):
    dxh = dh * g
    dx = r * (dxh - xh * jnp.mean(dxh * xh, axis=-1, keepdims=True))
    return dx, jnp.sum(dh * xh, axis=0, keepdims=True)


def _sds(shape, dtype):
    return jax.ShapeDtypeStruct(shape, dtype)


def _row_block(rows, limit=512):
    fits = [d for d in range(16, min(rows, limit) + 1, 16) if rows % d == 0]
    return fits[-1] if fits else rows


def _chunks(n, step):
    return [(o, min(step, n - o)) for o in range(0, n, step)]


def _resident(shape):
    return pl.BlockSpec(shape, lambda *_: (0,) * len(shape), pipeline_mode=pl.Buffered(1))


def _ffn_fwd(x, gain, wint, wout, tm, comm=None):
    S, D = x.shape
    F = wout.shape[0]

    def body(x_ref, g_ref, w_ref, wo_ref, xo_ref, gu_ref, a_ref):
        xh, _ = _rms(x_ref[...])
        h = (xh * g_ref[...]).astype(BF16)
        for o, n in _chunks(F, FFN_CHUNK):
            gb = _dot_nt(h, w_ref[o:o + n, :]).astype(BF16)
            ub = _dot_nt(h, w_ref[F + o:F + o + n, :]).astype(BF16)
            gu_ref[:, o:o + n] = gb
            gu_ref[:, F + o:F + o + n] = ub
            g = gb.astype(F32)
            a_ref[:, o:o + n] = (g * _sigmoid(g) * ub.astype(F32)).astype(BF16)
        xo_ref[...] = x_ref[...] + FFN_RES * _dot(a_ref[...], wo_ref[...])

    return _pcall(
        body, (x, gain, wint, wout), grid=(S // tm,),
        in_specs=[pl.BlockSpec((tm, D), lambda i: (i, 0)), _resident((1, D)),
                  _resident((2 * F, D)), _resident((F, D))],
        out_specs=[pl.BlockSpec((tm, D), lambda i: (i, 0)), pl.BlockSpec((tm, 2 * F), lambda i: (i, 0))],
        out_shape=[_sds((S, D), F32), _sds((S, 2 * F), BF16)],
        scratch_shapes=[pltpu.VMEM((tm, F), BF16)],
        sem=("parallel",), name="ffn_fwd", comm=comm)


def _ffn_bwd_a(x, gain, dxo, gu, wint, wout, tm, comm=None):
    S, D = x.shape
    F = wout.shape[0]

    def body(x_ref, g_ref, dxo_ref, gu_ref, w_ref, wo_ref, dx_ref, dgu_ref, dgain_ref, h_ref, dys_ref):
        i = pl.program_id(0)
        dys = (FFN_RES * dxo_ref[...]).astype(BF16)
        dys_ref[...] = dys
        for o, n in _chunks(F, FFN_CHUNK):
            dact = _dot_nt(dys, wo_ref[o:o + n, :])
            g = gu_ref[:, o:o + n].astype(F32)
            u = gu_ref[:, F + o:F + o + n].astype(F32)
            s = _sigmoid(g)
            dgu_ref[:, o:o + n] = (dact * u * (s * (1.0 + g * (1.0 - s)))).astype(BF16)
            dgu_ref[:, F + o:F + o + n] = (dact * (g * s)).astype(BF16)
        dh = _dot(dgu_ref[...], w_ref[...])
        xh, r = _rms(x_ref[...])
        h_ref[...] = (xh * g_ref[...]).astype(BF16)
        dxn, dgn = _rms_bwd(dh, xh, r, g_ref[...])
        dx_ref[...] = dxo_ref[...] + dxn

        @pl.when(i == 0)
        def _():
            dgain_ref[...] = dgn

        @pl.when(i > 0)
        def _():
            dgain_ref[...] += dgn

    tile = pl.BlockSpec((tm, D), lambda i: (i, 0))
    wide = pl.BlockSpec((tm, 2 * F), lambda i: (i, 0))
    return _pcall(
        body, (x, gain, dxo, gu, wint, wout), grid=(S // tm,),
        in_specs=[tile, _resident((1, D)), tile, wide, _resident((2 * F, D)), _resident((F, D))],
        out_specs=[tile, wide, pl.BlockSpec((1, D), lambda i: (0, 0)), tile, tile],
        out_shape=[_sds((S, D), F32), _sds((S, 2 * F), BF16), _sds((1, D), F32), _sds((S, D), BF16), _sds((S, D), BF16)],
        sem=("arbitrary",), name="ffn_bwd_a", comm=comm)


def _ffn_bwd_w(h, dys, gu, dgu, tk, comm=None):
    S, D = h.shape
    F = gu.shape[1] // 2
    FH = F // 2
    nk = S // tk

    def body(h_ref, dys_ref, gg_ref, gu_ref, dg_ref, du_ref, dw_ref, dwo_ref, accw_ref, acco_ref):
        k = pl.program_id(1)

        @pl.when(k == 0)
        def _():
            accw_ref[...] = jnp.zeros_like(accw_ref)
            acco_ref[...] = jnp.zeros_like(acco_ref)

        hv, dys = h_ref[...], dys_ref[...]
        for o, n in _chunks(FH, BWD_W_CHUNK):
            g = gg_ref[:, o:o + n].astype(F32)
            act = (g * _sigmoid(g) * gu_ref[:, o:o + n].astype(F32)).astype(BF16)
            accw_ref[0, o:o + n, :] += _dot_tn(dg_ref[:, o:o + n], hv)
            accw_ref[1, o:o + n, :] += _dot_tn(du_ref[:, o:o + n], hv)
            acco_ref[o:o + n, :] += _dot_tn(act, dys)

        @pl.when(k == nk - 1)
        def _():
            dw_ref[...] = accw_ref[...].astype(BF16)
            dwo_ref[...] = acco_ref[...].astype(BF16)

    tile = pl.BlockSpec((tk, D), lambda j, k: (k, 0))
    gate = pl.BlockSpec((tk, FH), lambda j, k: (k, j))
    up = pl.BlockSpec((tk, FH), lambda j, k: (k, j + 2))
    return _pcall(
        body, (h, dys, gu, gu, dgu, dgu), grid=(2, nk),
        in_specs=[tile, tile, gate, up, gate, up],
        out_specs=[pl.BlockSpec((2, FH, D), lambda j, k: (0, j, 0), pipeline_mode=pl.Buffered(1)),
                   pl.BlockSpec((FH, D), lambda j, k: (j, 0), pipeline_mode=pl.Buffered(1))],
        out_shape=[_sds((2, F, D), BF16), _sds((F, D), BF16)],
        scratch_shapes=[pltpu.VMEM((2, FH, D), F32), pltpu.VMEM((FH, D), F32)],
        sem=("parallel", "arbitrary"), name="ffn_bwd_w", comm=comm)


def _mixin_fwd(x, gain, wext, tm, comm=None):
    S, D = x.shape
    PW = wext.shape[0]

    def body(x_ref, g_ref, w_ref, p_ref):
        xh, _ = _rms(x_ref[...])
        p_ref[...] = _dot_nt((xh * g_ref[...]).astype(BF16), w_ref[...]).astype(BF16)

    return _pcall(
        body, (x, gain, wext), grid=(S // tm,),
        in_specs=[pl.BlockSpec((tm, D), lambda i: (i, 0)), _resident((1, D)), _resident((PW, D))],
        out_specs=[pl.BlockSpec((tm, PW), lambda i: (i, 0))],
        out_shape=[_sds((S, PW), BF16)],
        sem=("parallel",), name="mixin_fwd", comm=comm)


def _mixin_bwd(x, gain, dxo, dq, dkv, dag, wext, tm, comm=None):
    S, D = x.shape
    PW = wext.shape[0]
    QW = dq.shape[1]
    o1, o2 = QW, QW + 2 * KV_DUP

    def body(x_ref, g_ref, dxo_ref, dq_ref, dkv_ref, dag_ref, w_ref, dx_ref, dgain_ref, dw_ref):
        i = pl.program_id(0)
        xh, r = _rms(x_ref[...])
        h = (xh * g_ref[...]).astype(BF16)
        dqv, dkvv, dagv = dq_ref[...], dkv_ref[...], dag_ref[...]
        dh = _dot(dqv, w_ref[0:o1, :]) + _dot(dkvv, w_ref[o1:o2, :]) + _dot(dagv, w_ref[o2:PW, :])
        dxn, dgn = _rms_bwd(dh, xh, r, g_ref[...])
        dx_ref[...] = dxo_ref[...] + dxn

        @pl.when(i == 0)
        def _():
            dgain_ref[...] = dgn
            dw_ref[0:o1, :] = _dot_tn(dqv, h)
            dw_ref[o1:o2, :] = _dot_tn(dkvv, h)
            dw_ref[o2:PW, :] = _dot_tn(dagv, h)

        @pl.when(i > 0)
        def _():
            dgain_ref[...] += dgn
            dw_ref[0:o1, :] += _dot_tn(dqv, h)
            dw_ref[o1:o2, :] += _dot_tn(dkvv, h)
            dw_ref[o2:PW, :] += _dot_tn(dagv, h)

    return _pcall(
        body, (x, gain, dxo, dq, dkv, dag, wext), grid=(S // tm,),
        in_specs=[pl.BlockSpec((tm, D), lambda i: (i, 0)), _resident((1, D)),
                  pl.BlockSpec((tm, D), lambda i: (i, 0)),
                  pl.BlockSpec((tm, QW), lambda i: (i, 0)),
                  pl.BlockSpec((tm, 2 * KV_DUP), lambda i: (i, 0)),
                  pl.BlockSpec((tm, PW - o2), lambda i: (i, 0)),
                  _resident((PW, D))],
        out_specs=[pl.BlockSpec((tm, D), lambda i: (i, 0)),
                   pl.BlockSpec((1, D), lambda i: (0, 0)),
                   pl.BlockSpec((PW, D), lambda i: (0, 0))],
        out_shape=[_sds((S, D), F32), _sds((1, D), F32), _sds((PW, D), F32)],
        sem=("arbitrary",), name="mixin_bwd", comm=comm)


def _attn_group(n, group, slopes, sinks):
    rows = group * WINDOW
    r = lax.broadcasted_iota(jnp.int32, (rows, 2 * WINDOW), 0)
    s = lax.broadcasted_iota(jnp.int32, (rows, 2 * WINDOW), 1)
    dist = (r & (WINDOW - 1)) + WINDOW - s
    valid = (dist >= 0) & (dist < WINDOW) & jnp.logical_or(n > 0, s >= WINDOW)
    seg = lax.shift_right_logical(lax.broadcasted_iota(jnp.int32, (rows, 1), 0), WINDOW.bit_length() - 1)
    slope = jnp.zeros((rows, 1), F32)
    sink = jnp.zeros((rows, 1), F32)
    for i in range(group):
        slope = jnp.where(seg == i, slopes[i], slope)
        sink = jnp.where(seg == i, sinks[i], sink)
    return valid, -slope * dist.astype(F32), sink


def _lane_halves():
    lo = lax.broadcasted_iota(jnp.int32, (WINDOW, LANES), 1) < HEAD_DIM
    return lo, [jnp.where(lo, 1.0, 0.0).astype(BF16), jnp.where(lo, 0.0, 1.0).astype(BF16)]


def _stack_heads(ref, first_tile, n_tiles, halves):
    parts = []
    for t in range(first_tile, first_tile + n_tiles):
        tile = ref[:, LANES * t:LANES * (t + 1)]
        parts += [tile * halves[0], tile * halves[1]]
    return jnp.concatenate(parts, axis=0)


def _unstack_heads(ref, first_tile, n_tiles, lo, stacked):
    for i in range(n_tiles):
        a = stacked[2 * i * WINDOW:(2 * i + 1) * WINDOW]
        b = stacked[(2 * i + 1) * WINDOW:(2 * i + 2) * WINDOW]
        t = first_tile + i
        ref[:, LANES * t:LANES * (t + 1)] = jnp.where(lo, a, b).astype(ref.dtype)


def _attn_probs(qs, kd, scale, bias, valid, sink):
    sc = jnp.where(valid, _dot_nt(qs, kd) * scale + bias, NEG_INF)
    m = jnp.maximum(jnp.max(sc, axis=-1, keepdims=True), sink)
    p = jnp.exp(sc - m)
    es = jnp.exp(sink - m)
    inv = 1.0 / (jnp.sum(p, axis=-1, keepdims=True) + es)
    return p * inv, es * inv


def _attn_specs(n_heads, nb):
    QW = n_heads * HEAD_DIM
    kblk, vblk = QW // KV_DUP, QW // KV_DUP + 1
    last = nb - 1
    cur = lambda b: (lambda n: (jnp.minimum(n, last), b))
    prev = lambda b: (lambda n: (jnp.clip(n - 1, 0, last), b))
    kv = [pl.BlockSpec((WINDOW, KV_DUP), cur(kblk)), pl.BlockSpec((WINDOW, KV_DUP), prev(kblk)),
          pl.BlockSpec((WINDOW, KV_DUP), cur(vblk)), pl.BlockSpec((WINDOW, KV_DUP), prev(vblk))]
    return QW, cur, prev, kv


def _attn_fwd(proj, sinks, n_heads, comm=None):
    S = proj.shape[0]
    nb = S // WINDOW
    group = n_heads // N_KV_HEADS
    slopes = [2.0 ** (-8.0 * (h + 1) / n_heads) for h in range(n_heads)]
    scale = 1.0 / math.sqrt(HEAD_DIM)
    QW, cur, prev, kv_specs = _attn_specs(n_heads, nb)

    def body(sink_ref, q_ref, kc_ref, kp_ref, vc_ref, vp_ref, o_ref):
        n = pl.program_id(0)
        lo, halves = _lane_halves()
        for kh in range(N_KV_HEADS):
            heads = range(kh * group, (kh + 1) * group)
            valid, bias, sink = _attn_group(n, group, [slopes[h] for h in heads], [sink_ref[h] for h in heads])
            ksl = slice(LANES * kh, LANES * (kh + 1))
            kd = jnp.concatenate([kp_ref[:, ksl], kc_ref[:, ksl]], axis=0)
            vd = jnp.concatenate([vp_ref[:, ksl], vc_ref[:, ksl]], axis=0)
            qs = _stack_heads(q_ref, kh * group // 2, group // 2, halves)
            pn, _ = _attn_probs(qs, kd, scale, bias, valid, sink)
            _unstack_heads(o_ref, kh * group // 2, group // 2, lo, _dot(pn.astype(BF16), vd))

    return _pcall(
        body, (sinks, proj, proj, proj, proj, proj), grid=(nb,),
        in_specs=[pl.BlockSpec(memory_space=pltpu.SMEM), pl.BlockSpec((WINDOW, QW), cur(0))] + kv_specs,
        out_specs=[pl.BlockSpec((WINDOW, QW), cur(0))],
        out_shape=[_sds((S, QW), BF16)],
        sem=("parallel",), name="attn_fwd", comm=comm)


def _attn_bwd(proj, dmix, sinks, n_heads, comm=None):
    S = proj.shape[0]
    nb = S // WINDOW
    group = n_heads // N_KV_HEADS
    slopes = [2.0 ** (-8.0 * (h + 1) / n_heads) for h in range(n_heads)]
    scale = 1.0 / math.sqrt(HEAD_DIM)
    QW, cur, prev, kv_specs = _attn_specs(n_heads, nb)

    def body(sink_ref, q_ref, kc_ref, kp_ref, vc_ref, vp_ref, do_ref, dq_ref, dkv_ref, dsink_ref, carry_ref):
        n = pl.program_id(0)

        @pl.when(n == 0)
        def _():
            carry_ref[...] = jnp.zeros_like(carry_ref)
            dsink_ref[...] = jnp.zeros_like(dsink_ref)

        @pl.when(n < nb)
        def _():
            lo, halves = _lane_halves()
            lane1 = lax.broadcasted_iota(jnp.int32, (1, LANES), 1)
            dkd, dvd = [], []
            dsink = jnp.zeros((1, LANES), F32)
            for kh in range(N_KV_HEADS):
                heads = range(kh * group, (kh + 1) * group)
                valid, bias, sink = _attn_group(n, group, [slopes[h] for h in heads], [sink_ref[h] for h in heads])
                ksl = slice(LANES * kh, LANES * (kh + 1))
                kd = jnp.concatenate([kp_ref[:, ksl], kc_ref[:, ksl]], axis=0)
                vd = jnp.concatenate([vp_ref[:, ksl], vc_ref[:, ksl]], axis=0)
                qs = _stack_heads(q_ref, kh * group // 2, group // 2, halves)
                dos = _stack_heads(do_ref, kh * group // 2, group // 2, halves)
                pn, psink = _attn_probs(qs, kd, scale, bias, valid, sink)
                dp = _dot_nt(dos, vd)
                delta = jnp.sum(pn * dp, axis=-1, keepdims=True)
                dsb = (pn * (dp - delta) * scale).astype(BF16)
                sd = psink * delta
                for i, h in enumerate(heads):
                    dsink = dsink - jnp.where(lane1 == h, jnp.sum(sd[i * WINDOW:(i + 1) * WINDOW]), 0.0)
                _unstack_heads(dq_ref, kh * group // 2, group // 2, lo, _dot(dsb, kd))
                dkd.append(_dot_tn(dsb, qs))
                dvd.append(_dot_tn(pn.astype(BF16), dos))
            dsink_ref[...] += dsink
            both = jnp.concatenate(dkd + dvd, axis=1)
            dkv_ref[...] = (carry_ref[...] + both[0:WINDOW]).astype(BF16)
            carry_ref[...] = both[WINDOW:2 * WINDOW]

        @pl.when(n == nb)
        def _():
            dkv_ref[...] = carry_ref[...].astype(BF16)

    return _pcall(
        body, (sinks, proj, proj, proj, proj, proj, dmix), grid=(nb + 1,),
        in_specs=[pl.BlockSpec(memory_space=pltpu.SMEM), pl.BlockSpec((WINDOW, QW), cur(0))] + kv_specs
                 + [pl.BlockSpec((WINDOW, QW), cur(0))],
        out_specs=[pl.BlockSpec((WINDOW, QW), cur(0)),
                   pl.BlockSpec((WINDOW, 2 * KV_DUP), prev(0)),
                   pl.BlockSpec((1, LANES), lambda n: (0, 0))],
        out_shape=[_sds((S, QW), BF16), _sds((S, 2 * KV_DUP), BF16), _sds((1, LANES), F32)],
        scratch_shapes=[pltpu.VMEM((WINDOW, 2 * KV_DUP), F32)],
        sem=("arbitrary",), name="attn_bwd", comm=comm)


def _glu_window(a_ref, g_ref, ap_ref, gp_ref, win_ref, first):
    tm = a_ref.shape[0]
    zp = ap_ref[...].astype(F32) * _sigmoid(gp_ref[...].astype(F32))
    win_ref[0:HALO, :] = jnp.where(first, jnp.zeros_like(zp), zp)
    win_ref[HALO:HALO + tm, :] = a_ref[...].astype(F32) * _sigmoid(g_ref[...].astype(F32))


def _conv_fwd(proj, wdw, bdw, lng, lnb, n_heads, tm, comm=None):
    S = proj.shape[0]
    taps, C = wdw.shape
    ablk = (n_heads * HEAD_DIM + 2 * KV_DUP) // C
    hb = tm // HALO
    off = HALO - (taps - 1)

    def body(a_ref, g_ref, ap_ref, gp_ref, w_ref, b_ref, lg_ref, lb_ref, o_ref, y_ref, win_ref):
        _glu_window(a_ref, g_ref, ap_ref, gp_ref, win_ref, pl.program_id(0) == 0)
        for c in range(tm // ROWS):
            r0 = c * ROWS
            acc = jnp.zeros((ROWS, C), F32) + b_ref[...]
            for k in range(taps):
                acc = acc + w_ref[k:k + 1, :] * win_ref[r0 + off + k:r0 + off + k + ROWS, :]
            y_ref[r0:r0 + ROWS, :] = acc
        y = y_ref[...]
        mu = jnp.mean(y, axis=-1, keepdims=True)
        yc = y - mu
        yn = yc * lax.rsqrt(jnp.mean(yc * yc, axis=-1, keepdims=True) + RMS_EPS) * lg_ref[...] + lb_ref[...]
        o_ref[...] = (yn * _sigmoid(yn)).astype(BF16)

    vec = pl.BlockSpec((1, C), lambda i: (0, 0))
    halo = lambda b: pl.BlockSpec((HALO, C), lambda i: (jnp.maximum(i * hb - 1, 0), b))
    return _pcall(
        body, (proj, proj, proj, proj, wdw, bdw, lng, lnb,), grid=(S // tm,),
        in_specs=[pl.BlockSpec((tm, C), lambda i: (i, ablk)), pl.BlockSpec((tm, C), lambda i: (i, ablk + 1)),
                  halo(ablk), halo(ablk + 1),
                  pl.BlockSpec((taps, C), lambda i: (0, 0)), vec, vec, vec],
        out_specs=[pl.BlockSpec((tm, C), lambda i: (i, 0)), pl.BlockSpec((tm, C), lambda i: (i, 0))],
        out_shape=[_sds((S, C), BF16), _sds((S, C), F32)],
        scratch_shapes=[pltpu.VMEM((tm + HALO, C), F32)],
        sem=("parallel",), name="conv_fwd", comm=comm)


def _conv_bwd(proj, dmix, ysave, wdw, lng, lnb, n_heads, tm, comm=None):
    S = proj.shape[0]
    taps, C = wdw.shape
    QW = n_heads * HEAD_DIM
    ablk = (QW + 2 * KV_DUP) // C
    cblk = QW // C
    hb = tm // HALO
    nt = S // tm
    off = HALO - (taps - 1)

    def ln_bwd(dc, y, lg, lb):
        mu = jnp.mean(y, axis=-1, keepdims=True)
        yc = y - mu
        r = lax.rsqrt(jnp.mean(yc * yc, axis=-1, keepdims=True) + RMS_EPS)
        yh = yc * r
        yn = yh * lg + lb
        sg = _sigmoid(yn)
        dyn = dc * (sg * (1.0 + yn * (1.0 - sg)))
        dyh = dyn * lg
        dy = r * (dyh - jnp.mean(dyh, axis=-1, keepdims=True) - yh * jnp.mean(dyh * yh, axis=-1, keepdims=True))
        return dy, dyn, yh

    def body(dc_ref, dcn_ref, y_ref, yn_ref, a_ref, g_ref, ap_ref, gp_ref, w_ref, lg_ref, lb_ref,
             dag_ref, dw_ref, dvec_ref, zwin_ref, dyw_ref, dwacc_ref):
        i = pl.program_id(0)

        @pl.when(i == 0)
        def _():
            dwacc_ref[...] = jnp.zeros_like(dwacc_ref)
            dvec_ref[...] = jnp.zeros_like(dvec_ref)

        lg, lb = lg_ref[...], lb_ref[...]
        dy, dyn, yh = ln_bwd(dc_ref[...].astype(F32), y_ref[...], lg, lb)
        dy_next, _, _ = ln_bwd(dcn_ref[...].astype(F32), yn_ref[...], lg, lb)
        dyw_ref[0:tm, :] = dy
        dyw_ref[tm:tm + HALO, :] = jnp.where(i == nt - 1, jnp.zeros_like(dy_next), dy_next)
        dvec_ref[0:1, :] += jnp.sum(dy, axis=0, keepdims=True)
        dvec_ref[1:2, :] += jnp.sum(dyn * yh, axis=0, keepdims=True)
        dvec_ref[2:3, :] += jnp.sum(dyn, axis=0, keepdims=True)
        _glu_window(a_ref, g_ref, ap_ref, gp_ref, zwin_ref, i == 0)

        for c in range(tm // ROWS):
            r0 = c * ROWS
            dz = jnp.zeros((ROWS, C), F32)
            dyc = dyw_ref[r0:r0 + ROWS, :]
            for k in range(taps):
                dz = dz + w_ref[k:k + 1, :] * dyw_ref[r0 + taps - 1 - k:r0 + taps - 1 - k + ROWS, :]
                prod = dyc * zwin_ref[r0 + off + k:r0 + off + k + ROWS, :]
                dwacc_ref[k] += jnp.sum(prod.reshape(ROWS // 8, 8, C), axis=0)
            a = a_ref[r0:r0 + ROWS, :].astype(F32)
            s = _sigmoid(g_ref[r0:r0 + ROWS, :].astype(F32))
            dag_ref[r0:r0 + ROWS, 0:C] = (dz * s).astype(BF16)
            dag_ref[r0:r0 + ROWS, C:2 * C] = (dz * a * s * (1.0 - s)).astype(BF16)

        @pl.when(i == nt - 1)
        def _():
            dw_ref[...] = jnp.zeros_like(dw_ref)
            for k in range(taps):
                dw_ref[k:k + 1, :] = jnp.sum(dwacc_ref[k], axis=0, keepdims=True)

    vec = pl.BlockSpec((1, C), lambda i: (0, 0))
    tile = lambda b: pl.BlockSpec((tm, C), lambda i: (i, b))
    prev = lambda b: pl.BlockSpec((HALO, C), lambda i: (jnp.maximum(i * hb - 1, 0), b))
    nxt = lambda b: pl.BlockSpec((HALO, C), lambda i: (jnp.minimum((i + 1) * hb, S // HALO - 1), b))
    return _pcall(
        body, (dmix, dmix, ysave, ysave, proj, proj, proj, proj, wdw, lng, lnb,), grid=(nt,),
        in_specs=[tile(cblk), nxt(cblk), tile(0), nxt(0), tile(ablk), tile(ablk + 1), prev(ablk), prev(ablk + 1),
                  pl.BlockSpec((taps, C), lambda i: (0, 0)), vec, vec],
        out_specs=[pl.BlockSpec((tm, 2 * C), lambda i: (i, 0)),
                   pl.BlockSpec((HALO, C), lambda i: (0, 0)),
                   pl.BlockSpec((8, C), lambda i: (0, 0))],
        out_shape=[_sds((S, 2 * C), BF16), _sds((HALO, C), F32), _sds((8, C), F32)],
        scratch_shapes=[pltpu.VMEM((tm + HALO, C), F32), pltpu.VMEM((tm + HALO, C), F32),
                        pltpu.VMEM((taps, 8, C), F32)],
        sem=("arbitrary",), name="conv_bwd", comm=comm)


def _mixout_fwd(x, attn, conv, wo, tm, comm=None):
    S, D = x.shape
    QW, C = attn.shape[1], conv.shape[1]

    def body(x_ref, a_ref, c_ref, w_ref, o_ref):
        o_ref[...] = x_ref[...] + _dot(a_ref[...], w_ref[0:QW, :]) + _dot(c_ref[...], w_ref[QW:QW + C, :])

    return _pcall(
        body, (x, attn, conv, wo,), grid=(S // tm,),
        in_specs=[pl.BlockSpec((tm, D), lambda i: (i, 0)),
                  pl.BlockSpec((tm, QW), lambda i: (i, 0)),
                  pl.BlockSpec((tm, C), lambda i: (i, 0)),
                  pl.BlockSpec((QW + C, D), lambda i: (0, 0))],
        out_specs=[pl.BlockSpec((tm, D), lambda i: (i, 0))],
        out_shape=[_sds((S, D), F32)],
        sem=("parallel",), name="mixout_fwd", comm=comm)


def _mixout_bwd(dxo, attn, conv, wo, tm, comm=None):
    S, D = dxo.shape
    QW, C = attn.shape[1], conv.shape[1]
    nt = S // tm

    def body(dx_ref, a_ref, c_ref, w_ref, dm_ref, dw_ref, acc_ref):
        i = pl.program_id(0)
        dxb = dx_ref[...].astype(BF16)
        dm_ref[...] = _dot_nt(dxb, w_ref[...]).astype(BF16)

        @pl.when(i == 0)
        def _():
            acc_ref[...] = jnp.zeros_like(acc_ref)

        acc_ref[0:QW, :] += _dot_tn(a_ref[...], dxb)
        acc_ref[QW:QW + C, :] += _dot_tn(c_ref[...], dxb)

        @pl.when(i == nt - 1)
        def _():
            dw_ref[...] = acc_ref[...].astype(BF16)

    return _pcall(
        body, (dxo, attn, conv, wo,), grid=(nt,),
        in_specs=[pl.BlockSpec((tm, D), lambda i: (i, 0)),
                  pl.BlockSpec((tm, QW), lambda i: (i, 0)),
                  pl.BlockSpec((tm, C), lambda i: (i, 0)),
                  pl.BlockSpec((QW + C, D), lambda i: (0, 0))],
        out_specs=[pl.BlockSpec((tm, QW + C), lambda i: (i, 0)),
                   pl.BlockSpec((QW + C, D), lambda i: (0, 0))],
        out_shape=[_sds((S, QW + C), BF16), _sds((QW + C, D), BF16)],
        scratch_shapes=[pltpu.VMEM((QW + C, D), F32)],
        sem=("arbitrary",), name="mixout_bwd", comm=comm)


def _loss_head(x, gain, target, tm, comm=None):
    S, D = x.shape
    nt = S // tm

    def body(x_ref, g_ref, t_ref, dx_ref, loss_ref, dgain_ref):
        i = pl.program_id(0)
        xh, r = _rms(x_ref[...])
        e = xh * g_ref[...] - t_ref[...]
        loss_ref[...] = jnp.zeros((1, LANES), F32) + 0.5 * jnp.sum(jnp.mean(e * e, axis=-1, keepdims=True))
        dxn, dgn = _rms_bwd(e * (1.0 / D), xh, r, g_ref[...])
        dx_ref[...] = dxn

        @pl.when(i == 0)
        def _():
            dgain_ref[...] = dgn

        @pl.when(i > 0)
        def _():
            dgain_ref[...] += dgn

    return _pcall(
        body, (x, gain, target,), grid=(nt,),
        in_specs=[pl.BlockSpec((tm, D), lambda i: (i, 0)),
                  pl.BlockSpec((1, D), lambda i: (0, 0)),
                  pl.BlockSpec((tm, D), lambda i: (i, 0))],
        out_specs=[pl.BlockSpec((tm, D), lambda i: (i, 0)),
                   pl.BlockSpec((None, 1, LANES), lambda i: (i, 0, 0)),
                   pl.BlockSpec((1, D), lambda i: (0, 0))],
        out_shape=[_sds((S, D), F32), _sds((nt, 1, LANES), F32), _sds((1, D), F32)],
        sem=("arbitrary",), name="loss_head", comm=comm)


def _adam(w, m, v, parts, name):
    L, R, C = w.shape
    P = parts[0].shape[0]
    br = _row_block(R, 256)
    c1 = 1.0 - ADAM_B1 ** ADAM_STEP
    c2 = 1.0 - ADAM_B2 ** ADAM_STEP

    def body(w_ref, m_ref, v_ref, *rest):
        p_refs, (g_ref, d_ref, mo_ref, vo_ref) = rest[:L], rest[L:]
        layer = pl.program_id(0)

        def update(p_ref):
            g = p_ref[0].astype(F32)
            for k in range(1, P):
                g = g + p_ref[k].astype(F32)
            mn = ADAM_B1 * m_ref[...] + (1.0 - ADAM_B1) * g
            vn = ADAM_B2 * v_ref[...] + (1.0 - ADAM_B2) * (g * g)
            g_ref[...] = g
            mo_ref[...] = mn
            vo_ref[...] = vn
            d_ref[...] = -ADAM_LR * ((mn / c1) / (jnp.sqrt(vn / c2) + ADAM_EPS) + ADAM_WD * w_ref[...])

        for k in range(L):
            pl.when(layer == k)(functools.partial(update, p_refs[k]))

    blk = pl.BlockSpec((None, br, C), lambda l, i: (l, i, 0))
    part = lambda k: pl.BlockSpec((P, br, C), lambda l, i: (0, jnp.where(l == k, i, 0), 0))
    return _pcall(
        body, (w, m, v, *parts), grid=(L, R // br),
        in_specs=[blk, blk, blk] + [part(k) for k in range(L)],
        out_specs=[blk, blk, blk, blk],
        out_shape=[_sds((L, R, C), F32)] * 4,
        sem=("parallel", "parallel"), name=name)


def _sum_parts(parts):
    P, R, C = parts.shape

    def body(p_ref, o_ref):
        g = p_ref[0]
        for k in range(1, P):
            g = g + p_ref[k]
        o_ref[...] = g

    vmem = pl.BlockSpec(memory_space=pltpu.VMEM)
    return _pcall(body, (parts,), in_specs=[vmem], out_specs=[vmem], out_shape=[_sds((R, C), F32)],
                  name="sum_parts")[0]


def _place():
    x, y, c = lax.axis_index("x"), lax.axis_index("y"), lax.axis_index("c")
    return x, y, c, [(1 - x, y), (x, 1 - y), (1 - x, 1 - y)]


def _dev(px, py, pc):
    return 4 * px + 2 * py + pc


def _gather_comm(shards, fulls, slot_of):
    n = len(shards)

    def copies(srcs, outs, send_sems, recv_sems):
        x, y, c, chips = _place()

        def copy(a, k, block, to, from_shard=False):
            dst = slot_of[a](outs[a], _dev(*block))
            return pltpu.make_async_remote_copy(
                src_ref=srcs[a] if from_shard else dst, dst_ref=dst,
                send_sem=send_sems.at[a, k], recv_sem=recv_sems.at[a, k], device_id=to, device_id_type=MESH)

        return copy, (x, y, c), (x, y, 1 - c), chips

    def local(srcs, outs, local_sems):
        x, y, c, _ = _place()
        return [pltpu.make_async_copy(srcs[a], slot_of[a](outs[a], _dev(x, y, c)), local_sems.at[a])
                for a in range(n)]

    def first_copies(copy, me, sibling, chips):
        out = []
        for a in range(n):
            out.append(copy(a, 0, me, sibling, True))
            out += [copy(a, 1 + j, me, (*chip, me[2]), True) for j, chip in enumerate(chips)]
        return out

    def start(srcs, outs, sems):
        send_sems, recv_sems, local_sems = sems
        copy, me, sibling, chips = copies(srcs, outs, send_sems, recv_sems)
        for cp in local(srcs, outs, local_sems):
            cp.start()
        for cp in first_copies(copy, me, sibling, chips):
            cp.start()

    def finish(srcs, outs, sems):
        send_sems, recv_sems, local_sems = sems
        copy, me, sibling, chips = copies(srcs, outs, send_sems, recv_sems)
        c = me[2]
        passed = []
        for j, chip in enumerate(chips):
            for a in range(n):
                copy(a, 1 + j, (*chip, c), me).wait_recv()
                fwd = copy(a, 4 + j, (*chip, c), sibling)
                fwd.start()
                passed.append(fwd)
        for a in range(n):
            copy(a, 0, sibling, me).wait_recv()
            for j, chip in enumerate(chips):
                copy(a, 4 + j, (*chip, 1 - c), me).wait_recv()
        for cp in first_copies(copy, me, sibling, chips) + passed:
            cp.wait_send()
        for cp in local(srcs, outs, local_sems):
            cp.wait()

    sems = [pltpu.SemaphoreType.DMA((n, 7)), pltpu.SemaphoreType.DMA((n, 7)), pltpu.SemaphoreType.DMA((n,))]
    return _Comm(shards, fulls, sems, start, finish)


def _swap_comm(grads):
    n = len(grads)

    def copies(srcs, outs, sems):
        x, y, c, _ = _place()
        return [pltpu.make_async_remote_copy(
            src_ref=srcs[a].at[:, pl.ds(1 - c, 1)], dst_ref=outs[a],
            send_sem=sems[0].at[a], recv_sem=sems[1].at[a], device_id=(x, y, 1 - c), device_id_type=MESH)
            for a in range(n)]

    def start(srcs, outs, sems):
        for cp in copies(srcs, outs, sems):
            cp.start()

    def finish(srcs, outs, sems):
        for cp in copies(srcs, outs, sems):
            cp.wait()

    return _Comm(grads, [_sds((N_CHIP, 1) + g.shape[2:], g.dtype) for g in grads],
                 [pltpu.SemaphoreType.DMA((n,)), pltpu.SemaphoreType.DMA((n,))], start, finish)


def _exchange_comm(parts):
    n = len(parts)

    def copies(srcs, outs, sems):
        x, y, c, chips = _place()
        mine = 2 * x + y
        loc = [pltpu.make_async_copy(srcs[a].at[pl.ds(mine, 1)], outs[a].at[pl.ds(mine, 1)], sems[2].at[a])
               for a in range(n)]
        rem = [pltpu.make_async_remote_copy(
            src_ref=srcs[a].at[pl.ds(2 * px + py, 1)], dst_ref=outs[a].at[pl.ds(mine, 1)],
            send_sem=sems[0].at[a, j], recv_sem=sems[1].at[a, j], device_id=(px, py, c), device_id_type=MESH)
            for a in range(n) for j, (px, py) in enumerate(chips)]
        return loc + rem

    def start(srcs, outs, sems):
        for cp in copies(srcs, outs, sems):
            cp.start()

    def finish(srcs, outs, sems):
        for cp in copies(srcs, outs, sems):
            cp.wait()

    return _Comm(parts, [_sds(p.shape, p.dtype) for p in parts],
                 [pltpu.SemaphoreType.DMA((n, 3)), pltpu.SemaphoreType.DMA((n, 3)), pltpu.SemaphoreType.DMA((n,))],
                 start, finish)


FWD_KERNELS = [("ffn1", ("a1", "b1"), 26), ("mixin", ("wi",), 7), ("attn", (), 7), ("conv", (), 15), ("mixout", ("wo",), 6),
               ("ffn2", ("a2", "b2"), 26)]
GATHER_SHARE = 70


def _plan_gathers(n_layers, shard_bytes):
    per_layer = sum(shard_bytes.values())
    cost = {n: GATHER_SHARE * b / per_layer for n, b in shard_bytes.items()}
    room = {len(FWD_KERNELS) * l + i: k[2] for l in range(n_layers) for i, k in enumerate(FWD_KERNELS)}
    first, plan = [], {}
    for l in range(n_layers):
        for i, (_, needs, _) in enumerate(FWD_KERNELS):
            due = len(FWD_KERNELS) * l + i
            for name in needs:
                if due == 0:
                    first.append((name, l))
                    continue
                fits = [k for k in range(due) if room[k] >= cost[name]]
                k = fits[0] if fits else max(range(due), key=lambda k: room[k])
                room[k] -= cost[name]
                plan.setdefault(k, []).append((name, l))
    return first, plan


def _comm_only(comm, name):
    return _pcall(lambda: None, (), in_specs=[], out_specs=[], out_shape=[], name=name, comm=comm)


def _gather_small(v):
    R, C = v.shape

    def body(x_ref, out_ref, send_sems, recv_sems, local_sem):
        x, y, c, chips = _place()
        me, sibling = (x, y, c), (x, y, 1 - c)

        def copy(k, block, to, from_shard=False):
            dst = out_ref.at[_dev(*block)]
            return pltpu.make_async_remote_copy(
                src_ref=x_ref if from_shard else dst, dst_ref=dst,
                send_sem=send_sems.at[k], recv_sem=recv_sems.at[k], device_id=to, device_id_type=MESH)

        mine = pltpu.make_async_copy(x_ref, out_ref.at[_dev(*me)], local_sem)
        mine.start()
        first = [copy(0, me, sibling, True)] + [copy(1 + j, me, (*chip, c), True) for j, chip in enumerate(chips)]
        for cp in first:
            cp.start()
        passed = [copy(4 + j, (*chip, c), sibling) for j, chip in enumerate(chips)]
        for j, chip in enumerate(chips):
            copy(1 + j, (*chip, c), me).wait_recv()
            passed[j].start()
        copy(0, sibling, me).wait_recv()
        for j, chip in enumerate(chips):
            copy(4 + j, (*chip, 1 - c), me).wait_recv()
        for cp in first + passed:
            cp.wait_send()
        mine.wait()

    vmem = pl.BlockSpec(memory_space=pltpu.VMEM)
    return _pcall(
        body, (v,), in_specs=[vmem], out_specs=[vmem], out_shape=[_sds((N_DEV, R, C), F32)],
        scratch_shapes=[pltpu.SemaphoreType.DMA((7,)), pltpu.SemaphoreType.DMA((7,)), pltpu.SemaphoreType.DMA],
        name="gather_small")[0]


def _add_sibling(core, g, r):
    _, _, R, C = g.shape
    br = _row_block(R)

    def body(c_ref, g_ref, r_ref, o_ref):
        o_ref[...] = (g_ref[...].astype(F32) + r_ref[...].astype(F32)).astype(BF16)

    return _pcall(
        body, (core, g, r),
        grid_spec=pltpu.PrefetchScalarGridSpec(
            num_scalar_prefetch=1, grid=(N_CHIP, R // br),
            in_specs=[pl.BlockSpec((None, None, br, C), lambda k, i, c_ref: (k, c_ref[0], i, 0)),
                      pl.BlockSpec((None, None, br, C), lambda k, i, c_ref: (k, 0, i, 0))],
            out_specs=pl.BlockSpec((None, br, C), lambda k, i, c_ref: (k, i, 0))),
        out_shape=_sds((N_CHIP, R, C), BF16), sem=("parallel", "parallel"), name="rs_add_sibling")


def _by_chip(g):
    return g.reshape((N_CHIP, 2) + g.shape[1:])


def _pack_rows(vecs):
    flat = jnp.concatenate([v.reshape(-1).astype(F32) for v in vecs])
    rows = -(-flat.shape[0] // (8 * LANES)) * 8
    return jnp.pad(flat, (0, rows * LANES - flat.shape[0])).reshape(rows, LANES)


def _unpack_rows(rows, shapes):
    flat = rows.reshape(-1)
    out, o = [], 0
    for s in shapes:
        n = math.prod(s)
        out.append(flat[o:o + n].reshape(s))
        o += n
    return out


def _adam_any(w, m, v, g, name):
    shape = w.shape
    one = lambda t: t.reshape(1, -1, shape[-1])
    return tuple(t.reshape(shape) for t in _adam(one(w), one(m), one(v), [one(g)], name))


def kernel(x, norm_ffn1, w_ffn1_in, w_ffn1_out, norm_mix, w_in, sinks, w_dw, b_dw, conv_ln_g, conv_ln_b, w_out, norm_ffn2, w_ffn2_in, w_ffn2_out, final_norm, loss_target, m_norm_ffn1, m_w_ffn1_in, m_w_ffn1_out, m_norm_mix, m_w_in, m_sinks, m_w_dw, m_b_dw, m_conv_ln_g, m_conv_ln_b, m_w_out, m_norm_ffn2, m_w_ffn2_in, m_w_ffn2_out, m_final_norm, v_norm_ffn1, v_w_ffn1_in, v_w_ffn1_out, v_norm_mix, v_w_in, v_sinks, v_w_dw, v_b_dw, v_conv_ln_g, v_conv_ln_b, v_w_out, v_norm_ffn2, v_w_ffn2_in, v_w_ffn2_out, v_final_norm):
    _, S, D = x.shape
    L = norm_ffn1.shape[0]
    NF = w_ffn1_in.shape[2]
    RF = w_ffn1_out.shape[1]
    NW = w_in.shape[2]
    RO = w_out.shape[1]
    taps, CD = w_dw.shape[1], w_dw.shape[2]
    H = sinks.shape[1]
    C = N_DEV * CD
    QW = H * HEAD_DIM
    KVW = N_KV_HEADS * HEAD_DIM
    assert 2 * RF == NF and QW + C == N_DEV * RO and N_DEV * NW == QW + 2 * KVW + 2 * C
    tm = min(512, S)
    ta = min(256, S)
    tc = min(256, S)
    core = lax.axis_index("c").astype(jnp.int32).reshape(1)

    x0 = x[0]
    target = loss_target[0]

    wdw_all = _gather_small(_pack_rows([w_dw]))
    n_dw = L * taps * CD
    wdw_full = jnp.stack([wdw_all[d].reshape(-1)[:n_dw].reshape(L, taps, CD) for d in range(N_DEV)],
                         axis=2).reshape(L, taps, C)

    def shard(name, l):
        return {"a1": lambda: w_ffn1_in[l].T.astype(BF16), "a2": lambda: w_ffn2_in[l].T.astype(BF16),
                "b1": lambda: w_ffn1_out[l].astype(BF16), "b2": lambda: w_ffn2_out[l].astype(BF16),
                "wi": lambda: w_in[l].T.astype(BF16), "wo": lambda: w_out[l].astype(BF16)}[name]()

    rows_of = {"a1": NF, "a2": NF, "b1": RF, "b2": RF, "wi": NW, "wo": RO}
    full_of = {n: _sds((N_DEV * r, D), BF16) for n, r in rows_of.items()}
    slot_of = {n: (lambda ref, b, r=r: ref.at[pl.ds(b * r, r)]) for n, r in rows_of.items()}
    first, plan = _plan_gathers(L, {n: r * D * 2 for n, r in rows_of.items()})
    got = {}

    def gather(items):
        if not items:
            return None
        return _gather_comm([shard(n, l) for n, l in items], [full_of[n] for n, _ in items],
                            [slot_of[n] for n, _ in items])

    def carrying(k, call, n_own):
        items = plan.get(k, [])
        res = call(gather(items))
        got.update(zip(items, res[n_own:]))
        return res[:n_own]

    def wext_of(wi):
        q, k, v, u = wi[:QW], wi[QW:QW + KVW], wi[QW + KVW:QW + 2 * KVW], wi[QW + 2 * KVW:]
        dup = lambda t: jnp.concatenate(
            [t[HEAD_DIM * (i // 2):HEAD_DIM * (i // 2 + 1)] for i in range(2 * N_KV_HEADS)], axis=0)
        return jnp.concatenate([q, dup(k), dup(v), u], axis=0)

    got.update(zip(first, _comm_only(gather(first), "ag_first")))

    saved = []
    xc = x0
    for l in range(L):
        k0 = len(FWD_KERNELS) * l
        g1, gm, g2 = norm_ffn1[l][None], norm_mix[l][None], norm_ffn2[l][None]
        x1, gu1 = carrying(k0, lambda c: _ffn_fwd(xc, g1, got["a1", l], got["b1", l], tm, c), 2)
        wext = wext_of(got["wi", l])
        proj, = carrying(k0 + 1, lambda c: _mixin_fwd(x1, gm, wext, tm, c), 1)
        attn, = carrying(k0 + 2, lambda c: _attn_fwd(proj, sinks[l], H, c), 1)
        conv, ysave = carrying(k0 + 3, lambda c: _conv_fwd(proj, wdw_full[l], b_dw[l][None], conv_ln_g[l][None],
                                                           conv_ln_b[l][None], H, tc, c), 2)
        x2, = carrying(k0 + 4, lambda c: _mixout_fwd(x1, attn, conv, got["wo", l], tm, c), 1)
        x3, gu2 = carrying(k0 + 5, lambda c: _ffn_fwd(x2, g2, got["a2", l], got["b2", l], tm, c), 2)
        W = dict(a1=got["a1", l], b1=got["b1", l], a2=got["a2", l], b2=got["b2", l], wo=got["wo", l], wext=wext)
        saved.append(dict(W=W, x0=xc, x1=x1, x2=x2, gu1=gu1, gu2=gu2, proj=proj, attn=attn, conv=conv, ysave=ysave))
        xc = x3

    dx, loss_parts, dfinal = _loss_head(xc, final_norm[None], target, tm)
    loss = lax.psum(jnp.sum(loss_parts[:, 0, 0]), ("x", "y", "c"))

    def swap(gs):
        return _swap_comm([_by_chip(g) for g in gs]) if gs else None

    def added(gs, gots):
        return [_add_sibling(core, _by_chip(g), r) for g, r in zip(gs, gots)]

    small = [None] * L
    big = [dict() for _ in range(L)]
    carry = None
    for l in reversed(range(L)):
        sv = saved[l]
        W = sv["W"]
        g1, gm, g2 = norm_ffn1[l][None], norm_mix[l][None], norm_ffn2[l][None]
        fold = lambda t: jnp.concatenate(
            [t[2 * HEAD_DIM * i:2 * HEAD_DIM * i + HEAD_DIM] + t[2 * HEAD_DIM * i + HEAD_DIM:2 * HEAD_DIM * (i + 1)]
             for i in range(N_KV_HEADS)], axis=0)

        r = _ffn_bwd_a(sv["x2"], g2, dx, sv["gu2"], W["a2"], W["b2"], ta, swap(carry[1]) if carry else None)
        dx2, dgu2, dg2, h2, dys2 = r[:5]
        p_carry = added(carry[1], r[5:]) if carry else None
        r = _ffn_bwd_w(h2, dys2, sv["gu2"], dgu2, tm, _exchange_comm(p_carry) if carry else None)
        da2, db2 = r[0].reshape(N_DEV, NF, D), r[1].reshape(N_DEV, RF, D)
        if carry:
            big[carry[0]]["a1"], big[carry[0]]["b1"] = r[2:]
        r = _mixout_bwd(dx2, sv["attn"], sv["conv"], W["wo"], tm, swap([da2, db2]))
        dmix, dwo = r[0], r[1].reshape(N_DEV, RO, D)
        p_a2, p_b2 = added([da2, db2], r[2:])
        r = _conv_bwd(sv["proj"], dmix, sv["ysave"], wdw_full[l], conv_ln_g[l][None], conv_ln_b[l][None], H, tc,
                      _join(_exchange_comm([p_b2]), swap([dwo])))
        dag, dwdw, dvec = r[:3]
        big[l]["b2"] = r[3]
        p_wo, = added([dwo], r[4:])
        r = _attn_bwd(sv["proj"], dmix, sinks[l], H, _exchange_comm([p_a2]))
        dq, dkv, dsink = r[:3]
        big[l]["a2"] = r[3]
        r = _mixin_bwd(sv["x1"], gm, dx2, dq, dkv, dag, W["wext"], tm, _exchange_comm([p_wo]))
        dx1, dgm, dwext = r[:3]
        big[l]["wo"] = r[3]
        dwi = jnp.concatenate([dwext[:QW], fold(dwext[QW:QW + KV_DUP]),
                               fold(dwext[QW + KV_DUP:QW + 2 * KV_DUP]), dwext[QW + 2 * KV_DUP:]], axis=0)
        dwi = dwi.astype(BF16).reshape(N_DEV, NW, D)
        r = _ffn_bwd_a(sv["x0"], g1, dx1, sv["gu1"], W["a1"], W["b1"], ta, swap([dwi]))
        dx0, dgu1, dg1, h1, dys1 = r[:5]
        p_wi, = added([dwi], r[5:])
        r = _ffn_bwd_w(h1, dys1, sv["gu1"], dgu1, tm, _exchange_comm([p_wi]))
        carry = (l, [r[0].reshape(N_DEV, NF, D), r[1].reshape(N_DEV, RF, D)])
        big[l]["wi"] = r[2]
        dx = dx0
        small[l] = [dg1[0], dgm[0], dsink[0, :H], dwdw[:taps], dvec[0], dvec[1], dvec[2], dg2[0]]

    p_carry = added(carry[1], _comm_only(swap(carry[1]), "rs_swap_last"))
    big[carry[0]]["a1"], big[carry[0]]["b1"] = _comm_only(_exchange_comm(p_carry), "rs_exchange_last")
    grad_x = dx[None]

    small_shapes = [(D,), (D,), (H,), (taps, C), (C,), (C,), (C,), (D,)]
    packed = _pack_rows([t for l in range(L) for t in small[l]] + [dfinal[0]])
    total = _sum_parts(_gather_small(packed))
    flat = _unpack_rows(total, small_shapes * L + [(D,)])
    per = [jnp.stack([flat[l * len(small_shapes) + i] for l in range(L)]) for i in range(len(small_shapes))]
    g_nf1, g_nmix, g_sinks, g_wdw_full, g_bdw, g_lng, g_lnb, g_nf2 = per
    g_final = flat[-1]
    dev = _dev(lax.axis_index("x"), lax.axis_index("y"), lax.axis_index("c"))
    g_wdw = lax.dynamic_slice_in_dim(g_wdw_full, dev * CD, CD, axis=2)

    res = {}
    res["norm_ffn1"] = _adam_any(norm_ffn1, m_norm_ffn1, v_norm_ffn1, g_nf1, "adam_small")
    res["norm_mix"] = _adam_any(norm_mix, m_norm_mix, v_norm_mix, g_nmix, "adam_small")
    res["sinks"] = _adam_any(sinks, m_sinks, v_sinks, g_sinks, "adam_small")
    res["w_dw"] = _adam_any(w_dw, m_w_dw, v_w_dw, g_wdw, "adam_small")
    res["b_dw"] = _adam_any(b_dw, m_b_dw, v_b_dw, g_bdw, "adam_small")
    res["conv_ln_g"] = _adam_any(conv_ln_g, m_conv_ln_g, v_conv_ln_g, g_lng, "adam_small")
    res["conv_ln_b"] = _adam_any(conv_ln_b, m_conv_ln_b, v_conv_ln_b, g_lnb, "adam_small")
    res["norm_ffn2"] = _adam_any(norm_ffn2, m_norm_ffn2, v_norm_ffn2, g_nf2, "adam_small")
    res["final_norm"] = tuple(t[0] for t in _adam_any(final_norm[None], m_final_norm[None], v_final_norm[None],
                                                      g_final[None], "adam_small"))

    def adam_big(key, w, m, v, name, transposed=False):
        t = (lambda a: a.transpose(0, 2, 1)) if transposed else (lambda a: a)
        return tuple(t(o) for o in _adam(t(w), t(m), t(v), [big[l][key] for l in range(L)], name))

    res["w_ffn1_in"] = adam_big("a1", w_ffn1_in, m_w_ffn1_in, v_w_ffn1_in, "adam_ffn_in", True)
    res["w_ffn1_out"] = adam_big("b1", w_ffn1_out, m_w_ffn1_out, v_w_ffn1_out, "adam_ffn_out")
    res["w_ffn2_in"] = adam_big("a2", w_ffn2_in, m_w_ffn2_in, v_w_ffn2_in, "adam_ffn_in", True)
    res["w_ffn2_out"] = adam_big("b2", w_ffn2_out, m_w_ffn2_out, v_w_ffn2_out, "adam_ffn_out")
    res["w_in"] = adam_big("wi", w_in, m_w_in, v_w_in, "adam_w_in", True)
    res["w_out"] = adam_big("wo", w_out, m_w_out, v_w_out, "adam_w_out")

    order = ["norm_ffn1", "w_ffn1_in", "w_ffn1_out", "norm_mix", "w_in", "sinks", "w_dw", "b_dw", "conv_ln_g",
             "conv_ln_b", "w_out", "norm_ffn2", "w_ffn2_in", "w_ffn2_out", "final_norm"]
    return (loss, grad_x, *[res[n][0] for n in order], *[res[n][1] for n in order],
            *[res[n][2] for n in order], *[res[n][3] for n in order])
```

```python
import functools
import math

import jax
import jax.numpy as jnp
from jax import lax
from jax.experimental import pallas as pl
from jax.experimental.pallas import tpu as pltpu

F32 = jnp.float32
BF16 = jnp.bfloat16
MESH = pl.DeviceIdType.MESH

N_DEV = 8
N_CHIP = 4
HEAD_DIM = 64
N_KV_HEADS = 2
WINDOW = 128
KV_DUP = 2 * HEAD_DIM * N_KV_HEADS
RMS_EPS = 1e-6
NEG_INF = -1e30
FFN_RES = 0.5
HALO = 32
ROWS = 32
FFN_CHUNK = 512
BWD_W_CHUNK = 384
LANES = 128
SUBLANES = 8
V7X_VMEM_LIMIT = 56 * 1024 * 1024

ADAM_LR = 0.001
ADAM_B1 = 0.9
ADAM_B2 = 0.999
ADAM_EPS = 1e-08
ADAM_WD = 0.01
ADAM_STEP = 10


def _raw_call(body, **kw):
    return pl.pallas_call(body, **kw)


class _Comm:
    def __init__(self, ins, outs, sems, start, finish):
        self.ins, self.outs, self.sems, self.start, self.finish = list(ins), list(outs), list(sems), start, finish


def _join(*comms):
    comms = [c for c in comms if c is not None]
    if not comms:
        return None

    def split(refs, attr):
        out, o = [], 0
        for c in comms:
            n = len(getattr(c, attr))
            out.append(refs[o:o + n])
            o += n
        return out

    def run(which):
        def go(ins, outs, sems):
            for c, i, o, m in zip(comms, split(ins, "ins"), split(outs, "outs"), split(sems, "sems")):
                getattr(c, which)(i, o, m)
        return go

    return _Comm(sum((c.ins for c in comms), []), sum((c.outs for c in comms), []),
                 sum((c.sems for c in comms), []), run("start"), run("finish"))


def _pcall(body, args, *, name, out_shape, grid=(), in_specs=None, out_specs=None, scratch_shapes=(), sem=(),
           comm=None, grid_spec=None):
    if grid_spec is not None:
        return _raw_call(body, grid_spec=grid_spec, out_shape=out_shape, name=name,
                         compiler_params=_params(*sem))(*args)
    if comm is None:
        return _raw_call(body, grid=grid, in_specs=in_specs, out_specs=out_specs, out_shape=out_shape,
                         scratch_shapes=list(scratch_shapes), name=name, compiler_params=_params(*sem))(*args)
    n_in, n_out, n_scr = len(in_specs), len(out_shape), len(scratch_shapes)
    ci, co = len(comm.ins), len(comm.outs)

    def fused(*refs):
        cuts = [n_in, ci, n_out, co, n_scr]
        parts, o = [], 0
        for n in cuts:
            parts.append(refs[o:o + n])
            o += n
        ins, cins, outs, couts, scr = parts
        csems = refs[o:]
        if not grid:
            comm.start(cins, couts, csems)
            body(*ins, *outs, *scr)
            comm.finish(cins, couts, csems)
            return
        ids = [pl.program_id(a) for a in range(len(grid))]
        first = functools.reduce(jnp.logical_and, [i == 0 for i in ids])
        last = functools.reduce(jnp.logical_and, [i == g - 1 for i, g in zip(ids, grid)])

        @pl.when(first)
        def _():
            comm.start(cins, couts, csems)

        body(*ins, *outs, *scr)

        @pl.when(last)
        def _():
            comm.finish(cins, couts, csems)

    return _raw_call(
        fused, grid=grid, in_specs=list(in_specs) + [ANY] * ci, out_specs=list(out_specs) + [ANY] * co,
        out_shape=list(out_shape) + comm.outs, scratch_shapes=list(scratch_shapes) + comm.sems, name=name,
        compiler_params=_params(*(["arbitrary"] * len(grid))))(*args, *comm.ins)


ANY = pl.BlockSpec(memory_space=pl.ANY)


def _params(*sem):
    return pltpu.CompilerParams(dimension_semantics=sem, vmem_limit_bytes=V7X_VMEM_LIMIT)


def _dot(a, b):
    return jnp.dot(a, b, preferred_element_type=F32)


def _dot_nt(a, b):
    return lax.dot_general(a, b, (((1,), (1,)), ((), ())), preferred_element_type=F32)


def _dot_tn(a, b):
    return lax.dot_general(a, b, (((0,), (0,)), ((), ())), preferred_element_type=F32)


def _sigmoid(x):
    return 1.0 / (1.0 + jnp.exp(-x))


def _rms(x):
    r = lax.rsqrt(jnp.mean(x * x, axis=-1, keepdims=True) + RMS_EPS)
    return x * r, r


def _rms_bwd(dh, xh, r, g):
    dxh = dh * g
    dx = r * (dxh - xh * jnp.mean(dxh * xh, axis=-1, keepdims=True))
    return dx, jnp.sum(dh * xh, axis=0, keepdims=True)


def _sds(shape, dtype):
    return jax.ShapeDtypeStruct(shape, dtype)


def _row_block(rows, limit=512):
    fits = [d for d in range(16, min(rows, limit) + 1, 16) if rows % d == 0]
    return fits[-1] if fits else rows


def _chunks(n, step):
    return [(o, min(step, n - o)) for o in range(0, n, step)]


def _resident(shape):
    return pl.BlockSpec(shape, lambda *_: (0,) * len(shape), pipeline_mode=pl.Buffered(1))


def _ffn_fwd(x, gain, wint, wout, tm, comm=None):
    S, D = x.shape
    F = wout.shape[0]

    def body(x_ref, g_ref, w_ref, wo_ref, xo_ref, gu_ref, a_ref):
        xh, _ = _rms(x_ref[...])
        h = (xh * g_ref[...]).astype(BF16)
        for o, n in _chunks(F, FFN_CHUNK):
            gb = _dot_nt(h, w_ref[o:o + n, :]).astype(BF16)
            ub = _dot_nt(h, w_ref[F + o:F + o + n, :]).astype(BF16)
            gu_ref[:, o:o + n] = gb
            gu_ref[:, F + o:F + o + n] = ub
            g = gb.astype(F32)
            a_ref[:, o:o + n] = (g * _sigmoid(g) * ub.astype(F32)).astype(BF16)
        xo_ref[...] = x_ref[...] + FFN_RES * _dot(a_ref[...], wo_ref[...])

    return _pcall(
        body, (x, gain, wint, wout), grid=(S // tm,),
        in_specs=[pl.BlockSpec((tm, D), lambda i: (i, 0)), _resident((1, D)),
                  _resident((2 * F, D)), _resident((F, D))],
        out_specs=[pl.BlockSpec((tm, D), lambda i: (i, 0)), pl.BlockSpec((tm, 2 * F), lambda i: (i, 0))],
        out_shape=[_sds((S, D), F32), _sds((S, 2 * F), BF16)],
        scratch_shapes=[pltpu.VMEM((tm, F), BF16)],
        sem=("parallel",), name="ffn_fwd", comm=comm)


def _ffn_bwd_a(x, gain, dxo, gu, wint, wout, tm, comm=None):
    S, D = x.shape
    F = wout.shape[0]

    def body(x_ref, g_ref, dxo_ref, gu_ref, w_ref, wo_ref, dx_ref, dgu_ref, dgain_ref, h_ref, dys_ref):
        i = pl.program_id(0)
        dys = (FFN_RES * dxo_ref[...]).astype(BF16)
        dys_ref[...] = dys
        for o, n in _chunks(F, FFN_CHUNK):
            dact = _dot_nt(dys, wo_ref[o:o + n, :])
            g = gu_ref[:, o:o + n].astype(F32)
            u = gu_ref[:, F + o:F + o + n].astype(F32)
            s = _sigmoid(g)
            dgu_ref[:, o:o + n] = (dact * u * (s * (1.0 + g * (1.0 - s)))).astype(BF16)
            dgu_ref[:, F + o:F + o + n] = (dact * (g * s)).astype(BF16)
        dh = _dot(dgu_ref[...], w_ref[...])
        xh, r = _rms(x_ref[...])
        h_ref[...] = (xh * g_ref[...]).astype(BF16)
        dxn, dgn = _rms_bwd(dh, xh, r, g_ref[...])
        dx_ref[...] = dxo_ref[...] + dxn

        @pl.when(i == 0)
        def _():
            dgain_ref[...] = dgn

        @pl.when(i > 0)
        def _():
            dgain_ref[...] += dgn

    tile = pl.BlockSpec((tm, D), lambda i: (i, 0))
    wide = pl.BlockSpec((tm, 2 * F), lambda i: (i, 0))
    return _pcall(
        body, (x, gain, dxo, gu, wint, wout), grid=(S // tm,),
        in_specs=[tile, _resident((1, D)), tile, wide, _resident((2 * F, D)), _resident((F, D))],
        out_specs=[tile, wide, pl.BlockSpec((1, D), lambda i: (0, 0)), tile, tile],
        out_shape=[_sds((S, D), F32), _sds((S, 2 * F), BF16), _sds((1, D), F32), _sds((S, D), BF16), _sds((S, D), BF16)],
        sem=("arbitrary",), name="ffn_bwd_a", comm=comm)


def _ffn_bwd_w(h, dys, gu, dgu, tk, comm=None):
    S, D = h.shape
    F = gu.shape[1] // 2
    FH = F // 2
    nk = S // tk

    def body(h_ref, dys_ref, gg_ref, gu_ref, dg_ref, du_ref, dw_ref, dwo_ref, accw_ref, acco_ref):
        k = pl.program_id(1)

        @pl.when(k == 0)
        def _():
            accw_ref[...] = jnp.zeros_like(accw_ref)
            acco_ref[...] = jnp.zeros_like(acco_ref)

        hv, dys = h_ref[...], dys_ref[...]
        for o, n in _chunks(FH, BWD_W_CHUNK):
            g = gg_ref[:, o:o + n].astype(F32)
            act = (g * _sigmoid(g) * gu_ref[:, o:o + n].astype(F32)).astype(BF16)
            accw_ref[0, o:o + n, :] += _dot_tn(dg_ref[:, o:o + n], hv)
            accw_ref[1, o:o + n, :] += _dot_tn(du_ref[:, o:o + n], hv)
            acco_ref[o:o + n, :] += _dot_tn(act, dys)

        @pl.when(k == nk - 1)
        def _():
            dw_ref[...] = accw_ref[...].astype(BF16)
            dwo_ref[...] = acco_ref[...].astype(BF16)

    tile = pl.BlockSpec((tk, D), lambda j, k: (k, 0))
    gate = pl.BlockSpec((tk, FH), lambda j, k: (k, j))
    up = pl.BlockSpec((tk, FH), lambda j, k: (k, j + 2))
    return _pcall(
        body, (h, dys, gu, gu, dgu, dgu), grid=(2, nk),
        in_specs=[tile, tile, gate, up, gate, up],
        out_specs=[pl.BlockSpec((2, FH, D), lambda j, k: (0, j, 0), pipeline_mode=pl.Buffered(1)),
                   pl.BlockSpec((FH, D), lambda j, k: (j, 0), pipeline_mode=pl.Buffered(1))],
        out_shape=[_sds((2, F, D), BF16), _sds((F, D), BF16)],
        scratch_shapes=[pltpu.VMEM((2, FH, D), F32), pltpu.VMEM((FH, D), F32)],
        sem=("parallel", "arbitrary"), name="ffn_bwd_w", comm=comm)


def _mixin_fwd(x, gain, wext, tm, comm=None):
    S, D = x.shape
    PW = wext.shape[0]

    def body(x_ref, g_ref, w_ref, p_ref):
        xh, _ = _rms(x_ref[...])
        p_ref[...] = _dot_nt((xh * g_ref[...]).astype(BF16), w_ref[...]).astype(BF16)

    return _pcall(
        body, (x, gain, wext), grid=(S // tm,),
        in_specs=[pl.BlockSpec((tm, D), lambda i: (i, 0)), _resident((1, D)), _resident((PW, D))],
        out_specs=[pl.BlockSpec((tm, PW), lambda i: (i, 0))],
        out_shape=[_sds((S, PW), BF16)],
        sem=("parallel",), name="mixin_fwd", comm=comm)


def _mixin_bwd(x, gain, dxo, dq, dkv, dag, wext, tm, comm=None):
    S, D = x.shape
    PW = wext.shape[0]
    QW = dq.shape[1]
    o1, o2 = QW, QW + 2 * KV_DUP

    def body(x_ref, g_ref, dxo_ref, dq_ref, dkv_ref, dag_ref, w_ref, dx_ref, dgain_ref, dw_ref):
        i = pl.program_id(0)
        xh, r = _rms(x_ref[...])
        h = (xh * g_ref[...]).astype(BF16)
        dqv, dkvv, dagv = dq_ref[...], dkv_ref[...], dag_ref[...]
        dh = _dot(dqv, w_ref[0:o1, :]) + _dot(dkvv, w_ref[o1:o2, :]) + _dot(dagv, w_ref[o2:PW, :])
        dxn, dgn = _rms_bwd(dh, xh, r, g_ref[...])
        dx_ref[...] = dxo_ref[...] + dxn

        @pl.when(i == 0)
        def _():
            dgain_ref[...] = dgn
            dw_ref[0:o1, :] = _dot_tn(dqv, h)
            dw_ref[o1:o2, :] = _dot_tn(dkvv, h)
            dw_ref[o2:PW, :] = _dot_tn(dagv, h)

        @pl.when(i > 0)
        def _():
            dgain_ref[...] += dgn
            dw_ref[0:o1, :] += _dot_tn(dqv, h)
            dw_ref[o1:o2, :] += _dot_tn(dkvv, h)
            dw_ref[o2:PW, :] += _dot_tn(dagv, h)

    return _pcall(
        body, (x, gain, dxo, dq, dkv, dag, wext), grid=(S // tm,),
        in_specs=[pl.BlockSpec((tm, D), lambda i: (i, 0)), _resident((1, D)),
                  pl.BlockSpec((tm, D), lambda i: (i, 0)),
                  pl.BlockSpec((tm, QW), lambda i: (i, 0)),
                  pl.BlockSpec((tm, 2 * KV_DUP), lambda i: (i, 0)),
                  pl.BlockSpec((tm, PW - o2), lambda i: (i, 0)),
                  _resident((PW, D))],
        out_specs=[pl.BlockSpec((tm, D), lambda i: (i, 0)),
                   pl.BlockSpec((1, D), lambda i: (0, 0)),
                   pl.BlockSpec((PW, D), lambda i: (0, 0))],
        out_shape=[_sds((S, D), F32), _sds((1, D), F32), _sds((PW, D), F32)],
        sem=("arbitrary",), name="mixin_bwd", comm=comm)


def _attn_group(n, group, slopes, sinks):
    rows = group * WINDOW
    r = lax.broadcasted_iota(jnp.int32, (rows, 2 * WINDOW), 0)
    s = lax.broadcasted_iota(jnp.int32, (rows, 2 * WINDOW), 1)
    dist = (r & (WINDOW - 1)) + WINDOW - s
    valid = (dist >= 0) & (dist < WINDOW) & jnp.logical_or(n > 0, s >= WINDOW)
    seg = lax.shift_right_logical(lax.broadcasted_iota(jnp.int32, (rows, 1), 0), WINDOW.bit_length() - 1)
    slope = jnp.zeros((rows, 1), F32)
    sink = jnp.zeros((rows, 1), F32)
    for i in range(group):
        slope = jnp.where(seg == i, slopes[i], slope)
        sink = jnp.where(seg == i, sinks[i], sink)
    return valid, -slope * dist.astype(F32), sink


def _lane_halves():
    lo = lax.broadcasted_iota(jnp.int32, (WINDOW, LANES), 1) < HEAD_DIM
    return lo, [jnp.where(lo, 1.0, 0.0).astype(BF16), jnp.where(lo, 0.0, 1.0).astype(BF16)]


def _stack_heads(ref, first_tile, n_tiles, halves):
    parts = []
    for t in range(first_tile, first_tile + n_tiles):
        tile = ref[:, LANES * t:LANES * (t + 1)]
        parts += [tile * halves[0], tile * halves[1]]
    return jnp.concatenate(parts, axis=0)


def _unstack_heads(ref, first_tile, n_tiles, lo, stacked):
    for i in range(n_tiles):
        a = stacked[2 * i * WINDOW:(2 * i + 1) * WINDOW]
        b = stacked[(2 * i + 1) * WINDOW:(2 * i + 2) * WINDOW]
        t = first_tile + i
        ref[:, LANES * t:LANES * (t + 1)] = jnp.where(lo, a, b).astype(ref.dtype)


def _attn_probs(qs, kd, scale, bias, valid, sink):
    sc = jnp.where(valid, _dot_nt(qs, kd) * scale + bias, NEG_INF)
    m = jnp.maximum(jnp.max(sc, axis=-1, keepdims=True), sink)
    p = jnp.exp(sc - m)
    es = jnp.exp(sink - m)
    inv = 1.0 / (jnp.sum(p, axis=-1, keepdims=True) + es)
    return p * inv, es * inv


def _attn_specs(n_heads, nb):
    QW = n_heads * HEAD_DIM
    kblk, vblk = QW // KV_DUP, QW // KV_DUP + 1
    last = nb - 1
    cur = lambda b: (lambda n: (jnp.minimum(n, last), b))
    prev = lambda b: (lambda n: (jnp.clip(n - 1, 0, last), b))
    kv = [pl.BlockSpec((WINDOW, KV_DUP), cur(kblk)), pl.BlockSpec((WINDOW, KV_DUP), prev(kblk)),
          pl.BlockSpec((WINDOW, KV_DUP), cur(vblk)), pl.BlockSpec((WINDOW, KV_DUP), prev(vblk))]
    return QW, cur, prev, kv


def _attn_fwd(proj, sinks, n_heads, comm=None):
    S = proj.shape[0]
    nb = S // WINDOW
    group = n_heads // N_KV_HEADS
    slopes = [2.0 ** (-8.0 * (h + 1) / n_heads) for h in range(n_heads)]
    scale = 1.0 / math.sqrt(HEAD_DIM)
    QW, cur, prev, kv_specs = _attn_specs(n_heads, nb)

    def body(sink_ref, q_ref, kc_ref, kp_ref, vc_ref, vp_ref, o_ref):
        n = pl.program_id(0)
        lo, halves = _lane_halves()
        for kh in range(N_KV_HEADS):
            heads = range(kh * group, (kh + 1) * group)
            valid, bias, sink = _attn_group(n, group, [slopes[h] for h in heads], [sink_ref[h] for h in heads])
            ksl = slice(LANES * kh, LANES * (kh + 1))
            kd = jnp.concatenate([kp_ref[:, ksl], kc_ref[:, ksl]], axis=0)
            vd = jnp.concatenate([vp_ref[:, ksl], vc_ref[:, ksl]], axis=0)
            qs = _stack_heads(q_ref, kh * group // 2, group // 2, halves)
            pn, _ = _attn_probs(qs, kd, scale, bias, valid, sink)
            _unstack_heads(o_ref, kh * group // 2, group // 2, lo, _dot(pn.astype(BF16), vd))

    return _pcall(
        body, (sinks, proj, proj, proj, proj, proj), grid=(nb,),
        in_specs=[pl.BlockSpec(memory_space=pltpu.SMEM), pl.BlockSpec((WINDOW, QW), cur(0))] + kv_specs,
        out_specs=[pl.BlockSpec((WINDOW, QW), cur(0))],
        out_shape=[_sds((S, QW), BF16)],
        sem=("parallel",), name="attn_fwd", comm=comm)


def _attn_bwd(proj, dmix, sinks, n_heads, comm=None):
    S = proj.shape[0]
    nb = S // WINDOW
    group = n_heads // N_KV_HEADS
    slopes = [2.0 ** (-8.0 * (h + 1) / n_heads) for h in range(n_heads)]
    scale = 1.0 / math.sqrt(HEAD_DIM)
    QW, cur, prev, kv_specs = _attn_specs(n_heads, nb)

    def body(sink_ref, q_ref, kc_ref, kp_ref, vc_ref, vp_ref, do_ref, dq_ref, dkv_ref, dsink_ref, carry_ref):
        n = pl.program_id(0)

        @pl.when(n == 0)
        def _():
            carry_ref[...] = jnp.zeros_like(carry_ref)
            dsink_ref[...] = jnp.zeros_like(dsink_ref)

        @pl.when(n < nb)
        def _():
            lo, halves = _lane_halves()
            lane1 = lax.broadcasted_iota(jnp.int32, (1, LANES), 1)
            dkd, dvd = [], []
            dsink = jnp.zeros((1, LANES), F32)
            for kh in range(N_KV_HEADS):
                heads = range(kh * group, (kh + 1) * group)
                valid, bias, sink = _attn_group(n, group, [slopes[h] for h in heads], [sink_ref[h] for h in heads])
                ksl = slice(LANES * kh, LANES * (kh + 1))
                kd = jnp.concatenate([kp_ref[:, ksl], kc_ref[:, ksl]], axis=0)
                vd = jnp.concatenate([vp_ref[:, ksl], vc_ref[:, ksl]], axis=0)
                qs = _stack_heads(q_ref, kh * group // 2, group // 2, halves)
                dos = _stack_heads(do_ref, kh * group // 2, group // 2, halves)
                pn, psink = _attn_probs(qs, kd, scale, bias, valid, sink)
                dp = _dot_nt(dos, vd)
                delta = jnp.sum(pn * dp, axis=-1, keepdims=True)
                dsb = (pn * (dp - delta) * scale).astype(BF16)
                sd = psink * delta
                for i, h in enumerate(heads):
                    dsink = dsink - jnp.where(lane1 == h, jnp.sum(sd[i * WINDOW:(i + 1) * WINDOW]), 0.0)
                _unstack_heads(dq_ref, kh * group // 2, group // 2, lo, _dot(dsb, kd))
                dkd.append(_dot_tn(dsb, qs))
                dvd.append(_dot_tn(pn.astype(BF16), dos))
            dsink_ref[...] += dsink
            both = jnp.concatenate(dkd + dvd, axis=1)
            dkv_ref[...] = (carry_ref[...] + both[0:WINDOW]).astype(BF16)
            carry_ref[...] = both[WINDOW:2 * WINDOW]

        @pl.when(n == nb)
        def _():
            dkv_ref[...] = carry_ref[...].astype(BF16)

    return _pcall(
        body, (sinks, proj, proj, proj, proj, proj, dmix), grid=(nb + 1,),
        in_specs=[pl.BlockSpec(memory_space=pltpu.SMEM), pl.BlockSpec((WINDOW, QW), cur(0))] + kv_specs
                 + [pl.BlockSpec((WINDOW, QW), cur(0))],
        out_specs=[pl.BlockSpec((WINDOW, QW), cur(0)),
                   pl.BlockSpec((WINDOW, 2 * KV_DUP), prev(0)),
                   pl.BlockSpec((1, LANES), lambda n: (0, 0))],
        out_shape=[_sds((S, QW), BF16), _sds((S, 2 * KV_DUP), BF16), _sds((1, LANES), F32)],
        scratch_shapes=[pltpu.VMEM((WINDOW, 2 * KV_DUP), F32)],
        sem=("arbitrary",), name="attn_bwd", comm=comm)


def _glu_window(a_ref, g_ref, ap_ref, gp_ref, win_ref, first):
    tm = a_ref.shape[0]
    zp = ap_ref[...].astype(F32) * _sigmoid(gp_ref[...].astype(F32))
    win_ref[0:HALO, :] = jnp.where(first, jnp.zeros_like(zp), zp)
    win_ref[HALO:HALO + tm, :] = a_ref[...].astype(F32) * _sigmoid(g_ref[...].astype(F32))


def _preshift(win_ref, sh_ref):
    n = win_ref.shape[0] - SUBLANES
    for s in range(1, SUBLANES):
        sh_ref[s - 1, 0:n, :] = win_ref[s:s + n, :]


def _window(win_ref, sh_ref, start):
    s = start % SUBLANES
    if s == 0:
        return win_ref[start:start + ROWS, :]
    return sh_ref[s - 1, start - s:start - s + ROWS, :]


def _conv_fwd(proj, wdw, bdw, lng, lnb, n_heads, tm, comm=None):
    S = proj.shape[0]
    taps, C = wdw.shape
    ablk = (n_heads * HEAD_DIM + 2 * KV_DUP) // C
    hb = tm // HALO
    off = HALO - (taps - 1)

    def body(a_ref, g_ref, ap_ref, gp_ref, w_ref, b_ref, lg_ref, lb_ref, o_ref, y_ref, win_ref, sh_ref):
        _glu_window(a_ref, g_ref, ap_ref, gp_ref, win_ref, pl.program_id(0) == 0)
        _preshift(win_ref, sh_ref)
        for c in range(tm // ROWS):
            r0 = c * ROWS
            acc = jnp.zeros((ROWS, C), F32) + b_ref[...]
            for k in range(taps):
                acc = acc + w_ref[k:k + 1, :] * _window(win_ref, sh_ref, r0 + off + k)
            y_ref[r0:r0 + ROWS, :] = acc
        y = y_ref[...]
        mu = jnp.mean(y, axis=-1, keepdims=True)
        yc = y - mu
        yn = yc * lax.rsqrt(jnp.mean(yc * yc, axis=-1, keepdims=True) + RMS_EPS) * lg_ref[...] + lb_ref[...]
        o_ref[...] = (yn * _sigmoid(yn)).astype(BF16)

    vec = pl.BlockSpec((1, C), lambda i: (0, 0))
    halo = lambda b: pl.BlockSpec((HALO, C), lambda i: (jnp.maximum(i * hb - 1, 0), b))
    return _pcall(
        body, (proj, proj, proj, proj, wdw, bdw, lng, lnb,), grid=(S // tm,),
        in_specs=[pl.BlockSpec((tm, C), lambda i: (i, ablk)), pl.BlockSpec((tm, C), lambda i: (i, ablk + 1)),
                  halo(ablk), halo(ablk + 1),
                  pl.BlockSpec((taps, C), lambda i: (0, 0)), vec, vec, vec],
        out_specs=[pl.BlockSpec((tm, C), lambda i: (i, 0)), pl.BlockSpec((tm, C), lambda i: (i, 0))],
        out_shape=[_sds((S, C), BF16), _sds((S, C), F32)],
        scratch_shapes=[pltpu.VMEM((tm + HALO, C), F32), pltpu.VMEM((SUBLANES - 1, tm + HALO, C), F32)],
        sem=("parallel",), name="conv_fwd", comm=comm)


def _conv_bwd(proj, dmix, ysave, wdw, lng, lnb, n_heads, tm, comm=None):
    S = proj.shape[0]
    taps, C = wdw.shape
    QW = n_heads * HEAD_DIM
    ablk = (QW + 2 * KV_DUP) // C
    cblk = QW // C
    hb = tm // HALO
    nt = S // tm
    off = HALO - (taps - 1)

    def ln_bwd(dc, y, lg, lb):
        mu = jnp.mean(y, axis=-1, keepdims=True)
        yc = y - mu
        r = lax.rsqrt(jnp.mean(yc * yc, axis=-1, keepdims=True) + RMS_EPS)
        yh = yc * r
        yn = yh * lg + lb
        sg = _sigmoid(yn)
        dyn = dc * (sg * (1.0 + yn * (1.0 - sg)))
        dyh = dyn * lg
        dy = r * (dyh - jnp.mean(dyh, axis=-1, keepdims=True) - yh * jnp.mean(dyh * yh, axis=-1, keepdims=True))
        return dy, dyn, yh

    def body(dc_ref, dcn_ref, y_ref, yn_ref, a_ref, g_ref, ap_ref, gp_ref, w_ref, lg_ref, lb_ref,
             dag_ref, dw_ref, dvec_ref, zwin_ref, dyw_ref, dwacc_ref, zsh_ref, dysh_ref):
        i = pl.program_id(0)

        @pl.when(i == 0)
        def _():
            dwacc_ref[...] = jnp.zeros_like(dwacc_ref)
            dvec_ref[...] = jnp.zeros_like(dvec_ref)

        lg, lb = lg_ref[...], lb_ref[...]
        dy, dyn, yh = ln_bwd(dc_ref[...].astype(F32), y_ref[...], lg, lb)
        dy_next, _, _ = ln_bwd(dcn_ref[...].astype(F32), yn_ref[...], lg, lb)
        dyw_ref[0:tm, :] = dy
        dyw_ref[tm:tm + HALO, :] = jnp.where(i == nt - 1, jnp.zeros_like(dy_next), dy_next)
        dvec_ref[0:1, :] += jnp.sum(dy, axis=0, keepdims=True)
        dvec_ref[1:2, :] += jnp.sum(dyn * yh, axis=0, keepdims=True)
        dvec_ref[2:3, :] += jnp.sum(dyn, axis=0, keepdims=True)
        _glu_window(a_ref, g_ref, ap_ref, gp_ref, zwin_ref, i == 0)
        _preshift(zwin_ref, zsh_ref)
        _preshift(dyw_ref, dysh_ref)

        for c in range(tm // ROWS):
            r0 = c * ROWS
            dz = jnp.zeros((ROWS, C), F32)
            dyc = dyw_ref[r0:r0 + ROWS, :]
            for k in range(taps):
                dz = dz + w_ref[k:k + 1, :] * _window(dyw_ref, dysh_ref, r0 + taps - 1 - k)
                prod = dyc * _window(zwin_ref, zsh_ref, r0 + off + k)
                dwacc_ref[k] += jnp.sum(prod.reshape(ROWS // SUBLANES, SUBLANES, C), axis=0)
            a = a_ref[r0:r0 + ROWS, :].astype(F32)
            s = _sigmoid(g_ref[r0:r0 + ROWS, :].astype(F32))
            dag_ref[r0:r0 + ROWS, 0:C] = (dz * s).astype(BF16)
            dag_ref[r0:r0 + ROWS, C:2 * C] = (dz * a * s * (1.0 - s)).astype(BF16)

        @pl.when(i == nt - 1)
        def _():
            dw_ref[...] = jnp.zeros_like(dw_ref)
            for k in range(taps):
                dw_ref[k:k + 1, :] = jnp.sum(dwacc_ref[k], axis=0, keepdims=True)

    vec = pl.BlockSpec((1, C), lambda i: (0, 0))
    tile = lambda b: pl.BlockSpec((tm, C), lambda i: (i, b))
    prev = lambda b: pl.BlockSpec((HALO, C), lambda i: (jnp.maximum(i * hb - 1, 0), b))
    nxt = lambda b: pl.BlockSpec((HALO, C), lambda i: (jnp.minimum((i + 1) * hb, S // HALO - 1), b))
    return _pcall(
        body, (dmix, dmix, ysave, ysave, proj, proj, proj, proj, wdw, lng, lnb,), grid=(nt,),
        in_specs=[tile(cblk), nxt(cblk), tile(0), nxt(0), tile(ablk), tile(ablk + 1), prev(ablk), prev(ablk + 1),
                  pl.BlockSpec((taps, C), lambda i: (0, 0)), vec, vec],
        out_specs=[pl.BlockSpec((tm, 2 * C), lambda i: (i, 0)),
                   pl.BlockSpec((HALO, C), lambda i: (0, 0)),
                   pl.BlockSpec((8, C), lambda i: (0, 0))],
        out_shape=[_sds((S, 2 * C), BF16), _sds((HALO, C), F32), _sds((8, C), F32)],
        scratch_shapes=[pltpu.VMEM((tm + HALO, C), F32), pltpu.VMEM((tm + HALO, C), F32),
                        pltpu.VMEM((taps, SUBLANES, C), F32),
                        pltpu.VMEM((SUBLANES - 1, tm + HALO, C), F32), pltpu.VMEM((SUBLANES - 1, tm + HALO, C), F32)],
        sem=("arbitrary",), name="conv_bwd", comm=comm)


def _mixout_fwd(x, attn, conv, wo, tm, comm=None):
    S, D = x.shape
    QW, C = attn.shape[1], conv.shape[1]

    def body(x_ref, a_ref, c_ref, w_ref, o_ref):
        o_ref[...] = x_ref[...] + _dot(a_ref[...], w_ref[0:QW, :]) + _dot(c_ref[...], w_ref[QW:QW + C, :])

    return _pcall(
        body, (x, attn, conv, wo,), grid=(S // tm,),
        in_specs=[pl.BlockSpec((tm, D), lambda i: (i, 0)),
                  pl.BlockSpec((tm, QW), lambda i: (i, 0)),
                  pl.BlockSpec((tm, C), lambda i: (i, 0)),
                  pl.BlockSpec((QW + C, D), lambda i: (0, 0))],
        out_specs=[pl.BlockSpec((tm, D), lambda i: (i, 0))],
        out_shape=[_sds((S, D), F32)],
        sem=("parallel",), name="mixout_fwd", comm=comm)


def _mixout_bwd(dxo, attn, conv, wo, tm, comm=None):
    S, D = dxo.shape
    QW, C = attn.shape[1], conv.shape[1]
    nt = S // tm

    def body(dx_ref, a_ref, c_ref, w_ref, dm_ref, dw_ref, acc_ref):
        i = pl.program_id(0)
        dxb = dx_ref[...].astype(BF16)
        dm_ref[...] = _dot_nt(dxb, w_ref[...]).astype(BF16)

        @pl.when(i == 0)
        def _():
            acc_ref[...] = jnp.zeros_like(acc_ref)

        acc_ref[0:QW, :] += _dot_tn(a_ref[...], dxb)
        acc_ref[QW:QW + C, :] += _dot_tn(c_ref[...], dxb)

        @pl.when(i == nt - 1)
        def _():
            dw_ref[...] = acc_ref[...].astype(BF16)

    return _pcall(
        body, (dxo, attn, conv, wo,), grid=(nt,),
        in_specs=[pl.BlockSpec((tm, D), lambda i: (i, 0)),
                  pl.BlockSpec((tm, QW), lambda i: (i, 0)),
                  pl.BlockSpec((tm, C), lambda i: (i, 0)),
                  pl.BlockSpec((QW + C, D), lambda i: (0, 0))],
        out_specs=[pl.BlockSpec((tm, QW + C), lambda i: (i, 0)),
                   pl.BlockSpec((QW + C, D), lambda i: (0, 0))],
        out_shape=[_sds((S, QW + C), BF16), _sds((QW + C, D), BF16)],
        scratch_shapes=[pltpu.VMEM((QW + C, D), F32)],
        sem=("arbitrary",), name="mixout_bwd", comm=comm)


def _loss_head(x, gain, target, tm, comm=None):
    S, D = x.shape
    nt = S // tm

    def body(x_ref, g_ref, t_ref, dx_ref, loss_ref, dgain_ref):
        i = pl.program_id(0)
        xh, r = _rms(x_ref[...])
        e = xh * g_ref[...] - t_ref[...]
        loss_ref[...] = jnp.zeros((1, LANES), F32) + 0.5 * jnp.sum(jnp.mean(e * e, axis=-1, keepdims=True))
        dxn, dgn = _rms_bwd(e * (1.0 / D), xh, r, g_ref[...])
        dx_ref[...] = dxn

        @pl.when(i == 0)
        def _():
            dgain_ref[...] = dgn

        @pl.when(i > 0)
        def _():
            dgain_ref[...] += dgn

    return _pcall(
        body, (x, gain, target,), grid=(nt,),
        in_specs=[pl.BlockSpec((tm, D), lambda i: (i, 0)),
                  pl.BlockSpec((1, D), lambda i: (0, 0)),
                  pl.BlockSpec((tm, D), lambda i: (i, 0))],
        out_specs=[pl.BlockSpec((tm, D), lambda i: (i, 0)),
                   pl.BlockSpec((None, 1, LANES), lambda i: (i, 0, 0)),
                   pl.BlockSpec((1, D), lambda i: (0, 0))],
        out_shape=[_sds((S, D), F32), _sds((nt, 1, LANES), F32), _sds((1, D), F32)],
        sem=("arbitrary",), name="loss_head", comm=comm)


def _adam(w, m, v, parts, name, comm=None):
    L, R, C = w.shape
    P = parts[0].shape[0]
    br = _row_block(R, 256)
    c1 = 1.0 - ADAM_B1 ** ADAM_STEP
    c2 = 1.0 - ADAM_B2 ** ADAM_STEP

    def body(w_ref, m_ref, v_ref, *rest):
        p_refs, (g_ref, d_ref, mo_ref, vo_ref) = rest[:L], rest[L:]
        layer = pl.program_id(0)

        def update(p_ref):
            g = p_ref[0].astype(F32)
            for k in range(1, P):
                g = g + p_ref[k].astype(F32)
            mn = ADAM_B1 * m_ref[...] + (1.0 - ADAM_B1) * g
            vn = ADAM_B2 * v_ref[...] + (1.0 - ADAM_B2) * (g * g)
            g_ref[...] = g
            mo_ref[...] = mn
            vo_ref[...] = vn
            d_ref[...] = -ADAM_LR * ((mn / c1) / (jnp.sqrt(vn / c2) + ADAM_EPS) + ADAM_WD * w_ref[...])

        for k in range(L):
            pl.when(layer == k)(functools.partial(update, p_refs[k]))

    blk = pl.BlockSpec((None, br, C), lambda l, i: (l, i, 0))
    part = lambda k: pl.BlockSpec((P, br, C), lambda l, i: (0, jnp.where(l == k, i, 0), 0))
    return _pcall(
        body, (w, m, v, *parts), grid=(L, R // br),
        in_specs=[blk, blk, blk] + [part(k) for k in range(L)],
        out_specs=[blk, blk, blk, blk],
        out_shape=[_sds((L, R, C), F32)] * 4,
        sem=("parallel", "parallel"), name=name, comm=comm)


def _sum_parts(parts):
    P, R, C = parts.shape

    def body(p_ref, o_ref):
        g = p_ref[0]
        for k in range(1, P):
            g = g + p_ref[k]
        o_ref[...] = g

    vmem = pl.BlockSpec(memory_space=pltpu.VMEM)
    return _pcall(body, (parts,), in_specs=[vmem], out_specs=[vmem], out_shape=[_sds((R, C), F32)],
                  name="sum_parts")[0]


def _place():
    x, y, c = lax.axis_index("x"), lax.axis_index("y"), lax.axis_index("c")
    return x, y, c, [(1 - x, y), (x, 1 - y), (1 - x, 1 - y)]


def _dev(px, py, pc):
    return 4 * px + 2 * py + pc


def _gather_comm(shards, fulls, slot_of):
    n = len(shards)

    def copies(srcs, outs, send_sems, recv_sems):
        x, y, c, chips = _place()

        def copy(a, k, block, to, from_shard=False):
            dst = slot_of[a](outs[a], _dev(*block))
            return pltpu.make_async_remote_copy(
                src_ref=srcs[a] if from_shard else dst, dst_ref=dst,
                send_sem=send_sems.at[a, k], recv_sem=recv_sems.at[a, k], device_id=to, device_id_type=MESH)

        return copy, (x, y, c), (x, y, 1 - c), chips

    def local(srcs, outs, local_sems):
        x, y, c, _ = _place()
        return [pltpu.make_async_copy(srcs[a], slot_of[a](outs[a], _dev(x, y, c)), local_sems.at[a])
                for a in range(n)]

    def first_copies(copy, me, sibling, chips):
        out = []
        for a in range(n):
            out.append(copy(a, 0, me, sibling, True))
            out += [copy(a, 1 + j, me, (*chip, me[2]), True) for j, chip in enumerate(chips)]
        return out

    def start(srcs, outs, sems):
        send_sems, recv_sems, local_sems = sems
        copy, me, sibling, chips = copies(srcs, outs, send_sems, recv_sems)
        for cp in local(srcs, outs, local_sems):
            cp.start()
        for cp in first_copies(copy, me, sibling, chips):
            cp.start()

    def finish(srcs, outs, sems):
        send_sems, recv_sems, local_sems = sems
        copy, me, sibling, chips = copies(srcs, outs, send_sems, recv_sems)
        c = me[2]
        passed = []
        for j, chip in enumerate(chips):
            for a in range(n):
                copy(a, 1 + j, (*chip, c), me).wait_recv()
                fwd = copy(a, 4 + j, (*chip, c), sibling)
                fwd.start()
                passed.append(fwd)
        for a in range(n):
            copy(a, 0, sibling, me).wait_recv()
            for j, chip in enumerate(chips):
                copy(a, 4 + j, (*chip, 1 - c), me).wait_recv()
        for cp in first_copies(copy, me, sibling, chips) + passed:
            cp.wait_send()
        for cp in local(srcs, outs, local_sems):
            cp.wait()

    sems = [pltpu.SemaphoreType.DMA((n, 7)), pltpu.SemaphoreType.DMA((n, 7)), pltpu.SemaphoreType.DMA((n,))]
    return _Comm(shards, fulls, sems, start, finish)


def _swap_comm(grads):
    n = len(grads)

    def copies(srcs, outs, sems):
        x, y, c, _ = _place()
        return [pltpu.make_async_remote_copy(
            src_ref=srcs[a].at[:, pl.ds(1 - c, 1)], dst_ref=outs[a],
            send_sem=sems[0].at[a], recv_sem=sems[1].at[a], device_id=(x, y, 1 - c), device_id_type=MESH)
            for a in range(n)]

    def start(srcs, outs, sems):
        for cp in copies(srcs, outs, sems):
            cp.start()

    def finish(srcs, outs, sems):
        for cp in copies(srcs, outs, sems):
            cp.wait()

    return _Comm(grads, [_sds((N_CHIP, 1) + g.shape[2:], g.dtype) for g in grads],
                 [pltpu.SemaphoreType.DMA((n,)), pltpu.SemaphoreType.DMA((n,))], start, finish)


def _exchange_comm(parts):
    n = len(parts)

    def copies(srcs, outs, sems):
        x, y, c, chips = _place()
        mine = 2 * x + y
        loc = [pltpu.make_async_copy(srcs[a].at[pl.ds(mine, 1)], outs[a].at[pl.ds(mine, 1)], sems[2].at[a])
               for a in range(n)]
        rem = [pltpu.make_async_remote_copy(
            src_ref=srcs[a].at[pl.ds(2 * px + py, 1)], dst_ref=outs[a].at[pl.ds(mine, 1)],
            send_sem=sems[0].at[a, j], recv_sem=sems[1].at[a, j], device_id=(px, py, c), device_id_type=MESH)
            for a in range(n) for j, (px, py) in enumerate(chips)]
        return loc + rem

    def start(srcs, outs, sems):
        for cp in copies(srcs, outs, sems):
            cp.start()

    def finish(srcs, outs, sems):
        for cp in copies(srcs, outs, sems):
            cp.wait()

    return _Comm(parts, [_sds(p.shape, p.dtype) for p in parts],
                 [pltpu.SemaphoreType.DMA((n, 3)), pltpu.SemaphoreType.DMA((n, 3)), pltpu.SemaphoreType.DMA((n,))],
                 start, finish)


FWD_KERNELS = [("ffn1", ("a1", "b1"), 26), ("mixin", ("wi",), 7), ("attn", (), 7), ("conv", (), 15), ("mixout", ("wo",), 6),
               ("ffn2", ("a2", "b2"), 26)]
GATHER_SHARE = 70


def _plan_gathers(n_layers, shard_bytes):
    per_layer = sum(shard_bytes.values())
    cost = {n: GATHER_SHARE * b / per_layer for n, b in shard_bytes.items()}
    room = {len(FWD_KERNELS) * l + i: k[2] for l in range(n_layers) for i, k in enumerate(FWD_KERNELS)}
    first, plan = [], {}
    for l in range(n_layers):
        for i, (_, needs, _) in enumerate(FWD_KERNELS):
            due = len(FWD_KERNELS) * l + i
            for name in needs:
                if due == 0:
                    first.append((name, l))
                    continue
                fits = [k for k in range(due) if room[k] >= cost[name]]
                k = fits[0] if fits else max(range(due), key=lambda k: room[k])
                room[k] -= cost[name]
                plan.setdefault(k, []).append((name, l))
    return first, plan


def _comm_only(comm, name):
    return _pcall(lambda: None, (), in_specs=[], out_specs=[], out_shape=[], name=name, comm=comm)


def _gather_small(v):
    R, C = v.shape

    def body(x_ref, out_ref, send_sems, recv_sems, local_sem):
        x, y, c, chips = _place()
        me, sibling = (x, y, c), (x, y, 1 - c)

        def copy(k, block, to, from_shard=False):
            dst = out_ref.at[_dev(*block)]
            return pltpu.make_async_remote_copy(
                src_ref=x_ref if from_shard else dst, dst_ref=dst,
                send_sem=send_sems.at[k], recv_sem=recv_sems.at[k], device_id=to, device_id_type=MESH)

        mine = pltpu.make_async_copy(x_ref, out_ref.at[_dev(*me)], local_sem)
        mine.start()
        first = [copy(0, me, sibling, True)] + [copy(1 + j, me, (*chip, c), True) for j, chip in enumerate(chips)]
        for cp in first:
            cp.start()
        passed = [copy(4 + j, (*chip, c), sibling) for j, chip in enumerate(chips)]
        for j, chip in enumerate(chips):
            copy(1 + j, (*chip, c), me).wait_recv()
            passed[j].start()
        copy(0, sibling, me).wait_recv()
        for j, chip in enumerate(chips):
            copy(4 + j, (*chip, 1 - c), me).wait_recv()
        for cp in first + passed:
            cp.wait_send()
        mine.wait()

    vmem = pl.BlockSpec(memory_space=pltpu.VMEM)
    return _pcall(
        body, (v,), in_specs=[vmem], out_specs=[vmem], out_shape=[_sds((N_DEV, R, C), F32)],
        scratch_shapes=[pltpu.SemaphoreType.DMA((7,)), pltpu.SemaphoreType.DMA((7,)), pltpu.SemaphoreType.DMA],
        name="gather_small")[0]


def _add_sibling(core, g, r):
    _, _, R, C = g.shape
    br = _row_block(R)

    def body(c_ref, g_ref, r_ref, o_ref):
        o_ref[...] = (g_ref[...].astype(F32) + r_ref[...].astype(F32)).astype(BF16)

    return _pcall(
        body, (core, g, r),
        grid_spec=pltpu.PrefetchScalarGridSpec(
            num_scalar_prefetch=1, grid=(N_CHIP, R // br),
            in_specs=[pl.BlockSpec((None, None, br, C), lambda k, i, c_ref: (k, c_ref[0], i, 0)),
                      pl.BlockSpec((None, None, br, C), lambda k, i, c_ref: (k, 0, i, 0))],
            out_specs=pl.BlockSpec((None, br, C), lambda k, i, c_ref: (k, i, 0))),
        out_shape=_sds((N_CHIP, R, C), BF16), sem=("parallel", "parallel"), name="rs_add_sibling")


def _by_chip(g):
    return g.reshape((N_CHIP, 2) + g.shape[1:])


def _pack_rows(vecs):
    flat = jnp.concatenate([v.reshape(-1).astype(F32) for v in vecs])
    rows = -(-flat.shape[0] // (8 * LANES)) * 8
    return jnp.pad(flat, (0, rows * LANES - flat.shape[0])).reshape(rows, LANES)


def _unpack_rows(rows, shapes):
    flat = rows.reshape(-1)
    out, o = [], 0
    for s in shapes:
        n = math.prod(s)
        out.append(flat[o:o + n].reshape(s))
        o += n
    return out


def _adam_any(w, m, v, g, name):
    shape = w.shape
    one = lambda t: t.reshape(1, -1, shape[-1])
    return tuple(t.reshape(shape) for t in _adam(one(w), one(m), one(v), [one(g)], name))


def kernel(x, norm_ffn1, w_ffn1_in, w_ffn1_out, norm_mix, w_in, sinks, w_dw, b_dw, conv_ln_g, conv_ln_b, w_out, norm_ffn2, w_ffn2_in, w_ffn2_out, final_norm, loss_target, m_norm_ffn1, m_w_ffn1_in, m_w_ffn1_out, m_norm_mix, m_w_in, m_sinks, m_w_dw, m_b_dw, m_conv_ln_g, m_conv_ln_b, m_w_out, m_norm_ffn2, m_w_ffn2_in, m_w_ffn2_out, m_final_norm, v_norm_ffn1, v_w_ffn1_in, v_w_ffn1_out, v_norm_mix, v_w_in, v_sinks, v_w_dw, v_b_dw, v_conv_ln_g, v_conv_ln_b, v_w_out, v_norm_ffn2, v_w_ffn2_in, v_w_ffn2_out, v_final_norm):
    _, S, D = x.shape
    L = norm_ffn1.shape[0]
    NF = w_ffn1_in.shape[2]
    RF = w_ffn1_out.shape[1]
    NW = w_in.shape[2]
    RO = w_out.shape[1]
    taps, CD = w_dw.shape[1], w_dw.shape[2]
    H = sinks.shape[1]
    C = N_DEV * CD
    QW = H * HEAD_DIM
    KVW = N_KV_HEADS * HEAD_DIM
    assert 2 * RF == NF and QW + C == N_DEV * RO and N_DEV * NW == QW + 2 * KVW + 2 * C
    tm = min(512, S)
    ta = min(256, S)
    tc = min(256, S)
    core = lax.axis_index("c").astype(jnp.int32).reshape(1)

    x0 = x[0]
    target = loss_target[0]

    wdw_all = _gather_small(_pack_rows([w_dw]))
    n_dw = L * taps * CD
    wdw_full = jnp.stack([wdw_all[d].reshape(-1)[:n_dw].reshape(L, taps, CD) for d in range(N_DEV)],
                         axis=2).reshape(L, taps, C)

    def shard(name, l):
        return {"a1": lambda: w_ffn1_in[l].T.astype(BF16), "a2": lambda: w_ffn2_in[l].T.astype(BF16),
                "b1": lambda: w_ffn1_out[l].astype(BF16), "b2": lambda: w_ffn2_out[l].astype(BF16),
                "wi": lambda: w_in[l].T.astype(BF16), "wo": lambda: w_out[l].astype(BF16)}[name]()

    rows_of = {"a1": NF, "a2": NF, "b1": RF, "b2": RF, "wi": NW, "wo": RO}
    full_of = {n: _sds((N_DEV * r, D), BF16) for n, r in rows_of.items()}
    slot_of = {n: (lambda ref, b, r=r: ref.at[pl.ds(b * r, r)]) for n, r in rows_of.items()}
    first, plan = _plan_gathers(L, {n: r * D * 2 for n, r in rows_of.items()})
    got = {}

    def gather(items):
        if not items:
            return None
        return _gather_comm([shard(n, l) for n, l in items], [full_of[n] for n, _ in items],
                            [slot_of[n] for n, _ in items])

    def carrying(k, call, n_own):
        items = plan.get(k, [])
        res = call(gather(items))
        got.update(zip(items, res[n_own:]))
        return res[:n_own]

    def wext_of(wi):
        q, k, v, u = wi[:QW], wi[QW:QW + KVW], wi[QW + KVW:QW + 2 * KVW], wi[QW + 2 * KVW:]
        dup = lambda t: jnp.concatenate(
            [t[HEAD_DIM * (i // 2):HEAD_DIM * (i // 2 + 1)] for i in range(2 * N_KV_HEADS)], axis=0)
        return jnp.concatenate([q, dup(k), dup(v), u], axis=0)

    got.update(zip(first, _comm_only(gather(first), "ag_first")))

    saved = []
    xc = x0
    for l in range(L):
        k0 = len(FWD_KERNELS) * l
        g1, gm, g2 = norm_ffn1[l][None], norm_mix[l][None], norm_ffn2[l][None]
        x1, gu1 = carrying(k0, lambda c: _ffn_fwd(xc, g1, got["a1", l], got["b1", l], tm, c), 2)
        wext = wext_of(got["wi", l])
        proj, = carrying(k0 + 1, lambda c: _mixin_fwd(x1, gm, wext, tm, c), 1)
        attn, = carrying(k0 + 2, lambda c: _attn_fwd(proj, sinks[l], H, c), 1)
        conv, ysave = carrying(k0 + 3, lambda c: _conv_fwd(proj, wdw_full[l], b_dw[l][None], conv_ln_g[l][None],
                                                           conv_ln_b[l][None], H, tc, c), 2)
        x2, = carrying(k0 + 4, lambda c: _mixout_fwd(x1, attn, conv, got["wo", l], tm, c), 1)
        x3, gu2 = carrying(k0 + 5, lambda c: _ffn_fwd(x2, g2, got["a2", l], got["b2", l], tm, c), 2)
        W = dict(a1=got["a1", l], b1=got["b1", l], a2=got["a2", l], b2=got["b2", l], wo=got["wo", l], wext=wext)
        saved.append(dict(W=W, x0=xc, x1=x1, x2=x2, gu1=gu1, gu2=gu2, proj=proj, attn=attn, conv=conv, ysave=ysave))
        xc = x3

    dx, loss_parts, dfinal = _loss_head(xc, final_norm[None], target, tm)
    loss = lax.psum(jnp.sum(loss_parts[:, 0, 0]), ("x", "y", "c"))

    def swap(gs):
        return _swap_comm([_by_chip(g) for g in gs]) if gs else None

    def added(gs, gots):
        return [_add_sibling(core, _by_chip(g), r) for g, r in zip(gs, gots)]

    small = [None] * L
    big = [dict() for _ in range(L)]
    carry = None
    for l in reversed(range(L)):
        sv = saved[l]
        W = sv["W"]
        g1, gm, g2 = norm_ffn1[l][None], norm_mix[l][None], norm_ffn2[l][None]
        fold = lambda t: jnp.concatenate(
            [t[2 * HEAD_DIM * i:2 * HEAD_DIM * i + HEAD_DIM] + t[2 * HEAD_DIM * i + HEAD_DIM:2 * HEAD_DIM * (i + 1)]
             for i in range(N_KV_HEADS)], axis=0)

        r = _ffn_bwd_a(sv["x2"], g2, dx, sv["gu2"], W["a2"], W["b2"], ta, swap(carry[1]) if carry else None)
        dx2, dgu2, dg2, h2, dys2 = r[:5]
        p_carry = added(carry[1], r[5:]) if carry else None
        r = _ffn_bwd_w(h2, dys2, sv["gu2"], dgu2, tm, _exchange_comm(p_carry) if carry else None)
        da2, db2 = r[0].reshape(N_DEV, NF, D), r[1].reshape(N_DEV, RF, D)
        if carry:
            big[carry[0]]["a1"], big[carry[0]]["b1"] = r[2:]
        r = _mixout_bwd(dx2, sv["attn"], sv["conv"], W["wo"], tm, swap([da2, db2]))
        dmix, dwo = r[0], r[1].reshape(N_DEV, RO, D)
        p_a2, p_b2 = added([da2, db2], r[2:])
        r = _conv_bwd(sv["proj"], dmix, sv["ysave"], wdw_full[l], conv_ln_g[l][None], conv_ln_b[l][None], H, tc,
                      _join(_exchange_comm([p_b2]), swap([dwo])))
        dag, dwdw, dvec = r[:3]
        big[l]["b2"] = r[3]
        p_wo, = added([dwo], r[4:])
        r = _attn_bwd(sv["proj"], dmix, sinks[l], H, _exchange_comm([p_a2]))
        dq, dkv, dsink = r[:3]
        big[l]["a2"] = r[3]
        r = _mixin_bwd(sv["x1"], gm, dx2, dq, dkv, dag, W["wext"], tm, _exchange_comm([p_wo]))
        dx1, dgm, dwext = r[:3]
        big[l]["wo"] = r[3]
        dwi = jnp.concatenate([dwext[:QW], fold(dwext[QW:QW + KV_DUP]),
                               fold(dwext[QW + KV_DUP:QW + 2 * KV_DUP]), dwext[QW + 2 * KV_DUP:]], axis=0)
        dwi = dwi.astype(BF16).reshape(N_DEV, NW, D)
        r = _ffn_bwd_a(sv["x0"], g1, dx1, sv["gu1"], W["a1"], W["b1"], ta, swap([dwi]))
        dx0, dgu1, dg1, h1, dys1 = r[:5]
        p_wi, = added([dwi], r[5:])
        r = _ffn_bwd_w(h1, dys1, sv["gu1"], dgu1, tm, _exchange_comm([p_wi]))
        carry = (l, [r[0].reshape(N_DEV, NF, D), r[1].reshape(N_DEV, RF, D)])
        big[l]["wi"] = r[2]
        dx = dx0
        small[l] = [dg1[0], dgm[0], dsink[0, :H], dwdw[:taps], dvec[0], dvec[1], dvec[2], dg2[0]]

    grad_x = dx[None]

    small_shapes = [(D,), (D,), (H,), (taps, C), (C,), (C,), (C,), (D,)]
    packed = _pack_rows([t for l in range(L) for t in small[l]] + [dfinal[0]])
    total = _sum_parts(_gather_small(packed))
    flat = _unpack_rows(total, small_shapes * L + [(D,)])
    per = [jnp.stack([flat[l * len(small_shapes) + i] for l in range(L)]) for i in range(len(small_shapes))]
    g_nf1, g_nmix, g_sinks, g_wdw_full, g_bdw, g_lng, g_lnb, g_nf2 = per
    g_final = flat[-1]
    dev = _dev(lax.axis_index("x"), lax.axis_index("y"), lax.axis_index("c"))
    g_wdw = lax.dynamic_slice_in_dim(g_wdw_full, dev * CD, CD, axis=2)

    res = {}
    res["norm_ffn1"] = _adam_any(norm_ffn1, m_norm_ffn1, v_norm_ffn1, g_nf1, "adam_small")
    res["norm_mix"] = _adam_any(norm_mix, m_norm_mix, v_norm_mix, g_nmix, "adam_small")
    res["sinks"] = _adam_any(sinks, m_sinks, v_sinks, g_sinks, "adam_small")
    res["w_dw"] = _adam_any(w_dw, m_w_dw, v_w_dw, g_wdw, "adam_small")
    res["b_dw"] = _adam_any(b_dw, m_b_dw, v_b_dw, g_bdw, "adam_small")
    res["conv_ln_g"] = _adam_any(conv_ln_g, m_conv_ln_g, v_conv_ln_g, g_lng, "adam_small")
    res["conv_ln_b"] = _adam_any(conv_ln_b, m_conv_ln_b, v_conv_ln_b, g_lnb, "adam_small")
    res["norm_ffn2"] = _adam_any(norm_ffn2, m_norm_ffn2, v_norm_ffn2, g_nf2, "adam_small")
    res["final_norm"] = tuple(t[0] for t in _adam_any(final_norm[None], m_final_norm[None], v_final_norm[None],
                                                      g_final[None], "adam_small"))

    def adam_big(key, w, m, v, name, transposed=False, comm=None):
        t = (lambda a: a.transpose(0, 2, 1)) if transposed else (lambda a: a)
        r = _adam(t(w), t(m), t(v), [big[l][key] for l in range(L)], name, comm)
        return tuple(t(o) for o in r[:4]), r[4:]

    res["w_ffn2_in"], r = adam_big("a2", w_ffn2_in, m_w_ffn2_in, v_w_ffn2_in, "adam_ffn_in", True, swap(carry[1]))
    p_a1, p_b1 = added(carry[1], r)
    res["w_ffn2_out"], r = adam_big("b2", w_ffn2_out, m_w_ffn2_out, v_w_ffn2_out, "adam_ffn_out", False,
                                    _exchange_comm([p_b1]))
    big[carry[0]]["b1"], = r
    res["w_in"], r = adam_big("wi", w_in, m_w_in, v_w_in, "adam_w_in", True, _exchange_comm([p_a1]))
    big[carry[0]]["a1"], = r
    res["w_out"], _ = adam_big("wo", w_out, m_w_out, v_w_out, "adam_w_out")
    res["w_ffn1_in"], _ = adam_big("a1", w_ffn1_in, m_w_ffn1_in, v_w_ffn1_in, "adam_ffn_in", True)
    res["w_ffn1_out"], _ = adam_big("b1", w_ffn1_out, m_w_ffn1_out, v_w_ffn1_out, "adam_ffn_out")

    order = ["norm_ffn1", "w_ffn1_in", "w_ffn1_out", "norm_mix", "w_in", "sinks", "w_dw", "b_dw", "conv_ln_g",
             "conv_ln_b", "w_out", "norm_ffn2", "w_ffn2_in", "w_ffn2_out", "final_norm"]
    return (loss, grad_x, *[res[n][0] for n in order], *[res[n][1] for n in order],
            *[res[n][2] for n in order], *[res[n][3] for n in order])
```

```python
import functools
import math

import jax
import jax.numpy as jnp
from jax import lax
from jax.experimental import pallas as pl
from jax.experimental.pallas import tpu as pltpu

F32 = jnp.float32
BF16 = jnp.bfloat16
MESH = pl.DeviceIdType.MESH

N_DEV = 8
N_CHIP = 4
HEAD_DIM = 64
N_KV_HEADS = 2
WINDOW = 128
KV_DUP = 2 * HEAD_DIM * N_KV_HEADS
RMS_EPS = 1e-6
NEG_INF = -1e30
FFN_RES = 0.5
HALO = 32
ROWS = 32
FFN_CHUNK = 512
BWD_W_CHUNK = 384
LANES = 128
SUBLANES = 8
V7X_VMEM_LIMIT = 56 * 1024 * 1024

ADAM_LR = 0.001
ADAM_B1 = 0.9
ADAM_B2 = 0.999
ADAM_EPS = 1e-08
ADAM_WD = 0.01
ADAM_STEP = 10


def _raw_call(body, **kw):
    return pl.pallas_call(body, **kw)


class _Comm:
    def __init__(self, ins, outs, sems, start, finish, mid=None):
        self.ins, self.outs, self.sems, self.start, self.finish = list(ins), list(outs), list(sems), start, finish
        self.mid = mid or (lambda ins, outs, sems: None)


def _join(*comms):
    comms = [c for c in comms if c is not None]
    if not comms:
        return None

    def split(refs, attr):
        out, o = [], 0
        for c in comms:
            n = len(getattr(c, attr))
            out.append(refs[o:o + n])
            o += n
        return out

    def run(which):
        def go(ins, outs, sems):
            for c, i, o, m in zip(comms, split(ins, "ins"), split(outs, "outs"), split(sems, "sems")):
                getattr(c, which)(i, o, m)
        return go

    return _Comm(sum((c.ins for c in comms), []), sum((c.outs for c in comms), []),
                 sum((c.sems for c in comms), []), run("start"), run("finish"), run("mid"))


def _pcall(body, args, *, name, out_shape, grid=(), in_specs=None, out_specs=None, scratch_shapes=(), sem=(),
           comm=None, grid_spec=None):
    if grid_spec is not None:
        return _raw_call(body, grid_spec=grid_spec, out_shape=out_shape, name=name,
                         compiler_params=_params(*sem))(*args)
    if comm is None:
        return _raw_call(body, grid=grid, in_specs=in_specs, out_specs=out_specs, out_shape=out_shape,
                         scratch_shapes=list(scratch_shapes), name=name, compiler_params=_params(*sem))(*args)
    n_in, n_out, n_scr = len(in_specs), len(out_shape), len(scratch_shapes)
    ci, co = len(comm.ins), len(comm.outs)

    def fused(*refs):
        cuts = [n_in, ci, n_out, co, n_scr]
        parts, o = [], 0
        for n in cuts:
            parts.append(refs[o:o + n])
            o += n
        ins, cins, outs, couts, scr = parts
        csems = refs[o:]
        steps = math.prod(grid)
        if steps < 3:
            comm.start(cins, couts, csems)
            body(*ins, *outs, *scr)
            comm.mid(cins, couts, csems)
            comm.finish(cins, couts, csems)
            return
        step = functools.reduce(lambda acc, a: acc * grid[a] + pl.program_id(a), range(len(grid)), 0)

        @pl.when(step == 0)
        def _():
            comm.start(cins, couts, csems)

        @pl.when(step == steps - 2)
        def _():
            comm.mid(cins, couts, csems)

        body(*ins, *outs, *scr)

        @pl.when(step == steps - 1)
        def _():
            comm.finish(cins, couts, csems)

    return _raw_call(
        fused, grid=grid, in_specs=list(in_specs) + [ANY] * ci, out_specs=list(out_specs) + [ANY] * co,
        out_shape=list(out_shape) + comm.outs, scratch_shapes=list(scratch_shapes) + comm.sems, name=name,
        compiler_params=_params(*(["arbitrary"] * len(grid))))(*args, *comm.ins)


ANY = pl.BlockSpec(memory_space=pl.ANY)


def _params(*sem):
    return pltpu.CompilerParams(dimension_semantics=sem, vmem_limit_bytes=V7X_VMEM_LIMIT)


def _dot(a, b):
    return jnp.dot(a, b, preferred_element_type=F32)


def _dot_nt(a, b):
    return lax.dot_general(a, b, (((1,), (1,)), ((), ())), preferred_element_type=F32)


def _dot_tn(a, b):
    return lax.dot_general(a, b, (((0,), (0,)), ((), ())), preferred_element_type=F32)


def _sigmoid(x):
    return 1.0 / (1.0 + jnp.exp(-x))


def _rms(x):
    r = lax.rsqrt(jnp.mean(x * x, axis=-1, keepdims=True) + RMS_EPS)
    return x * r, r


def _rms_bwd(dh, xh, r, g):
    dxh = dh * g
    dx = r * (dxh - xh * jnp.mean(dxh * xh, axis=-1, keepdims=True))
    return dx, jnp.sum(dh * xh, axis=0, keepdims=True)


def _sds(shape, dtype):
    return jax.ShapeDtypeStruct(shape, dtype)


def _row_block(rows, limit=512):
    fits = [d for d in range(16, min(rows, limit) + 1, 16) if rows % d == 0]
    return fits[-1] if fits else rows


def _chunks(n, step):
    return [(o, min(step, n - o)) for o in range(0, n, step)]


def _resident(shape):
    return pl.BlockSpec(shape, lambda *_: (0,) * len(shape), pipeline_mode=pl.Buffered(1))


def _ffn_fwd(x, gain, wint, wout, tm, comm=None):
    S, D = x.shape
    F = wout.shape[0]

    def body(x_ref, g_ref, w_ref, wo_ref, xo_ref, gu_ref, a_ref):
        xh, _ = _rms(x_ref[...])
        h = (xh * g_ref[...]).astype(BF16)
        for o, n in _chunks(F, FFN_CHUNK):
            gb = _dot_nt(h, w_ref[o:o + n, :]).astype(BF16)
            ub = _dot_nt(h, w_ref[F + o:F + o + n, :]).astype(BF16)
            gu_ref[:, o:o + n] = gb
            gu_ref[:, F + o:F + o + n] = ub
            g = gb.astype(F32)
            a_ref[:, o:o + n] = (g * _sigmoid(g) * ub.astype(F32)).astype(BF16)
        xo_ref[...] = x_ref[...] + FFN_RES * _dot(a_ref[...], wo_ref[...])

    return _pcall(
        body, (x, gain, wint, wout), grid=(S // tm,),
        in_specs=[pl.BlockSpec((tm, D), lambda i: (i, 0)), _resident((1, D)),
                  _resident((2 * F, D)), _resident((F, D))],
        out_specs=[pl.BlockSpec((tm, D), lambda i: (i, 0)), pl.BlockSpec((tm, 2 * F), lambda i: (i, 0))],
        out_shape=[_sds((S, D), F32), _sds((S, 2 * F), BF16)],
        scratch_shapes=[pltpu.VMEM((tm, F), BF16)],
        sem=("parallel",), name="ffn_fwd", comm=comm)


def _ffn_bwd_a(x, gain, dxo, gu, wint, wout, tm, comm=None):
    S, D = x.shape
    F = wout.shape[0]

    def body(x_ref, g_ref, dxo_ref, gu_ref, w_ref, wo_ref, dx_ref, dgu_ref, dgain_ref, h_ref, dys_ref):
        i = pl.program_id(0)
        dys = (FFN_RES * dxo_ref[...]).astype(BF16)
        dys_ref[...] = dys
        for o, n in _chunks(F, FFN_CHUNK):
            dact = _dot_nt(dys, wo_ref[o:o + n, :])
            g = gu_ref[:, o:o + n].astype(F32)
            u = gu_ref[:, F + o:F + o + n].astype(F32)
            s = _sigmoid(g)
            dgu_ref[:, o:o + n] = (dact * u * (s * (1.0 + g * (1.0 - s)))).astype(BF16)
            dgu_ref[:, F + o:F + o + n] = (dact * (g * s)).astype(BF16)
        dh = _dot(dgu_ref[...], w_ref[...])
        xh, r = _rms(x_ref[...])
        h_ref[...] = (xh * g_ref[...]).astype(BF16)
        dxn, dgn = _rms_bwd(dh, xh, r, g_ref[...])
        dx_ref[...] = dxo_ref[...] + dxn

        @pl.when(i == 0)
        def _():
            dgain_ref[...] = dgn

        @pl.when(i > 0)
        def _():
            dgain_ref[...] += dgn

    tile = pl.BlockSpec((tm, D), lambda i: (i, 0))
    wide = pl.BlockSpec((tm, 2 * F), lambda i: (i, 0))
    return _pcall(
        body, (x, gain, dxo, gu, wint, wout), grid=(S // tm,),
        in_specs=[tile, _resident((1, D)), tile, wide, _resident((2 * F, D)), _resident((F, D))],
        out_specs=[tile, wide, pl.BlockSpec((1, D), lambda i: (0, 0)), tile, tile],
        out_shape=[_sds((S, D), F32), _sds((S, 2 * F), BF16), _sds((1, D), F32), _sds((S, D), BF16), _sds((S, D), BF16)],
        sem=("arbitrary",), name="ffn_bwd_a", comm=comm)


def _ffn_bwd_w(h, dys, gu, dgu, tk, comm=None):
    S, D = h.shape
    F = gu.shape[1] // 2
    FH = F // 2
    nk = S // tk

    def body(h_ref, dys_ref, gg_ref, gu_ref, dg_ref, du_ref, dw_ref, dwo_ref, accw_ref, acco_ref):
        k = pl.program_id(1)

        @pl.when(k == 0)
        def _():
            accw_ref[...] = jnp.zeros_like(accw_ref)
            acco_ref[...] = jnp.zeros_like(acco_ref)

        hv, dys = h_ref[...], dys_ref[...]
        for o, n in _chunks(FH, BWD_W_CHUNK):
            g = gg_ref[:, o:o + n].astype(F32)
            act = (g * _sigmoid(g) * gu_ref[:, o:o + n].astype(F32)).astype(BF16)
            accw_ref[0, o:o + n, :] += _dot_tn(dg_ref[:, o:o + n], hv)
            accw_ref[1, o:o + n, :] += _dot_tn(du_ref[:, o:o + n], hv)
            acco_ref[o:o + n, :] += _dot_tn(act, dys)

        @pl.when(k == nk - 1)
        def _():
            dw_ref[...] = accw_ref[...].astype(BF16)
            dwo_ref[...] = acco_ref[...].astype(BF16)

    tile = pl.BlockSpec((tk, D), lambda j, k: (k, 0))
    gate = pl.BlockSpec((tk, FH), lambda j, k: (k, j))
    up = pl.BlockSpec((tk, FH), lambda j, k: (k, j + 2))
    return _pcall(
        body, (h, dys, gu, gu, dgu, dgu), grid=(2, nk),
        in_specs=[tile, tile, gate, up, gate, up],
        out_specs=[pl.BlockSpec((2, FH, D), lambda j, k: (0, j, 0), pipeline_mode=pl.Buffered(1)),
                   pl.BlockSpec((FH, D), lambda j, k: (j, 0), pipeline_mode=pl.Buffered(1))],
        out_shape=[_sds((2, F, D), BF16), _sds((F, D), BF16)],
        scratch_shapes=[pltpu.VMEM((2, FH, D), F32), pltpu.VMEM((FH, D), F32)],
        sem=("parallel", "arbitrary"), name="ffn_bwd_w", comm=comm)


def _mixin_fwd(x, gain, wext, tm, comm=None):
    S, D = x.shape
    PW = wext.shape[0]

    def body(x_ref, g_ref, w_ref, p_ref):
        xh, _ = _rms(x_ref[...])
        p_ref[...] = _dot_nt((xh * g_ref[...]).astype(BF16), w_ref[...]).astype(BF16)

    return _pcall(
        body, (x, gain, wext), grid=(S // tm,),
        in_specs=[pl.BlockSpec((tm, D), lambda i: (i, 0)), _resident((1, D)), _resident((PW, D))],
        out_specs=[pl.BlockSpec((tm, PW), lambda i: (i, 0))],
        out_shape=[_sds((S, PW), BF16)],
        sem=("parallel",), name="mixin_fwd", comm=comm)


def _mixin_bwd(x, gain, dxo, dq, dkv, dag, wext, tm, comm=None):
    S, D = x.shape
    PW = wext.shape[0]
    QW = dq.shape[1]
    o1, o2 = QW, QW + 2 * KV_DUP

    def body(x_ref, g_ref, dxo_ref, dq_ref, dkv_ref, dag_ref, w_ref, dx_ref, dgain_ref, dw_ref):
        i = pl.program_id(0)
        xh, r = _rms(x_ref[...])
        h = (xh * g_ref[...]).astype(BF16)
        dqv, dkvv, dagv = dq_ref[...], dkv_ref[...], dag_ref[...]
        dh = _dot(dqv, w_ref[0:o1, :]) + _dot(dkvv, w_ref[o1:o2, :]) + _dot(dagv, w_ref[o2:PW, :])
        dxn, dgn = _rms_bwd(dh, xh, r, g_ref[...])
        dx_ref[...] = dxo_ref[...] + dxn

        @pl.when(i == 0)
        def _():
            dgain_ref[...] = dgn
            dw_ref[0:o1, :] = _dot_tn(dqv, h)
            dw_ref[o1:o2, :] = _dot_tn(dkvv, h)
            dw_ref[o2:PW, :] = _dot_tn(dagv, h)

        @pl.when(i > 0)
        def _():
            dgain_ref[...] += dgn
            dw_ref[0:o1, :] += _dot_tn(dqv, h)
            dw_ref[o1:o2, :] += _dot_tn(dkvv, h)
            dw_ref[o2:PW, :] += _dot_tn(dagv, h)

    return _pcall(
        body, (x, gain, dxo, dq, dkv, dag, wext), grid=(S // tm,),
        in_specs=[pl.BlockSpec((tm, D), lambda i: (i, 0)), _resident((1, D)),
                  pl.BlockSpec((tm, D), lambda i: (i, 0)),
                  pl.BlockSpec((tm, QW), lambda i: (i, 0)),
                  pl.BlockSpec((tm, 2 * KV_DUP), lambda i: (i, 0)),
                  pl.BlockSpec((tm, PW - o2), lambda i: (i, 0)),
                  _resident((PW, D))],
        out_specs=[pl.BlockSpec((tm, D), lambda i: (i, 0)),
                   pl.BlockSpec((1, D), lambda i: (0, 0)),
                   pl.BlockSpec((PW, D), lambda i: (0, 0))],
        out_shape=[_sds((S, D), F32), _sds((1, D), F32), _sds((PW, D), F32)],
        sem=("arbitrary",), name="mixin_bwd", comm=comm)


def _attn_group(n, group, slopes, sinks):
    rows = group * WINDOW
    r = lax.broadcasted_iota(jnp.int32, (rows, 2 * WINDOW), 0)
    s = lax.broadcasted_iota(jnp.int32, (rows, 2 * WINDOW), 1)
    dist = (r & (WINDOW - 1)) + WINDOW - s
    valid = (dist >= 0) & (dist < WINDOW) & jnp.logical_or(n > 0, s >= WINDOW)
    seg = lax.shift_right_logical(lax.broadcasted_iota(jnp.int32, (rows, 1), 0), WINDOW.bit_length() - 1)
    slope = jnp.zeros((rows, 1), F32)
    sink = jnp.zeros((rows, 1), F32)
    for i in range(group):
        slope = jnp.where(seg == i, slopes[i], slope)
        sink = jnp.where(seg == i, sinks[i], sink)
    return valid, -slope * dist.astype(F32), sink


def _lane_halves():
    lo = lax.broadcasted_iota(jnp.int32, (WINDOW, LANES), 1) < HEAD_DIM
    return lo, [jnp.where(lo, 1.0, 0.0).astype(BF16), jnp.where(lo, 0.0, 1.0).astype(BF16)]


def _stack_heads(ref, first_tile, n_tiles, halves):
    parts = []
    for t in range(first_tile, first_tile + n_tiles):
        tile = ref[:, LANES * t:LANES * (t + 1)]
        parts += [tile * halves[0], tile * halves[1]]
    return jnp.concatenate(parts, axis=0)


def _unstack_heads(ref, first_tile, n_tiles, lo, stacked):
    for i in range(n_tiles):
        a = stacked[2 * i * WINDOW:(2 * i + 1) * WINDOW]
        b = stacked[(2 * i + 1) * WINDOW:(2 * i + 2) * WINDOW]
        t = first_tile + i
        ref[:, LANES * t:LANES * (t + 1)] = jnp.where(lo, a, b).astype(ref.dtype)


def _attn_probs(qs, kd, scale, bias, valid, sink):
    sc = jnp.where(valid, _dot_nt(qs, kd) * scale + bias, NEG_INF)
    m = jnp.maximum(jnp.max(sc, axis=-1, keepdims=True), sink)
    p = jnp.exp(sc - m)
    es = jnp.exp(sink - m)
    inv = 1.0 / (jnp.sum(p, axis=-1, keepdims=True) + es)
    return p * inv, es * inv


def _attn_specs(n_heads, nb):
    QW = n_heads * HEAD_DIM
    kblk, vblk = QW // KV_DUP, QW // KV_DUP + 1
    last = nb - 1
    cur = lambda b: (lambda n: (jnp.minimum(n, last), b))
    prev = lambda b: (lambda n: (jnp.clip(n - 1, 0, last), b))
    kv = [pl.BlockSpec((WINDOW, KV_DUP), cur(kblk)), pl.BlockSpec((WINDOW, KV_DUP), prev(kblk)),
          pl.BlockSpec((WINDOW, KV_DUP), cur(vblk)), pl.BlockSpec((WINDOW, KV_DUP), prev(vblk))]
    return QW, cur, prev, kv


def _attn_fwd(proj, sinks, n_heads, comm=None):
    S = proj.shape[0]
    nb = S // WINDOW
    group = n_heads // N_KV_HEADS
    slopes = [2.0 ** (-8.0 * (h + 1) / n_heads) for h in range(n_heads)]
    scale = 1.0 / math.sqrt(HEAD_DIM)
    QW, cur, prev, kv_specs = _attn_specs(n_heads, nb)

    def body(sink_ref, q_ref, kc_ref, kp_ref, vc_ref, vp_ref, o_ref):
        n = pl.program_id(0)
        lo, halves = _lane_halves()
        for kh in range(N_KV_HEADS):
            heads = range(kh * group, (kh + 1) * group)
            valid, bias, sink = _attn_group(n, group, [slopes[h] for h in heads], [sink_ref[h] for h in heads])
            ksl = slice(LANES * kh, LANES * (kh + 1))
            kd = jnp.concatenate([kp_ref[:, ksl], kc_ref[:, ksl]], axis=0)
            vd = jnp.concatenate([vp_ref[:, ksl], vc_ref[:, ksl]], axis=0)
            qs = _stack_heads(q_ref, kh * group // 2, group // 2, halves)
            pn, _ = _attn_probs(qs, kd, scale, bias, valid, sink)
            _unstack_heads(o_ref, kh * group // 2, group // 2, lo, _dot(pn.astype(BF16), vd))

    return _pcall(
        body, (sinks, proj, proj, proj, proj, proj), grid=(nb,),
        in_specs=[pl.BlockSpec(memory_space=pltpu.SMEM), pl.BlockSpec((WINDOW, QW), cur(0))] + kv_specs,
        out_specs=[pl.BlockSpec((WINDOW, QW), cur(0))],
        out_shape=[_sds((S, QW), BF16)],
        sem=("parallel",), name="attn_fwd", comm=comm)


def _attn_bwd(proj, dmix, sinks, n_heads, comm=None):
    S = proj.shape[0]
    nb = S // WINDOW
    group = n_heads // N_KV_HEADS
    slopes = [2.0 ** (-8.0 * (h + 1) / n_heads) for h in range(n_heads)]
    scale = 1.0 / math.sqrt(HEAD_DIM)
    QW, cur, prev, kv_specs = _attn_specs(n_heads, nb)

    def body(sink_ref, q_ref, kc_ref, kp_ref, vc_ref, vp_ref, do_ref, dq_ref, dkv_ref, dsink_ref, carry_ref):
        n = pl.program_id(0)

        @pl.when(n == 0)
        def _():
            carry_ref[...] = jnp.zeros_like(carry_ref)
            dsink_ref[...] = jnp.zeros_like(dsink_ref)

        @pl.when(n < nb)
        def _():
            lo, halves = _lane_halves()
            lane1 = lax.broadcasted_iota(jnp.int32, (1, LANES), 1)
            dkd, dvd = [], []
            dsink = jnp.zeros((1, LANES), F32)
            for kh in range(N_KV_HEADS):
                heads = range(kh * group, (kh + 1) * group)
                valid, bias, sink = _attn_group(n, group, [slopes[h] for h in heads], [sink_ref[h] for h in heads])
                ksl = slice(LANES * kh, LANES * (kh + 1))
                kd = jnp.concatenate([kp_ref[:, ksl], kc_ref[:, ksl]], axis=0)
                vd = jnp.concatenate([vp_ref[:, ksl], vc_ref[:, ksl]], axis=0)
                qs = _stack_heads(q_ref, kh * group // 2, group // 2, halves)
                dos = _stack_heads(do_ref, kh * group // 2, group // 2, halves)
                pn, psink = _attn_probs(qs, kd, scale, bias, valid, sink)
                dp = _dot_nt(dos, vd)
                delta = jnp.sum(pn * dp, axis=-1, keepdims=True)
                dsb = (pn * (dp - delta) * scale).astype(BF16)
                sd = psink * delta
                for i, h in enumerate(heads):
                    dsink = dsink - jnp.where(lane1 == h, jnp.sum(sd[i * WINDOW:(i + 1) * WINDOW]), 0.0)
                _unstack_heads(dq_ref, kh * group // 2, group // 2, lo, _dot(dsb, kd))
                dkd.append(_dot_tn(dsb, qs))
                dvd.append(_dot_tn(pn.astype(BF16), dos))
            dsink_ref[...] += dsink
            both = jnp.concatenate(dkd + dvd, axis=1)
            dkv_ref[...] = (carry_ref[...] + both[0:WINDOW]).astype(BF16)
            carry_ref[...] = both[WINDOW:2 * WINDOW]

        @pl.when(n == nb)
        def _():
            dkv_ref[...] = carry_ref[...].astype(BF16)

    return _pcall(
        body, (sinks, proj, proj, proj, proj, proj, dmix), grid=(nb + 1,),
        in_specs=[pl.BlockSpec(memory_space=pltpu.SMEM), pl.BlockSpec((WINDOW, QW), cur(0))] + kv_specs
                 + [pl.BlockSpec((WINDOW, QW), cur(0))],
        out_specs=[pl.BlockSpec((WINDOW, QW), cur(0)),
                   pl.BlockSpec((WINDOW, 2 * KV_DUP), prev(0)),
                   pl.BlockSpec((1, LANES), lambda n: (0, 0))],
        out_shape=[_sds((S, QW), BF16), _sds((S, 2 * KV_DUP), BF16), _sds((1, LANES), F32)],
        scratch_shapes=[pltpu.VMEM((WINDOW, 2 * KV_DUP), F32)],
        sem=("arbitrary",), name="attn_bwd", comm=comm)


def _glu_window(a_ref, g_ref, ap_ref, gp_ref, win_ref, first):
    tm = a_ref.shape[0]
    zp = ap_ref[...].astype(F32) * _sigmoid(gp_ref[...].astype(F32))
    win_ref[0:HALO, :] = jnp.where(first, jnp.zeros_like(zp), zp)
    win_ref[HALO:HALO + tm, :] = a_ref[...].astype(F32) * _sigmoid(g_ref[...].astype(F32))


def _preshift(win_ref, sh_ref):
    n = win_ref.shape[0] - SUBLANES
    for s in range(1, SUBLANES):
        sh_ref[s - 1, 0:n, :] = win_ref[s:s + n, :]


def _window(win_ref, sh_ref, start):
    s = start % SUBLANES
    if s == 0:
        return win_ref[start:start + ROWS, :]
    return sh_ref[s - 1, start - s:start - s + ROWS, :]


def _conv_fwd(proj, wdw, bdw, lng, lnb, n_heads, tm, comm=None):
    S = proj.shape[0]
    taps, C = wdw.shape
    ablk = (n_heads * HEAD_DIM + 2 * KV_DUP) // C
    hb = tm // HALO
    off = HALO - (taps - 1)

    def body(a_ref, g_ref, ap_ref, gp_ref, w_ref, b_ref, lg_ref, lb_ref, o_ref, y_ref, win_ref, sh_ref):
        _glu_window(a_ref, g_ref, ap_ref, gp_ref, win_ref, pl.program_id(0) == 0)
        _preshift(win_ref, sh_ref)
        for c in range(tm // ROWS):
            r0 = c * ROWS
            acc = jnp.zeros((ROWS, C), F32) + b_ref[...]
            for k in range(taps):
                acc = acc + w_ref[k:k + 1, :] * _window(win_ref, sh_ref, r0 + off + k)
            y_ref[r0:r0 + ROWS, :] = acc
        y = y_ref[...]
        mu = jnp.mean(y, axis=-1, keepdims=True)
        yc = y - mu
        yn = yc * lax.rsqrt(jnp.mean(yc * yc, axis=-1, keepdims=True) + RMS_EPS) * lg_ref[...] + lb_ref[...]
        o_ref[...] = (yn * _sigmoid(yn)).astype(BF16)

    vec = pl.BlockSpec((1, C), lambda i: (0, 0))
    halo = lambda b: pl.BlockSpec((HALO, C), lambda i: (jnp.maximum(i * hb - 1, 0), b))
    return _pcall(
        body, (proj, proj, proj, proj, wdw, bdw, lng, lnb,), grid=(S // tm,),
        in_specs=[pl.BlockSpec((tm, C), lambda i: (i, ablk)), pl.BlockSpec((tm, C), lambda i: (i, ablk + 1)),
                  halo(ablk), halo(ablk + 1),
                  pl.BlockSpec((taps, C), lambda i: (0, 0)), vec, vec, vec],
        out_specs=[pl.BlockSpec((tm, C), lambda i: (i, 0)), pl.BlockSpec((tm, C), lambda i: (i, 0))],
        out_shape=[_sds((S, C), BF16), _sds((S, C), F32)],
        scratch_shapes=[pltpu.VMEM((tm + HALO, C), F32), pltpu.VMEM((SUBLANES - 1, tm + HALO, C), F32)],
        sem=("parallel",), name="conv_fwd", comm=comm)


def _conv_bwd(proj, dmix, ysave, wdw, lng, lnb, n_heads, tm, comm=None):
    S = proj.shape[0]
    taps, C = wdw.shape
    QW = n_heads * HEAD_DIM
    ablk = (QW + 2 * KV_DUP) // C
    cblk = QW // C
    hb = tm // HALO
    nt = S // tm
    off = HALO - (taps - 1)

    def ln_bwd(dc, y, lg, lb):
        mu = jnp.mean(y, axis=-1, keepdims=True)
        yc = y - mu
        r = lax.rsqrt(jnp.mean(yc * yc, axis=-1, keepdims=True) + RMS_EPS)
        yh = yc * r
        yn = yh * lg + lb
        sg = _sigmoid(yn)
        dyn = dc * (sg * (1.0 + yn * (1.0 - sg)))
        dyh = dyn * lg
        dy = r * (dyh - jnp.mean(dyh, axis=-1, keepdims=True) - yh * jnp.mean(dyh * yh, axis=-1, keepdims=True))
        return dy, dyn, yh

    def body(dc_ref, dcn_ref, y_ref, yn_ref, a_ref, g_ref, ap_ref, gp_ref, w_ref, lg_ref, lb_ref,
             dag_ref, dw_ref, dvec_ref, zwin_ref, dyw_ref, dwacc_ref, zsh_ref, dysh_ref):
        i = pl.program_id(0)

        @pl.when(i == 0)
        def _():
            dwacc_ref[...] = jnp.zeros_like(dwacc_ref)
            dvec_ref[...] = jnp.zeros_like(dvec_ref)

        lg, lb = lg_ref[...], lb_ref[...]
        dy, dyn, yh = ln_bwd(dc_ref[...].astype(F32), y_ref[...], lg, lb)
        dy_next, _, _ = ln_bwd(dcn_ref[...].astype(F32), yn_ref[...], lg, lb)
        dyw_ref[0:tm, :] = dy
        dyw_ref[tm:tm + HALO, :] = jnp.where(i == nt - 1, jnp.zeros_like(dy_next), dy_next)
        dvec_ref[0:1, :] += jnp.sum(dy, axis=0, keepdims=True)
        dvec_ref[1:2, :] += jnp.sum(dyn * yh, axis=0, keepdims=True)
        dvec_ref[2:3, :] += jnp.sum(dyn, axis=0, keepdims=True)
        _glu_window(a_ref, g_ref, ap_ref, gp_ref, zwin_ref, i == 0)
        _preshift(zwin_ref, zsh_ref)
        _preshift(dyw_ref, dysh_ref)

        for c in range(tm // ROWS):
            r0 = c * ROWS
            dz = jnp.zeros((ROWS, C), F32)
            dyc = dyw_ref[r0:r0 + ROWS, :]
            for k in range(taps):
                dz = dz + w_ref[k:k + 1, :] * _window(dyw_ref, dysh_ref, r0 + taps - 1 - k)
                prod = dyc * _window(zwin_ref, zsh_ref, r0 + off + k)
                dwacc_ref[k] += jnp.sum(prod.reshape(ROWS // SUBLANES, SUBLANES, C), axis=0)
            a = a_ref[r0:r0 + ROWS, :].astype(F32)
            s = _sigmoid(g_ref[r0:r0 + ROWS, :].astype(F32))
            dag_ref[r0:r0 + ROWS, 0:C] = (dz * s).astype(BF16)
            dag_ref[r0:r0 + ROWS, C:2 * C] = (dz * a * s * (1.0 - s)).astype(BF16)

        @pl.when(i == nt - 1)
        def _():
            dw_ref[...] = jnp.zeros_like(dw_ref)
            for k in range(taps):
                dw_ref[k:k + 1, :] = jnp.sum(dwacc_ref[k], axis=0, keepdims=True)

    vec = pl.BlockSpec((1, C), lambda i: (0, 0))
    tile = lambda b: pl.BlockSpec((tm, C), lambda i: (i, b))
    prev = lambda b: pl.BlockSpec((HALO, C), lambda i: (jnp.maximum(i * hb - 1, 0), b))
    nxt = lambda b: pl.BlockSpec((HALO, C), lambda i: (jnp.minimum((i + 1) * hb, S // HALO - 1), b))
    return _pcall(
        body, (dmix, dmix, ysave, ysave, proj, proj, proj, proj, wdw, lng, lnb,), grid=(nt,),
        in_specs=[tile(cblk), nxt(cblk), tile(0), nxt(0), tile(ablk), tile(ablk + 1), prev(ablk), prev(ablk + 1),
                  pl.BlockSpec((taps, C), lambda i: (0, 0)), vec, vec],
        out_specs=[pl.BlockSpec((tm, 2 * C), lambda i: (i, 0)),
                   pl.BlockSpec((HALO, C), lambda i: (0, 0)),
                   pl.BlockSpec((8, C), lambda i: (0, 0))],
        out_shape=[_sds((S, 2 * C), BF16), _sds((HALO, C), F32), _sds((8, C), F32)],
        scratch_shapes=[pltpu.VMEM((tm + HALO, C), F32), pltpu.VMEM((tm + HALO, C), F32),
                        pltpu.VMEM((taps, SUBLANES, C), F32),
                        pltpu.VMEM((SUBLANES - 1, tm + HALO, C), F32), pltpu.VMEM((SUBLANES - 1, tm + HALO, C), F32)],
        sem=("arbitrary",), name="conv_bwd", comm=comm)


def _mixout_fwd(x, attn, conv, wo, tm, comm=None):
    S, D = x.shape
    QW, C = attn.shape[1], conv.shape[1]

    def body(x_ref, a_ref, c_ref, w_ref, o_ref):
        o_ref[...] = x_ref[...] + _dot(a_ref[...], w_ref[0:QW, :]) + _dot(c_ref[...], w_ref[QW:QW + C, :])

    return _pcall(
        body, (x, attn, conv, wo,), grid=(S // tm,),
        in_specs=[pl.BlockSpec((tm, D), lambda i: (i, 0)),
                  pl.BlockSpec((tm, QW), lambda i: (i, 0)),
                  pl.BlockSpec((tm, C), lambda i: (i, 0)),
                  pl.BlockSpec((QW + C, D), lambda i: (0, 0))],
        out_specs=[pl.BlockSpec((tm, D), lambda i: (i, 0))],
        out_shape=[_sds((S, D), F32)],
        sem=("parallel",), name="mixout_fwd", comm=comm)


def _mixout_bwd(dxo, attn, conv, wo, tm, comm=None):
    S, D = dxo.shape
    QW, C = attn.shape[1], conv.shape[1]
    nt = S // tm

    def body(dx_ref, a_ref, c_ref, w_ref, dm_ref, dw_ref, acc_ref):
        i = pl.program_id(0)
        dxb = dx_ref[...].astype(BF16)
        dm_ref[...] = _dot_nt(dxb, w_ref[...]).astype(BF16)

        @pl.when(i == 0)
        def _():
            acc_ref[...] = jnp.zeros_like(acc_ref)

        acc_ref[0:QW, :] += _dot_tn(a_ref[...], dxb)
        acc_ref[QW:QW + C, :] += _dot_tn(c_ref[...], dxb)

        @pl.when(i == nt - 1)
        def _():
            dw_ref[...] = acc_ref[...].astype(BF16)

    return _pcall(
        body, (dxo, attn, conv, wo,), grid=(nt,),
        in_specs=[pl.BlockSpec((tm, D), lambda i: (i, 0)),
                  pl.BlockSpec((tm, QW), lambda i: (i, 0)),
                  pl.BlockSpec((tm, C), lambda i: (i, 0)),
                  pl.BlockSpec((QW + C, D), lambda i: (0, 0))],
        out_specs=[pl.BlockSpec((tm, QW + C), lambda i: (i, 0)),
                   pl.BlockSpec((QW + C, D), lambda i: (0, 0))],
        out_shape=[_sds((S, QW + C), BF16), _sds((QW + C, D), BF16)],
        scratch_shapes=[pltpu.VMEM((QW + C, D), F32)],
        sem=("arbitrary",), name="mixout_bwd", comm=comm)


def _loss_head(x, gain, target, tm, comm=None):
    S, D = x.shape
    nt = S // tm

    def body(x_ref, g_ref, t_ref, dx_ref, loss_ref, dgain_ref):
        i = pl.program_id(0)
        xh, r = _rms(x_ref[...])
        e = xh * g_ref[...] - t_ref[...]
        loss_ref[...] = jnp.zeros((1, LANES), F32) + 0.5 * jnp.sum(jnp.mean(e * e, axis=-1, keepdims=True))
        dxn, dgn = _rms_bwd(e * (1.0 / D), xh, r, g_ref[...])
        dx_ref[...] = dxn

        @pl.when(i == 0)
        def _():
            dgain_ref[...] = dgn

        @pl.when(i > 0)
        def _():
            dgain_ref[...] += dgn

    return _pcall(
        body, (x, gain, target,), grid=(nt,),
        in_specs=[pl.BlockSpec((tm, D), lambda i: (i, 0)),
                  pl.BlockSpec((1, D), lambda i: (0, 0)),
                  pl.BlockSpec((tm, D), lambda i: (i, 0))],
        out_specs=[pl.BlockSpec((tm, D), lambda i: (i, 0)),
                   pl.BlockSpec((None, 1, LANES), lambda i: (i, 0, 0)),
                   pl.BlockSpec((1, D), lambda i: (0, 0))],
        out_shape=[_sds((S, D), F32), _sds((nt, 1, LANES), F32), _sds((1, D), F32)],
        sem=("arbitrary",), name="loss_head", comm=comm)


def _adam(w, m, v, parts, name, comm=None):
    L, R, C = w.shape
    P = parts[0].shape[0]
    br = _row_block(R, 256)
    c1 = 1.0 - ADAM_B1 ** ADAM_STEP
    c2 = 1.0 - ADAM_B2 ** ADAM_STEP

    def body(w_ref, m_ref, v_ref, *rest):
        p_refs, (g_ref, d_ref, mo_ref, vo_ref) = rest[:L], rest[L:]
        layer = pl.program_id(0)

        def update(p_ref):
            g = p_ref[0].astype(F32)
            for k in range(1, P):
                g = g + p_ref[k].astype(F32)
            mn = ADAM_B1 * m_ref[...] + (1.0 - ADAM_B1) * g
            vn = ADAM_B2 * v_ref[...] + (1.0 - ADAM_B2) * (g * g)
            g_ref[...] = g
            mo_ref[...] = mn
            vo_ref[...] = vn
            d_ref[...] = -ADAM_LR * ((mn / c1) / (jnp.sqrt(vn / c2) + ADAM_EPS) + ADAM_WD * w_ref[...])

        for k in range(L):
            pl.when(layer == k)(functools.partial(update, p_refs[k]))

    blk = pl.BlockSpec((None, br, C), lambda l, i: (l, i, 0))
    part = lambda k: pl.BlockSpec((P, br, C), lambda l, i: (0, jnp.where(l == k, i, 0), 0))
    return _pcall(
        body, (w, m, v, *parts), grid=(L, R // br),
        in_specs=[blk, blk, blk] + [part(k) for k in range(L)],
        out_specs=[blk, blk, blk, blk],
        out_shape=[_sds((L, R, C), F32)] * 4,
        sem=("parallel", "parallel"), name=name, comm=comm)


def _sum_parts(parts):
    P, R, C = parts.shape

    def body(p_ref, o_ref):
        g = p_ref[0]
        for k in range(1, P):
            g = g + p_ref[k]
        o_ref[...] = g

    vmem = pl.BlockSpec(memory_space=pltpu.VMEM)
    return _pcall(body, (parts,), in_specs=[vmem], out_specs=[vmem], out_shape=[_sds((R, C), F32)],
                  name="sum_parts")[0]


def _place():
    x, y, c = lax.axis_index("x"), lax.axis_index("y"), lax.axis_index("c")
    return x, y, c, [(1 - x, y), (x, 1 - y), (1 - x, 1 - y)]


def _dev(px, py, pc):
    return 4 * px + 2 * py + pc


def _gather_comm(shards, fulls, slot_of):
    n = len(shards)

    def copies(srcs, outs, send_sems, recv_sems):
        x, y, c, chips = _place()

        def copy(a, k, block, to, from_shard=False):
            dst = slot_of[a](outs[a], _dev(*block))
            return pltpu.make_async_remote_copy(
                src_ref=srcs[a] if from_shard else dst, dst_ref=dst,
                send_sem=send_sems.at[a, k], recv_sem=recv_sems.at[a, k], device_id=to, device_id_type=MESH)

        return copy, (x, y, c), (x, y, 1 - c), chips

    def local(srcs, outs, local_sems):
        x, y, c, _ = _place()
        return [pltpu.make_async_copy(srcs[a], slot_of[a](outs[a], _dev(x, y, c)), local_sems.at[a])
                for a in range(n)]

    def first_copies(copy, me, sibling, chips):
        out = []
        for a in range(n):
            out.append(copy(a, 0, me, sibling, True))
            out += [copy(a, 1 + j, me, (*chip, me[2]), True) for j, chip in enumerate(chips)]
        return out

    def start(srcs, outs, sems):
        send_sems, recv_sems, local_sems = sems
        copy, me, sibling, chips = copies(srcs, outs, send_sems, recv_sems)
        for cp in local(srcs, outs, local_sems):
            cp.start()
        for cp in first_copies(copy, me, sibling, chips):
            cp.start()

    def forwards(copy, me, sibling, chips):
        return [copy(a, 4 + j, (*chip, me[2]), sibling) for j, chip in enumerate(chips) for a in range(n)]

    def mid(srcs, outs, sems):
        send_sems, recv_sems, local_sems = sems
        copy, me, sibling, chips = copies(srcs, outs, send_sems, recv_sems)
        for j, chip in enumerate(chips):
            for a in range(n):
                copy(a, 1 + j, (*chip, me[2]), me).wait_recv()
                copy(a, 4 + j, (*chip, me[2]), sibling).start()

    def finish(srcs, outs, sems):
        send_sems, recv_sems, local_sems = sems
        copy, me, sibling, chips = copies(srcs, outs, send_sems, recv_sems)
        c = me[2]
        for a in range(n):
            copy(a, 0, sibling, me).wait_recv()
            for j, chip in enumerate(chips):
                copy(a, 4 + j, (*chip, 1 - c), me).wait_recv()
        for cp in first_copies(copy, me, sibling, chips) + forwards(copy, me, sibling, chips):
            cp.wait_send()
        for cp in local(srcs, outs, local_sems):
            cp.wait()

    sems = [pltpu.SemaphoreType.DMA((n, 7)), pltpu.SemaphoreType.DMA((n, 7)), pltpu.SemaphoreType.DMA((n,))]
    return _Comm(shards, fulls, sems, start, finish, mid)


def _swap_comm(grads):
    n = len(grads)

    def copies(srcs, outs, sems):
        x, y, c, _ = _place()
        return [pltpu.make_async_remote_copy(
            src_ref=srcs[a].at[:, pl.ds(1 - c, 1)], dst_ref=outs[a],
            send_sem=sems[0].at[a], recv_sem=sems[1].at[a], device_id=(x, y, 1 - c), device_id_type=MESH)
            for a in range(n)]

    def start(srcs, outs, sems):
        for cp in copies(srcs, outs, sems):
            cp.start()

    def finish(srcs, outs, sems):
        for cp in copies(srcs, outs, sems):
            cp.wait()

    return _Comm(grads, [_sds((N_CHIP, 1) + g.shape[2:], g.dtype) for g in grads],
                 [pltpu.SemaphoreType.DMA((n,)), pltpu.SemaphoreType.DMA((n,))], start, finish)


def _exchange_comm(parts):
    n = len(parts)

    def copies(srcs, outs, sems):
        x, y, c, chips = _place()
        mine = 2 * x + y
        loc = [pltpu.make_async_copy(srcs[a].at[pl.ds(mine, 1)], outs[a].at[pl.ds(mine, 1)], sems[2].at[a])
               for a in range(n)]
        rem = [pltpu.make_async_remote_copy(
            src_ref=srcs[a].at[pl.ds(2 * px + py, 1)], dst_ref=outs[a].at[pl.ds(mine, 1)],
            send_sem=sems[0].at[a, j], recv_sem=sems[1].at[a, j], device_id=(px, py, c), device_id_type=MESH)
            for a in range(n) for j, (px, py) in enumerate(chips)]
        return loc + rem

    def start(srcs, outs, sems):
        for cp in copies(srcs, outs, sems):
            cp.start()

    def finish(srcs, outs, sems):
        for cp in copies(srcs, outs, sems):
            cp.wait()

    return _Comm(parts, [_sds(p.shape, p.dtype) for p in parts],
                 [pltpu.SemaphoreType.DMA((n, 3)), pltpu.SemaphoreType.DMA((n, 3)), pltpu.SemaphoreType.DMA((n,))],
                 start, finish)


FWD_KERNELS = [("ffn1", ("a1", "b1"), 26), ("mixin", ("wi",), 7), ("attn", (), 7), ("conv", (), 15), ("mixout", ("wo",), 6),
               ("ffn2", ("a2", "b2"), 26)]
GATHER_SHARE = 70


def _plan_gathers(n_layers, shard_bytes):
    per_layer = sum(shard_bytes.values())
    cost = {n: GATHER_SHARE * b / per_layer for n, b in shard_bytes.items()}
    room = {len(FWD_KERNELS) * l + i: k[2] for l in range(n_layers) for i, k in enumerate(FWD_KERNELS)}
    first, plan = [], {}
    for l in range(n_layers):
        for i, (_, needs, _) in enumerate(FWD_KERNELS):
            due = len(FWD_KERNELS) * l + i
            for name in needs:
                if due == 0:
                    first.append((name, l))
                    continue
                fits = [k for k in range(due) if room[k] >= cost[name]]
                k = fits[0] if fits else max(range(due), key=lambda k: room[k])
                room[k] -= cost[name]
                plan.setdefault(k, []).append((name, l))
    return first, plan


def _comm_only(comm, name):
    return _pcall(lambda: None, (), in_specs=[], out_specs=[], out_shape=[], name=name, comm=comm)


def _gather_small(v):
    R, C = v.shape

    def body(x_ref, out_ref, send_sems, recv_sems, local_sem):
        x, y, c, chips = _place()
        me, sibling = (x, y, c), (x, y, 1 - c)

        def copy(k, block, to, from_shard=False):
            dst = out_ref.at[_dev(*block)]
            return pltpu.make_async_remote_copy(
                src_ref=x_ref if from_shard else dst, dst_ref=dst,
                send_sem=send_sems.at[k], recv_sem=recv_sems.at[k], device_id=to, device_id_type=MESH)

        mine = pltpu.make_async_copy(x_ref, out_ref.at[_dev(*me)], local_sem)
        mine.start()
        first = [copy(0, me, sibling, True)] + [copy(1 + j, me, (*chip, c), True) for j, chip in enumerate(chips)]
        for cp in first:
            cp.start()
        passed = [copy(4 + j, (*chip, c), sibling) for j, chip in enumerate(chips)]
        for j, chip in enumerate(chips):
            copy(1 + j, (*chip, c), me).wait_recv()
            passed[j].start()
        copy(0, sibling, me).wait_recv()
        for j, chip in enumerate(chips):
            copy(4 + j, (*chip, 1 - c), me).wait_recv()
        for cp in first + passed:
            cp.wait_send()
        mine.wait()

    vmem = pl.BlockSpec(memory_space=pltpu.VMEM)
    return _pcall(
        body, (v,), in_specs=[vmem], out_specs=[vmem], out_shape=[_sds((N_DEV, R, C), F32)],
        scratch_shapes=[pltpu.SemaphoreType.DMA((7,)), pltpu.SemaphoreType.DMA((7,)), pltpu.SemaphoreType.DMA],
        name="gather_small")[0]


def _add_sibling(core, g, r):
    _, _, R, C = g.shape
    br = _row_block(R)

    def body(c_ref, g_ref, r_ref, o_ref):
        o_ref[...] = (g_ref[...].astype(F32) + r_ref[...].astype(F32)).astype(BF16)

    return _pcall(
        body, (core, g, r),
        grid_spec=pltpu.PrefetchScalarGridSpec(
            num_scalar_prefetch=1, grid=(N_CHIP, R // br),
            in_specs=[pl.BlockSpec((None, None, br, C), lambda k, i, c_ref: (k, c_ref[0], i, 0)),
                      pl.BlockSpec((None, None, br, C), lambda k, i, c_ref: (k, 0, i, 0))],
            out_specs=pl.BlockSpec((None, br, C), lambda k, i, c_ref: (k, i, 0))),
        out_shape=_sds((N_CHIP, R, C), BF16), sem=("parallel", "parallel"), name="rs_add_sibling")


def _by_chip(g):
    return g.reshape((N_CHIP, 2) + g.shape[1:])


def _pack_rows(vecs):
    flat = jnp.concatenate([v.reshape(-1).astype(F32) for v in vecs])
    rows = -(-flat.shape[0] // (8 * LANES)) * 8
    return jnp.pad(flat, (0, rows * LANES - flat.shape[0])).reshape(rows, LANES)


def _unpack_rows(rows, shapes):
    flat = rows.reshape(-1)
    out, o = [], 0
    for s in shapes:
        n = math.prod(s)
        out.append(flat[o:o + n].reshape(s))
        o += n
    return out


def _adam_any(w, m, v, g, name):
    shape = w.shape
    one = lambda t: t.reshape(1, -1, shape[-1])
    return tuple(t.reshape(shape) for t in _adam(one(w), one(m), one(v), [one(g)], name))


def kernel(x, norm_ffn1, w_ffn1_in, w_ffn1_out, norm_mix, w_in, sinks, w_dw, b_dw, conv_ln_g, conv_ln_b, w_out, norm_ffn2, w_ffn2_in, w_ffn2_out, final_norm, loss_target, m_norm_ffn1, m_w_ffn1_in, m_w_ffn1_out, m_norm_mix, m_w_in, m_sinks, m_w_dw, m_b_dw, m_conv_ln_g, m_conv_ln_b, m_w_out, m_norm_ffn2, m_w_ffn2_in, m_w_ffn2_out, m_final_norm, v_norm_ffn1, v_w_ffn1_in, v_w_ffn1_out, v_norm_mix, v_w_in, v_sinks, v_w_dw, v_b_dw, v_conv_ln_g, v_conv_ln_b, v_w_out, v_norm_ffn2, v_w_ffn2_in, v_w_ffn2_out, v_final_norm):
    _, S, D = x.shape
    L = norm_ffn1.shape[0]
    NF = w_ffn1_in.shape[2]
    RF = w_ffn1_out.shape[1]
    NW = w_in.shape[2]
    RO = w_out.shape[1]
    taps, CD = w_dw.shape[1], w_dw.shape[2]
    H = sinks.shape[1]
    C = N_DEV * CD
    QW = H * HEAD_DIM
    KVW = N_KV_HEADS * HEAD_DIM
    assert 2 * RF == NF and QW + C == N_DEV * RO and N_DEV * NW == QW + 2 * KVW + 2 * C
    tm = min(512, S)
    ta = min(256, S)
    tc = min(256, S)
    core = lax.axis_index("c").astype(jnp.int32).reshape(1)

    x0 = x[0]
    target = loss_target[0]

    wdw_all = _gather_small(_pack_rows([w_dw]))
    n_dw = L * taps * CD
    wdw_full = jnp.stack([wdw_all[d].reshape(-1)[:n_dw].reshape(L, taps, CD) for d in range(N_DEV)],
                         axis=2).reshape(L, taps, C)

    def shard(name, l):
        return {"a1": lambda: w_ffn1_in[l].T.astype(BF16), "a2": lambda: w_ffn2_in[l].T.astype(BF16),
                "b1": lambda: w_ffn1_out[l].astype(BF16), "b2": lambda: w_ffn2_out[l].astype(BF16),
                "wi": lambda: w_in[l].T.astype(BF16), "wo": lambda: w_out[l].astype(BF16)}[name]()

    rows_of = {"a1": NF, "a2": NF, "b1": RF, "b2": RF, "wi": NW, "wo": RO}
    full_of = {n: _sds((N_DEV * r, D), BF16) for n, r in rows_of.items()}
    slot_of = {n: (lambda ref, b, r=r: ref.at[pl.ds(b * r, r)]) for n, r in rows_of.items()}
    first, plan = _plan_gathers(L, {n: r * D * 2 for n, r in rows_of.items()})
    got = {}

    def gather(items):
        if not items:
            return None
        return _gather_comm([shard(n, l) for n, l in items], [full_of[n] for n, _ in items],
                            [slot_of[n] for n, _ in items])

    def carrying(k, call, n_own):
        items = plan.get(k, [])
        res = call(gather(items))
        got.update(zip(items, res[n_own:]))
        return res[:n_own]

    def wext_of(wi):
        q, k, v, u = wi[:QW], wi[QW:QW + KVW], wi[QW + KVW:QW + 2 * KVW], wi[QW + 2 * KVW:]
        dup = lambda t: jnp.concatenate(
            [t[HEAD_DIM * (i // 2):HEAD_DIM * (i // 2 + 1)] for i in range(2 * N_KV_HEADS)], axis=0)
        return jnp.concatenate([q, dup(k), dup(v), u], axis=0)

    got.update(zip(first, _comm_only(gather(first), "ag_first")))

    saved = []
    xc = x0
    for l in range(L):
        k0 = len(FWD_KERNELS) * l
        g1, gm, g2 = norm_ffn1[l][None], norm_mix[l][None], norm_ffn2[l][None]
        x1, gu1 = carrying(k0, lambda c: _ffn_fwd(xc, g1, got["a1", l], got["b1", l], tm, c), 2)
        wext = wext_of(got["wi", l])
        proj, = carrying(k0 + 1, lambda c: _mixin_fwd(x1, gm, wext, tm, c), 1)
        attn, = carrying(k0 + 2, lambda c: _attn_fwd(proj, sinks[l], H, c), 1)
        conv, ysave = carrying(k0 + 3, lambda c: _conv_fwd(proj, wdw_full[l], b_dw[l][None], conv_ln_g[l][None],
                                                           conv_ln_b[l][None], H, tc, c), 2)
        x2, = carrying(k0 + 4, lambda c: _mixout_fwd(x1, attn, conv, got["wo", l], tm, c), 1)
        x3, gu2 = carrying(k0 + 5, lambda c: _ffn_fwd(x2, g2, got["a2", l], got["b2", l], tm, c), 2)
        W = dict(a1=got["a1", l], b1=got["b1", l], a2=got["a2", l], b2=got["b2", l], wo=got["wo", l], wext=wext)
        saved.append(dict(W=W, x0=xc, x1=x1, x2=x2, gu1=gu1, gu2=gu2, proj=proj, attn=attn, conv=conv, ysave=ysave))
        xc = x3

    dx, loss_parts, dfinal = _loss_head(xc, final_norm[None], target, tm)
    loss = lax.psum(jnp.sum(loss_parts[:, 0, 0]), ("x", "y", "c"))

    def swap(gs):
        return _swap_comm([_by_chip(g) for g in gs]) if gs else None

    def added(gs, gots):
        return [_add_sibling(core, _by_chip(g), r) for g, r in zip(gs, gots)]

    small = [None] * L
    big = [dict() for _ in range(L)]
    carry = None
    for l in reversed(range(L)):
        sv = saved[l]
        W = sv["W"]
        g1, gm, g2 = norm_ffn1[l][None], norm_mix[l][None], norm_ffn2[l][None]
        fold = lambda t: jnp.concatenate(
            [t[2 * HEAD_DIM * i:2 * HEAD_DIM * i + HEAD_DIM] + t[2 * HEAD_DIM * i + HEAD_DIM:2 * HEAD_DIM * (i + 1)]
             for i in range(N_KV_HEADS)], axis=0)

        r = _ffn_bwd_a(sv["x2"], g2, dx, sv["gu2"], W["a2"], W["b2"], ta, swap(carry[1]) if carry else None)
        dx2, dgu2, dg2, h2, dys2 = r[:5]
        p_carry = added(carry[1], r[5:]) if carry else None
        r = _ffn_bwd_w(h2, dys2, sv["gu2"], dgu2, tm, _exchange_comm(p_carry) if carry else None)
        da2, db2 = r[0].reshape(N_DEV, NF, D), r[1].reshape(N_DEV, RF, D)
        if carry:
            big[carry[0]]["a1"], big[carry[0]]["b1"] = r[2:]
        dmix, dwo = _mixout_bwd(dx2, sv["attn"], sv["conv"], W["wo"], tm)
        dwo = dwo.reshape(N_DEV, RO, D)
        r = _conv_bwd(sv["proj"], dmix, sv["ysave"], wdw_full[l], conv_ln_g[l][None], conv_ln_b[l][None], H, tc,
                      swap([da2, db2, dwo]))
        dag, dwdw, dvec = r[:3]
        p_a2, p_b2, p_wo = added([da2, db2, dwo], r[3:])
        r = _attn_bwd(sv["proj"], dmix, sinks[l], H, _exchange_comm([p_a2]))
        dq, dkv, dsink = r[:3]
        big[l]["a2"] = r[3]
        r = _mixin_bwd(sv["x1"], gm, dx2, dq, dkv, dag, W["wext"], tm, _exchange_comm([p_b2, p_wo]))
        dx1, dgm, dwext = r[:3]
        big[l]["b2"], big[l]["wo"] = r[3:]
        dwi = jnp.concatenate([dwext[:QW], fold(dwext[QW:QW + KV_DUP]),
                               fold(dwext[QW + KV_DUP:QW + 2 * KV_DUP]), dwext[QW + 2 * KV_DUP:]], axis=0)
        dwi = dwi.astype(BF16).reshape(N_DEV, NW, D)
        r = _ffn_bwd_a(sv["x0"], g1, dx1, sv["gu1"], W["a1"], W["b1"], ta, swap([dwi]))
        dx0, dgu1, dg1, h1, dys1 = r[:5]
        p_wi, = added([dwi], r[5:])
        r = _ffn_bwd_w(h1, dys1, sv["gu1"], dgu1, tm, _exchange_comm([p_wi]))
        carry = (l, [r[0].reshape(N_DEV, NF, D), r[1].reshape(N_DEV, RF, D)])
        big[l]["wi"] = r[2]
        dx = dx0
        small[l] = [dg1[0], dgm[0], dsink[0, :H], dwdw[:taps], dvec[0], dvec[1], dvec[2], dg2[0]]

    p_carry = added(carry[1], _comm_only(swap(carry[1]), "rs_swap_last"))
    big[carry[0]]["a1"], big[carry[0]]["b1"] = _comm_only(_exchange_comm(p_carry), "rs_exchange_last")
    grad_x = dx[None]

    small_shapes = [(D,), (D,), (H,), (taps, C), (C,), (C,), (C,), (D,)]
    packed = _pack_rows([t for l in range(L) for t in small[l]] + [dfinal[0]])
    total = _sum_parts(_gather_small(packed))
    flat = _unpack_rows(total, small_shapes * L + [(D,)])
    per = [jnp.stack([flat[l * len(small_shapes) + i] for l in range(L)]) for i in range(len(small_shapes))]
    g_nf1, g_nmix, g_sinks, g_wdw_full, g_bdw, g_lng, g_lnb, g_nf2 = per
    g_final = flat[-1]
    dev = _dev(lax.axis_index("x"), lax.axis_index("y"), lax.axis_index("c"))
    g_wdw = lax.dynamic_slice_in_dim(g_wdw_full, dev * CD, CD, axis=2)

    res = {}
    res["norm_ffn1"] = _adam_any(norm_ffn1, m_norm_ffn1, v_norm_ffn1, g_nf1, "adam_small")
    res["norm_mix"] = _adam_any(norm_mix, m_norm_mix, v_norm_mix, g_nmix, "adam_small")
    res["sinks"] = _adam_any(sinks, m_sinks, v_sinks, g_sinks, "adam_small")
    res["w_dw"] = _adam_any(w_dw, m_w_dw, v_w_dw, g_wdw, "adam_small")
    res["b_dw"] = _adam_any(b_dw, m_b_dw, v_b_dw, g_bdw, "adam_small")
    res["conv_ln_g"] = _adam_any(conv_ln_g, m_conv_ln_g, v_conv_ln_g, g_lng, "adam_small")
    res["conv_ln_b"] = _adam_any(conv_ln_b, m_conv_ln_b, v_conv_ln_b, g_lnb, "adam_small")
    res["norm_ffn2"] = _adam_any(norm_ffn2, m_norm_ffn2, v_norm_ffn2, g_nf2, "adam_small")
    res["final_norm"] = tuple(t[0] for t in _adam_any(final_norm[None], m_final_norm[None], v_final_norm[None],
                                                      g_final[None], "adam_small"))

    def adam_big(key, w, m, v, name, transposed=False, comm=None):
        t = (lambda a: a.transpose(0, 2, 1)) if transposed else (lambda a: a)
        r = _adam(t(w), t(m), t(v), [big[l][key] for l in range(L)], name, comm)
        return tuple(t(o) for o in r[:4]), r[4:]

    res["w_ffn1_in"], _ = adam_big("a1", w_ffn1_in, m_w_ffn1_in, v_w_ffn1_in, "adam_ffn_in", True)
    res["w_ffn1_out"], _ = adam_big("b1", w_ffn1_out, m_w_ffn1_out, v_w_ffn1_out, "adam_ffn_out")
    res["w_ffn2_in"], _ = adam_big("a2", w_ffn2_in, m_w_ffn2_in, v_w_ffn2_in, "adam_ffn_in", True)
    res["w_ffn2_out"], _ = adam_big("b2", w_ffn2_out, m_w_ffn2_out, v_w_ffn2_out, "adam_ffn_out")
    res["w_in"], _ = adam_big("wi", w_in, m_w_in, v_w_in, "adam_w_in", True)
    res["w_out"], _ = adam_big("wo", w_out, m_w_out, v_w_out, "adam_w_out")

    order = ["norm_ffn1", "w_ffn1_in", "w_ffn1_out", "norm_mix", "w_in", "sinks", "w_dw", "b_dw", "conv_ln_g",
             "conv_ln_b", "w_out", "norm_ffn2", "w_ffn2_in", "w_ffn2_out", "final_norm"]
    return (loss, grad_x, *[res[n][0] for n in order], *[res[n][1] for n in order],
            *[res[n][2] for n in order], *[res[n][3] for n in order])
```

```python
import functools
import math

import jax
import jax.numpy as jnp
from jax import lax
from jax.experimental import pallas as pl
from jax.experimental.pallas import tpu as pltpu

F32 = jnp.float32
BF16 = jnp.bfloat16
MESH = pl.DeviceIdType.MESH

N_DEV = 8
N_CHIP = 4
HEAD_DIM = 64
N_KV_HEADS = 2
WINDOW = 128
KV_DUP = 2 * HEAD_DIM * N_KV_HEADS
RMS_EPS = 1e-6
NEG_INF = -1e30
FFN_RES = 0.5
HALO = 32
ROWS = 32
FFN_CHUNK = 512
BWD_W_CHUNK = 384
LANES = 128
SUBLANES = 8
V7X_VMEM_LIMIT = 56 * 1024 * 1024

ADAM_LR = 0.001
ADAM_B1 = 0.9
ADAM_B2 = 0.999
ADAM_EPS = 1e-08
ADAM_WD = 0.01
ADAM_STEP = 10


def _raw_call(body, **kw):
    return pl.pallas_call(body, **kw)


class _Comm:
    def __init__(self, ins, outs, sems, start, finish, mid=None):
        self.ins, self.outs, self.sems, self.start, self.finish = list(ins), list(outs), list(sems), start, finish
        self.mid = mid or (lambda ins, outs, sems: None)


def _join(*comms):
    comms = [c for c in comms if c is not None]
    if not comms:
        return None

    def split(refs, attr):
        out, o = [], 0
        for c in comms:
            n = len(getattr(c, attr))
            out.append(refs[o:o + n])
            o += n
        return out

    def run(which):
        def go(ins, outs, sems):
            for c, i, o, m in zip(comms, split(ins, "ins"), split(outs, "outs"), split(sems, "sems")):
                getattr(c, which)(i, o, m)
        return go

    return _Comm(sum((c.ins for c in comms), []), sum((c.outs for c in comms), []),
                 sum((c.sems for c in comms), []), run("start"), run("finish"), run("mid"))


def _pcall(body, args, *, name, out_shape, grid=(), in_specs=None, out_specs=None, scratch_shapes=(), sem=(),
           comm=None, grid_spec=None):
    if grid_spec is not None:
        return _raw_call(body, grid_spec=grid_spec, out_shape=out_shape, name=name,
                         compiler_params=_params(*sem))(*args)
    if comm is None:
        return _raw_call(body, grid=grid, in_specs=in_specs, out_specs=out_specs, out_shape=out_shape,
                         scratch_shapes=list(scratch_shapes), name=name, compiler_params=_params(*sem))(*args)
    n_in, n_out, n_scr = len(in_specs), len(out_shape), len(scratch_shapes)
    ci, co = len(comm.ins), len(comm.outs)

    def fused(*refs):
        cuts = [n_in, ci, n_out, co, n_scr]
        parts, o = [], 0
        for n in cuts:
            parts.append(refs[o:o + n])
            o += n
        ins, cins, outs, couts, scr = parts
        csems = refs[o:]
        steps = math.prod(grid)
        if steps < 3:
            comm.start(cins, couts, csems)
            body(*ins, *outs, *scr)
            comm.mid(cins, couts, csems)
            comm.finish(cins, couts, csems)
            return
        step = functools.reduce(lambda acc, a: acc * grid[a] + pl.program_id(a), range(len(grid)), 0)

        @pl.when(step == 0)
        def _():
            comm.start(cins, couts, csems)

        @pl.when(step == steps - 2)
        def _():
            comm.mid(cins, couts, csems)

        body(*ins, *outs, *scr)

        @pl.when(step == steps - 1)
        def _():
            comm.finish(cins, couts, csems)

    return _raw_call(
        fused, grid=grid, in_specs=list(in_specs) + [ANY] * ci, out_specs=list(out_specs) + [ANY] * co,
        out_shape=list(out_shape) + comm.outs, scratch_shapes=list(scratch_shapes) + comm.sems, name=name,
        compiler_params=_params(*(["arbitrary"] * len(grid))))(*args, *comm.ins)


ANY = pl.BlockSpec(memory_space=pl.ANY)


def _params(*sem):
    return pltpu.CompilerParams(dimension_semantics=sem, vmem_limit_bytes=V7X_VMEM_LIMIT)


def _dot(a, b):
    return jnp.dot(a, b, preferred_element_type=F32)


def _dot_nt(a, b):
    return lax.dot_general(a, b, (((1,), (1,)), ((), ())), preferred_element_type=F32)


def _dot_tn(a, b):
    return lax.dot_general(a, b, (((0,), (0,)), ((), ())), preferred_element_type=F32)


def _sigmoid(x):
    return 1.0 / (1.0 + jnp.exp(-x))


def _rms(x):
    r = lax.rsqrt(jnp.mean(x * x, axis=-1, keepdims=True) + RMS_EPS)
    return x * r, r


def _rms_bwd(dh, xh, r, g):
    dxh = dh * g
    dx = r * (dxh - xh * jnp.mean(dxh * xh, axis=-1, keepdims=True))
    return dx, jnp.sum(dh * xh, axis=0, keepdims=True)


def _sds(shape, dtype):
    return jax.ShapeDtypeStruct(shape, dtype)


def _row_block(rows, limit=512):
    fits = [d for d in range(16, min(rows, limit) + 1, 16) if rows % d == 0]
    return fits[-1] if fits else rows


def _chunks(n, step):
    return [(o, min(step, n - o)) for o in range(0, n, step)]


def _resident(shape):
    return pl.BlockSpec(shape, lambda *_: (0,) * len(shape), pipeline_mode=pl.Buffered(1))


def _ffn_fwd(x, gain, wint, wout, tm, comm=None):
    S, D = x.shape
    F = wout.shape[0]

    def body(x_ref, g_ref, w_ref, wo_ref, xo_ref, gu_ref, a_ref):
        xh, _ = _rms(x_ref[...])
        h = (xh * g_ref[...]).astype(BF16)
        for o, n in _chunks(F, FFN_CHUNK):
            gb = _dot_nt(h, w_ref[o:o + n, :]).astype(BF16)
            ub = _dot_nt(h, w_ref[F + o:F + o + n, :]).astype(BF16)
            gu_ref[:, o:o + n] = gb
            gu_ref[:, F + o:F + o + n] = ub
            g = gb.astype(F32)
            a_ref[:, o:o + n] = (g * _sigmoid(g) * ub.astype(F32)).astype(BF16)
        xo_ref[...] = x_ref[...] + FFN_RES * _dot(a_ref[...], wo_ref[...])

    return _pcall(
        body, (x, gain, wint, wout), grid=(S // tm,),
        in_specs=[pl.BlockSpec((tm, D), lambda i: (i, 0)), _resident((1, D)),
                  _resident((2 * F, D)), _resident((F, D))],
        out_specs=[pl.BlockSpec((tm, D), lambda i: (i, 0)), pl.BlockSpec((tm, 2 * F), lambda i: (i, 0))],
        out_shape=[_sds((S, D), F32), _sds((S, 2 * F), BF16)],
        scratch_shapes=[pltpu.VMEM((tm, F), BF16)],
        sem=("parallel",), name="ffn_fwd", comm=comm)


def _ffn_bwd_a(x, gain, dxo, gu, wint, wout, tm, comm=None):
    S, D = x.shape
    F = wout.shape[0]

    def body(x_ref, g_ref, dxo_ref, gu_ref, w_ref, wo_ref, dx_ref, dgu_ref, dgain_ref, h_ref, dys_ref):
        i = pl.program_id(0)
        dys = (FFN_RES * dxo_ref[...]).astype(BF16)
        dys_ref[...] = dys
        for o, n in _chunks(F, FFN_CHUNK):
            dact = _dot_nt(dys, wo_ref[o:o + n, :])
            g = gu_ref[:, o:o + n].astype(F32)
            u = gu_ref[:, F + o:F + o + n].astype(F32)
            s = _sigmoid(g)
            dgu_ref[:, o:o + n] = (dact * u * (s * (1.0 + g * (1.0 - s)))).astype(BF16)
            dgu_ref[:, F + o:F + o + n] = (dact * (g * s)).astype(BF16)
        dh = _dot(dgu_ref[...], w_ref[...])
        xh, r = _rms(x_ref[...])
        h_ref[...] = (xh * g_ref[...]).astype(BF16)
        dxn, dgn = _rms_bwd(dh, xh, r, g_ref[...])
        dx_ref[...] = dxo_ref[...] + dxn

        @pl.when(i == 0)
        def _():
            dgain_ref[...] = dgn

        @pl.when(i > 0)
        def _():
            dgain_ref[...] += dgn

    tile = pl.BlockSpec((tm, D), lambda i: (i, 0))
    wide = pl.BlockSpec((tm, 2 * F), lambda i: (i, 0))
    return _pcall(
        body, (x, gain, dxo, gu, wint, wout), grid=(S // tm,),
        in_specs=[tile, _resident((1, D)), tile, wide, _resident((2 * F, D)), _resident((F, D))],
        out_specs=[tile, wide, pl.BlockSpec((1, D), lambda i: (0, 0)), tile, tile],
        out_shape=[_sds((S, D), F32), _sds((S, 2 * F), BF16), _sds((1, D), F32), _sds((S, D), BF16), _sds((S, D), BF16)],
        sem=("arbitrary",), name="ffn_bwd_a", comm=comm)


def _ffn_bwd_w(h, dys, gu, dgu, tk, comm=None):
    S, D = h.shape
    F = gu.shape[1] // 2
    FH = F // 2
    nk = S // tk

    def body(h_ref, dys_ref, gg_ref, gu_ref, dg_ref, du_ref, dw_ref, dwo_ref, accw_ref, acco_ref):
        k = pl.program_id(1)

        @pl.when(k == 0)
        def _():
            accw_ref[...] = jnp.zeros_like(accw_ref)
            acco_ref[...] = jnp.zeros_like(acco_ref)

        hv, dys = h_ref[...], dys_ref[...]
        for o, n in _chunks(FH, BWD_W_CHUNK):
            g = gg_ref[:, o:o + n].astype(F32)
            act = (g * _sigmoid(g) * gu_ref[:, o:o + n].astype(F32)).astype(BF16)
            accw_ref[0, o:o + n, :] += _dot_tn(dg_ref[:, o:o + n], hv)
            accw_ref[1, o:o + n, :] += _dot_tn(du_ref[:, o:o + n], hv)
            acco_ref[o:o + n, :] += _dot_tn(act, dys)

        @pl.when(k == nk - 1)
        def _():
            dw_ref[...] = accw_ref[...].astype(BF16)
            dwo_ref[...] = acco_ref[...].astype(BF16)

    tile = pl.BlockSpec((tk, D), lambda j, k: (k, 0))
    gate = pl.BlockSpec((tk, FH), lambda j, k: (k, j))
    up = pl.BlockSpec((tk, FH), lambda j, k: (k, j + 2))
    return _pcall(
        body, (h, dys, gu, gu, dgu, dgu), grid=(2, nk),
        in_specs=[tile, tile, gate, up, gate, up],
        out_specs=[pl.BlockSpec((2, FH, D), lambda j, k: (0, j, 0), pipeline_mode=pl.Buffered(1)),
                   pl.BlockSpec((FH, D), lambda j, k: (j, 0), pipeline_mode=pl.Buffered(1))],
        out_shape=[_sds((2, F, D), BF16), _sds((F, D), BF16)],
        scratch_shapes=[pltpu.VMEM((2, FH, D), F32), pltpu.VMEM((FH, D), F32)],
        sem=("parallel", "arbitrary"), name="ffn_bwd_w", comm=comm)


def _mixin_fwd(x, gain, wext, tm, comm=None):
    S, D = x.shape
    PW = wext.shape[0]

    def body(x_ref, g_ref, w_ref, p_ref):
        xh, _ = _rms(x_ref[...])
        p_ref[...] = _dot_nt((xh * g_ref[...]).astype(BF16), w_ref[...]).astype(BF16)

    return _pcall(
        body, (x, gain, wext), grid=(S // tm,),
        in_specs=[pl.BlockSpec((tm, D), lambda i: (i, 0)), _resident((1, D)), _resident((PW, D))],
        out_specs=[pl.BlockSpec((tm, PW), lambda i: (i, 0))],
        out_shape=[_sds((S, PW), BF16)],
        sem=("parallel",), name="mixin_fwd", comm=comm)


def _mixin_bwd(x, gain, dxo, dq, dkv, dag, wext, tm, comm=None):
    S, D = x.shape
    PW = wext.shape[0]
    QW = dq.shape[1]
    o1, o2 = QW, QW + 2 * KV_DUP

    def body(x_ref, g_ref, dxo_ref, dq_ref, dkv_ref, dag_ref, w_ref, dx_ref, dgain_ref, dw_ref):
        i = pl.program_id(0)
        xh, r = _rms(x_ref[...])
        h = (xh * g_ref[...]).astype(BF16)
        dqv, dkvv, dagv = dq_ref[...], dkv_ref[...], dag_ref[...]
        dh = _dot(dqv, w_ref[0:o1, :]) + _dot(dkvv, w_ref[o1:o2, :]) + _dot(dagv, w_ref[o2:PW, :])
        dxn, dgn = _rms_bwd(dh, xh, r, g_ref[...])
        dx_ref[...] = dxo_ref[...] + dxn

        @pl.when(i == 0)
        def _():
            dgain_ref[...] = dgn
            dw_ref[0:o1, :] = _dot_tn(dqv, h)
            dw_ref[o1:o2, :] = _dot_tn(dkvv, h)
            dw_ref[o2:PW, :] = _dot_tn(dagv, h)

        @pl.when(i > 0)
        def _():
            dgain_ref[...] += dgn
            dw_ref[0:o1, :] += _dot_tn(dqv, h)
            dw_ref[o1:o2, :] += _dot_tn(dkvv, h)
            dw_ref[o2:PW, :] += _dot_tn(dagv, h)

    return _pcall(
        body, (x, gain, dxo, dq, dkv, dag, wext), grid=(S // tm,),
        in_specs=[pl.BlockSpec((tm, D), lambda i: (i, 0)), _resident((1, D)),
                  pl.BlockSpec((tm, D), lambda i: (i, 0)),
                  pl.BlockSpec((tm, QW), lambda i: (i, 0)),
                  pl.BlockSpec((tm, 2 * KV_DUP), lambda i: (i, 0)),
                  pl.BlockSpec((tm, PW - o2), lambda i: (i, 0)),
                  _resident((PW, D))],
        out_specs=[pl.BlockSpec((tm, D), lambda i: (i, 0)),
                   pl.BlockSpec((1, D), lambda i: (0, 0)),
                   pl.BlockSpec((PW, D), lambda i: (0, 0))],
        out_shape=[_sds((S, D), F32), _sds((1, D), F32), _sds((PW, D), F32)],
        sem=("arbitrary",), name="mixin_bwd", comm=comm)


def _attn_group(n, group, slopes, sinks):
    rows = group * WINDOW
    r = lax.broadcasted_iota(jnp.int32, (rows, 2 * WINDOW), 0)
    s = lax.broadcasted_iota(jnp.int32, (rows, 2 * WINDOW), 1)
    dist = (r & (WINDOW - 1)) + WINDOW - s
    valid = (dist >= 0) & (dist < WINDOW) & jnp.logical_or(n > 0, s >= WINDOW)
    seg = lax.shift_right_logical(lax.broadcasted_iota(jnp.int32, (rows, 1), 0), WINDOW.bit_length() - 1)
    slope = jnp.zeros((rows, 1), F32)
    sink = jnp.zeros((rows, 1), F32)
    for i in range(group):
        slope = jnp.where(seg == i, slopes[i], slope)
        sink = jnp.where(seg == i, sinks[i], sink)
    return valid, -slope * dist.astype(F32), sink


def _lane_halves():
    lo = lax.broadcasted_iota(jnp.int32, (WINDOW, LANES), 1) < HEAD_DIM
    return lo, [jnp.where(lo, 1.0, 0.0).astype(BF16), jnp.where(lo, 0.0, 1.0).astype(BF16)]


def _stack_heads(ref, first_tile, n_tiles, halves):
    parts = []
    for t in range(first_tile, first_tile + n_tiles):
        tile = ref[:, LANES * t:LANES * (t + 1)]
        parts += [tile * halves[0], tile * halves[1]]
    return jnp.concatenate(parts, axis=0)


def _unstack_heads(ref, first_tile, n_tiles, lo, stacked):
    for i in range(n_tiles):
        a = stacked[2 * i * WINDOW:(2 * i + 1) * WINDOW]
        b = stacked[(2 * i + 1) * WINDOW:(2 * i + 2) * WINDOW]
        t = first_tile + i
        ref[:, LANES * t:LANES * (t + 1)] = jnp.where(lo, a, b).astype(ref.dtype)


def _attn_probs(qs, kd, scale, bias, valid, sink):
    sc = jnp.where(valid, _dot_nt(qs, kd) * scale + bias, NEG_INF)
    m = jnp.maximum(jnp.max(sc, axis=-1, keepdims=True), sink)
    p = jnp.exp(sc - m)
    es = jnp.exp(sink - m)
    inv = 1.0 / (jnp.sum(p, axis=-1, keepdims=True) + es)
    return p * inv, es * inv


def _attn_specs(n_heads, nb):
    QW = n_heads * HEAD_DIM
    kblk, vblk = QW // KV_DUP, QW // KV_DUP + 1
    last = nb - 1
    cur = lambda b: (lambda n: (jnp.minimum(n, last), b))
    prev = lambda b: (lambda n: (jnp.clip(n - 1, 0, last), b))
    kv = [pl.BlockSpec((WINDOW, KV_DUP), cur(kblk)), pl.BlockSpec((WINDOW, KV_DUP), prev(kblk)),
          pl.BlockSpec((WINDOW, KV_DUP), cur(vblk)), pl.BlockSpec((WINDOW, KV_DUP), prev(vblk))]
    return QW, cur, prev, kv


def _attn_fwd(proj, sinks, n_heads, comm=None):
    S = proj.shape[0]
    nb = S // WINDOW
    group = n_heads // N_KV_HEADS
    slopes = [2.0 ** (-8.0 * (h + 1) / n_heads) for h in range(n_heads)]
    scale = 1.0 / math.sqrt(HEAD_DIM)
    QW, cur, prev, kv_specs = _attn_specs(n_heads, nb)

    def body(sink_ref, q_ref, kc_ref, kp_ref, vc_ref, vp_ref, o_ref):
        n = pl.program_id(0)
        lo, halves = _lane_halves()
        for kh in range(N_KV_HEADS):
            heads = range(kh * group, (kh + 1) * group)
            valid, bias, sink = _attn_group(n, group, [slopes[h] for h in heads], [sink_ref[h] for h in heads])
            ksl = slice(LANES * kh, LANES * (kh + 1))
            kd = jnp.concatenate([kp_ref[:, ksl], kc_ref[:, ksl]], axis=0)
            vd = jnp.concatenate([vp_ref[:, ksl], vc_ref[:, ksl]], axis=0)
            qs = _stack_heads(q_ref, kh * group // 2, group // 2, halves)
            pn, _ = _attn_probs(qs, kd, scale, bias, valid, sink)
            _unstack_heads(o_ref, kh * group // 2, group // 2, lo, _dot(pn.astype(BF16), vd))

    return _pcall(
        body, (sinks, proj, proj, proj, proj, proj), grid=(nb,),
        in_specs=[pl.BlockSpec(memory_space=pltpu.SMEM), pl.BlockSpec((WINDOW, QW), cur(0))] + kv_specs,
        out_specs=[pl.BlockSpec((WINDOW, QW), cur(0))],
        out_shape=[_sds((S, QW), BF16)],
        sem=("parallel",), name="attn_fwd", comm=comm)


def _attn_bwd(proj, dmix, sinks, n_heads, comm=None):
    S = proj.shape[0]
    nb = S // WINDOW
    group = n_heads // N_KV_HEADS
    slopes = [2.0 ** (-8.0 * (h + 1) / n_heads) for h in range(n_heads)]
    scale = 1.0 / math.sqrt(HEAD_DIM)
    QW, cur, prev, kv_specs = _attn_specs(n_heads, nb)

    def body(sink_ref, q_ref, kc_ref, kp_ref, vc_ref, vp_ref, do_ref, dq_ref, dkv_ref, dsink_ref, carry_ref):
        n = pl.program_id(0)

        @pl.when(n == 0)
        def _():
            carry_ref[...] = jnp.zeros_like(carry_ref)
            dsink_ref[...] = jnp.zeros_like(dsink_ref)

        @pl.when(n < nb)
        def _():
            lo, halves = _lane_halves()
            lane1 = lax.broadcasted_iota(jnp.int32, (1, LANES), 1)
            dkd, dvd = [], []
            dsink = jnp.zeros((1, LANES), F32)
            for kh in range(N_KV_HEADS):
                heads = range(kh * group, (kh + 1) * group)
                valid, bias, sink = _attn_group(n, group, [slopes[h] for h in heads], [sink_ref[h] for h in heads])
                ksl = slice(LANES * kh, LANES * (kh + 1))
                kd = jnp.concatenate([kp_ref[:, ksl], kc_ref[:, ksl]], axis=0)
                vd = jnp.concatenate([vp_ref[:, ksl], vc_ref[:, ksl]], axis=0)
                qs = _stack_heads(q_ref, kh * group // 2, group // 2, halves)
                dos = _stack_heads(do_ref, kh * group // 2, group // 2, halves)
                pn, psink = _attn_probs(qs, kd, scale, bias, valid, sink)
                dp = _dot_nt(dos, vd)
                delta = jnp.sum(pn * dp, axis=-1, keepdims=True)
                dsb = (pn * (dp - delta) * scale).astype(BF16)
                sd = psink * delta
                for i, h in enumerate(heads):
                    dsink = dsink - jnp.where(lane1 == h, jnp.sum(sd[i * WINDOW:(i + 1) * WINDOW]), 0.0)
                _unstack_heads(dq_ref, kh * group // 2, group // 2, lo, _dot(dsb, kd))
                dkd.append(_dot_tn(dsb, qs))
                dvd.append(_dot_tn(pn.astype(BF16), dos))
            dsink_ref[...] += dsink
            both = jnp.concatenate(dkd + dvd, axis=1)
            dkv_ref[...] = (carry_ref[...] + both[0:WINDOW]).astype(BF16)
            carry_ref[...] = both[WINDOW:2 * WINDOW]

        @pl.when(n == nb)
        def _():
            dkv_ref[...] = carry_ref[...].astype(BF16)

    return _pcall(
        body, (sinks, proj, proj, proj, proj, proj, dmix), grid=(nb + 1,),
        in_specs=[pl.BlockSpec(memory_space=pltpu.SMEM), pl.BlockSpec((WINDOW, QW), cur(0))] + kv_specs
                 + [pl.BlockSpec((WINDOW, QW), cur(0))],
        out_specs=[pl.BlockSpec((WINDOW, QW), cur(0)),
                   pl.BlockSpec((WINDOW, 2 * KV_DUP), prev(0)),
                   pl.BlockSpec((1, LANES), lambda n: (0, 0))],
        out_shape=[_sds((S, QW), BF16), _sds((S, 2 * KV_DUP), BF16), _sds((1, LANES), F32)],
        scratch_shapes=[pltpu.VMEM((WINDOW, 2 * KV_DUP), F32)],
        sem=("arbitrary",), name="attn_bwd", comm=comm)


def _glu_window(a_ref, g_ref, ap_ref, gp_ref, win_ref, first):
    tm = a_ref.shape[0]
    zp = ap_ref[...].astype(F32) * _sigmoid(gp_ref[...].astype(F32))
    win_ref[0:HALO, :] = jnp.where(first, jnp.zeros_like(zp), zp)
    win_ref[HALO:HALO + tm, :] = a_ref[...].astype(F32) * _sigmoid(g_ref[...].astype(F32))


def _preshift(win_ref, sh_ref):
    n = win_ref.shape[0] - SUBLANES
    for s in range(1, SUBLANES):
        sh_ref[s - 1, 0:n, :] = win_ref[s:s + n, :]


def _window(win_ref, sh_ref, start):
    s = start % SUBLANES
    if s == 0:
        return win_ref[start:start + ROWS, :]
    return sh_ref[s - 1, start - s:start - s + ROWS, :]


def _conv_fwd(proj, wdw, bdw, lng, lnb, n_heads, tm, comm=None):
    S = proj.shape[0]
    taps, C = wdw.shape
    ablk = (n_heads * HEAD_DIM + 2 * KV_DUP) // C
    hb = tm // HALO
    off = HALO - (taps - 1)

    def body(a_ref, g_ref, ap_ref, gp_ref, w_ref, b_ref, lg_ref, lb_ref, o_ref, y_ref, win_ref, sh_ref):
        _glu_window(a_ref, g_ref, ap_ref, gp_ref, win_ref, pl.program_id(0) == 0)
        _preshift(win_ref, sh_ref)
        for c in range(tm // ROWS):
            r0 = c * ROWS
            acc = jnp.zeros((ROWS, C), F32) + b_ref[...]
            for k in range(taps):
                acc = acc + w_ref[k:k + 1, :] * _window(win_ref, sh_ref, r0 + off + k)
            y_ref[r0:r0 + ROWS, :] = acc
        y = y_ref[...]
        mu = jnp.mean(y, axis=-1, keepdims=True)
        yc = y - mu
        yn = yc * lax.rsqrt(jnp.mean(yc * yc, axis=-1, keepdims=True) + RMS_EPS) * lg_ref[...] + lb_ref[...]
        o_ref[...] = (yn * _sigmoid(yn)).astype(BF16)

    vec = pl.BlockSpec((1, C), lambda i: (0, 0))
    halo = lambda b: pl.BlockSpec((HALO, C), lambda i: (jnp.maximum(i * hb - 1, 0), b))
    return _pcall(
        body, (proj, proj, proj, proj, wdw, bdw, lng, lnb,), grid=(S // tm,),
        in_specs=[pl.BlockSpec((tm, C), lambda i: (i, ablk)), pl.BlockSpec((tm, C), lambda i: (i, ablk + 1)),
                  halo(ablk), halo(ablk + 1),
                  pl.BlockSpec((taps, C), lambda i: (0, 0)), vec, vec, vec],
        out_specs=[pl.BlockSpec((tm, C), lambda i: (i, 0)), pl.BlockSpec((tm, C), lambda i: (i, 0))],
        out_shape=[_sds((S, C), BF16), _sds((S, C), F32)],
        scratch_shapes=[pltpu.VMEM((tm + HALO, C), F32), pltpu.VMEM((SUBLANES - 1, tm + HALO, C), F32)],
        sem=("parallel",), name="conv_fwd", comm=comm)


def _conv_bwd(proj, dmix, ysave, wdw, lng, lnb, n_heads, tm, comm=None):
    S = proj.shape[0]
    taps, C = wdw.shape
    QW = n_heads * HEAD_DIM
    ablk = (QW + 2 * KV_DUP) // C
    cblk = QW // C
    hb = tm // HALO
    nt = S // tm
    off = HALO - (taps - 1)

    def ln_bwd(dc, y, lg, lb):
        mu = jnp.mean(y, axis=-1, keepdims=True)
        yc = y - mu
        r = lax.rsqrt(jnp.mean(yc * yc, axis=-1, keepdims=True) + RMS_EPS)
        yh = yc * r
        yn = yh * lg + lb
        sg = _sigmoid(yn)
        dyn = dc * (sg * (1.0 + yn * (1.0 - sg)))
        dyh = dyn * lg
        dy = r * (dyh - jnp.mean(dyh, axis=-1, keepdims=True) - yh * jnp.mean(dyh * yh, axis=-1, keepdims=True))
        return dy, dyn, yh

    def body(dc_ref, dcn_ref, y_ref, yn_ref, a_ref, g_ref, ap_ref, gp_ref, w_ref, lg_ref, lb_ref,
             dag_ref, dw_ref, dvec_ref, zwin_ref, dyw_ref, dwacc_ref, zsh_ref, dysh_ref):
        i = pl.program_id(0)

        @pl.when(i == 0)
        def _():
            dwacc_ref[...] = jnp.zeros_like(dwacc_ref)
            dvec_ref[...] = jnp.zeros_like(dvec_ref)

        lg, lb = lg_ref[...], lb_ref[...]
        dy, dyn, yh = ln_bwd(dc_ref[...].astype(F32), y_ref[...], lg, lb)
        dy_next, _, _ = ln_bwd(dcn_ref[...].astype(F32), yn_ref[...], lg, lb)
        dyw_ref[0:tm, :] = dy
        dyw_ref[tm:tm + HALO, :] = jnp.where(i == nt - 1, jnp.zeros_like(dy_next), dy_next)
        dvec_ref[0:1, :] += jnp.sum(dy, axis=0, keepdims=True)
        dvec_ref[1:2, :] += jnp.sum(dyn * yh, axis=0, keepdims=True)
        dvec_ref[2:3, :] += jnp.sum(dyn, axis=0, keepdims=True)
        _glu_window(a_ref, g_ref, ap_ref, gp_ref, zwin_ref, i == 0)
        _preshift(zwin_ref, zsh_ref)
        _preshift(dyw_ref, dysh_ref)

        for c in range(tm // ROWS):
            r0 = c * ROWS
            dz = jnp.zeros((ROWS, C), F32)
            dyc = dyw_ref[r0:r0 + ROWS, :]
            for k in range(taps):
                dz = dz + w_ref[k:k + 1, :] * _window(dyw_ref, dysh_ref, r0 + taps - 1 - k)
                prod = dyc * _window(zwin_ref, zsh_ref, r0 + off + k)
                dwacc_ref[k] += jnp.sum(prod.reshape(ROWS // SUBLANES, SUBLANES, C), axis=0)
            a = a_ref[r0:r0 + ROWS, :].astype(F32)
            s = _sigmoid(g_ref[r0:r0 + ROWS, :].astype(F32))
            dag_ref[r0:r0 + ROWS, 0:C] = (dz * s).astype(BF16)
            dag_ref[r0:r0 + ROWS, C:2 * C] = (dz * a * s * (1.0 - s)).astype(BF16)

        @pl.when(i == nt - 1)
        def _():
            dw_ref[...] = jnp.zeros_like(dw_ref)
            for k in range(taps):
                dw_ref[k:k + 1, :] = jnp.sum(dwacc_ref[k], axis=0, keepdims=True)

    vec = pl.BlockSpec((1, C), lambda i: (0, 0))
    tile = lambda b: pl.BlockSpec((tm, C), lambda i: (i, b))
    prev = lambda b: pl.BlockSpec((HALO, C), lambda i: (jnp.maximum(i * hb - 1, 0), b))
    nxt = lambda b: pl.BlockSpec((HALO, C), lambda i: (jnp.minimum((i + 1) * hb, S // HALO - 1), b))
    return _pcall(
        body, (dmix, dmix, ysave, ysave, proj, proj, proj, proj, wdw, lng, lnb,), grid=(nt,),
        in_specs=[tile(cblk), nxt(cblk), tile(0), nxt(0), tile(ablk), tile(ablk + 1), prev(ablk), prev(ablk + 1),
                  pl.BlockSpec((taps, C), lambda i: (0, 0)), vec, vec],
        out_specs=[pl.BlockSpec((tm, 2 * C), lambda i: (i, 0)),
                   pl.BlockSpec((HALO, C), lambda i: (0, 0)),
                   pl.BlockSpec((8, C), lambda i: (0, 0))],
        out_shape=[_sds((S, 2 * C), BF16), _sds((HALO, C), F32), _sds((8, C), F32)],
        scratch_shapes=[pltpu.VMEM((tm + HALO, C), F32), pltpu.VMEM((tm + HALO, C), F32),
                        pltpu.VMEM((taps, SUBLANES, C), F32),
                        pltpu.VMEM((SUBLANES - 1, tm + HALO, C), F32), pltpu.VMEM((SUBLANES - 1, tm + HALO, C), F32)],
        sem=("arbitrary",), name="conv_bwd", comm=comm)


def _mixout_fwd(x, attn, conv, wo, tm, comm=None):
    S, D = x.shape
    QW, C = attn.shape[1], conv.shape[1]

    def body(x_ref, a_ref, c_ref, w_ref, o_ref):
        o_ref[...] = x_ref[...] + _dot(a_ref[...], w_ref[0:QW, :]) + _dot(c_ref[...], w_ref[QW:QW + C, :])

    return _pcall(
        body, (x, attn, conv, wo,), grid=(S // tm,),
        in_specs=[pl.BlockSpec((tm, D), lambda i: (i, 0)),
                  pl.BlockSpec((tm, QW), lambda i: (i, 0)),
                  pl.BlockSpec((tm, C), lambda i: (i, 0)),
                  pl.BlockSpec((QW + C, D), lambda i: (0, 0))],
        out_specs=[pl.BlockSpec((tm, D), lambda i: (i, 0))],
        out_shape=[_sds((S, D), F32)],
        sem=("parallel",), name="mixout_fwd", comm=comm)


def _mixout_bwd(dxo, attn, conv, wo, tm, comm=None):
    S, D = dxo.shape
    QW, C = attn.shape[1], conv.shape[1]
    nt = S // tm

    def body(dx_ref, a_ref, c_ref, w_ref, dm_ref, dw_ref, acc_ref):
        i = pl.program_id(0)
        dxb = dx_ref[...].astype(BF16)
        dm_ref[...] = _dot_nt(dxb, w_ref[...]).astype(BF16)

        @pl.when(i == 0)
        def _():
            acc_ref[...] = jnp.zeros_like(acc_ref)

        acc_ref[0:QW, :] += _dot_tn(a_ref[...], dxb)
        acc_ref[QW:QW + C, :] += _dot_tn(c_ref[...], dxb)

        @pl.when(i == nt - 1)
        def _():
            dw_ref[...] = acc_ref[...].astype(BF16)

    return _pcall(
        body, (dxo, attn, conv, wo,), grid=(nt,),
        in_specs=[pl.BlockSpec((tm, D), lambda i: (i, 0)),
                  pl.BlockSpec((tm, QW), lambda i: (i, 0)),
                  pl.BlockSpec((tm, C), lambda i: (i, 0)),
                  pl.BlockSpec((QW + C, D), lambda i: (0, 0))],
        out_specs=[pl.BlockSpec((tm, QW + C), lambda i: (i, 0)),
                   pl.BlockSpec((QW + C, D), lambda i: (0, 0))],
        out_shape=[_sds((S, QW + C), BF16), _sds((QW + C, D), BF16)],
        scratch_shapes=[pltpu.VMEM((QW + C, D), F32)],
        sem=("arbitrary",), name="mixout_bwd", comm=comm)


def _loss_head(x, gain, target, tm, comm=None):
    S, D = x.shape
    nt = S // tm

    def body(x_ref, g_ref, t_ref, dx_ref, loss_ref, dgain_ref):
        i = pl.program_id(0)
        xh, r = _rms(x_ref[...])
        e = xh * g_ref[...] - t_ref[...]
        loss_ref[...] = jnp.zeros((1, LANES), F32) + 0.5 * jnp.sum(jnp.mean(e * e, axis=-1, keepdims=True))
        dxn, dgn = _rms_bwd(e * (1.0 / D), xh, r, g_ref[...])
        dx_ref[...] = dxn

        @pl.when(i == 0)
        def _():
            dgain_ref[...] = dgn

        @pl.when(i > 0)
        def _():
            dgain_ref[...] += dgn

    return _pcall(
        body, (x, gain, target,), grid=(nt,),
        in_specs=[pl.BlockSpec((tm, D), lambda i: (i, 0)),
                  pl.BlockSpec((1, D), lambda i: (0, 0)),
                  pl.BlockSpec((tm, D), lambda i: (i, 0))],
        out_specs=[pl.BlockSpec((tm, D), lambda i: (i, 0)),
                   pl.BlockSpec((None, 1, LANES), lambda i: (i, 0, 0)),
                   pl.BlockSpec((1, D), lambda i: (0, 0))],
        out_shape=[_sds((S, D), F32), _sds((nt, 1, LANES), F32), _sds((1, D), F32)],
        sem=("arbitrary",), name="loss_head", comm=comm)


def _adam(w, m, v, parts, name, comm=None):
    L, R, C = w.shape
    P = parts[0].shape[0]
    br = _row_block(R, 256)
    c1 = 1.0 - ADAM_B1 ** ADAM_STEP
    c2 = 1.0 - ADAM_B2 ** ADAM_STEP

    def body(w_ref, m_ref, v_ref, *rest):
        p_refs, (g_ref, d_ref, mo_ref, vo_ref) = rest[:L], rest[L:]
        layer = pl.program_id(0)

        def update(p_ref):
            g = p_ref[0].astype(F32)
            for k in range(1, P):
                g = g + p_ref[k].astype(F32)
            mn = ADAM_B1 * m_ref[...] + (1.0 - ADAM_B1) * g
            vn = ADAM_B2 * v_ref[...] + (1.0 - ADAM_B2) * (g * g)
            g_ref[...] = g
            mo_ref[...] = mn
            vo_ref[...] = vn
            d_ref[...] = -ADAM_LR * ((mn / c1) / (jnp.sqrt(vn / c2) + ADAM_EPS) + ADAM_WD * w_ref[...])

        for k in range(L):
            pl.when(layer == k)(functools.partial(update, p_refs[k]))

    blk = pl.BlockSpec((None, br, C), lambda l, i: (l, i, 0))
    part = lambda k: pl.BlockSpec((P, br, C), lambda l, i: (0, jnp.where(l == k, i, 0), 0))
    return _pcall(
        body, (w, m, v, *parts), grid=(L, R // br),
        in_specs=[blk, blk, blk] + [part(k) for k in range(L)],
        out_specs=[blk, blk, blk, blk],
        out_shape=[_sds((L, R, C), F32)] * 4,
        sem=("parallel", "parallel"), name=name, comm=comm)


def _sum_parts(parts):
    P, R, C = parts.shape

    def body(p_ref, o_ref):
        g = p_ref[0]
        for k in range(1, P):
            g = g + p_ref[k]
        o_ref[...] = g

    vmem = pl.BlockSpec(memory_space=pltpu.VMEM)
    return _pcall(body, (parts,), in_specs=[vmem], out_specs=[vmem], out_shape=[_sds((R, C), F32)],
                  name="sum_parts")[0]


def _place():
    x, y, c = lax.axis_index("x"), lax.axis_index("y"), lax.axis_index("c")
    return x, y, c, [(1 - x, y), (x, 1 - y), (1 - x, 1 - y)]


def _dev(px, py, pc):
    return 4 * px + 2 * py + pc


def _gather_comm(shards, fulls, slot_of):
    n = len(shards)

    def copies(srcs, outs, send_sems, recv_sems):
        x, y, c, chips = _place()

        def copy(a, k, block, to, from_shard=False):
            dst = slot_of[a](outs[a], _dev(*block))
            return pltpu.make_async_remote_copy(
                src_ref=srcs[a] if from_shard else dst, dst_ref=dst,
                send_sem=send_sems.at[a, k], recv_sem=recv_sems.at[a, k], device_id=to, device_id_type=MESH)

        return copy, (x, y, c), (x, y, 1 - c), chips

    def local(srcs, outs, local_sems):
        x, y, c, _ = _place()
        return [pltpu.make_async_copy(srcs[a], slot_of[a](outs[a], _dev(x, y, c)), local_sems.at[a])
                for a in range(n)]

    def first_copies(copy, me, sibling, chips):
        out = []
        for a in range(n):
            out.append(copy(a, 0, me, sibling, True))
            out += [copy(a, 1 + j, me, (*chip, me[2]), True) for j, chip in enumerate(chips)]
        return out

    def start(srcs, outs, sems):
        send_sems, recv_sems, local_sems = sems
        copy, me, sibling, chips = copies(srcs, outs, send_sems, recv_sems)
        for cp in local(srcs, outs, local_sems):
            cp.start()
        for cp in first_copies(copy, me, sibling, chips):
            cp.start()

    def forwards(copy, me, sibling, chips):
        return [copy(a, 4 + j, (*chip, me[2]), sibling) for j, chip in enumerate(chips) for a in range(n)]

    def mid(srcs, outs, sems):
        send_sems, recv_sems, local_sems = sems
        copy, me, sibling, chips = copies(srcs, outs, send_sems, recv_sems)
        for j, chip in enumerate(chips):
            for a in range(n):
                copy(a, 1 + j, (*chip, me[2]), me).wait_recv()
                copy(a, 4 + j, (*chip, me[2]), sibling).start()

    def finish(srcs, outs, sems):
        send_sems, recv_sems, local_sems = sems
        copy, me, sibling, chips = copies(srcs, outs, send_sems, recv_sems)
        c = me[2]
        for a in range(n):
            copy(a, 0, sibling, me).wait_recv()
            for j, chip in enumerate(chips):
                copy(a, 4 + j, (*chip, 1 - c), me).wait_recv()
        for cp in first_copies(copy, me, sibling, chips) + forwards(copy, me, sibling, chips):
            cp.wait_send()
        for cp in local(srcs, outs, local_sems):
            cp.wait()

    sems = [pltpu.SemaphoreType.DMA((n, 7)), pltpu.SemaphoreType.DMA((n, 7)), pltpu.SemaphoreType.DMA((n,))]
    return _Comm(shards, fulls, sems, start, finish, mid)


def _swap_comm(grads):
    n = len(grads)

    def copies(srcs, outs, sems):
        x, y, c, _ = _place()
        return [pltpu.make_async_remote_copy(
            src_ref=srcs[a].at[:, pl.ds(1 - c, 1)], dst_ref=outs[a],
            send_sem=sems[0].at[a], recv_sem=sems[1].at[a], device_id=(x, y, 1 - c), device_id_type=MESH)
            for a in range(n)]

    def start(srcs, outs, sems):
        for cp in copies(srcs, outs, sems):
            cp.start()

    def finish(srcs, outs, sems):
        for cp in copies(srcs, outs, sems):
            cp.wait()

    return _Comm(grads, [_sds((N_CHIP, 1) + g.shape[2:], g.dtype) for g in grads],
                 [pltpu.SemaphoreType.DMA((n,)), pltpu.SemaphoreType.DMA((n,))], start, finish)


def _exchange_comm(parts):
    n = len(parts)

    def copies(srcs, outs, sems):
        x, y, c, chips = _place()
        mine = 2 * x + y
        loc = [pltpu.make_async_copy(srcs[a].at[pl.ds(mine, 1)], outs[a].at[pl.ds(mine, 1)], sems[2].at[a])
               for a in range(n)]
        rem = [pltpu.make_async_remote_copy(
            src_ref=srcs[a].at[pl.ds(2 * px + py, 1)], dst_ref=outs[a].at[pl.ds(mine, 1)],
            send_sem=sems[0].at[a, j], recv_sem=sems[1].at[a, j], device_id=(px, py, c), device_id_type=MESH)
            for a in range(n) for j, (px, py) in enumerate(chips)]
        return loc + rem

    def start(srcs, outs, sems):
        for cp in copies(srcs, outs, sems):
            cp.start()

    def finish(srcs, outs, sems):
        for cp in copies(srcs, outs, sems):
            cp.wait()

    return _Comm(parts, [_sds(p.shape, p.dtype) for p in parts],
                 [pltpu.SemaphoreType.DMA((n, 3)), pltpu.SemaphoreType.DMA((n, 3)), pltpu.SemaphoreType.DMA((n,))],
                 start, finish)


FWD_KERNELS = [("ffn1", ("a1", "b1"), 26), ("mixin", ("wi",), 7), ("attn", (), 7), ("conv", (), 15), ("mixout", ("wo",), 6),
               ("ffn2", ("a2", "b2"), 26)]
GATHER_SHARE = 70
FIRST_GATHERED = ("a1", "b1", "wi", "wo")


def _plan_gathers(n_layers, shard_bytes):
    per_layer = sum(shard_bytes.values())
    cost = {n: GATHER_SHARE * b / per_layer for n, b in shard_bytes.items()}
    room = {len(FWD_KERNELS) * l + i: k[2] for l in range(n_layers) for i, k in enumerate(FWD_KERNELS)}
    first, plan = [], {}
    for l in range(n_layers):
        for i, (_, needs, _) in enumerate(FWD_KERNELS):
            due = len(FWD_KERNELS) * l + i
            for name in needs:
                if l == 0 and name in FIRST_GATHERED:
                    first.append((name, l))
                    continue
                fits = [k for k in range(due) if room[k] >= cost[name]]
                k = fits[0] if fits else max(range(due), key=lambda k: room[k])
                room[k] -= cost[name]
                plan.setdefault(k, []).append((name, l))
    return first, plan


def _comm_only(comm, name):
    return _pcall(lambda: None, (), in_specs=[], out_specs=[], out_shape=[], name=name, comm=comm)


def _gather_small(v):
    R, C = v.shape

    def body(x_ref, out_ref, send_sems, recv_sems, local_sem):
        x, y, c, chips = _place()
        me, sibling = (x, y, c), (x, y, 1 - c)

        def copy(k, block, to, from_shard=False):
            dst = out_ref.at[_dev(*block)]
            return pltpu.make_async_remote_copy(
                src_ref=x_ref if from_shard else dst, dst_ref=dst,
                send_sem=send_sems.at[k], recv_sem=recv_sems.at[k], device_id=to, device_id_type=MESH)

        mine = pltpu.make_async_copy(x_ref, out_ref.at[_dev(*me)], local_sem)
        mine.start()
        first = [copy(0, me, sibling, True)] + [copy(1 + j, me, (*chip, c), True) for j, chip in enumerate(chips)]
        for cp in first:
            cp.start()
        passed = [copy(4 + j, (*chip, c), sibling) for j, chip in enumerate(chips)]
        for j, chip in enumerate(chips):
            copy(1 + j, (*chip, c), me).wait_recv()
            passed[j].start()
        copy(0, sibling, me).wait_recv()
        for j, chip in enumerate(chips):
            copy(4 + j, (*chip, 1 - c), me).wait_recv()
        for cp in first + passed:
            cp.wait_send()
        mine.wait()

    vmem = pl.BlockSpec(memory_space=pltpu.VMEM)
    return _pcall(
        body, (v,), in_specs=[vmem], out_specs=[vmem], out_shape=[_sds((N_DEV, R, C), F32)],
        scratch_shapes=[pltpu.SemaphoreType.DMA((7,)), pltpu.SemaphoreType.DMA((7,)), pltpu.SemaphoreType.DMA],
        name="gather_small")[0]


def _add_sibling(core, g, r):
    _, _, R, C = g.shape
    br = _row_block(R)

    def body(c_ref, g_ref, r_ref, o_ref):
        o_ref[...] = (g_ref[...].astype(F32) + r_ref[...].astype(F32)).astype(BF16)

    return _pcall(
        body, (core, g, r),
        grid_spec=pltpu.PrefetchScalarGridSpec(
            num_scalar_prefetch=1, grid=(N_CHIP, R // br),
            in_specs=[pl.BlockSpec((None, None, br, C), lambda k, i, c_ref: (k, c_ref[0], i, 0)),
                      pl.BlockSpec((None, None, br, C), lambda k, i, c_ref: (k, 0, i, 0))],
            out_specs=pl.BlockSpec((None, br, C), lambda k, i, c_ref: (k, i, 0))),
        out_shape=_sds((N_CHIP, R, C), BF16), sem=("parallel", "parallel"), name="rs_add_sibling")


def _by_chip(g):
    return g.reshape((N_CHIP, 2) + g.shape[1:])


def _pack_rows(vecs):
    flat = jnp.concatenate([v.reshape(-1).astype(F32) for v in vecs])
    rows = -(-flat.shape[0] // (8 * LANES)) * 8
    return jnp.pad(flat, (0, rows * LANES - flat.shape[0])).reshape(rows, LANES)


def _unpack_rows(rows, shapes):
    flat = rows.reshape(-1)
    out, o = [], 0
    for s in shapes:
        n = math.prod(s)
        out.append(flat[o:o + n].reshape(s))
        o += n
    return out


def _adam_any(w, m, v, g, name):
    shape = w.shape
    one = lambda t: t.reshape(1, -1, shape[-1])
    return tuple(t.reshape(shape) for t in _adam(one(w), one(m), one(v), [one(g)], name))


def kernel(x, norm_ffn1, w_ffn1_in, w_ffn1_out, norm_mix, w_in, sinks, w_dw, b_dw, conv_ln_g, conv_ln_b, w_out, norm_ffn2, w_ffn2_in, w_ffn2_out, final_norm, loss_target, m_norm_ffn1, m_w_ffn1_in, m_w_ffn1_out, m_norm_mix, m_w_in, m_sinks, m_w_dw, m_b_dw, m_conv_ln_g, m_conv_ln_b, m_w_out, m_norm_ffn2, m_w_ffn2_in, m_w_ffn2_out, m_final_norm, v_norm_ffn1, v_w_ffn1_in, v_w_ffn1_out, v_norm_mix, v_w_in, v_sinks, v_w_dw, v_b_dw, v_conv_ln_g, v_conv_ln_b, v_w_out, v_norm_ffn2, v_w_ffn2_in, v_w_ffn2_out, v_final_norm):
    _, S, D = x.shape
    L = norm_ffn1.shape[0]
    NF = w_ffn1_in.shape[2]
    RF = w_ffn1_out.shape[1]
    NW = w_in.shape[2]
    RO = w_out.shape[1]
    taps, CD = w_dw.shape[1], w_dw.shape[2]
    H = sinks.shape[1]
    C = N_DEV * CD
    QW = H * HEAD_DIM
    KVW = N_KV_HEADS * HEAD_DIM
    assert 2 * RF == NF and QW + C == N_DEV * RO and N_DEV * NW == QW + 2 * KVW + 2 * C
    tm = min(512, S)
    ta = min(256, S)
    tc = min(256, S)
    core = lax.axis_index("c").astype(jnp.int32).reshape(1)

    x0 = x[0]
    target = loss_target[0]

    wdw_all = _gather_small(_pack_rows([w_dw]))
    n_dw = L * taps * CD
    wdw_full = jnp.stack([wdw_all[d].reshape(-1)[:n_dw].reshape(L, taps, CD) for d in range(N_DEV)],
                         axis=2).reshape(L, taps, C)

    def shard(name, l):
        return {"a1": lambda: w_ffn1_in[l].T.astype(BF16), "a2": lambda: w_ffn2_in[l].T.astype(BF16),
                "b1": lambda: w_ffn1_out[l].astype(BF16), "b2": lambda: w_ffn2_out[l].astype(BF16),
                "wi": lambda: w_in[l].T.astype(BF16), "wo": lambda: w_out[l].astype(BF16)}[name]()

    rows_of = {"a1": NF, "a2": NF, "b1": RF, "b2": RF, "wi": NW, "wo": RO}
    full_of = {n: _sds((N_DEV * r, D), BF16) for n, r in rows_of.items()}
    slot_of = {n: (lambda ref, b, r=r: ref.at[pl.ds(b * r, r)]) for n, r in rows_of.items()}
    first, plan = _plan_gathers(L, {n: r * D * 2 for n, r in rows_of.items()})
    got = {}

    def gather(items):
        if not items:
            return None
        return _gather_comm([shard(n, l) for n, l in items], [full_of[n] for n, _ in items],
                            [slot_of[n] for n, _ in items])

    def carrying(k, call, n_own):
        items = plan.get(k, [])
        res = call(gather(items))
        got.update(zip(items, res[n_own:]))
        return res[:n_own]

    def wext_of(wi):
        q, k, v, u = wi[:QW], wi[QW:QW + KVW], wi[QW + KVW:QW + 2 * KVW], wi[QW + 2 * KVW:]
        dup = lambda t: jnp.concatenate(
            [t[HEAD_DIM * (i // 2):HEAD_DIM * (i // 2 + 1)] for i in range(2 * N_KV_HEADS)], axis=0)
        return jnp.concatenate([q, dup(k), dup(v), u], axis=0)

    got.update(zip(first, _comm_only(gather(first), "ag_first")))

    saved = []
    xc = x0
    for l in range(L):
        k0 = len(FWD_KERNELS) * l
        g1, gm, g2 = norm_ffn1[l][None], norm_mix[l][None], norm_ffn2[l][None]
        x1, gu1 = carrying(k0, lambda c: _ffn_fwd(xc, g1, got["a1", l], got["b1", l], tm, c), 2)
        wext = wext_of(got["wi", l])
        proj, = carrying(k0 + 1, lambda c: _mixin_fwd(x1, gm, wext, tm, c), 1)
        attn, = carrying(k0 + 2, lambda c: _attn_fwd(proj, sinks[l], H, c), 1)
        conv, ysave = carrying(k0 + 3, lambda c: _conv_fwd(proj, wdw_full[l], b_dw[l][None], conv_ln_g[l][None],
                                                           conv_ln_b[l][None], H, tc, c), 2)
        x2, = carrying(k0 + 4, lambda c: _mixout_fwd(x1, attn, conv, got["wo", l], tm, c), 1)
        x3, gu2 = carrying(k0 + 5, lambda c: _ffn_fwd(x2, g2, got["a2", l], got["b2", l], tm, c), 2)
        W = dict(a1=got["a1", l], b1=got["b1", l], a2=got["a2", l], b2=got["b2", l], wo=got["wo", l], wext=wext)
        saved.append(dict(W=W, x0=xc, x1=x1, x2=x2, gu1=gu1, gu2=gu2, proj=proj, attn=attn, conv=conv, ysave=ysave))
        xc = x3

    dx, loss_parts, dfinal = _loss_head(xc, final_norm[None], target, tm)
    loss = lax.psum(jnp.sum(loss_parts[:, 0, 0]), ("x", "y", "c"))

    def swap(gs):
        return _swap_comm([_by_chip(g) for g in gs]) if gs else None

    def added(gs, gots):
        return [_add_sibling(core, _by_chip(g), r) for g, r in zip(gs, gots)]

    small = [None] * L
    big = [dict() for _ in range(L)]
    carry = None
    for l in reversed(range(L)):
        sv = saved[l]
        W = sv["W"]
        g1, gm, g2 = norm_ffn1[l][None], norm_mix[l][None], norm_ffn2[l][None]
        fold = lambda t: jnp.concatenate(
            [t[2 * HEAD_DIM * i:2 * HEAD_DIM * i + HEAD_DIM] + t[2 * HEAD_DIM * i + HEAD_DIM:2 * HEAD_DIM * (i + 1)]
             for i in range(N_KV_HEADS)], axis=0)

        r = _ffn_bwd_a(sv["x2"], g2, dx, sv["gu2"], W["a2"], W["b2"], ta, swap(carry[1]) if carry else None)
        dx2, dgu2, dg2, h2, dys2 = r[:5]
        p_a1, p_b1 = added(carry[1], r[5:]) if carry else (None, None)
        r = _ffn_bwd_w(h2, dys2, sv["gu2"], dgu2, tm, _exchange_comm([p_a1]) if carry else None)
        da2, db2 = r[0].reshape(N_DEV, NF, D), r[1].reshape(N_DEV, RF, D)
        if carry:
            big[carry[0]]["a1"], = r[2:]
        dmix, dwo = _mixout_bwd(dx2, sv["attn"], sv["conv"], W["wo"], tm)
        dwo = dwo.reshape(N_DEV, RO, D)
        r = _conv_bwd(sv["proj"], dmix, sv["ysave"], wdw_full[l], conv_ln_g[l][None], conv_ln_b[l][None], H, tc,
                      _join(_exchange_comm([p_b1]) if carry else None, swap([da2, db2, dwo])))
        dag, dwdw, dvec = r[:3]
        r = r[3:]
        if carry:
            big[carry[0]]["b1"], r = r[0], r[1:]
        p_a2, p_b2, p_wo = added([da2, db2, dwo], r)
        r = _attn_bwd(sv["proj"], dmix, sinks[l], H, _exchange_comm([p_b2]))
        dq, dkv, dsink = r[:3]
        big[l]["b2"] = r[3]
        r = _mixin_bwd(sv["x1"], gm, dx2, dq, dkv, dag, W["wext"], tm, _exchange_comm([p_wo]))
        dx1, dgm, dwext = r[:3]
        big[l]["wo"] = r[3]
        dwi = jnp.concatenate([dwext[:QW], fold(dwext[QW:QW + KV_DUP]),
                               fold(dwext[QW + KV_DUP:QW + 2 * KV_DUP]), dwext[QW + 2 * KV_DUP:]], axis=0)
        dwi = dwi.astype(BF16).reshape(N_DEV, NW, D)
        r = _ffn_bwd_a(sv["x0"], g1, dx1, sv["gu1"], W["a1"], W["b1"], ta, _join(_exchange_comm([p_a2]), swap([dwi])))
        dx0, dgu1, dg1, h1, dys1 = r[:5]
        big[l]["a2"] = r[5]
        p_wi, = added([dwi], r[6:])
        r = _ffn_bwd_w(h1, dys1, sv["gu1"], dgu1, tm, _exchange_comm([p_wi]))
        carry = (l, [r[0].reshape(N_DEV, NF, D), r[1].reshape(N_DEV, RF, D)])
        big[l]["wi"] = r[2]
        dx = dx0
        small[l] = [dg1[0], dgm[0], dsink[0, :H], dwdw[:taps], dvec[0], dvec[1], dvec[2], dg2[0]]

    p_carry = added(carry[1], _comm_only(swap(carry[1]), "rs_swap_last"))
    big[carry[0]]["a1"], big[carry[0]]["b1"] = _comm_only(_exchange_comm(p_carry), "rs_exchange_last")
    grad_x = dx[None]

    small_shapes = [(D,), (D,), (H,), (taps, C), (C,), (C,), (C,), (D,)]
    packed = _pack_rows([t for l in range(L) for t in small[l]] + [dfinal[0]])
    total = _sum_parts(_gather_small(packed))
    flat = _unpack_rows(total, small_shapes * L + [(D,)])
    per = [jnp.stack([flat[l * len(small_shapes) + i] for l in range(L)]) for i in range(len(small_shapes))]
    g_nf1, g_nmix, g_sinks, g_wdw_full, g_bdw, g_lng, g_lnb, g_nf2 = per
    g_final = flat[-1]
    dev = _dev(lax.axis_index("x"), lax.axis_index("y"), lax.axis_index("c"))
    g_wdw = lax.dynamic_slice_in_dim(g_wdw_full, dev * CD, CD, axis=2)

    res = {}
    res["norm_ffn1"] = _adam_any(norm_ffn1, m_norm_ffn1, v_norm_ffn1, g_nf1, "adam_small")
    res["norm_mix"] = _adam_any(norm_mix, m_norm_mix, v_norm_mix, g_nmix, "adam_small")
    res["sinks"] = _adam_any(sinks, m_sinks, v_sinks, g_sinks, "adam_small")
    res["w_dw"] = _adam_any(w_dw, m_w_dw, v_w_dw, g_wdw, "adam_small")
    res["b_dw"] = _adam_any(b_dw, m_b_dw, v_b_dw, g_bdw, "adam_small")
    res["conv_ln_g"] = _adam_any(conv_ln_g, m_conv_ln_g, v_conv_ln_g, g_lng, "adam_small")
    res["conv_ln_b"] = _adam_any(conv_ln_b, m_conv_ln_b, v_conv_ln_b, g_lnb, "adam_small")
    res["norm_ffn2"] = _adam_any(norm_ffn2, m_norm_ffn2, v_norm_ffn2, g_nf2, "adam_small")
    res["final_norm"] = tuple(t[0] for t in _adam_any(final_norm[None], m_final_norm[None], v_final_norm[None],
                                                      g_final[None], "adam_small"))

    def adam_big(key, w, m, v, name, transposed=False, comm=None):
        t = (lambda a: a.transpose(0, 2, 1)) if transposed else (lambda a: a)
        r = _adam(t(w), t(m), t(v), [big[l][key] for l in range(L)], name, comm)
        return tuple(t(o) for o in r[:4]), r[4:]

    res["w_ffn1_in"], _ = adam_big("a1", w_ffn1_in, m_w_ffn1_in, v_w_ffn1_in, "adam_ffn_in", True)
    res["w_ffn1_out"], _ = adam_big("b1", w_ffn1_out, m_w_ffn1_out, v_w_ffn1_out, "adam_ffn_out")
    res["w_ffn2_in"], _ = adam_big("a2", w_ffn2_in, m_w_ffn2_in, v_w_ffn2_in, "adam_ffn_in", True)
    res["w_ffn2_out"], _ = adam_big("b2", w_ffn2_out, m_w_ffn2_out, v_w_ffn2_out, "adam_ffn_out")
    res["w_in"], _ = adam_big("wi", w_in, m_w_in, v_w_in, "adam_w_in", True)
    res["w_out"], _ = adam_big("wo", w_out, m_w_out, v_w_out, "adam_w_out")

    order = ["norm_ffn1", "w_ffn1_in", "w_ffn1_out", "norm_mix", "w_in", "sinks", "w_dw", "b_dw", "conv_ln_g",
             "conv_ln_b", "w_out", "norm_ffn2", "w_ffn2_in", "w_ffn2_out", "final_norm"]
    return (loss, grad_x, *[res[n][0] for n in order], *[res[n][1] for n in order],
            *[res[n][2] for n in order], *[res[n][3] for n in order])
```

```python
import functools
import math

import jax
import jax.numpy as jnp
from jax import lax
from jax.experimental import pallas as pl
from jax.experimental.pallas import tpu as pltpu

F32 = jnp.float32
BF16 = jnp.bfloat16
MESH = pl.DeviceIdType.MESH

N_DEV = 8
N_CHIP = 4
HEAD_DIM = 64
N_KV_HEADS = 2
WINDOW = 128
KV_DUP = 2 * HEAD_DIM * N_KV_HEADS
RMS_EPS = 1e-6
NEG_INF = -1e30
FFN_RES = 0.5
HALO = 32
ROWS = 32
FFN_CHUNK = 512
BWD_W_CHUNK = 384
LANES = 128
SUBLANES = 8
V7X_VMEM_LIMIT = 56 * 1024 * 1024

ADAM_LR = 0.001
ADAM_B1 = 0.9
ADAM_B2 = 0.999
ADAM_EPS = 1e-08
ADAM_WD = 0.01
ADAM_STEP = 10


def _raw_call(body, **kw):
    return pl.pallas_call(body, **kw)


class _Comm:
    def __init__(self, ins, outs, sems, start, finish, mid=None):
        self.ins, self.outs, self.sems, self.start, self.finish = list(ins), list(outs), list(sems), start, finish
        self.mid = mid or (lambda ins, outs, sems: None)


def _join(*comms):
    comms = [c for c in comms if c is not None]
    if not comms:
        return None

    def split(refs, attr):
        out, o = [], 0
        for c in comms:
            n = len(getattr(c, attr))
            out.append(refs[o:o + n])
            o += n
        return out

    def run(which):
        def go(ins, outs, sems):
            for c, i, o, m in zip(comms, split(ins, "ins"), split(outs, "outs"), split(sems, "sems")):
                getattr(c, which)(i, o, m)
        return go

    return _Comm(sum((c.ins for c in comms), []), sum((c.outs for c in comms), []),
                 sum((c.sems for c in comms), []), run("start"), run("finish"), run("mid"))


def _pcall(body, args, *, name, out_shape, grid=(), in_specs=None, out_specs=None, scratch_shapes=(), sem=(),
           comm=None, grid_spec=None):
    if grid_spec is not None:
        return _raw_call(body, grid_spec=grid_spec, out_shape=out_shape, name=name,
                         compiler_params=_params(*sem))(*args)
    if comm is None:
        return _raw_call(body, grid=grid, in_specs=in_specs, out_specs=out_specs, out_shape=out_shape,
                         scratch_shapes=list(scratch_shapes), name=name, compiler_params=_params(*sem))(*args)
    n_in, n_out, n_scr = len(in_specs), len(out_shape), len(scratch_shapes)
    ci, co = len(comm.ins), len(comm.outs)

    def fused(*refs):
        cuts = [n_in, ci, n_out, co, n_scr]
        parts, o = [], 0
        for n in cuts:
            parts.append(refs[o:o + n])
            o += n
        ins, cins, outs, couts, scr = parts
        csems = refs[o:]
        steps = math.prod(grid)
        if steps < 3:
            comm.start(cins, couts, csems)
            body(*ins, *outs, *scr)
            comm.mid(cins, couts, csems)
            comm.finish(cins, couts, csems)
            return
        step = functools.reduce(lambda acc, a: acc * grid[a] + pl.program_id(a), range(len(grid)), 0)

        @pl.when(step == 0)
        def _():
            comm.start(cins, couts, csems)

        @pl.when(step == steps - 2)
        def _():
            comm.mid(cins, couts, csems)

        body(*ins, *outs, *scr)

        @pl.when(step == steps - 1)
        def _():
            comm.finish(cins, couts, csems)

    return _raw_call(
        fused, grid=grid, in_specs=list(in_specs) + [ANY] * ci, out_specs=list(out_specs) + [ANY] * co,
        out_shape=list(out_shape) + comm.outs, scratch_shapes=list(scratch_shapes) + comm.sems, name=name,
        compiler_params=_params(*(["arbitrary"] * len(grid))))(*args, *comm.ins)


ANY = pl.BlockSpec(memory_space=pl.ANY)


def _params(*sem):
    return pltpu.CompilerParams(dimension_semantics=sem, vmem_limit_bytes=V7X_VMEM_LIMIT)


def _dot(a, b):
    return jnp.dot(a, b, preferred_element_type=F32)


def _dot_nt(a, b):
    return lax.dot_general(a, b, (((1,), (1,)), ((), ())), preferred_element_type=F32)


def _dot_tn(a, b):
    return lax.dot_general(a, b, (((0,), (0,)), ((), ())), preferred_element_type=F32)


def _sigmoid(x):
    return 1.0 / (1.0 + jnp.exp(-x))


def _rms(x):
    r = lax.rsqrt(jnp.mean(x * x, axis=-1, keepdims=True) + RMS_EPS)
    return x * r, r


def _rms_bwd(dh, xh, r, g):
    dxh = dh * g
    dx = r * (dxh - xh * jnp.mean(dxh * xh, axis=-1, keepdims=True))
    return dx, jnp.sum(dh * xh, axis=0, keepdims=True)


def _sds(shape, dtype):
    return jax.ShapeDtypeStruct(shape, dtype)


def _row_block(rows, limit=512):
    fits = [d for d in range(16, min(rows, limit) + 1, 16) if rows % d == 0]
    return fits[-1] if fits else rows


def _chunks(n, step):
    return [(o, min(step, n - o)) for o in range(0, n, step)]


def _resident(shape):
    return pl.BlockSpec(shape, lambda *_: (0,) * len(shape), pipeline_mode=pl.Buffered(1))


def _ffn_fwd(x, gain, wint, wout, tm, comm=None):
    S, D = x.shape
    F = wout.shape[0]

    def body(x_ref, g_ref, w_ref, wo_ref, xo_ref, gu_ref, a_ref):
        xh, _ = _rms(x_ref[...])
        h = (xh * g_ref[...]).astype(BF16)
        for o, n in _chunks(F, FFN_CHUNK):
            gb = _dot_nt(h, w_ref[o:o + n, :]).astype(BF16)
            ub = _dot_nt(h, w_ref[F + o:F + o + n, :]).astype(BF16)
            gu_ref[:, o:o + n] = gb
            gu_ref[:, F + o:F + o + n] = ub
            g = gb.astype(F32)
            a_ref[:, o:o + n] = (g * _sigmoid(g) * ub.astype(F32)).astype(BF16)
        xo_ref[...] = x_ref[...] + FFN_RES * _dot(a_ref[...], wo_ref[...])

    return _pcall(
        body, (x, gain, wint, wout), grid=(S // tm,),
        in_specs=[pl.BlockSpec((tm, D), lambda i: (i, 0)), _resident((1, D)),
                  _resident((2 * F, D)), _resident((F, D))],
        out_specs=[pl.BlockSpec((tm, D), lambda i: (i, 0)), pl.BlockSpec((tm, 2 * F), lambda i: (i, 0))],
        out_shape=[_sds((S, D), F32), _sds((S, 2 * F), BF16)],
        scratch_shapes=[pltpu.VMEM((tm, F), BF16)],
        sem=("parallel",), name="ffn_fwd", comm=comm)


def _ffn_bwd_a(x, gain, dxo, gu, wint, wout, tm, comm=None):
    S, D = x.shape
    F = wout.shape[0]

    def body(x_ref, g_ref, dxo_ref, gu_ref, w_ref, wo_ref, dx_ref, dgu_ref, dgain_ref, h_ref, dys_ref):
        i = pl.program_id(0)
        dys = (FFN_RES * dxo_ref[...]).astype(BF16)
        dys_ref[...] = dys
        for o, n in _chunks(F, FFN_CHUNK):
            dact = _dot_nt(dys, wo_ref[o:o + n, :])
            g = gu_ref[:, o:o + n].astype(F32)
            u = gu_ref[:, F + o:F + o + n].astype(F32)
            s = _sigmoid(g)
            dgu_ref[:, o:o + n] = (dact * u * (s * (1.0 + g * (1.0 - s)))).astype(BF16)
            dgu_ref[:, F + o:F + o + n] = (dact * (g * s)).astype(BF16)
        dh = _dot(dgu_ref[...], w_ref[...])
        xh, r = _rms(x_ref[...])
        h_ref[...] = (xh * g_ref[...]).astype(BF16)
        dxn, dgn = _rms_bwd(dh, xh, r, g_ref[...])
        dx_ref[...] = dxo_ref[...] + dxn

        @pl.when(i == 0)
        def _():
            dgain_ref[...] = dgn

        @pl.when(i > 0)
        def _():
            dgain_ref[...] += dgn

    tile = pl.BlockSpec((tm, D), lambda i: (i, 0))
    wide = pl.BlockSpec((tm, 2 * F), lambda i: (i, 0))
    return _pcall(
        body, (x, gain, dxo, gu, wint, wout), grid=(S // tm,),
        in_specs=[tile, _resident((1, D)), tile, wide, _resident((2 * F, D)), _resident((F, D))],
        out_specs=[tile, wide, pl.BlockSpec((1, D), lambda i: (0, 0)), tile, tile],
        out_shape=[_sds((S, D), F32), _sds((S, 2 * F), BF16), _sds((1, D), F32), _sds((S, D), BF16), _sds((S, D), BF16)],
        sem=("arbitrary",), name="ffn_bwd_a", comm=comm)


def _ffn_bwd_w(h, dys, gu, dgu, tk, comm=None):
    S, D = h.shape
    F = gu.shape[1] // 2
    FH = F // 2
    nk = S // tk

    def body(h_ref, dys_ref, gg_ref, gu_ref, dg_ref, du_ref, dw_ref, dwo_ref, accw_ref, acco_ref):
        k = pl.program_id(1)

        @pl.when(k == 0)
        def _():
            accw_ref[...] = jnp.zeros_like(accw_ref)
            acco_ref[...] = jnp.zeros_like(acco_ref)

        hv, dys = h_ref[...], dys_ref[...]
        for o, n in _chunks(FH, BWD_W_CHUNK):
            g = gg_ref[:, o:o + n].astype(F32)
            act = (g * _sigmoid(g) * gu_ref[:, o:o + n].astype(F32)).astype(BF16)
            accw_ref[0, o:o + n, :] += _dot_tn(dg_ref[:, o:o + n], hv)
            accw_ref[1, o:o + n, :] += _dot_tn(du_ref[:, o:o + n], hv)
            acco_ref[o:o + n, :] += _dot_tn(act, dys)

        @pl.when(k == nk - 1)
        def _():
            dw_ref[...] = accw_ref[...].astype(BF16)
            dwo_ref[...] = acco_ref[...].astype(BF16)

    tile = pl.BlockSpec((tk, D), lambda j, k: (k, 0))
    gate = pl.BlockSpec((tk, FH), lambda j, k: (k, j))
    up = pl.BlockSpec((tk, FH), lambda j, k: (k, j + 2))
    return _pcall(
        body, (h, dys, gu, gu, dgu, dgu), grid=(2, nk),
        in_specs=[tile, tile, gate, up, gate, up],
        out_specs=[pl.BlockSpec((2, FH, D), lambda j, k: (0, j, 0), pipeline_mode=pl.Buffered(1)),
                   pl.BlockSpec((FH, D), lambda j, k: (j, 0), pipeline_mode=pl.Buffered(1))],
        out_shape=[_sds((2, F, D), BF16), _sds((F, D), BF16)],
        scratch_shapes=[pltpu.VMEM((2, FH, D), F32), pltpu.VMEM((FH, D), F32)],
        sem=("parallel", "arbitrary"), name="ffn_bwd_w", comm=comm)


def _mixin_fwd(x, gain, wext, tm, comm=None):
    S, D = x.shape
    PW = wext.shape[0]

    def body(x_ref, g_ref, w_ref, p_ref):
        xh, _ = _rms(x_ref[...])
        p_ref[...] = _dot_nt((xh * g_ref[...]).astype(BF16), w_ref[...]).astype(BF16)

    return _pcall(
        body, (x, gain, wext), grid=(S // tm,),
        in_specs=[pl.BlockSpec((tm, D), lambda i: (i, 0)), _resident((1, D)), _resident((PW, D))],
        out_specs=[pl.BlockSpec((tm, PW), lambda i: (i, 0))],
        out_shape=[_sds((S, PW), BF16)],
        sem=("parallel",), name="mixin_fwd", comm=comm)


def _mixin_bwd(x, gain, dxo, dq, dkv, dag, wext, tm, comm=None):
    S, D = x.shape
    PW = wext.shape[0]
    QW = dq.shape[1]
    o1, o2 = QW, QW + 2 * KV_DUP

    def body(x_ref, g_ref, dxo_ref, dq_ref, dkv_ref, dag_ref, w_ref, dx_ref, dgain_ref, dw_ref):
        i = pl.program_id(0)
        xh, r = _rms(x_ref[...])
        h = (xh * g_ref[...]).astype(BF16)
        dqv, dkvv, dagv = dq_ref[...], dkv_ref[...], dag_ref[...]
        dh = _dot(dqv, w_ref[0:o1, :]) + _dot(dkvv, w_ref[o1:o2, :]) + _dot(dagv, w_ref[o2:PW, :])
        dxn, dgn = _rms_bwd(dh, xh, r, g_ref[...])
        dx_ref[...] = dxo_ref[...] + dxn

        @pl.when(i == 0)
        def _():
            dgain_ref[...] = dgn
            dw_ref[0:o1, :] = _dot_tn(dqv, h)
            dw_ref[o1:o2, :] = _dot_tn(dkvv, h)
            dw_ref[o2:PW, :] = _dot_tn(dagv, h)

        @pl.when(i > 0)
        def _():
            dgain_ref[...] += dgn
            dw_ref[0:o1, :] += _dot_tn(dqv, h)
            dw_ref[o1:o2, :] += _dot_tn(dkvv, h)
            dw_ref[o2:PW, :] += _dot_tn(dagv, h)

    return _pcall(
        body, (x, gain, dxo, dq, dkv, dag, wext), grid=(S // tm,),
        in_specs=[pl.BlockSpec((tm, D), lambda i: (i, 0)), _resident((1, D)),
                  pl.BlockSpec((tm, D), lambda i: (i, 0)),
                  pl.BlockSpec((tm, QW), lambda i: (i, 0)),
                  pl.BlockSpec((tm, 2 * KV_DUP), lambda i: (i, 0)),
                  pl.BlockSpec((tm, PW - o2), lambda i: (i, 0)),
                  _resident((PW, D))],
        out_specs=[pl.BlockSpec((tm, D), lambda i: (i, 0)),
                   pl.BlockSpec((1, D), lambda i: (0, 0)),
                   pl.BlockSpec((PW, D), lambda i: (0, 0))],
        out_shape=[_sds((S, D), F32), _sds((1, D), F32), _sds((PW, D), F32)],
        sem=("arbitrary",), name="mixin_bwd", comm=comm)


def _attn_bias(n_heads):
    group = n_heads // N_KV_HEADS
    rows = group * WINDOW
    r = jnp.arange(rows)[:, None]
    s = jnp.arange(2 * WINDOW)[None, :]
    dist = (r % WINDOW) + WINDOW - s
    window = (dist >= 0) & (dist < WINDOW)
    out = []
    for first in (True, False):
        valid = window & (s >= WINDOW) if first else window
        tiles = []
        for kh in range(N_KV_HEADS):
            slope = jnp.asarray([2.0 ** (-8.0 * (kh * group + i + 1) / n_heads) for i in range(group)], F32)
            bias = -slope[r // WINDOW] * dist.astype(F32)
            tiles.append(jnp.where(valid, bias, NEG_INF))
        out.append(jnp.stack(tiles))
    return jnp.stack(out)


def _sink_column(group, sinks):
    rows = group * WINDOW
    seg = lax.shift_right_logical(lax.broadcasted_iota(jnp.int32, (rows, 1), 0), WINDOW.bit_length() - 1)
    sink = jnp.zeros((rows, 1), F32)
    for i in range(group):
        sink = jnp.where(seg == i, sinks[i], sink)
    return sink


def _lane_halves():
    lo = lax.broadcasted_iota(jnp.int32, (WINDOW, LANES), 1) < HEAD_DIM
    return lo, [jnp.where(lo, 1.0, 0.0).astype(BF16), jnp.where(lo, 0.0, 1.0).astype(BF16)]


def _stack_heads(ref, first_tile, n_tiles, halves, row0=0):
    parts = []
    for t in range(first_tile, first_tile + n_tiles):
        tile = ref[row0:row0 + WINDOW, LANES * t:LANES * (t + 1)]
        parts += [tile * halves[0], tile * halves[1]]
    return jnp.concatenate(parts, axis=0)


def _unstack_heads(ref, first_tile, n_tiles, lo, stacked, row0=0):
    for i in range(n_tiles):
        a = stacked[2 * i * WINDOW:(2 * i + 1) * WINDOW]
        b = stacked[(2 * i + 1) * WINDOW:(2 * i + 2) * WINDOW]
        t = first_tile + i
        ref[row0:row0 + WINDOW, LANES * t:LANES * (t + 1)] = jnp.where(lo, a, b).astype(ref.dtype)


def _attn_probs(qs, kds, bias, sink):
    return _softmax_sink(_dot_nt(qs, kds) + bias, sink)


def _softmax_sink(sc, sink):
    m = jnp.maximum(jnp.max(sc, axis=-1, keepdims=True), sink)
    p = jnp.exp(sc - m)
    es = jnp.exp(sink - m)
    inv = 1.0 / (jnp.sum(p, axis=-1, keepdims=True) + es)
    return p * inv, es * inv


def _attn_specs(n_heads, nb):
    QW = n_heads * HEAD_DIM
    kblk, vblk = QW // KV_DUP, QW // KV_DUP + 1
    last = nb - 1
    cur = lambda b: (lambda n: (jnp.minimum(n, last), b))
    prev = lambda b: (lambda n: (jnp.clip(n - 1, 0, last), b))
    kv = [pl.BlockSpec((WINDOW, KV_DUP), cur(kblk)), pl.BlockSpec((WINDOW, KV_DUP), prev(kblk)),
          pl.BlockSpec((WINDOW, KV_DUP), cur(vblk)), pl.BlockSpec((WINDOW, KV_DUP), prev(vblk))]
    rows = n_heads // N_KV_HEADS * WINDOW
    bias = pl.BlockSpec((None, N_KV_HEADS, rows, 2 * WINDOW), lambda n: (jnp.minimum(n, 1), 0, 0, 0))
    return QW, cur, prev, kv, bias


def _attn_fwd(proj, sinks, bias, n_heads, comm=None):
    S = proj.shape[0]
    nb = S // WINDOW
    group = n_heads // N_KV_HEADS
    scale = 1.0 / math.sqrt(HEAD_DIM)
    QW = n_heads * HEAD_DIM
    kblk, vblk = QW // KV_DUP, QW // KV_DUP + 1

    def body(sink_ref, bias_ref, q_ref, kc_ref, kp_ref, vc_ref, vp_ref, o_ref):
        n = pl.program_id(0)
        lo, halves = _lane_halves()
        tiles = group // 2
        chains = [(j, kh) for j in range(2) for kh in range(N_KV_HEADS)]
        sink = [_sink_column(group, [sink_ref[h] for h in range(kh * group, (kh + 1) * group)]) for kh in range(N_KV_HEADS)]

        def keys(cur_ref, prev_ref, j, kh):
            lanes = slice(LANES * kh, LANES * (kh + 1))
            if j == 1:
                return cur_ref[:, lanes]
            return jnp.concatenate([prev_ref[:, lanes], cur_ref[0:WINDOW, lanes]], axis=0)

        kds = [keys(kc_ref, kp_ref, j, kh) * scale for j, kh in chains]
        vd = [keys(vc_ref, vp_ref, j, kh) for j, kh in chains]
        qs = [_stack_heads(q_ref, kh * tiles, tiles, halves, j * WINDOW) for j, kh in chains]
        bias_of = lambda j, kh: bias_ref[jnp.minimum(n, 1), kh] if j == 0 else bias_ref[1, kh]
        sc = [_dot_nt(qs[c], kds[c]) + bias_of(j, kh) for c, (j, kh) in enumerate(chains)]
        pn = [_softmax_sink(sc[c], sink[kh])[0] for c, (j, kh) in enumerate(chains)]
        out = [_dot(pn[c].astype(BF16), vd[c]) for c in range(len(chains))]
        for c, (j, kh) in enumerate(chains):
            _unstack_heads(o_ref, kh * tiles, tiles, lo, out[c], j * WINDOW)

    pair = lambda b: pl.BlockSpec((2 * WINDOW, KV_DUP), lambda n: (n, b))
    before = lambda b: pl.BlockSpec((WINDOW, KV_DUP), lambda n: (jnp.maximum(2 * n - 1, 0), b))
    return _pcall(
        body, (sinks, bias, proj, proj, proj, proj, proj), grid=(nb // 2,),
        in_specs=[pl.BlockSpec(memory_space=pltpu.SMEM), _resident(bias.shape),
                  pl.BlockSpec((2 * WINDOW, QW), lambda n: (n, 0)), pair(kblk), before(kblk), pair(vblk), before(vblk)],
        out_specs=[pl.BlockSpec((2 * WINDOW, QW), lambda n: (n, 0))],
        out_shape=[_sds((S, QW), BF16)],
        sem=("parallel",), name="attn_fwd", comm=comm)


def _attn_bwd(proj, dmix, sinks, bias, n_heads, comm=None):
    S = proj.shape[0]
    nb = S // WINDOW
    group = n_heads // N_KV_HEADS
    scale = 1.0 / math.sqrt(HEAD_DIM)
    QW, cur, prev, kv_specs, bias_spec = _attn_specs(n_heads, nb)

    def body(sink_ref, bias_ref, q_ref, kc_ref, kp_ref, vc_ref, vp_ref, do_ref, dq_ref, dkv_ref, dsink_ref, carry_ref):
        n = pl.program_id(0)

        @pl.when(n == 0)
        def _():
            carry_ref[...] = jnp.zeros_like(carry_ref)
            dsink_ref[...] = jnp.zeros_like(dsink_ref)

        @pl.when(n < nb)
        def _():
            lo, halves = _lane_halves()
            lane1 = lax.broadcasted_iota(jnp.int32, (1, LANES), 1)
            kv = range(N_KV_HEADS)
            tiles = group // 2
            sink = [_sink_column(group, [sink_ref[h] for h in range(kh * group, (kh + 1) * group)]) for kh in kv]
            ksl = [slice(LANES * kh, LANES * (kh + 1)) for kh in kv]
            kds = [jnp.concatenate([kp_ref[:, ksl[kh]], kc_ref[:, ksl[kh]]], axis=0) * scale for kh in kv]
            vd = [jnp.concatenate([vp_ref[:, ksl[kh]], vc_ref[:, ksl[kh]]], axis=0) for kh in kv]
            qs = [_stack_heads(q_ref, kh * tiles, tiles, halves) for kh in kv]
            dos = [_stack_heads(do_ref, kh * tiles, tiles, halves) for kh in kv]
            sc = [_dot_nt(qs[kh], kds[kh]) + bias_ref[kh] for kh in kv]
            dp = [_dot_nt(dos[kh], vd[kh]) for kh in kv]
            probs = [_softmax_sink(sc[kh], sink[kh]) for kh in kv]
            pn = [p[0] for p in probs]
            delta = [jnp.sum(pn[kh] * dp[kh], axis=-1, keepdims=True) for kh in kv]
            dsb = [(pn[kh] * (dp[kh] - delta[kh])).astype(BF16) for kh in kv]
            dqs = [_dot(dsb[kh], kds[kh]) for kh in kv]
            dkd = [_dot_tn(dsb[kh], qs[kh]) * scale for kh in kv]
            dvd = [_dot_tn(pn[kh].astype(BF16), dos[kh]) for kh in kv]
            dsink = jnp.zeros((1, LANES), F32)
            for kh in kv:
                sd = probs[kh][1] * delta[kh]
                for i in range(group):
                    dsink = dsink - jnp.where(lane1 == kh * group + i, jnp.sum(sd[i * WINDOW:(i + 1) * WINDOW]), 0.0)
                _unstack_heads(dq_ref, kh * tiles, tiles, lo, dqs[kh])
            dsink_ref[...] += dsink
            both = jnp.concatenate(dkd + dvd, axis=1)
            dkv_ref[...] = (carry_ref[...] + both[0:WINDOW]).astype(BF16)
            carry_ref[...] = both[WINDOW:2 * WINDOW]

        @pl.when(n == nb)
        def _():
            dkv_ref[...] = carry_ref[...].astype(BF16)

    return _pcall(
        body, (sinks, bias, proj, proj, proj, proj, proj, dmix), grid=(nb + 1,),
        in_specs=[pl.BlockSpec(memory_space=pltpu.SMEM), bias_spec, pl.BlockSpec((WINDOW, QW), cur(0))] + kv_specs
                 + [pl.BlockSpec((WINDOW, QW), cur(0))],
        out_specs=[pl.BlockSpec((WINDOW, QW), cur(0)),
                   pl.BlockSpec((WINDOW, 2 * KV_DUP), prev(0)),
                   pl.BlockSpec((1, LANES), lambda n: (0, 0))],
        out_shape=[_sds((S, QW), BF16), _sds((S, 2 * KV_DUP), BF16), _sds((1, LANES), F32)],
        scratch_shapes=[pltpu.VMEM((WINDOW, 2 * KV_DUP), F32)],
        sem=("arbitrary",), name="attn_bwd", comm=comm)


def _glu_window(a_ref, g_ref, ap_ref, gp_ref, win_ref, first):
    tm = a_ref.shape[0]
    zp = ap_ref[...].astype(F32) * _sigmoid(gp_ref[...].astype(F32))
    win_ref[0:HALO, :] = jnp.where(first, jnp.zeros_like(zp), zp)
    win_ref[HALO:HALO + tm, :] = a_ref[...].astype(F32) * _sigmoid(g_ref[...].astype(F32))


def _preshift(win_ref, sh_ref):
    n = win_ref.shape[0] - SUBLANES
    for s in range(1, SUBLANES):
        sh_ref[s - 1, 0:n, :] = win_ref[s:s + n, :]


def _window(win_ref, sh_ref, start):
    s = start % SUBLANES
    if s == 0:
        return win_ref[start:start + ROWS, :]
    return sh_ref[s - 1, start - s:start - s + ROWS, :]


def _conv_fwd(proj, wdw, bdw, lng, lnb, n_heads, tm, comm=None):
    S = proj.shape[0]
    taps, C = wdw.shape
    ablk = (n_heads * HEAD_DIM + 2 * KV_DUP) // C
    hb = tm // HALO
    off = HALO - (taps - 1)

    def body(a_ref, g_ref, ap_ref, gp_ref, w_ref, b_ref, lg_ref, lb_ref, o_ref, y_ref, win_ref, sh_ref):
        _glu_window(a_ref, g_ref, ap_ref, gp_ref, win_ref, pl.program_id(0) == 0)
        _preshift(win_ref, sh_ref)
        for c in range(tm // ROWS):
            r0 = c * ROWS
            acc = jnp.zeros((ROWS, C), F32) + b_ref[...]
            for k in range(taps):
                acc = acc + w_ref[k:k + 1, :] * _window(win_ref, sh_ref, r0 + off + k)
            y_ref[r0:r0 + ROWS, :] = acc
        y = y_ref[...]
        mu = jnp.mean(y, axis=-1, keepdims=True)
        yc = y - mu
        yn = yc * lax.rsqrt(jnp.mean(yc * yc, axis=-1, keepdims=True) + RMS_EPS) * lg_ref[...] + lb_ref[...]
        o_ref[...] = (yn * _sigmoid(yn)).astype(BF16)

    vec = pl.BlockSpec((1, C), lambda i: (0, 0))
    halo = lambda b: pl.BlockSpec((HALO, C), lambda i: (jnp.maximum(i * hb - 1, 0), b))
    return _pcall(
        body, (proj, proj, proj, proj, wdw, bdw, lng, lnb,), grid=(S // tm,),
        in_specs=[pl.BlockSpec((tm, C), lambda i: (i, ablk)), pl.BlockSpec((tm, C), lambda i: (i, ablk + 1)),
                  halo(ablk), halo(ablk + 1),
                  pl.BlockSpec((taps, C), lambda i: (0, 0)), vec, vec, vec],
        out_specs=[pl.BlockSpec((tm, C), lambda i: (i, 0)), pl.BlockSpec((tm, C), lambda i: (i, 0))],
        out_shape=[_sds((S, C), BF16), _sds((S, C), F32)],
        scratch_shapes=[pltpu.VMEM((tm + HALO, C), F32), pltpu.VMEM((SUBLANES - 1, tm + HALO, C), F32)],
        sem=("parallel",), name="conv_fwd", comm=comm)


def _conv_bwd(proj, dmix, ysave, wdw, lng, lnb, n_heads, tm, comm=None):
    S = proj.shape[0]
    taps, C = wdw.shape
    QW = n_heads * HEAD_DIM
    ablk = (QW + 2 * KV_DUP) // C
    cblk = QW // C
    hb = tm // HALO
    nt = S // tm
    off = HALO - (taps - 1)

    def ln_bwd(dc, y, lg, lb):
        mu = jnp.mean(y, axis=-1, keepdims=True)
        yc = y - mu
        r = lax.rsqrt(jnp.mean(yc * yc, axis=-1, keepdims=True) + RMS_EPS)
        yh = yc * r
        yn = yh * lg + lb
        sg = _sigmoid(yn)
        dyn = dc * (sg * (1.0 + yn * (1.0 - sg)))
        dyh = dyn * lg
        dy = r * (dyh - jnp.mean(dyh, axis=-1, keepdims=True) - yh * jnp.mean(dyh * yh, axis=-1, keepdims=True))
        return dy, dyn, yh

    def body(dc_ref, dcn_ref, y_ref, yn_ref, a_ref, g_ref, ap_ref, gp_ref, w_ref, lg_ref, lb_ref,
             dag_ref, dw_ref, dvec_ref, zwin_ref, dyw_ref, dwacc_ref, zsh_ref, dysh_ref):
        i = pl.program_id(0)

        @pl.when(i == 0)
        def _():
            dwacc_ref[...] = jnp.zeros_like(dwacc_ref)
            dvec_ref[...] = jnp.zeros_like(dvec_ref)

        lg, lb = lg_ref[...], lb_ref[...]
        dy, dyn, yh = ln_bwd(dc_ref[...].astype(F32), y_ref[...], lg, lb)
        dy_next, _, _ = ln_bwd(dcn_ref[...].astype(F32), yn_ref[...], lg, lb)
        dyw_ref[0:tm, :] = dy
        dyw_ref[tm:tm + HALO, :] = jnp.where(i == nt - 1, jnp.zeros_like(dy_next), dy_next)
        dvec_ref[0:1, :] += jnp.sum(dy, axis=0, keepdims=True)
        dvec_ref[1:2, :] += jnp.sum(dyn * yh, axis=0, keepdims=True)
        dvec_ref[2:3, :] += jnp.sum(dyn, axis=0, keepdims=True)
        _glu_window(a_ref, g_ref, ap_ref, gp_ref, zwin_ref, i == 0)
        _preshift(zwin_ref, zsh_ref)
        _preshift(dyw_ref, dysh_ref)

        for c in range(tm // ROWS):
            r0 = c * ROWS
            dz = jnp.zeros((ROWS, C), F32)
            dyc = dyw_ref[r0:r0 + ROWS, :]
            for k in range(taps):
                dz = dz + w_ref[k:k + 1, :] * _window(dyw_ref, dysh_ref, r0 + taps - 1 - k)
                prod = dyc * _window(zwin_ref, zsh_ref, r0 + off + k)
                dwacc_ref[k] += jnp.sum(prod.reshape(ROWS // SUBLANES, SUBLANES, C), axis=0)
            a = a_ref[r0:r0 + ROWS, :].astype(F32)
            s = _sigmoid(g_ref[r0:r0 + ROWS, :].astype(F32))
            dag_ref[r0:r0 + ROWS, 0:C] = (dz * s).astype(BF16)
            dag_ref[r0:r0 + ROWS, C:2 * C] = (dz * a * s * (1.0 - s)).astype(BF16)

        @pl.when(i == nt - 1)
        def _():
            dw_ref[...] = jnp.zeros_like(dw_ref)
            for k in range(taps):
                dw_ref[k:k + 1, :] = jnp.sum(dwacc_ref[k], axis=0, keepdims=True)

    vec = pl.BlockSpec((1, C), lambda i: (0, 0))
    tile = lambda b: pl.BlockSpec((tm, C), lambda i: (i, b))
    prev = lambda b: pl.BlockSpec((HALO, C), lambda i: (jnp.maximum(i * hb - 1, 0), b))
    nxt = lambda b: pl.BlockSpec((HALO, C), lambda i: (jnp.minimum((i + 1) * hb, S // HALO - 1), b))
    return _pcall(
        body, (dmix, dmix, ysave, ysave, proj, proj, proj, proj, wdw, lng, lnb,), grid=(nt,),
        in_specs=[tile(cblk), nxt(cblk), tile(0), nxt(0), tile(ablk), tile(ablk + 1), prev(ablk), prev(ablk + 1),
                  pl.BlockSpec((taps, C), lambda i: (0, 0)), vec, vec],
        out_specs=[pl.BlockSpec((tm, 2 * C), lambda i: (i, 0)),
                   pl.BlockSpec((HALO, C), lambda i: (0, 0)),
                   pl.BlockSpec((8, C), lambda i: (0, 0))],
        out_shape=[_sds((S, 2 * C), BF16), _sds((HALO, C), F32), _sds((8, C), F32)],
        scratch_shapes=[pltpu.VMEM((tm + HALO, C), F32), pltpu.VMEM((tm + HALO, C), F32),
                        pltpu.VMEM((taps, SUBLANES, C), F32),
                        pltpu.VMEM((SUBLANES - 1, tm + HALO, C), F32), pltpu.VMEM((SUBLANES - 1, tm + HALO, C), F32)],
        sem=("arbitrary",), name="conv_bwd", comm=comm)


def _mixout_fwd(x, attn, conv, wo, tm, comm=None):
    S, D = x.shape
    QW, C = attn.shape[1], conv.shape[1]

    def body(x_ref, a_ref, c_ref, w_ref, o_ref):
        o_ref[...] = x_ref[...] + _dot(a_ref[...], w_ref[0:QW, :]) + _dot(c_ref[...], w_ref[QW:QW + C, :])

    return _pcall(
        body, (x, attn, conv, wo,), grid=(S // tm,),
        in_specs=[pl.BlockSpec((tm, D), lambda i: (i, 0)),
                  pl.BlockSpec((tm, QW), lambda i: (i, 0)),
                  pl.BlockSpec((tm, C), lambda i: (i, 0)),
                  pl.BlockSpec((QW + C, D), lambda i: (0, 0))],
        out_specs=[pl.BlockSpec((tm, D), lambda i: (i, 0))],
        out_shape=[_sds((S, D), F32)],
        sem=("parallel",), name="mixout_fwd", comm=comm)


def _mixout_bwd(dxo, attn, conv, wo, tm, comm=None):
    S, D = dxo.shape
    QW, C = attn.shape[1], conv.shape[1]
    nt = S // tm

    def body(dx_ref, a_ref, c_ref, w_ref, dm_ref, dw_ref, acc_ref):
        i = pl.program_id(0)
        dxb = dx_ref[...].astype(BF16)
        dm_ref[...] = _dot_nt(dxb, w_ref[...]).astype(BF16)

        @pl.when(i == 0)
        def _():
            acc_ref[...] = jnp.zeros_like(acc_ref)

        acc_ref[0:QW, :] += _dot_tn(a_ref[...], dxb)
        acc_ref[QW:QW + C, :] += _dot_tn(c_ref[...], dxb)

        @pl.when(i == nt - 1)
        def _():
            dw_ref[...] = acc_ref[...].astype(BF16)

    return _pcall(
        body, (dxo, attn, conv, wo,), grid=(nt,),
        in_specs=[pl.BlockSpec((tm, D), lambda i: (i, 0)),
                  pl.BlockSpec((tm, QW), lambda i: (i, 0)),
                  pl.BlockSpec((tm, C), lambda i: (i, 0)),
                  pl.BlockSpec((QW + C, D), lambda i: (0, 0))],
        out_specs=[pl.BlockSpec((tm, QW + C), lambda i: (i, 0)),
                   pl.BlockSpec((QW + C, D), lambda i: (0, 0))],
        out_shape=[_sds((S, QW + C), BF16), _sds((QW + C, D), BF16)],
        scratch_shapes=[pltpu.VMEM((QW + C, D), F32)],
        sem=("arbitrary",), name="mixout_bwd", comm=comm)


def _loss_head(x, gain, target, tm, comm=None):
    S, D = x.shape
    nt = S // tm

    def body(x_ref, g_ref, t_ref, dx_ref, loss_ref, dgain_ref):
        i = pl.program_id(0)
        xh, r = _rms(x_ref[...])
        e = xh * g_ref[...] - t_ref[...]
        loss_ref[...] = jnp.zeros((1, LANES), F32) + 0.5 * jnp.sum(jnp.mean(e * e, axis=-1, keepdims=True))
        dxn, dgn = _rms_bwd(e * (1.0 / D), xh, r, g_ref[...])
        dx_ref[...] = dxn

        @pl.when(i == 0)
        def _():
            dgain_ref[...] = dgn

        @pl.when(i > 0)
        def _():
            dgain_ref[...] += dgn

    return _pcall(
        body, (x, gain, target,), grid=(nt,),
        in_specs=[pl.BlockSpec((tm, D), lambda i: (i, 0)),
                  pl.BlockSpec((1, D), lambda i: (0, 0)),
                  pl.BlockSpec((tm, D), lambda i: (i, 0))],
        out_specs=[pl.BlockSpec((tm, D), lambda i: (i, 0)),
                   pl.BlockSpec((None, 1, LANES), lambda i: (i, 0, 0)),
                   pl.BlockSpec((1, D), lambda i: (0, 0))],
        out_shape=[_sds((S, D), F32), _sds((nt, 1, LANES), F32), _sds((1, D), F32)],
        sem=("arbitrary",), name="loss_head", comm=comm)


def _adam(w, m, v, parts, name, comm=None):
    L, R, C = w.shape
    P = parts[0].shape[0]
    br = _row_block(R, 256)
    c1 = 1.0 - ADAM_B1 ** ADAM_STEP
    c2 = 1.0 - ADAM_B2 ** ADAM_STEP

    def body(w_ref, m_ref, v_ref, *rest):
        p_refs, (g_ref, d_ref, mo_ref, vo_ref) = rest[:L], rest[L:]
        layer = pl.program_id(0)

        def update(p_ref):
            g = p_ref[0].astype(F32)
            for k in range(1, P):
                g = g + p_ref[k].astype(F32)
            mn = ADAM_B1 * m_ref[...] + (1.0 - ADAM_B1) * g
            vn = ADAM_B2 * v_ref[...] + (1.0 - ADAM_B2) * (g * g)
            g_ref[...] = g
            mo_ref[...] = mn
            vo_ref[...] = vn
            d_ref[...] = -ADAM_LR * ((mn / c1) / (jnp.sqrt(vn / c2) + ADAM_EPS) + ADAM_WD * w_ref[...])

        for k in range(L):
            pl.when(layer == k)(functools.partial(update, p_refs[k]))

    blk = pl.BlockSpec((None, br, C), lambda l, i: (l, i, 0))
    part = lambda k: pl.BlockSpec((P, br, C), lambda l, i: (0, jnp.where(l == k, i, 0), 0))
    return _pcall(
        body, (w, m, v, *parts), grid=(L, R // br),
        in_specs=[blk, blk, blk] + [part(k) for k in range(L)],
        out_specs=[blk, blk, blk, blk],
        out_shape=[_sds((L, R, C), F32)] * 4,
        sem=("parallel", "parallel"), name=name, comm=comm)


def _sum_parts(parts):
    P, R, C = parts.shape

    def body(p_ref, o_ref):
        g = p_ref[0]
        for k in range(1, P):
            g = g + p_ref[k]
        o_ref[...] = g

    vmem = pl.BlockSpec(memory_space=pltpu.VMEM)
    return _pcall(body, (parts,), in_specs=[vmem], out_specs=[vmem], out_shape=[_sds((R, C), F32)],
                  name="sum_parts")[0]


def _place():
    x, y, c = lax.axis_index("x"), lax.axis_index("y"), lax.axis_index("c")
    return x, y, c, [(1 - x, y), (x, 1 - y), (1 - x, 1 - y)]


def _dev(px, py, pc):
    return 4 * px + 2 * py + pc


def _gather_comm(shards, fulls, slot_of):
    n = len(shards)

    def copies(srcs, outs, send_sems, recv_sems):
        x, y, c, chips = _place()

        def copy(a, k, block, to, from_shard=False):
            dst = slot_of[a](outs[a], _dev(*block))
            return pltpu.make_async_remote_copy(
                src_ref=srcs[a] if from_shard else dst, dst_ref=dst,
                send_sem=send_sems.at[a, k], recv_sem=recv_sems.at[a, k], device_id=to, device_id_type=MESH)

        return copy, (x, y, c), (x, y, 1 - c), chips

    def local(srcs, outs, local_sems):
        x, y, c, _ = _place()
        return [pltpu.make_async_copy(srcs[a], slot_of[a](outs[a], _dev(x, y, c)), local_sems.at[a])
                for a in range(n)]

    def first_copies(copy, me, sibling, chips):
        out = []
        for a in range(n):
            out.append(copy(a, 0, me, sibling, True))
            out += [copy(a, 1 + j, me, (*chip, me[2]), True) for j, chip in enumerate(chips)]
        return out

    def start(srcs, outs, sems):
        send_sems, recv_sems, local_sems = sems
        copy, me, sibling, chips = copies(srcs, outs, send_sems, recv_sems)
        for cp in local(srcs, outs, local_sems):
            cp.start()
        for cp in first_copies(copy, me, sibling, chips):
            cp.start()

    def forwards(copy, me, sibling, chips):
        return [copy(a, 4 + j, (*chip, me[2]), sibling) for j, chip in enumerate(chips) for a in range(n)]

    def mid(srcs, outs, sems):
        send_sems, recv_sems, local_sems = sems
        copy, me, sibling, chips = copies(srcs, outs, send_sems, recv_sems)
        for j, chip in enumerate(chips):
            for a in range(n):
                copy(a, 1 + j, (*chip, me[2]), me).wait_recv()
                copy(a, 4 + j, (*chip, me[2]), sibling).start()

    def finish(srcs, outs, sems):
        send_sems, recv_sems, local_sems = sems
        copy, me, sibling, chips = copies(srcs, outs, send_sems, recv_sems)
        c = me[2]
        for a in range(n):
            copy(a, 0, sibling, me).wait_recv()
            for j, chip in enumerate(chips):
                copy(a, 4 + j, (*chip, 1 - c), me).wait_recv()
        for cp in first_copies(copy, me, sibling, chips) + forwards(copy, me, sibling, chips):
            cp.wait_send()
        for cp in local(srcs, outs, local_sems):
            cp.wait()

    sems = [pltpu.SemaphoreType.DMA((n, 7)), pltpu.SemaphoreType.DMA((n, 7)), pltpu.SemaphoreType.DMA((n,))]
    return _Comm(shards, fulls, sems, start, finish, mid)


def _swap_comm(grads):
    n = len(grads)

    def copies(srcs, outs, sems):
        x, y, c, _ = _place()
        return [pltpu.make_async_remote_copy(
            src_ref=srcs[a].at[:, pl.ds(1 - c, 1)], dst_ref=outs[a],
            send_sem=sems[0].at[a], recv_sem=sems[1].at[a], device_id=(x, y, 1 - c), device_id_type=MESH)
            for a in range(n)]

    def start(srcs, outs, sems):
        for cp in copies(srcs, outs, sems):
            cp.start()

    def finish(srcs, outs, sems):
        for cp in copies(srcs, outs, sems):
            cp.wait()

    return _Comm(grads, [_sds((N_CHIP, 1) + g.shape[2:], g.dtype) for g in grads],
                 [pltpu.SemaphoreType.DMA((n,)), pltpu.SemaphoreType.DMA((n,))], start, finish)


def _exchange_comm(parts):
    n = len(parts)

    def copies(srcs, outs, sems):
        x, y, c, chips = _place()
        mine = 2 * x + y
        loc = [pltpu.make_async_copy(srcs[a].at[pl.ds(mine, 1)], outs[a].at[pl.ds(mine, 1)], sems[2].at[a])
               for a in range(n)]
        rem = [pltpu.make_async_remote_copy(
            src_ref=srcs[a].at[pl.ds(2 * px + py, 1)], dst_ref=outs[a].at[pl.ds(mine, 1)],
            send_sem=sems[0].at[a, j], recv_sem=sems[1].at[a, j], device_id=(px, py, c), device_id_type=MESH)
            for a in range(n) for j, (px, py) in enumerate(chips)]
        return loc + rem

    def start(srcs, outs, sems):
        for cp in copies(srcs, outs, sems):
            cp.start()

    def finish(srcs, outs, sems):
        for cp in copies(srcs, outs, sems):
            cp.wait()

    return _Comm(parts, [_sds(p.shape, p.dtype) for p in parts],
                 [pltpu.SemaphoreType.DMA((n, 3)), pltpu.SemaphoreType.DMA((n, 3)), pltpu.SemaphoreType.DMA((n,))],
                 start, finish)


FWD_KERNELS = [("ffn1", ("a1", "b1"), 26), ("mixin", ("wi",), 7), ("attn", (), 7), ("conv", (), 15), ("mixout", ("wo",), 6),
               ("ffn2", ("a2", "b2"), 26)]
GATHER_SHARE = 70
FIRST_GATHERED = ("a1", "b1", "wi", "wo")


def _plan_gathers(n_layers, shard_bytes):
    per_layer = sum(shard_bytes.values())
    cost = {n: GATHER_SHARE * b / per_layer for n, b in shard_bytes.items()}
    room = {len(FWD_KERNELS) * l + i: k[2] for l in range(n_layers) for i, k in enumerate(FWD_KERNELS)}
    first, plan = [], {}
    for l in range(n_layers):
        for i, (_, needs, _) in enumerate(FWD_KERNELS):
            due = len(FWD_KERNELS) * l + i
            for name in needs:
                if l == 0 and name in FIRST_GATHERED:
                    first.append((name, l))
                    continue
                fits = [k for k in range(due) if room[k] >= cost[name]]
                k = fits[0] if fits else max(range(due), key=lambda k: room[k])
                room[k] -= cost[name]
                plan.setdefault(k, []).append((name, l))
    return first, plan


def _comm_only(comm, name):
    return _pcall(lambda: None, (), in_specs=[], out_specs=[], out_shape=[], name=name, comm=comm)


def _gather_small(v):
    R, C = v.shape

    def body(x_ref, out_ref, send_sems, recv_sems, local_sem):
        x, y, c, chips = _place()
        me, sibling = (x, y, c), (x, y, 1 - c)

        def copy(k, block, to, from_shard=False):
            dst = out_ref.at[_dev(*block)]
            return pltpu.make_async_remote_copy(
                src_ref=x_ref if from_shard else dst, dst_ref=dst,
                send_sem=send_sems.at[k], recv_sem=recv_sems.at[k], device_id=to, device_id_type=MESH)

        mine = pltpu.make_async_copy(x_ref, out_ref.at[_dev(*me)], local_sem)
        mine.start()
        first = [copy(0, me, sibling, True)] + [copy(1 + j, me, (*chip, c), True) for j, chip in enumerate(chips)]
        for cp in first:
            cp.start()
        passed = [copy(4 + j, (*chip, c), sibling) for j, chip in enumerate(chips)]
        for j, chip in enumerate(chips):
            copy(1 + j, (*chip, c), me).wait_recv()
            passed[j].start()
        copy(0, sibling, me).wait_recv()
        for j, chip in enumerate(chips):
            copy(4 + j, (*chip, 1 - c), me).wait_recv()
        for cp in first + passed:
            cp.wait_send()
        mine.wait()

    vmem = pl.BlockSpec(memory_space=pltpu.VMEM)
    return _pcall(
        body, (v,), in_specs=[vmem], out_specs=[vmem], out_shape=[_sds((N_DEV, R, C), F32)],
        scratch_shapes=[pltpu.SemaphoreType.DMA((7,)), pltpu.SemaphoreType.DMA((7,)), pltpu.SemaphoreType.DMA],
        name="gather_small")[0]


def _add_sibling(core, g, r):
    _, _, R, C = g.shape
    br = _row_block(R)

    def body(c_ref, g_ref, r_ref, o_ref):
        o_ref[...] = (g_ref[...].astype(F32) + r_ref[...].astype(F32)).astype(BF16)

    return _pcall(
        body, (core, g, r),
        grid_spec=pltpu.PrefetchScalarGridSpec(
            num_scalar_prefetch=1, grid=(N_CHIP, R // br),
            in_specs=[pl.BlockSpec((None, None, br, C), lambda k, i, c_ref: (k, c_ref[0], i, 0)),
                      pl.BlockSpec((None, None, br, C), lambda k, i, c_ref: (k, 0, i, 0))],
            out_specs=pl.BlockSpec((None, br, C), lambda k, i, c_ref: (k, i, 0))),
        out_shape=_sds((N_CHIP, R, C), BF16), sem=("parallel", "parallel"), name="rs_add_sibling")


def _by_chip(g):
    return g.reshape((N_CHIP, 2) + g.shape[1:])


def _pack_rows(vecs):
    flat = jnp.concatenate([v.reshape(-1).astype(F32) for v in vecs])
    rows = -(-flat.shape[0] // (8 * LANES)) * 8
    return jnp.pad(flat, (0, rows * LANES - flat.shape[0])).reshape(rows, LANES)


def _unpack_rows(rows, shapes):
    flat = rows.reshape(-1)
    out, o = [], 0
    for s in shapes:
        n = math.prod(s)
        out.append(flat[o:o + n].reshape(s))
        o += n
    return out


def _adam_any(w, m, v, g, name):
    shape = w.shape
    one = lambda t: t.reshape(1, -1, shape[-1])
    return tuple(t.reshape(shape) for t in _adam(one(w), one(m), one(v), [one(g)], name))


def kernel(x, norm_ffn1, w_ffn1_in, w_ffn1_out, norm_mix, w_in, sinks, w_dw, b_dw, conv_ln_g, conv_ln_b, w_out, norm_ffn2, w_ffn2_in, w_ffn2_out, final_norm, loss_target, m_norm_ffn1, m_w_ffn1_in, m_w_ffn1_out, m_norm_mix, m_w_in, m_sinks, m_w_dw, m_b_dw, m_conv_ln_g, m_conv_ln_b, m_w_out, m_norm_ffn2, m_w_ffn2_in, m_w_ffn2_out, m_final_norm, v_norm_ffn1, v_w_ffn1_in, v_w_ffn1_out, v_norm_mix, v_w_in, v_sinks, v_w_dw, v_b_dw, v_conv_ln_g, v_conv_ln_b, v_w_out, v_norm_ffn2, v_w_ffn2_in, v_w_ffn2_out, v_final_norm):
    _, S, D = x.shape
    L = norm_ffn1.shape[0]
    NF = w_ffn1_in.shape[2]
    RF = w_ffn1_out.shape[1]
    NW = w_in.shape[2]
    RO = w_out.shape[1]
    taps, CD = w_dw.shape[1], w_dw.shape[2]
    H = sinks.shape[1]
    C = N_DEV * CD
    QW = H * HEAD_DIM
    KVW = N_KV_HEADS * HEAD_DIM
    assert 2 * RF == NF and QW + C == N_DEV * RO and N_DEV * NW == QW + 2 * KVW + 2 * C
    tm = min(512, S)
    ta = min(256, S)
    tc = min(256, S)
    core = lax.axis_index("c").astype(jnp.int32).reshape(1)

    x0 = x[0]
    target = loss_target[0]
    attn_bias = _attn_bias(H)

    wdw_all = _gather_small(_pack_rows([w_dw]))
    n_dw = L * taps * CD
    wdw_full = jnp.stack([wdw_all[d].reshape(-1)[:n_dw].reshape(L, taps, CD) for d in range(N_DEV)],
                         axis=2).reshape(L, taps, C)

    def shard(name, l):
        return {"a1": lambda: w_ffn1_in[l].T.astype(BF16), "a2": lambda: w_ffn2_in[l].T.astype(BF16),
                "b1": lambda: w_ffn1_out[l].astype(BF16), "b2": lambda: w_ffn2_out[l].astype(BF16),
                "wi": lambda: w_in[l].T.astype(BF16), "wo": lambda: w_out[l].astype(BF16)}[name]()

    rows_of = {"a1": NF, "a2": NF, "b1": RF, "b2": RF, "wi": NW, "wo": RO}
    full_of = {n: _sds((N_DEV * r, D), BF16) for n, r in rows_of.items()}
    slot_of = {n: (lambda ref, b, r=r: ref.at[pl.ds(b * r, r)]) for n, r in rows_of.items()}
    first, plan = _plan_gathers(L, {n: r * D * 2 for n, r in rows_of.items()})
    got = {}

    def gather(items):
        if not items:
            return None
        return _gather_comm([shard(n, l) for n, l in items], [full_of[n] for n, _ in items],
                            [slot_of[n] for n, _ in items])

    def carrying(k, call, n_own):
        items = plan.get(k, [])
        res = call(gather(items))
        got.update(zip(items, res[n_own:]))
        return res[:n_own]

    def wext_of(wi):
        q, k, v, u = wi[:QW], wi[QW:QW + KVW], wi[QW + KVW:QW + 2 * KVW], wi[QW + 2 * KVW:]
        dup = lambda t: jnp.concatenate(
            [t[HEAD_DIM * (i // 2):HEAD_DIM * (i // 2 + 1)] for i in range(2 * N_KV_HEADS)], axis=0)
        return jnp.concatenate([q, dup(k), dup(v), u], axis=0)

    got.update(zip(first, _comm_only(gather(first), "ag_first")))

    saved = []
    xc = x0
    for l in range(L):
        k0 = len(FWD_KERNELS) * l
        g1, gm, g2 = norm_ffn1[l][None], norm_mix[l][None], norm_ffn2[l][None]
        x1, gu1 = carrying(k0, lambda c: _ffn_fwd(xc, g1, got["a1", l], got["b1", l], tm, c), 2)
        wext = wext_of(got["wi", l])
        proj, = carrying(k0 + 1, lambda c: _mixin_fwd(x1, gm, wext, tm, c), 1)
        attn, = carrying(k0 + 2, lambda c: _attn_fwd(proj, sinks[l], attn_bias, H, c), 1)
        conv, ysave = carrying(k0 + 3, lambda c: _conv_fwd(proj, wdw_full[l], b_dw[l][None], conv_ln_g[l][None],
                                                           conv_ln_b[l][None], H, tc, c), 2)
        x2, = carrying(k0 + 4, lambda c: _mixout_fwd(x1, attn, conv, got["wo", l], tm, c), 1)
        x3, gu2 = carrying(k0 + 5, lambda c: _ffn_fwd(x2, g2, got["a2", l], got["b2", l], tm, c), 2)
        W = dict(a1=got["a1", l], b1=got["b1", l], a2=got["a2", l], b2=got["b2", l], wo=got["wo", l], wext=wext)
        saved.append(dict(W=W, x0=xc, x1=x1, x2=x2, gu1=gu1, gu2=gu2, proj=proj, attn=attn, conv=conv, ysave=ysave))
        xc = x3

    dx, loss_parts, dfinal = _loss_head(xc, final_norm[None], target, tm)
    loss = lax.psum(jnp.sum(loss_parts[:, 0, 0]), ("x", "y", "c"))

    def swap(gs):
        return _swap_comm([_by_chip(g) for g in gs]) if gs else None

    def added(gs, gots):
        return [_add_sibling(core, _by_chip(g), r) for g, r in zip(gs, gots)]

    small = [None] * L
    big = [dict() for _ in range(L)]
    carry = None
    for l in reversed(range(L)):
        sv = saved[l]
        W = sv["W"]
        g1, gm, g2 = norm_ffn1[l][None], norm_mix[l][None], norm_ffn2[l][None]
        fold = lambda t: jnp.concatenate(
            [t[2 * HEAD_DIM * i:2 * HEAD_DIM * i + HEAD_DIM] + t[2 * HEAD_DIM * i + HEAD_DIM:2 * HEAD_DIM * (i + 1)]
             for i in range(N_KV_HEADS)], axis=0)

        r = _ffn_bwd_a(sv["x2"], g2, dx, sv["gu2"], W["a2"], W["b2"], ta, swap(carry[1]) if carry else None)
        dx2, dgu2, dg2, h2, dys2 = r[:5]
        p_a1, p_b1 = added(carry[1], r[5:]) if carry else (None, None)
        r = _ffn_bwd_w(h2, dys2, sv["gu2"], dgu2, tm, _exchange_comm([p_a1]) if carry else None)
        da2, db2 = r[0].reshape(N_DEV, NF, D), r[1].reshape(N_DEV, RF, D)
        if carry:
            big[carry[0]]["a1"], = r[2:]
        dmix, dwo = _mixout_bwd(dx2, sv["attn"], sv["conv"], W["wo"], tm)
        dwo = dwo.reshape(N_DEV, RO, D)
        r = _conv_bwd(sv["proj"], dmix, sv["ysave"], wdw_full[l], conv_ln_g[l][None], conv_ln_b[l][None], H, tc,
                      _join(_exchange_comm([p_b1]) if carry else None, swap([da2, db2, dwo])))
        dag, dwdw, dvec = r[:3]
        r = r[3:]
        if carry:
            big[carry[0]]["b1"], r = r[0], r[1:]
        p_a2, p_b2, p_wo = added([da2, db2, dwo], r)
        r = _attn_bwd(sv["proj"], dmix, sinks[l], attn_bias, H, _exchange_comm([p_b2]))
        dq, dkv, dsink = r[:3]
        big[l]["b2"] = r[3]
        r = _mixin_bwd(sv["x1"], gm, dx2, dq, dkv, dag, W["wext"], tm, _exchange_comm([p_wo]))
        dx1, dgm, dwext = r[:3]
        big[l]["wo"] = r[3]
        dwi = jnp.concatenate([dwext[:QW], fold(dwext[QW:QW + KV_DUP]),
                               fold(dwext[QW + KV_DUP:QW + 2 * KV_DUP]), dwext[QW + 2 * KV_DUP:]], axis=0)
        dwi = dwi.astype(BF16).reshape(N_DEV, NW, D)
        r = _ffn_bwd_a(sv["x0"], g1, dx1, sv["gu1"], W["a1"], W["b1"], ta, _join(_exchange_comm([p_a2]), swap([dwi])))
        dx0, dgu1, dg1, h1, dys1 = r[:5]
        big[l]["a2"] = r[5]
        p_wi, = added([dwi], r[6:])
        r = _ffn_bwd_w(h1, dys1, sv["gu1"], dgu1, tm, _exchange_comm([p_wi]))
        carry = (l, [r[0].reshape(N_DEV, NF, D), r[1].reshape(N_DEV, RF, D)])
        big[l]["wi"] = r[2]
        dx = dx0
        small[l] = [dg1[0], dgm[0], dsink[0, :H], dwdw[:taps], dvec[0], dvec[1], dvec[2], dg2[0]]

    p_carry = added(carry[1], _comm_only(swap(carry[1]), "rs_swap_last"))
    big[carry[0]]["a1"], big[carry[0]]["b1"] = _comm_only(_exchange_comm(p_carry), "rs_exchange_last")
    grad_x = dx[None]

    small_shapes = [(D,), (D,), (H,), (taps, C), (C,), (C,), (C,), (D,)]
    packed = _pack_rows([t for l in range(L) for t in small[l]] + [dfinal[0]])
    total = _sum_parts(_gather_small(packed))
    flat = _unpack_rows(total, small_shapes * L + [(D,)])
    per = [jnp.stack([flat[l * len(small_shapes) + i] for l in range(L)]) for i in range(len(small_shapes))]
    g_nf1, g_nmix, g_sinks, g_wdw_full, g_bdw, g_lng, g_lnb, g_nf2 = per
    g_final = flat[-1]
    dev = _dev(lax.axis_index("x"), lax.axis_index("y"), lax.axis_index("c"))
    g_wdw = lax.dynamic_slice_in_dim(g_wdw_full, dev * CD, CD, axis=2)

    res = {}
    res["norm_ffn1"] = _adam_any(norm_ffn1, m_norm_ffn1, v_norm_ffn1, g_nf1, "adam_small")
    res["norm_mix"] = _adam_any(norm_mix, m_norm_mix, v_norm_mix, g_nmix, "adam_small")
    res["sinks"] = _adam_any(sinks, m_sinks, v_sinks, g_sinks, "adam_small")
    res["w_dw"] = _adam_any(w_dw, m_w_dw, v_w_dw, g_wdw, "adam_small")
    res["b_dw"] = _adam_any(b_dw, m_b_dw, v_b_dw, g_bdw, "adam_small")
    res["conv_ln_g"] = _adam_any(conv_ln_g, m_conv_ln_g, v_conv_ln_g, g_lng, "adam_small")
    res["conv_ln_b"] = _adam_any(conv_ln_b, m_conv_ln_b, v_conv_ln_b, g_lnb, "adam_small")
    res["norm_ffn2"] = _adam_any(norm_ffn2, m_norm_ffn2, v_norm_ffn2, g_nf2, "adam_small")
    res["final_norm"] = tuple(t[0] for t in _adam_any(final_norm[None], m_final_norm[None], v_final_norm[None],
                                                      g_final[None], "adam_small"))

    def adam_big(key, w, m, v, name, transposed=False, comm=None):
        t = (lambda a: a.transpose(0, 2, 1)) if transposed else (lambda a: a)
        r = _adam(t(w), t(m), t(v), [big[l][key] for l in range(L)], name, comm)
        return tuple(t(o) for o in r[:4]), r[4:]

    res["w_ffn1_in"], _ = adam_big("a1", w_ffn1_in, m_w_ffn1_in, v_w_ffn1_in, "adam_ffn_in", True)
    res["w_ffn1_out"], _ = adam_big("b1", w_ffn1_out, m_w_ffn1_out, v_w_ffn1_out, "adam_ffn_out")
    res["w_ffn2_in"], _ = adam_big("a2", w_ffn2_in, m_w_ffn2_in, v_w_ffn2_in, "adam_ffn_in", True)
    res["w_ffn2_out"], _ = adam_big("b2", w_ffn2_out, m_w_ffn2_out, v_w_ffn2_out, "adam_ffn_out")
    res["w_in"], _ = adam_big("wi", w_in, m_w_in, v_w_in, "adam_w_in", True)
    res["w_out"], _ = adam_big("wo", w_out, m_w_out, v_w_out, "adam_w_out")

    order = ["norm_ffn1", "w_ffn1_in", "w_ffn1_out", "norm_mix", "w_in", "sinks", "w_dw", "b_dw", "conv_ln_g",
             "conv_ln_b", "w_out", "norm_ffn2", "w_ffn2_in", "w_ffn2_out", "final_norm"]
    return (loss, grad_x, *[res[n][0] for n in order], *[res[n][1] for n in order],
            *[res[n][2] for n in order], *[res[n][3] for n in order])
```

```python
import functools
import math

import jax
import jax.numpy as jnp
from jax import lax
from jax.experimental import pallas as pl
from jax.experimental.pallas import tpu as pltpu

F32 = jnp.float32
BF16 = jnp.bfloat16
MESH = pl.DeviceIdType.MESH

N_DEV = 8
N_CHIP = 4
HEAD_DIM = 64
N_KV_HEADS = 2
WINDOW = 128
KV_DUP = 2 * HEAD_DIM * N_KV_HEADS
RMS_EPS = 1e-6
NEG_INF = -1e30
FFN_RES = 0.5
HALO = 32
ROWS = 32
FFN_CHUNK = 512
BWD_W_CHUNK = 384
LANES = 128
SUBLANES = 8
V7X_VMEM_LIMIT = 56 * 1024 * 1024

ADAM_LR = 0.001
ADAM_B1 = 0.9
ADAM_B2 = 0.999
ADAM_EPS = 1e-08
ADAM_WD = 0.01
ADAM_STEP = 10


def _raw_call(body, **kw):
    return pl.pallas_call(body, **kw)


class _Comm:
    def __init__(self, ins, outs, sems, start, finish, mid=None):
        self.ins, self.outs, self.sems, self.start, self.finish = list(ins), list(outs), list(sems), start, finish
        self.mid = mid or (lambda ins, outs, sems: None)


def _join(*comms):
    comms = [c for c in comms if c is not None]
    if not comms:
        return None

    def split(refs, attr):
        out, o = [], 0
        for c in comms:
            n = len(getattr(c, attr))
            out.append(refs[o:o + n])
            o += n
        return out

    def run(which):
        def go(ins, outs, sems):
            for c, i, o, m in zip(comms, split(ins, "ins"), split(outs, "outs"), split(sems, "sems")):
                getattr(c, which)(i, o, m)
        return go

    return _Comm(sum((c.ins for c in comms), []), sum((c.outs for c in comms), []),
                 sum((c.sems for c in comms), []), run("start"), run("finish"), run("mid"))


def _pcall(body, args, *, name, out_shape, grid=(), in_specs=None, out_specs=None, scratch_shapes=(), sem=(),
           comm=None, grid_spec=None):
    if grid_spec is not None:
        return _raw_call(body, grid_spec=grid_spec, out_shape=out_shape, name=name,
                         compiler_params=_params(*sem))(*args)
    if comm is None:
        return _raw_call(body, grid=grid, in_specs=in_specs, out_specs=out_specs, out_shape=out_shape,
                         scratch_shapes=list(scratch_shapes), name=name, compiler_params=_params(*sem))(*args)
    n_in, n_out, n_scr = len(in_specs), len(out_shape), len(scratch_shapes)
    ci, co = len(comm.ins), len(comm.outs)

    def fused(*refs):
        cuts = [n_in, ci, n_out, co, n_scr]
        parts, o = [], 0
        for n in cuts:
            parts.append(refs[o:o + n])
            o += n
        ins, cins, outs, couts, scr = parts
        csems = refs[o:]
        steps = math.prod(grid)
        if steps < 3:
            comm.start(cins, couts, csems)
            body(*ins, *outs, *scr)
            comm.mid(cins, couts, csems)
            comm.finish(cins, couts, csems)
            return
        step = functools.reduce(lambda acc, a: acc * grid[a] + pl.program_id(a), range(len(grid)), 0)

        @pl.when(step == 0)
        def _():
            comm.start(cins, couts, csems)

        @pl.when(step == steps - 2)
        def _():
            comm.mid(cins, couts, csems)

        body(*ins, *outs, *scr)

        @pl.when(step == steps - 1)
        def _():
            comm.finish(cins, couts, csems)

    return _raw_call(
        fused, grid=grid, in_specs=list(in_specs) + [ANY] * ci, out_specs=list(out_specs) + [ANY] * co,
        out_shape=list(out_shape) + comm.outs, scratch_shapes=list(scratch_shapes) + comm.sems, name=name,
        compiler_params=_params(*(["arbitrary"] * len(grid))))(*args, *comm.ins)


ANY = pl.BlockSpec(memory_space=pl.ANY)


def _params(*sem):
    return pltpu.CompilerParams(dimension_semantics=sem, vmem_limit_bytes=V7X_VMEM_LIMIT)


def _dot(a, b):
    return jnp.dot(a, b, preferred_element_type=F32)


def _dot_nt(a, b):
    return lax.dot_general(a, b, (((1,), (1,)), ((), ())), preferred_element_type=F32)


def _dot_tn(a, b):
    return lax.dot_general(a, b, (((0,), (0,)), ((), ())), preferred_element_type=F32)


def _sigmoid(x):
    return 1.0 / (1.0 + jnp.exp(-x))


def _rms(x):
    r = lax.rsqrt(jnp.mean(x * x, axis=-1, keepdims=True) + RMS_EPS)
    return x * r, r


def _rms_bwd(dh, xh, r, g):
    dxh = dh * g
    dx = r * (dxh - xh * jnp.mean(dxh * xh, axis=-1, keepdims=True))
    return dx, jnp.sum(dh * xh, axis=0, keepdims=True)


def _sds(shape, dtype):
    return jax.ShapeDtypeStruct(shape, dtype)


def _row_block(rows, limit=512):
    fits = [d for d in range(16, min(rows, limit) + 1, 16) if rows % d == 0]
    return fits[-1] if fits else rows


def _chunks(n, step):
    return [(o, min(step, n - o)) for o in range(0, n, step)]


def _resident(shape):
    return pl.BlockSpec(shape, lambda *_: (0,) * len(shape), pipeline_mode=pl.Buffered(1))


def _ffn_fwd(x, gain, wint, wout, tm, comm=None):
    S, D = x.shape
    F = wout.shape[0]

    def body(x_ref, g_ref, w_ref, wo_ref, xo_ref, gu_ref, a_ref):
        xh, _ = _rms(x_ref[...])
        h = (xh * g_ref[...]).astype(BF16)
        for o, n in _chunks(F, FFN_CHUNK):
            gb = _dot_nt(h, w_ref[o:o + n, :]).astype(BF16)
            ub = _dot_nt(h, w_ref[F + o:F + o + n, :]).astype(BF16)
            gu_ref[:, o:o + n] = gb
            gu_ref[:, F + o:F + o + n] = ub
            g = gb.astype(F32)
            a_ref[:, o:o + n] = (g * _sigmoid(g) * ub.astype(F32)).astype(BF16)
        xo_ref[...] = x_ref[...] + FFN_RES * _dot(a_ref[...], wo_ref[...])

    return _pcall(
        body, (x, gain, wint, wout), grid=(S // tm,),
        in_specs=[pl.BlockSpec((tm, D), lambda i: (i, 0)), _resident((1, D)),
                  _resident((2 * F, D)), _resident((F, D))],
        out_specs=[pl.BlockSpec((tm, D), lambda i: (i, 0)), pl.BlockSpec((tm, 2 * F), lambda i: (i, 0))],
        out_shape=[_sds((S, D), F32), _sds((S, 2 * F), BF16)],
        scratch_shapes=[pltpu.VMEM((tm, F), BF16)],
        sem=("parallel",), name="ffn_fwd", comm=comm)


def _ffn_bwd_a(x, gain, dxo, gu, wint, wout, tm, comm=None):
    S, D = x.shape
    F = wout.shape[0]

    def body(x_ref, g_ref, dxo_ref, gu_ref, w_ref, wo_ref, dx_ref, dgu_ref, dgain_ref, h_ref, dys_ref):
        i = pl.program_id(0)
        dys = (FFN_RES * dxo_ref[...]).astype(BF16)
        dys_ref[...] = dys
        for o, n in _chunks(F, FFN_CHUNK):
            dact = _dot_nt(dys, wo_ref[o:o + n, :])
            g = gu_ref[:, o:o + n].astype(F32)
            u = gu_ref[:, F + o:F + o + n].astype(F32)
            s = _sigmoid(g)
            dgu_ref[:, o:o + n] = (dact * u * (s * (1.0 + g * (1.0 - s)))).astype(BF16)
            dgu_ref[:, F + o:F + o + n] = (dact * (g * s)).astype(BF16)
        dh = _dot(dgu_ref[...], w_ref[...])
        xh, r = _rms(x_ref[...])
        h_ref[...] = (xh * g_ref[...]).astype(BF16)
        dxn, dgn = _rms_bwd(dh, xh, r, g_ref[...])
        dx_ref[...] = dxo_ref[...] + dxn

        @pl.when(i == 0)
        def _():
            dgain_ref[...] = dgn

        @pl.when(i > 0)
        def _():
            dgain_ref[...] += dgn

    tile = pl.BlockSpec((tm, D), lambda i: (i, 0))
    wide = pl.BlockSpec((tm, 2 * F), lambda i: (i, 0))
    return _pcall(
        body, (x, gain, dxo, gu, wint, wout), grid=(S // tm,),
        in_specs=[tile, _resident((1, D)), tile, wide, _resident((2 * F, D)), _resident((F, D))],
        out_specs=[tile, wide, pl.BlockSpec((1, D), lambda i: (0, 0)), tile, tile],
        out_shape=[_sds((S, D), F32), _sds((S, 2 * F), BF16), _sds((1, D), F32), _sds((S, D), BF16), _sds((S, D), BF16)],
        sem=("arbitrary",), name="ffn_bwd_a", comm=comm)


def _ffn_bwd_w(h, dys, gu, dgu, tk, comm=None):
    S, D = h.shape
    F = gu.shape[1] // 2
    FH = F // 2
    nk = S // tk

    def body(h_ref, dys_ref, gg_ref, gu_ref, dg_ref, du_ref, dw_ref, dwo_ref, accw_ref, acco_ref):
        k = pl.program_id(1)

        @pl.when(k == 0)
        def _():
            accw_ref[...] = jnp.zeros_like(accw_ref)
            acco_ref[...] = jnp.zeros_like(acco_ref)

        hv, dys = h_ref[...], dys_ref[...]
        for o, n in _chunks(FH, BWD_W_CHUNK):
            g = gg_ref[:, o:o + n].astype(F32)
            act = (g * _sigmoid(g) * gu_ref[:, o:o + n].astype(F32)).astype(BF16)
            accw_ref[0, o:o + n, :] += _dot_tn(dg_ref[:, o:o + n], hv)
            accw_ref[1, o:o + n, :] += _dot_tn(du_ref[:, o:o + n], hv)
            acco_ref[o:o + n, :] += _dot_tn(act, dys)

        @pl.when(k == nk - 1)
        def _():
            dw_ref[...] = accw_ref[...].astype(BF16)
            dwo_ref[...] = acco_ref[...].astype(BF16)

    tile = pl.BlockSpec((tk, D), lambda j, k: (k, 0))
    gate = pl.BlockSpec((tk, FH), lambda j, k: (k, j))
    up = pl.BlockSpec((tk, FH), lambda j, k: (k, j + 2))
    return _pcall(
        body, (h, dys, gu, gu, dgu, dgu), grid=(2, nk),
        in_specs=[tile, tile, gate, up, gate, up],
        out_specs=[pl.BlockSpec((2, FH, D), lambda j, k: (0, j, 0), pipeline_mode=pl.Buffered(1)),
                   pl.BlockSpec((FH, D), lambda j, k: (j, 0), pipeline_mode=pl.Buffered(1))],
        out_shape=[_sds((2, F, D), BF16), _sds((F, D), BF16)],
        scratch_shapes=[pltpu.VMEM((2, FH, D), F32), pltpu.VMEM((FH, D), F32)],
        sem=("parallel", "arbitrary"), name="ffn_bwd_w", comm=comm)


def _mixin_fwd(x, gain, wext, tm, comm=None):
    S, D = x.shape
    PW = wext.shape[0]

    def body(x_ref, g_ref, w_ref, p_ref):
        xh, _ = _rms(x_ref[...])
        p_ref[...] = _dot_nt((xh * g_ref[...]).astype(BF16), w_ref[...]).astype(BF16)

    return _pcall(
        body, (x, gain, wext), grid=(S // tm,),
        in_specs=[pl.BlockSpec((tm, D), lambda i: (i, 0)), _resident((1, D)), _resident((PW, D))],
        out_specs=[pl.BlockSpec((tm, PW), lambda i: (i, 0))],
        out_shape=[_sds((S, PW), BF16)],
        sem=("parallel",), name="mixin_fwd", comm=comm)


def _mixin_bwd(x, gain, dxo, dq, dkv, dag, wext, tm, comm=None):
    S, D = x.shape
    PW = wext.shape[0]
    QW = dq.shape[1]
    o1, o2 = QW, QW + 2 * KV_DUP

    def body(x_ref, g_ref, dxo_ref, dq_ref, dkv_ref, dag_ref, w_ref, dx_ref, dgain_ref, dw_ref):
        i = pl.program_id(0)
        xh, r = _rms(x_ref[...])
        h = (xh * g_ref[...]).astype(BF16)
        dqv, dkvv, dagv = dq_ref[...], dkv_ref[...], dag_ref[...]
        dh = _dot(dqv, w_ref[0:o1, :]) + _dot(dkvv, w_ref[o1:o2, :]) + _dot(dagv, w_ref[o2:PW, :])
        dxn, dgn = _rms_bwd(dh, xh, r, g_ref[...])
        dx_ref[...] = dxo_ref[...] + dxn

        @pl.when(i == 0)
        def _():
            dgain_ref[...] = dgn
            dw_ref[0:o1, :] = _dot_tn(dqv, h)
            dw_ref[o1:o2, :] = _dot_tn(dkvv, h)
            dw_ref[o2:PW, :] = _dot_tn(dagv, h)

        @pl.when(i > 0)
        def _():
            dgain_ref[...] += dgn
            dw_ref[0:o1, :] += _dot_tn(dqv, h)
            dw_ref[o1:o2, :] += _dot_tn(dkvv, h)
            dw_ref[o2:PW, :] += _dot_tn(dagv, h)

    return _pcall(
        body, (x, gain, dxo, dq, dkv, dag, wext), grid=(S // tm,),
        in_specs=[pl.BlockSpec((tm, D), lambda i: (i, 0)), _resident((1, D)),
                  pl.BlockSpec((tm, D), lambda i: (i, 0)),
                  pl.BlockSpec((tm, QW), lambda i: (i, 0)),
                  pl.BlockSpec((tm, 2 * KV_DUP), lambda i: (i, 0)),
                  pl.BlockSpec((tm, PW - o2), lambda i: (i, 0)),
                  _resident((PW, D))],
        out_specs=[pl.BlockSpec((tm, D), lambda i: (i, 0)),
                   pl.BlockSpec((1, D), lambda i: (0, 0)),
                   pl.BlockSpec((PW, D), lambda i: (0, 0))],
        out_shape=[_sds((S, D), F32), _sds((1, D), F32), _sds((PW, D), F32)],
        sem=("arbitrary",), name="mixin_bwd", comm=comm)


def _attn_bias(n_heads):
    group = n_heads // N_KV_HEADS
    rows = group * WINDOW
    r = jnp.arange(rows)[:, None]
    s = jnp.arange(2 * WINDOW)[None, :]
    dist = (r % WINDOW) + WINDOW - s
    window = (dist >= 0) & (dist < WINDOW)
    out = []
    for first in (True, False):
        valid = window & (s >= WINDOW) if first else window
        tiles = []
        for kh in range(N_KV_HEADS):
            slope = jnp.asarray([2.0 ** (-8.0 * (kh * group + i + 1) / n_heads) for i in range(group)], F32)
            bias = -slope[r // WINDOW] * dist.astype(F32)
            tiles.append(jnp.where(valid, bias, NEG_INF))
        out.append(jnp.stack(tiles))
    return jnp.stack(out)


def _sink_column(group, sinks):
    rows = group * WINDOW
    seg = lax.shift_right_logical(lax.broadcasted_iota(jnp.int32, (rows, 1), 0), WINDOW.bit_length() - 1)
    sink = jnp.zeros((rows, 1), F32)
    for i in range(group):
        sink = jnp.where(seg == i, sinks[i], sink)
    return sink


def _lane_halves():
    lo = lax.broadcasted_iota(jnp.int32, (WINDOW, LANES), 1) < HEAD_DIM
    return lo, [jnp.where(lo, 1.0, 0.0).astype(BF16), jnp.where(lo, 0.0, 1.0).astype(BF16)]


def _stack_heads(ref, first_tile, n_tiles, halves, row0=0):
    parts = []
    for t in range(first_tile, first_tile + n_tiles):
        tile = ref[row0:row0 + WINDOW, LANES * t:LANES * (t + 1)]
        parts += [tile * halves[0], tile * halves[1]]
    return jnp.concatenate(parts, axis=0)


def _unstack_heads(ref, first_tile, n_tiles, lo, stacked, row0=0):
    for i in range(n_tiles):
        a = stacked[2 * i * WINDOW:(2 * i + 1) * WINDOW]
        b = stacked[(2 * i + 1) * WINDOW:(2 * i + 2) * WINDOW]
        t = first_tile + i
        ref[row0:row0 + WINDOW, LANES * t:LANES * (t + 1)] = jnp.where(lo, a, b).astype(ref.dtype)


def _softmax_sink(sc, sink):
    m = jnp.maximum(jnp.max(sc, axis=-1, keepdims=True), sink)
    p = jnp.exp(sc - m)
    es = jnp.exp(sink - m)
    inv = 1.0 / (jnp.sum(p, axis=-1, keepdims=True) + es)
    return p * inv, es * inv


def _attn_fwd(proj, sinks, bias, n_heads, comm=None):
    S = proj.shape[0]
    nb = S // WINDOW
    group = n_heads // N_KV_HEADS
    scale = 1.0 / math.sqrt(HEAD_DIM)
    QW = n_heads * HEAD_DIM
    kblk, vblk = QW // KV_DUP, QW // KV_DUP + 1

    def body(sink_ref, bias_ref, q_ref, kc_ref, kp_ref, vc_ref, vp_ref, o_ref):
        n = pl.program_id(0)
        lo, halves = _lane_halves()
        tiles = group // 2
        chains = [(j, kh) for j in range(2) for kh in range(N_KV_HEADS)]
        sink = [_sink_column(group, [sink_ref[h] for h in range(kh * group, (kh + 1) * group)]) for kh in range(N_KV_HEADS)]

        def keys(cur_ref, prev_ref, j, kh):
            lanes = slice(LANES * kh, LANES * (kh + 1))
            if j == 1:
                return cur_ref[:, lanes]
            return jnp.concatenate([prev_ref[:, lanes], cur_ref[0:WINDOW, lanes]], axis=0)

        kds = [keys(kc_ref, kp_ref, j, kh) * scale for j, kh in chains]
        vd = [keys(vc_ref, vp_ref, j, kh) for j, kh in chains]
        qs = [_stack_heads(q_ref, kh * tiles, tiles, halves, j * WINDOW) for j, kh in chains]
        bias_of = lambda j, kh: bias_ref[jnp.minimum(n, 1), kh] if j == 0 else bias_ref[1, kh]
        sc = [_dot_nt(qs[c], kds[c]) + bias_of(j, kh) for c, (j, kh) in enumerate(chains)]
        pn = [_softmax_sink(sc[c], sink[kh])[0] for c, (j, kh) in enumerate(chains)]
        out = [_dot(pn[c].astype(BF16), vd[c]) for c in range(len(chains))]
        for c, (j, kh) in enumerate(chains):
            _unstack_heads(o_ref, kh * tiles, tiles, lo, out[c], j * WINDOW)

    pair = lambda b: pl.BlockSpec((2 * WINDOW, KV_DUP), lambda n: (n, b))
    before = lambda b: pl.BlockSpec((WINDOW, KV_DUP), lambda n: (jnp.maximum(2 * n - 1, 0), b))
    return _pcall(
        body, (sinks, bias, proj, proj, proj, proj, proj), grid=(nb // 2,),
        in_specs=[pl.BlockSpec(memory_space=pltpu.SMEM), _resident(bias.shape),
                  pl.BlockSpec((2 * WINDOW, QW), lambda n: (n, 0)), pair(kblk), before(kblk), pair(vblk), before(vblk)],
        out_specs=[pl.BlockSpec((2 * WINDOW, QW), lambda n: (n, 0))],
        out_shape=[_sds((S, QW), BF16)],
        sem=("parallel",), name="attn_fwd", comm=comm)


def _attn_bwd(proj, dmix, sinks, bias, n_heads, comm=None):
    S = proj.shape[0]
    nb = S // WINDOW
    npair = nb // 2
    group = n_heads // N_KV_HEADS
    scale = 1.0 / math.sqrt(HEAD_DIM)
    QW = n_heads * HEAD_DIM
    kblk, vblk = QW // KV_DUP, QW // KV_DUP + 1

    def body(sink_ref, bias_ref, q_ref, kc_ref, kp_ref, vc_ref, vp_ref, do_ref, dq_ref, dkv_ref, dsink_ref, carry_ref):
        n = pl.program_id(0)

        @pl.when(n == 0)
        def _():
            carry_ref[...] = jnp.zeros_like(carry_ref)
            dsink_ref[...] = jnp.zeros_like(dsink_ref)

        @pl.when(n < npair)
        def _():
            lo, halves = _lane_halves()
            lane1 = lax.broadcasted_iota(jnp.int32, (1, LANES), 1)
            tiles = group // 2
            chains = [(j, kh) for j in range(2) for kh in range(N_KV_HEADS)]
            cs = range(len(chains))
            sink = [_sink_column(group, [sink_ref[h] for h in range(kh * group, (kh + 1) * group)])
                    for kh in range(N_KV_HEADS)]

            def keys(cur_ref, prev_ref, j, kh):
                lanes = slice(LANES * kh, LANES * (kh + 1))
                if j == 1:
                    return cur_ref[:, lanes]
                return jnp.concatenate([prev_ref[:, lanes], cur_ref[0:WINDOW, lanes]], axis=0)

            kds = [keys(kc_ref, kp_ref, j, kh) * scale for j, kh in chains]
            vd = [keys(vc_ref, vp_ref, j, kh) for j, kh in chains]
            qs = [_stack_heads(q_ref, kh * tiles, tiles, halves, j * WINDOW) for j, kh in chains]
            dos = [_stack_heads(do_ref, kh * tiles, tiles, halves, j * WINDOW) for j, kh in chains]
            bias_of = lambda j, kh: bias_ref[jnp.minimum(n, 1), kh] if j == 0 else bias_ref[1, kh]
            sc = [_dot_nt(qs[c], kds[c]) + bias_of(*chains[c]) for c in cs]
            dp = [_dot_nt(dos[c], vd[c]) for c in cs]
            probs = [_softmax_sink(sc[c], sink[chains[c][1]]) for c in cs]
            pn = [p[0] for p in probs]
            delta = [jnp.sum(pn[c] * dp[c], axis=-1, keepdims=True) for c in cs]
            dsb = [(pn[c] * (dp[c] - delta[c])).astype(BF16) for c in cs]
            dqs = [_dot(dsb[c], kds[c]) for c in cs]
            dkd = [_dot_tn(dsb[c], qs[c]) * scale for c in cs]
            dvd = [_dot_tn(pn[c].astype(BF16), dos[c]) for c in cs]
            dsink = jnp.zeros((1, LANES), F32)
            for c, (j, kh) in enumerate(chains):
                sd = probs[c][1] * delta[c]
                for i in range(group):
                    dsink = dsink - jnp.where(lane1 == kh * group + i, jnp.sum(sd[i * WINDOW:(i + 1) * WINDOW]), 0.0)
                _unstack_heads(dq_ref, kh * tiles, tiles, lo, dqs[c], j * WINDOW)
            dsink_ref[...] += dsink
            both = [jnp.concatenate([dkd[c] for c in cs if chains[c][0] == j] + [dvd[c] for c in cs if chains[c][0] == j],
                                    axis=1) for j in range(2)]
            dkv_ref[0:WINDOW, :] = carry_ref[0].astype(BF16)
            dkv_ref[WINDOW:2 * WINDOW, :] = (carry_ref[1] + both[0][0:WINDOW]).astype(BF16)
            carry_ref[0] = both[0][WINDOW:2 * WINDOW] + both[1][0:WINDOW]
            carry_ref[1] = both[1][WINDOW:2 * WINDOW]

        @pl.when(n == npair)
        def _():
            dkv_ref[0:WINDOW, :] = carry_ref[0].astype(BF16)
            dkv_ref[WINDOW:2 * WINDOW, :] = carry_ref[1].astype(BF16)

    last = npair - 1
    cur = lambda n: jnp.minimum(n, last)
    pair = lambda b: pl.BlockSpec((2 * WINDOW, KV_DUP), lambda n: (cur(n), b))
    before = lambda b: pl.BlockSpec((WINDOW, KV_DUP), lambda n: (jnp.clip(2 * n - 1, 0, nb - 1), b))
    wide = pl.BlockSpec((2 * WINDOW, QW), lambda n: (cur(n), 0))
    return _pcall(
        body, (sinks, bias, proj, proj, proj, proj, proj, dmix), grid=(npair + 1,),
        in_specs=[pl.BlockSpec(memory_space=pltpu.SMEM), _resident(bias.shape), wide,
                  pair(kblk), before(kblk), pair(vblk), before(vblk), wide],
        out_specs=[wide,
                   pl.BlockSpec((2 * WINDOW, 2 * KV_DUP), lambda n: (jnp.clip(n - 1, 0, last), 0)),
                   pl.BlockSpec((1, LANES), lambda n: (0, 0))],
        out_shape=[_sds((S, QW), BF16), _sds((S, 2 * KV_DUP), BF16), _sds((1, LANES), F32)],
        scratch_shapes=[pltpu.VMEM((2, WINDOW, 2 * KV_DUP), F32)],
        sem=("arbitrary",), name="attn_bwd", comm=comm)


def _glu_window(a_ref, g_ref, ap_ref, gp_ref, win_ref, first):
    tm = a_ref.shape[0]
    zp = ap_ref[...].astype(F32) * _sigmoid(gp_ref[...].astype(F32))
    win_ref[0:HALO, :] = jnp.where(first, jnp.zeros_like(zp), zp)
    win_ref[HALO:HALO + tm, :] = a_ref[...].astype(F32) * _sigmoid(g_ref[...].astype(F32))


def _preshift(win_ref, sh_ref):
    n = win_ref.shape[0] - SUBLANES
    for s in range(1, SUBLANES):
        sh_ref[s - 1, 0:n, :] = win_ref[s:s + n, :]


def _window(win_ref, sh_ref, start):
    s = start % SUBLANES
    if s == 0:
        return win_ref[start:start + ROWS, :]
    return sh_ref[s - 1, start - s:start - s + ROWS, :]


def _conv_fwd(proj, wdw, bdw, lng, lnb, n_heads, tm, comm=None):
    S = proj.shape[0]
    taps, C = wdw.shape
    ablk = (n_heads * HEAD_DIM + 2 * KV_DUP) // C
    hb = tm // HALO
    off = HALO - (taps - 1)

    def body(a_ref, g_ref, ap_ref, gp_ref, w_ref, b_ref, lg_ref, lb_ref, o_ref, y_ref, win_ref, sh_ref):
        _glu_window(a_ref, g_ref, ap_ref, gp_ref, win_ref, pl.program_id(0) == 0)
        _preshift(win_ref, sh_ref)
        for c in range(tm // ROWS):
            r0 = c * ROWS
            acc = jnp.zeros((ROWS, C), F32) + b_ref[...]
            for k in range(taps):
                acc = acc + w_ref[k:k + 1, :] * _window(win_ref, sh_ref, r0 + off + k)
            y_ref[r0:r0 + ROWS, :] = acc
        y = y_ref[...]
        mu = jnp.mean(y, axis=-1, keepdims=True)
        yc = y - mu
        yn = yc * lax.rsqrt(jnp.mean(yc * yc, axis=-1, keepdims=True) + RMS_EPS) * lg_ref[...] + lb_ref[...]
        o_ref[...] = (yn * _sigmoid(yn)).astype(BF16)

    vec = pl.BlockSpec((1, C), lambda i: (0, 0))
    halo = lambda b: pl.BlockSpec((HALO, C), lambda i: (jnp.maximum(i * hb - 1, 0), b))
    return _pcall(
        body, (proj, proj, proj, proj, wdw, bdw, lng, lnb,), grid=(S // tm,),
        in_specs=[pl.BlockSpec((tm, C), lambda i: (i, ablk)), pl.BlockSpec((tm, C), lambda i: (i, ablk + 1)),
                  halo(ablk), halo(ablk + 1),
                  pl.BlockSpec((taps, C), lambda i: (0, 0)), vec, vec, vec],
        out_specs=[pl.BlockSpec((tm, C), lambda i: (i, 0)), pl.BlockSpec((tm, C), lambda i: (i, 0))],
        out_shape=[_sds((S, C), BF16), _sds((S, C), F32)],
        scratch_shapes=[pltpu.VMEM((tm + HALO, C), F32), pltpu.VMEM((SUBLANES - 1, tm + HALO, C), F32)],
        sem=("parallel",), name="conv_fwd", comm=comm)


def _conv_bwd(proj, dmix, ysave, wdw, lng, lnb, n_heads, tm, comm=None):
    S = proj.shape[0]
    taps, C = wdw.shape
    QW = n_heads * HEAD_DIM
    ablk = (QW + 2 * KV_DUP) // C
    cblk = QW // C
    hb = tm // HALO
    nt = S // tm
    off = HALO - (taps - 1)

    def ln_bwd(dc, y, lg, lb):
        mu = jnp.mean(y, axis=-1, keepdims=True)
        yc = y - mu
        r = lax.rsqrt(jnp.mean(yc * yc, axis=-1, keepdims=True) + RMS_EPS)
        yh = yc * r
        yn = yh * lg + lb
        sg = _sigmoid(yn)
        dyn = dc * (sg * (1.0 + yn * (1.0 - sg)))
        dyh = dyn * lg
        dy = r * (dyh - jnp.mean(dyh, axis=-1, keepdims=True) - yh * jnp.mean(dyh * yh, axis=-1, keepdims=True))
        return dy, dyn, yh

    def body(dc_ref, dcn_ref, y_ref, yn_ref, a_ref, g_ref, ap_ref, gp_ref, w_ref, lg_ref, lb_ref,
             dag_ref, dw_ref, dvec_ref, zwin_ref, dyw_ref, dwacc_ref, zsh_ref, dysh_ref):
        i = pl.program_id(0)

        @pl.when(i == 0)
        def _():
            dwacc_ref[...] = jnp.zeros_like(dwacc_ref)
            dvec_ref[...] = jnp.zeros_like(dvec_ref)

        lg, lb = lg_ref[...], lb_ref[...]
        dy, dyn, yh = ln_bwd(dc_ref[...].astype(F32), y_ref[...], lg, lb)
        dy_next, _, _ = ln_bwd(dcn_ref[...].astype(F32), yn_ref[...], lg, lb)
        dyw_ref[0:tm, :] = dy
        dyw_ref[tm:tm + HALO, :] = jnp.where(i == nt - 1, jnp.zeros_like(dy_next), dy_next)
        dvec_ref[0:1, :] += jnp.sum(dy, axis=0, keepdims=True)
        dvec_ref[1:2, :] += jnp.sum(dyn * yh, axis=0, keepdims=True)
        dvec_ref[2:3, :] += jnp.sum(dyn, axis=0, keepdims=True)
        _glu_window(a_ref, g_ref, ap_ref, gp_ref, zwin_ref, i == 0)
        _preshift(zwin_ref, zsh_ref)
        _preshift(dyw_ref, dysh_ref)

        for c in range(tm // ROWS):
            r0 = c * ROWS
            dz = jnp.zeros((ROWS, C), F32)
            dyc = dyw_ref[r0:r0 + ROWS, :]
            for k in range(taps):
                dz = dz + w_ref[k:k + 1, :] * _window(dyw_ref, dysh_ref, r0 + taps - 1 - k)
                prod = dyc * _window(zwin_ref, zsh_ref, r0 + off + k)
                dwacc_ref[k] += jnp.sum(prod.reshape(ROWS // SUBLANES, SUBLANES, C), axis=0)
            a = a_ref[r0:r0 + ROWS, :].astype(F32)
            s = _sigmoid(g_ref[r0:r0 + ROWS, :].astype(F32))
            dag_ref[r0:r0 + ROWS, 0:C] = (dz * s).astype(BF16)
            dag_ref[r0:r0 + ROWS, C:2 * C] = (dz * a * s * (1.0 - s)).astype(BF16)

        @pl.when(i == nt - 1)
        def _():
            dw_ref[...] = jnp.zeros_like(dw_ref)
            for k in range(taps):
                dw_ref[k:k + 1, :] = jnp.sum(dwacc_ref[k], axis=0, keepdims=True)

    vec = pl.BlockSpec((1, C), lambda i: (0, 0))
    tile = lambda b: pl.BlockSpec((tm, C), lambda i: (i, b))
    prev = lambda b: pl.BlockSpec((HALO, C), lambda i: (jnp.maximum(i * hb - 1, 0), b))
    nxt = lambda b: pl.BlockSpec((HALO, C), lambda i: (jnp.minimum((i + 1) * hb, S // HALO - 1), b))
    return _pcall(
        body, (dmix, dmix, ysave, ysave, proj, proj, proj, proj, wdw, lng, lnb,), grid=(nt,),
        in_specs=[tile(cblk), nxt(cblk), tile(0), nxt(0), tile(ablk), tile(ablk + 1), prev(ablk), prev(ablk + 1),
                  pl.BlockSpec((taps, C), lambda i: (0, 0)), vec, vec],
        out_specs=[pl.BlockSpec((tm, 2 * C), lambda i: (i, 0)),
                   pl.BlockSpec((HALO, C), lambda i: (0, 0)),
                   pl.BlockSpec((8, C), lambda i: (0, 0))],
        out_shape=[_sds((S, 2 * C), BF16), _sds((HALO, C), F32), _sds((8, C), F32)],
        scratch_shapes=[pltpu.VMEM((tm + HALO, C), F32), pltpu.VMEM((tm + HALO, C), F32),
                        pltpu.VMEM((taps, SUBLANES, C), F32),
                        pltpu.VMEM((SUBLANES - 1, tm + HALO, C), F32), pltpu.VMEM((SUBLANES - 1, tm + HALO, C), F32)],
        sem=("arbitrary",), name="conv_bwd", comm=comm)


def _mixout_fwd(x, attn, conv, wo, tm, comm=None):
    S, D = x.shape
    QW, C = attn.shape[1], conv.shape[1]

    def body(x_ref, a_ref, c_ref, w_ref, o_ref):
        o_ref[...] = x_ref[...] + _dot(a_ref[...], w_ref[0:QW, :]) + _dot(c_ref[...], w_ref[QW:QW + C, :])

    return _pcall(
        body, (x, attn, conv, wo,), grid=(S // tm,),
        in_specs=[pl.BlockSpec((tm, D), lambda i: (i, 0)),
                  pl.BlockSpec((tm, QW), lambda i: (i, 0)),
                  pl.BlockSpec((tm, C), lambda i: (i, 0)),
                  pl.BlockSpec((QW + C, D), lambda i: (0, 0))],
        out_specs=[pl.BlockSpec((tm, D), lambda i: (i, 0))],
        out_shape=[_sds((S, D), F32)],
        sem=("parallel",), name="mixout_fwd", comm=comm)


def _mixout_bwd(dxo, attn, conv, wo, tm, comm=None):
    S, D = dxo.shape
    QW, C = attn.shape[1], conv.shape[1]
    nt = S // tm

    def body(dx_ref, a_ref, c_ref, w_ref, dm_ref, dw_ref, acc_ref):
        i = pl.program_id(0)
        dxb = dx_ref[...].astype(BF16)
        dm_ref[...] = _dot_nt(dxb, w_ref[...]).astype(BF16)

        @pl.when(i == 0)
        def _():
            acc_ref[...] = jnp.zeros_like(acc_ref)

        acc_ref[0:QW, :] += _dot_tn(a_ref[...], dxb)
        acc_ref[QW:QW + C, :] += _dot_tn(c_ref[...], dxb)

        @pl.when(i == nt - 1)
        def _():
            dw_ref[...] = acc_ref[...].astype(BF16)

    return _pcall(
        body, (dxo, attn, conv, wo,), grid=(nt,),
        in_specs=[pl.BlockSpec((tm, D), lambda i: (i, 0)),
                  pl.BlockSpec((tm, QW), lambda i: (i, 0)),
                  pl.BlockSpec((tm, C), lambda i: (i, 0)),
                  pl.BlockSpec((QW + C, D), lambda i: (0, 0))],
        out_specs=[pl.BlockSpec((tm, QW + C), lambda i: (i, 0)),
                   pl.BlockSpec((QW + C, D), lambda i: (0, 0))],
        out_shape=[_sds((S, QW + C), BF16), _sds((QW + C, D), BF16)],
        scratch_shapes=[pltpu.VMEM((QW + C, D), F32)],
        sem=("arbitrary",), name="mixout_bwd", comm=comm)


def _loss_head(x, gain, target, tm, comm=None):
    S, D = x.shape
    nt = S // tm

    def body(x_ref, g_ref, t_ref, dx_ref, loss_ref, dgain_ref):
        i = pl.program_id(0)
        xh, r = _rms(x_ref[...])
        e = xh * g_ref[...] - t_ref[...]
        loss_ref[...] = jnp.zeros((1, LANES), F32) + 0.5 * jnp.sum(jnp.mean(e * e, axis=-1, keepdims=True))
        dxn, dgn = _rms_bwd(e * (1.0 / D), xh, r, g_ref[...])
        dx_ref[...] = dxn

        @pl.when(i == 0)
        def _():
            dgain_ref[...] = dgn

        @pl.when(i > 0)
        def _():
            dgain_ref[...] += dgn

    return _pcall(
        body, (x, gain, target,), grid=(nt,),
        in_specs=[pl.BlockSpec((tm, D), lambda i: (i, 0)),
                  pl.BlockSpec((1, D), lambda i: (0, 0)),
                  pl.BlockSpec((tm, D), lambda i: (i, 0))],
        out_specs=[pl.BlockSpec((tm, D), lambda i: (i, 0)),
                   pl.BlockSpec((None, 1, LANES), lambda i: (i, 0, 0)),
                   pl.BlockSpec((1, D), lambda i: (0, 0))],
        out_shape=[_sds((S, D), F32), _sds((nt, 1, LANES), F32), _sds((1, D), F32)],
        sem=("arbitrary",), name="loss_head", comm=comm)


def _adam(w, m, v, parts, name, comm=None):
    L, R, C = w.shape
    P = parts[0].shape[0]
    br = _row_block(R, 256)
    c1 = 1.0 - ADAM_B1 ** ADAM_STEP
    c2 = 1.0 - ADAM_B2 ** ADAM_STEP

    def body(w_ref, m_ref, v_ref, *rest):
        p_refs, (g_ref, d_ref, mo_ref, vo_ref) = rest[:L], rest[L:]
        layer = pl.program_id(0)

        def update(p_ref):
            g = p_ref[0].astype(F32)
            for k in range(1, P):
                g = g + p_ref[k].astype(F32)
            mn = ADAM_B1 * m_ref[...] + (1.0 - ADAM_B1) * g
            vn = ADAM_B2 * v_ref[...] + (1.0 - ADAM_B2) * (g * g)
            g_ref[...] = g
            mo_ref[...] = mn
            vo_ref[...] = vn
            d_ref[...] = -ADAM_LR * ((mn / c1) / (jnp.sqrt(vn / c2) + ADAM_EPS) + ADAM_WD * w_ref[...])

        for k in range(L):
            pl.when(layer == k)(functools.partial(update, p_refs[k]))

    blk = pl.BlockSpec((None, br, C), lambda l, i: (l, i, 0))
    part = lambda k: pl.BlockSpec((P, br, C), lambda l, i: (0, jnp.where(l == k, i, 0), 0))
    return _pcall(
        body, (w, m, v, *parts), grid=(L, R // br),
        in_specs=[blk, blk, blk] + [part(k) for k in range(L)],
        out_specs=[blk, blk, blk, blk],
        out_shape=[_sds((L, R, C), F32)] * 4,
        sem=("parallel", "parallel"), name=name, comm=comm)


def _sum_parts(parts):
    P, R, C = parts.shape

    def body(p_ref, o_ref):
        g = p_ref[0]
        for k in range(1, P):
            g = g + p_ref[k]
        o_ref[...] = g

    vmem = pl.BlockSpec(memory_space=pltpu.VMEM)
    return _pcall(body, (parts,), in_specs=[vmem], out_specs=[vmem], out_shape=[_sds((R, C), F32)],
                  name="sum_parts")[0]


def _place():
    x, y, c = lax.axis_index("x"), lax.axis_index("y"), lax.axis_index("c")
    return x, y, c, [(1 - x, y), (x, 1 - y), (1 - x, 1 - y)]


def _dev(px, py, pc):
    return 4 * px + 2 * py + pc


def _gather_comm(shards, fulls, slot_of):
    n = len(shards)

    def copies(srcs, outs, send_sems, recv_sems):
        x, y, c, chips = _place()

        def copy(a, k, block, to, from_shard=False):
            dst = slot_of[a](outs[a], _dev(*block))
            return pltpu.make_async_remote_copy(
                src_ref=srcs[a] if from_shard else dst, dst_ref=dst,
                send_sem=send_sems.at[a, k], recv_sem=recv_sems.at[a, k], device_id=to, device_id_type=MESH)

        return copy, (x, y, c), (x, y, 1 - c), chips

    def local(srcs, outs, local_sems):
        x, y, c, _ = _place()
        return [pltpu.make_async_copy(srcs[a], slot_of[a](outs[a], _dev(x, y, c)), local_sems.at[a])
                for a in range(n)]

    def first_copies(copy, me, sibling, chips):
        out = []
        for a in range(n):
            out.append(copy(a, 0, me, sibling, True))
            out += [copy(a, 1 + j, me, (*chip, me[2]), True) for j, chip in enumerate(chips)]
        return out

    def start(srcs, outs, sems):
        send_sems, recv_sems, local_sems = sems
        copy, me, sibling, chips = copies(srcs, outs, send_sems, recv_sems)
        for cp in local(srcs, outs, local_sems):
            cp.start()
        for cp in first_copies(copy, me, sibling, chips):
            cp.start()

    def forwards(copy, me, sibling, chips):
        return [copy(a, 4 + j, (*chip, me[2]), sibling) for j, chip in enumerate(chips) for a in range(n)]

    def mid(srcs, outs, sems):
        send_sems, recv_sems, local_sems = sems
        copy, me, sibling, chips = copies(srcs, outs, send_sems, recv_sems)
        for j, chip in enumerate(chips):
            for a in range(n):
                copy(a, 1 + j, (*chip, me[2]), me).wait_recv()
                copy(a, 4 + j, (*chip, me[2]), sibling).start()

    def finish(srcs, outs, sems):
        send_sems, recv_sems, local_sems = sems
        copy, me, sibling, chips = copies(srcs, outs, send_sems, recv_sems)
        c = me[2]
        for a in range(n):
            copy(a, 0, sibling, me).wait_recv()
            for j, chip in enumerate(chips):
                copy(a, 4 + j, (*chip, 1 - c), me).wait_recv()
        for cp in first_copies(copy, me, sibling, chips) + forwards(copy, me, sibling, chips):
            cp.wait_send()
        for cp in local(srcs, outs, local_sems):
            cp.wait()

    sems = [pltpu.SemaphoreType.DMA((n, 7)), pltpu.SemaphoreType.DMA((n, 7)), pltpu.SemaphoreType.DMA((n,))]
    return _Comm(shards, fulls, sems, start, finish, mid)


def _swap_comm(grads):
    n = len(grads)

    def copies(srcs, outs, sems):
        x, y, c, _ = _place()
        return [pltpu.make_async_remote_copy(
            src_ref=srcs[a].at[:, pl.ds(1 - c, 1)], dst_ref=outs[a],
            send_sem=sems[0].at[a], recv_sem=sems[1].at[a], device_id=(x, y, 1 - c), device_id_type=MESH)
            for a in range(n)]

    def start(srcs, outs, sems):
        for cp in copies(srcs, outs, sems):
            cp.start()

    def finish(srcs, outs, sems):
        for cp in copies(srcs, outs, sems):
            cp.wait()

    return _Comm(grads, [_sds((N_CHIP, 1) + g.shape[2:], g.dtype) for g in grads],
                 [pltpu.SemaphoreType.DMA((n,)), pltpu.SemaphoreType.DMA((n,))], start, finish)


def _exchange_comm(parts):
    n = len(parts)

    def copies(srcs, outs, sems):
        x, y, c, chips = _place()
        mine = 2 * x + y
        loc = [pltpu.make_async_copy(srcs[a].at[pl.ds(mine, 1)], outs[a].at[pl.ds(mine, 1)], sems[2].at[a])
               for a in range(n)]
        rem = [pltpu.make_async_remote_copy(
            src_ref=srcs[a].at[pl.ds(2 * px + py, 1)], dst_ref=outs[a].at[pl.ds(mine, 1)],
            send_sem=sems[0].at[a, j], recv_sem=sems[1].at[a, j], device_id=(px, py, c), device_id_type=MESH)
            for a in range(n) for j, (px, py) in enumerate(chips)]
        return loc + rem

    def start(srcs, outs, sems):
        for cp in copies(srcs, outs, sems):
            cp.start()

    def finish(srcs, outs, sems):
        for cp in copies(srcs, outs, sems):
            cp.wait()

    return _Comm(parts, [_sds(p.shape, p.dtype) for p in parts],
                 [pltpu.SemaphoreType.DMA((n, 3)), pltpu.SemaphoreType.DMA((n, 3)), pltpu.SemaphoreType.DMA((n,))],
                 start, finish)


FWD_KERNELS = [("ffn1", ("a1", "b1"), 26), ("mixin", ("wi",), 7), ("attn", (), 7), ("conv", (), 15), ("mixout", ("wo",), 6),
               ("ffn2", ("a2", "b2"), 26)]
GATHER_SHARE = 70
FIRST_GATHERED = ("a1", "b1", "wi", "wo")


def _plan_gathers(n_layers, shard_bytes):
    per_layer = sum(shard_bytes.values())
    cost = {n: GATHER_SHARE * b / per_layer for n, b in shard_bytes.items()}
    room = {len(FWD_KERNELS) * l + i: k[2] for l in range(n_layers) for i, k in enumerate(FWD_KERNELS)}
    first, plan = [], {}
    for l in range(n_layers):
        for i, (_, needs, _) in enumerate(FWD_KERNELS):
            due = len(FWD_KERNELS) * l + i
            for name in needs:
                if l == 0 and name in FIRST_GATHERED:
                    first.append((name, l))
                    continue
                fits = [k for k in range(due) if room[k] >= cost[name]]
                k = fits[0] if fits else max(range(due), key=lambda k: room[k])
                room[k] -= cost[name]
                plan.setdefault(k, []).append((name, l))
    return first, plan


def _comm_only(comm, name):
    return _pcall(lambda: None, (), in_specs=[], out_specs=[], out_shape=[], name=name, comm=comm)


def _gather_small(v):
    R, C = v.shape

    def body(x_ref, out_ref, send_sems, recv_sems, local_sem):
        x, y, c, chips = _place()
        me, sibling = (x, y, c), (x, y, 1 - c)

        def copy(k, block, to, from_shard=False):
            dst = out_ref.at[_dev(*block)]
            return pltpu.make_async_remote_copy(
                src_ref=x_ref if from_shard else dst, dst_ref=dst,
                send_sem=send_sems.at[k], recv_sem=recv_sems.at[k], device_id=to, device_id_type=MESH)

        mine = pltpu.make_async_copy(x_ref, out_ref.at[_dev(*me)], local_sem)
        mine.start()
        first = [copy(0, me, sibling, True)] + [copy(1 + j, me, (*chip, c), True) for j, chip in enumerate(chips)]
        for cp in first:
            cp.start()
        passed = [copy(4 + j, (*chip, c), sibling) for j, chip in enumerate(chips)]
        for j, chip in enumerate(chips):
            copy(1 + j, (*chip, c), me).wait_recv()
            passed[j].start()
        copy(0, sibling, me).wait_recv()
        for j, chip in enumerate(chips):
            copy(4 + j, (*chip, 1 - c), me).wait_recv()
        for cp in first + passed:
            cp.wait_send()
        mine.wait()

    vmem = pl.BlockSpec(memory_space=pltpu.VMEM)
    return _pcall(
        body, (v,), in_specs=[vmem], out_specs=[vmem], out_shape=[_sds((N_DEV, R, C), F32)],
        scratch_shapes=[pltpu.SemaphoreType.DMA((7,)), pltpu.SemaphoreType.DMA((7,)), pltpu.SemaphoreType.DMA],
        name="gather_small")[0]


def _add_sibling(core, g, r):
    _, _, R, C = g.shape
    br = _row_block(R)

    def body(c_ref, g_ref, r_ref, o_ref):
        o_ref[...] = (g_ref[...].astype(F32) + r_ref[...].astype(F32)).astype(BF16)

    return _pcall(
        body, (core, g, r),
        grid_spec=pltpu.PrefetchScalarGridSpec(
            num_scalar_prefetch=1, grid=(N_CHIP, R // br),
            in_specs=[pl.BlockSpec((None, None, br, C), lambda k, i, c_ref: (k, c_ref[0], i, 0)),
                      pl.BlockSpec((None, None, br, C), lambda k, i, c_ref: (k, 0, i, 0))],
            out_specs=pl.BlockSpec((None, br, C), lambda k, i, c_ref: (k, i, 0))),
        out_shape=_sds((N_CHIP, R, C), BF16), sem=("parallel", "parallel"), name="rs_add_sibling")


def _by_chip(g):
    return g.reshape((N_CHIP, 2) + g.shape[1:])


def _pack_rows(vecs):
    flat = jnp.concatenate([v.reshape(-1).astype(F32) for v in vecs])
    rows = -(-flat.shape[0] // (8 * LANES)) * 8
    return jnp.pad(flat, (0, rows * LANES - flat.shape[0])).reshape(rows, LANES)


def _unpack_rows(rows, shapes):
    flat = rows.reshape(-1)
    out, o = [], 0
    for s in shapes:
        n = math.prod(s)
        out.append(flat[o:o + n].reshape(s))
        o += n
    return out


def _adam_any(w, m, v, g, name):
    shape = w.shape
    one = lambda t: t.reshape(1, -1, shape[-1])
    return tuple(t.reshape(shape) for t in _adam(one(w), one(m), one(v), [one(g)], name))


def kernel(x, norm_ffn1, w_ffn1_in, w_ffn1_out, norm_mix, w_in, sinks, w_dw, b_dw, conv_ln_g, conv_ln_b, w_out, norm_ffn2, w_ffn2_in, w_ffn2_out, final_norm, loss_target, m_norm_ffn1, m_w_ffn1_in, m_w_ffn1_out, m_norm_mix, m_w_in, m_sinks, m_w_dw, m_b_dw, m_conv_ln_g, m_conv_ln_b, m_w_out, m_norm_ffn2, m_w_ffn2_in, m_w_ffn2_out, m_final_norm, v_norm_ffn1, v_w_ffn1_in, v_w_ffn1_out, v_norm_mix, v_w_in, v_sinks, v_w_dw, v_b_dw, v_conv_ln_g, v_conv_ln_b, v_w_out, v_norm_ffn2, v_w_ffn2_in, v_w_ffn2_out, v_final_norm):
    _, S, D = x.shape
    L = norm_ffn1.shape[0]
    NF = w_ffn1_in.shape[2]
    RF = w_ffn1_out.shape[1]
    NW = w_in.shape[2]
    RO = w_out.shape[1]
    taps, CD = w_dw.shape[1], w_dw.shape[2]
    H = sinks.shape[1]
    C = N_DEV * CD
    QW = H * HEAD_DIM
    KVW = N_KV_HEADS * HEAD_DIM
    assert 2 * RF == NF and QW + C == N_DEV * RO and N_DEV * NW == QW + 2 * KVW + 2 * C
    tm = min(512, S)
    ta = min(256, S)
    tc = min(256, S)
    core = lax.axis_index("c").astype(jnp.int32).reshape(1)

    x0 = x[0]
    target = loss_target[0]
    attn_bias = _attn_bias(H)

    wdw_all = _gather_small(_pack_rows([w_dw]))
    n_dw = L * taps * CD
    wdw_full = jnp.stack([wdw_all[d].reshape(-1)[:n_dw].reshape(L, taps, CD) for d in range(N_DEV)],
                         axis=2).reshape(L, taps, C)

    def shard(name, l):
        return {"a1": lambda: w_ffn1_in[l].T.astype(BF16), "a2": lambda: w_ffn2_in[l].T.astype(BF16),
                "b1": lambda: w_ffn1_out[l].astype(BF16), "b2": lambda: w_ffn2_out[l].astype(BF16),
                "wi": lambda: w_in[l].T.astype(BF16), "wo": lambda: w_out[l].astype(BF16)}[name]()

    rows_of = {"a1": NF, "a2": NF, "b1": RF, "b2": RF, "wi": NW, "wo": RO}
    full_of = {n: _sds((N_DEV * r, D), BF16) for n, r in rows_of.items()}
    slot_of = {n: (lambda ref, b, r=r: ref.at[pl.ds(b * r, r)]) for n, r in rows_of.items()}
    first, plan = _plan_gathers(L, {n: r * D * 2 for n, r in rows_of.items()})
    got = {}

    def gather(items):
        if not items:
            return None
        return _gather_comm([shard(n, l) for n, l in items], [full_of[n] for n, _ in items],
                            [slot_of[n] for n, _ in items])

    def carrying(k, call, n_own):
        items = plan.get(k, [])
        res = call(gather(items))
        got.update(zip(items, res[n_own:]))
        return res[:n_own]

    def wext_of(wi):
        q, k, v, u = wi[:QW], wi[QW:QW + KVW], wi[QW + KVW:QW + 2 * KVW], wi[QW + 2 * KVW:]
        dup = lambda t: jnp.concatenate(
            [t[HEAD_DIM * (i // 2):HEAD_DIM * (i // 2 + 1)] for i in range(2 * N_KV_HEADS)], axis=0)
        return jnp.concatenate([q, dup(k), dup(v), u], axis=0)

    got.update(zip(first, _comm_only(gather(first), "ag_first")))

    saved = []
    xc = x0
    for l in range(L):
        k0 = len(FWD_KERNELS) * l
        g1, gm, g2 = norm_ffn1[l][None], norm_mix[l][None], norm_ffn2[l][None]
        x1, gu1 = carrying(k0, lambda c: _ffn_fwd(xc, g1, got["a1", l], got["b1", l], tm, c), 2)
        wext = wext_of(got["wi", l])
        proj, = carrying(k0 + 1, lambda c: _mixin_fwd(x1, gm, wext, tm, c), 1)
        attn, = carrying(k0 + 2, lambda c: _attn_fwd(proj, sinks[l], attn_bias, H, c), 1)
        conv, ysave = carrying(k0 + 3, lambda c: _conv_fwd(proj, wdw_full[l], b_dw[l][None], conv_ln_g[l][None],
                                                           conv_ln_b[l][None], H, tc, c), 2)
        x2, = carrying(k0 + 4, lambda c: _mixout_fwd(x1, attn, conv, got["wo", l], tm, c), 1)
        x3, gu2 = carrying(k0 + 5, lambda c: _ffn_fwd(x2, g2, got["a2", l], got["b2", l], tm, c), 2)
        W = dict(a1=got["a1", l], b1=got["b1", l], a2=got["a2", l], b2=got["b2", l], wo=got["wo", l], wext=wext)
        saved.append(dict(W=W, x0=xc, x1=x1, x2=x2, gu1=gu1, gu2=gu2, proj=proj, attn=attn, conv=conv, ysave=ysave))
        xc = x3

    dx, loss_parts, dfinal = _loss_head(xc, final_norm[None], target, tm)
    loss = lax.psum(jnp.sum(loss_parts[:, 0, 0]), ("x", "y", "c"))

    def swap(gs):
        return _swap_comm([_by_chip(g) for g in gs]) if gs else None

    def added(gs, gots):
        return [_add_sibling(core, _by_chip(g), r) for g, r in zip(gs, gots)]

    small = [None] * L
    big = [dict() for _ in range(L)]
    carry = None
    for l in reversed(range(L)):
        sv = saved[l]
        W = sv["W"]
        g1, gm, g2 = norm_ffn1[l][None], norm_mix[l][None], norm_ffn2[l][None]
        fold = lambda t: jnp.concatenate(
            [t[2 * HEAD_DIM * i:2 * HEAD_DIM * i + HEAD_DIM] + t[2 * HEAD_DIM * i + HEAD_DIM:2 * HEAD_DIM * (i + 1)]
             for i in range(N_KV_HEADS)], axis=0)

        r = _ffn_bwd_a(sv["x2"], g2, dx, sv["gu2"], W["a2"], W["b2"], ta, swap(carry[1]) if carry else None)
        dx2, dgu2, dg2, h2, dys2 = r[:5]
        p_a1, p_b1 = added(carry[1], r[5:]) if carry else (None, None)
        r = _ffn_bwd_w(h2, dys2, sv["gu2"], dgu2, tm, _exchange_comm([p_a1]) if carry else None)
        da2, db2 = r[0].reshape(N_DEV, NF, D), r[1].reshape(N_DEV, RF, D)
        if carry:
            big[carry[0]]["a1"], = r[2:]
        dmix, dwo = _mixout_bwd(dx2, sv["attn"], sv["conv"], W["wo"], tm)
        dwo = dwo.reshape(N_DEV, RO, D)
        r = _conv_bwd(sv["proj"], dmix, sv["ysave"], wdw_full[l], conv_ln_g[l][None], conv_ln_b[l][None], H, tc,
                      _join(_exchange_comm([p_b1]) if carry else None, swap([da2, db2, dwo])))
        dag, dwdw, dvec = r[:3]
        r = r[3:]
        if carry:
            big[carry[0]]["b1"], r = r[0], r[1:]
        p_a2, p_b2, p_wo = added([da2, db2, dwo], r)
        r = _attn_bwd(sv["proj"], dmix, sinks[l], attn_bias, H, _exchange_comm([p_b2]))
        dq, dkv, dsink = r[:3]
        big[l]["b2"] = r[3]
        r = _mixin_bwd(sv["x1"], gm, dx2, dq, dkv, dag, W["wext"], tm, _exchange_comm([p_wo]))
        dx1, dgm, dwext = r[:3]
        big[l]["wo"] = r[3]
        dwi = jnp.concatenate([dwext[:QW], fold(dwext[QW:QW + KV_DUP]),
                               fold(dwext[QW + KV_DUP:QW + 2 * KV_DUP]), dwext[QW + 2 * KV_DUP:]], axis=0)
        dwi = dwi.astype(BF16).reshape(N_DEV, NW, D)
        r = _ffn_bwd_a(sv["x0"], g1, dx1, sv["gu1"], W["a1"], W["b1"], ta, _join(_exchange_comm([p_a2]), swap([dwi])))
        dx0, dgu1, dg1, h1, dys1 = r[:5]
        big[l]["a2"] = r[5]
        p_wi, = added([dwi], r[6:])
        r = _ffn_bwd_w(h1, dys1, sv["gu1"], dgu1, tm, _exchange_comm([p_wi]))
        carry = (l, [r[0].reshape(N_DEV, NF, D), r[1].reshape(N_DEV, RF, D)])
        big[l]["wi"] = r[2]
        dx = dx0
        small[l] = [dg1[0], dgm[0], dsink[0, :H], dwdw[:taps], dvec[0], dvec[1], dvec[2], dg2[0]]

    p_carry = added(carry[1], _comm_only(swap(carry[1]), "rs_swap_last"))
    big[carry[0]]["a1"], big[carry[0]]["b1"] = _comm_only(_exchange_comm(p_carry), "rs_exchange_last")
    grad_x = dx[None]

    small_shapes = [(D,), (D,), (H,), (taps, C), (C,), (C,), (C,), (D,)]
    packed = _pack_rows([t for l in range(L) for t in small[l]] + [dfinal[0]])
    total = _sum_parts(_gather_small(packed))
    flat = _unpack_rows(total, small_shapes * L + [(D,)])
    per = [jnp.stack([flat[l * len(small_shapes) + i] for l in range(L)]) for i in range(len(small_shapes))]
    g_nf1, g_nmix, g_sinks, g_wdw_full, g_bdw, g_lng, g_lnb, g_nf2 = per
    g_final = flat[-1]
    dev = _dev(lax.axis_index("x"), lax.axis_index("y"), lax.axis_index("c"))
    g_wdw = lax.dynamic_slice_in_dim(g_wdw_full, dev * CD, CD, axis=2)

    res = {}
    res["norm_ffn1"] = _adam_any(norm_ffn1, m_norm_ffn1, v_norm_ffn1, g_nf1, "adam_small")
    res["norm_mix"] = _adam_any(norm_mix, m_norm_mix, v_norm_mix, g_nmix, "adam_small")
    res["sinks"] = _adam_any(sinks, m_sinks, v_sinks, g_sinks, "adam_small")
    res["w_dw"] = _adam_any(w_dw, m_w_dw, v_w_dw, g_wdw, "adam_small")
    res["b_dw"] = _adam_any(b_dw, m_b_dw, v_b_dw, g_bdw, "adam_small")
    res["conv_ln_g"] = _adam_any(conv_ln_g, m_conv_ln_g, v_conv_ln_g, g_lng, "adam_small")
    res["conv_ln_b"] = _adam_any(conv_ln_b, m_conv_ln_b, v_conv_ln_b, g_lnb, "adam_small")
    res["norm_ffn2"] = _adam_any(norm_ffn2, m_norm_ffn2, v_norm_ffn2, g_nf2, "adam_small")
    res["final_norm"] = tuple(t[0] for t in _adam_any(final_norm[None], m_final_norm[None], v_final_norm[None],
                                                      g_final[None], "adam_small"))

    def adam_big(key, w, m, v, name, transposed=False, comm=None):
        t = (lambda a: a.transpose(0, 2, 1)) if transposed else (lambda a: a)
        r = _adam(t(w), t(m), t(v), [big[l][key] for l in range(L)], name, comm)
        return tuple(t(o) for o in r[:4]), r[4:]

    res["w_ffn1_in"], _ = adam_big("a1", w_ffn1_in, m_w_ffn1_in, v_w_ffn1_in, "adam_ffn_in", True)
    res["w_ffn1_out"], _ = adam_big("b1", w_ffn1_out, m_w_ffn1_out, v_w_ffn1_out, "adam_ffn_out")
    res["w_ffn2_in"], _ = adam_big("a2", w_ffn2_in, m_w_ffn2_in, v_w_ffn2_in, "adam_ffn_in", True)
    res["w_ffn2_out"], _ = adam_big("b2", w_ffn2_out, m_w_ffn2_out, v_w_ffn2_out, "adam_ffn_out")
    res["w_in"], _ = adam_big("wi", w_in, m_w_in, v_w_in, "adam_w_in", True)
    res["w_out"], _ = adam_big("wo", w_out, m_w_out, v_w_out, "adam_w_out")

    order = ["norm_ffn1", "w_ffn1_in", "w_ffn1_out", "norm_mix", "w_in", "sinks", "w_dw", "b_dw", "conv_ln_g",
             "conv_ln_b", "w_out", "norm_ffn2", "w_ffn2_in", "w_ffn2_out", "final_norm"]
    return (loss, grad_x, *[res[n][0] for n in order], *[res[n][1] for n in order],
            *[res[n][2] for n in order], *[res[n][3] for n in order])
```

```python
import functools
import math

import jax
import jax.numpy as jnp
from jax import lax
from jax.experimental import pallas as pl
from jax.experimental.pallas import tpu as pltpu

F32 = jnp.float32
BF16 = jnp.bfloat16
MESH = pl.DeviceIdType.MESH

N_DEV = 8
N_CHIP = 4
HEAD_DIM = 64
N_KV_HEADS = 2
WINDOW = 128
KV_DUP = 2 * HEAD_DIM * N_KV_HEADS
RMS_EPS = 1e-6
NEG_INF = -1e30
FFN_RES = 0.5
HALO = 32
ROWS = 32
FFN_CHUNK = 512
BWD_W_CHUNK = 384
LANES = 128
SUBLANES = 8
V7X_VMEM_LIMIT = 56 * 1024 * 1024

ADAM_LR = 0.001
ADAM_B1 = 0.9
ADAM_B2 = 0.999
ADAM_EPS = 1e-08
ADAM_WD = 0.01
ADAM_STEP = 10


def _raw_call(body, **kw):
    return pl.pallas_call(body, **kw)


class _Comm:
    def __init__(self, ins, outs, sems, start, finish, mid=None):
        self.ins, self.outs, self.sems, self.start, self.finish = list(ins), list(outs), list(sems), start, finish
        self.mid = mid or (lambda ins, outs, sems: None)


def _join(*comms):
    comms = [c for c in comms if c is not None]
    if not comms:
        return None

    def split(refs, attr):
        out, o = [], 0
        for c in comms:
            n = len(getattr(c, attr))
            out.append(refs[o:o + n])
            o += n
        return out

    def run(which):
        def go(ins, outs, sems):
            for c, i, o, m in zip(comms, split(ins, "ins"), split(outs, "outs"), split(sems, "sems")):
                getattr(c, which)(i, o, m)
        return go

    return _Comm(sum((c.ins for c in comms), []), sum((c.outs for c in comms), []),
                 sum((c.sems for c in comms), []), run("start"), run("finish"), run("mid"))


def _pcall(body, args, *, name, out_shape, grid=(), in_specs=None, out_specs=None, scratch_shapes=(), sem=(),
           comm=None, grid_spec=None):
    if grid_spec is not None:
        return _raw_call(body, grid_spec=grid_spec, out_shape=out_shape, name=name,
                         compiler_params=_params(*sem))(*args)
    if comm is None:
        return _raw_call(body, grid=grid, in_specs=in_specs, out_specs=out_specs, out_shape=out_shape,
                         scratch_shapes=list(scratch_shapes), name=name, compiler_params=_params(*sem))(*args)
    n_in, n_out, n_scr = len(in_specs), len(out_shape), len(scratch_shapes)
    ci, co = len(comm.ins), len(comm.outs)

    def fused(*refs):
        cuts = [n_in, ci, n_out, co, n_scr]
        parts, o = [], 0
        for n in cuts:
            parts.append(refs[o:o + n])
            o += n
        ins, cins, outs, couts, scr = parts
        csems = refs[o:]
        steps = math.prod(grid)
        if steps < 3:
            comm.start(cins, couts, csems)
            body(*ins, *outs, *scr)
            comm.mid(cins, couts, csems)
            comm.finish(cins, couts, csems)
            return
        step = functools.reduce(lambda acc, a: acc * grid[a] + pl.program_id(a), range(len(grid)), 0)

        @pl.when(step == 0)
        def _():
            comm.start(cins, couts, csems)

        @pl.when(step == steps - 2)
        def _():
            comm.mid(cins, couts, csems)

        body(*ins, *outs, *scr)

        @pl.when(step == steps - 1)
        def _():
            comm.finish(cins, couts, csems)

    return _raw_call(
        fused, grid=grid, in_specs=list(in_specs) + [ANY] * ci, out_specs=list(out_specs) + [ANY] * co,
        out_shape=list(out_shape) + comm.outs, scratch_shapes=list(scratch_shapes) + comm.sems, name=name,
        compiler_params=_params(*(["arbitrary"] * len(grid))))(*args, *comm.ins)


ANY = pl.BlockSpec(memory_space=pl.ANY)


def _params(*sem):
    return pltpu.CompilerParams(dimension_semantics=sem, vmem_limit_bytes=V7X_VMEM_LIMIT)


def _dot(a, b):
    return jnp.dot(a, b, preferred_element_type=F32)


def _dot_nt(a, b):
    return lax.dot_general(a, b, (((1,), (1,)), ((), ())), preferred_element_type=F32)


def _dot_tn(a, b):
    return lax.dot_general(a, b, (((0,), (0,)), ((), ())), preferred_element_type=F32)


def _sigmoid(x):
    return 1.0 / (1.0 + jnp.exp(-x))


def _rms(x):
    r = lax.rsqrt(jnp.mean(x * x, axis=-1, keepdims=True) + RMS_EPS)
    return x * r, r


def _rms_bwd(dh, xh, r, g):
    dxh = dh * g
    dx = r * (dxh - xh * jnp.mean(dxh * xh, axis=-1, keepdims=True))
    return dx, jnp.sum(dh * xh, axis=0, keepdims=True)


def _sds(shape, dtype):
    return jax.ShapeDtypeStruct(shape, dtype)


def _row_block(rows, limit=512):
    fits = [d for d in range(16, min(rows, limit) + 1, 16) if rows % d == 0]
    return fits[-1] if fits else rows


def _chunks(n, step):
    return [(o, min(step, n - o)) for o in range(0, n, step)]


def _resident(shape):
    return pl.BlockSpec(shape, lambda *_: (0,) * len(shape), pipeline_mode=pl.Buffered(1))


def _ffn_fwd(x, gain, wint, wout, tm, comm=None):
    S, D = x.shape
    F = wout.shape[0]

    def body(x_ref, g_ref, w_ref, wo_ref, xo_ref, gu_ref, a_ref):
        xh, _ = _rms(x_ref[...])
        h = (xh * g_ref[...]).astype(BF16)
        for o, n in _chunks(F, FFN_CHUNK):
            gb = _dot_nt(h, w_ref[o:o + n, :]).astype(BF16)
            ub = _dot_nt(h, w_ref[F + o:F + o + n, :]).astype(BF16)
            gu_ref[:, o:o + n] = gb
            gu_ref[:, F + o:F + o + n] = ub
            g = gb.astype(F32)
            a_ref[:, o:o + n] = (g * _sigmoid(g) * ub.astype(F32)).astype(BF16)
        xo_ref[...] = x_ref[...] + FFN_RES * _dot(a_ref[...], wo_ref[...])

    return _pcall(
        body, (x, gain, wint, wout), grid=(S // tm,),
        in_specs=[pl.BlockSpec((tm, D), lambda i: (i, 0)), _resident((1, D)),
                  _resident((2 * F, D)), _resident((F, D))],
        out_specs=[pl.BlockSpec((tm, D), lambda i: (i, 0)), pl.BlockSpec((tm, 2 * F), lambda i: (i, 0))],
        out_shape=[_sds((S, D), F32), _sds((S, 2 * F), BF16)],
        scratch_shapes=[pltpu.VMEM((tm, F), BF16)],
        sem=("parallel",), name="ffn_fwd", comm=comm)


def _ffn_bwd_a(x, gain, dxo, gu, wint, wout, tm, comm=None):
    S, D = x.shape
    F = wout.shape[0]

    def body(x_ref, g_ref, dxo_ref, gu_ref, w_ref, wo_ref, dx_ref, dgu_ref, dgain_ref, h_ref, dys_ref):
        i = pl.program_id(0)
        dys = (FFN_RES * dxo_ref[...]).astype(BF16)
        dys_ref[...] = dys
        for o, n in _chunks(F, FFN_CHUNK):
            dact = _dot_nt(dys, wo_ref[o:o + n, :])
            g = gu_ref[:, o:o + n].astype(F32)
            u = gu_ref[:, F + o:F + o + n].astype(F32)
            s = _sigmoid(g)
            dgu_ref[:, o:o + n] = (dact * u * (s * (1.0 + g * (1.0 - s)))).astype(BF16)
            dgu_ref[:, F + o:F + o + n] = (dact * (g * s)).astype(BF16)
        dh = _dot(dgu_ref[...], w_ref[...])
        xh, r = _rms(x_ref[...])
        h_ref[...] = (xh * g_ref[...]).astype(BF16)
        dxn, dgn = _rms_bwd(dh, xh, r, g_ref[...])
        dx_ref[...] = dxo_ref[...] + dxn

        @pl.when(i == 0)
        def _():
            dgain_ref[...] = dgn

        @pl.when(i > 0)
        def _():
            dgain_ref[...] += dgn

    tile = pl.BlockSpec((tm, D), lambda i: (i, 0))
    wide = pl.BlockSpec((tm, 2 * F), lambda i: (i, 0))
    return _pcall(
        body, (x, gain, dxo, gu, wint, wout), grid=(S // tm,),
        in_specs=[tile, _resident((1, D)), tile, wide, _resident((2 * F, D)), _resident((F, D))],
        out_specs=[tile, wide, pl.BlockSpec((1, D), lambda i: (0, 0)), tile, tile],
        out_shape=[_sds((S, D), F32), _sds((S, 2 * F), BF16), _sds((1, D), F32), _sds((S, D), BF16), _sds((S, D), BF16)],
        sem=("arbitrary",), name="ffn_bwd_a", comm=comm)


def _ffn_bwd_w(h, dys, gu, dgu, tk, comm=None):
    S, D = h.shape
    F = gu.shape[1] // 2
    FH = F // 2
    nk = S // tk

    def body(h_ref, dys_ref, gg_ref, gu_ref, dg_ref, du_ref, dw_ref, dwo_ref, accw_ref, acco_ref):
        k = pl.program_id(1)

        @pl.when(k == 0)
        def _():
            accw_ref[...] = jnp.zeros_like(accw_ref)
            acco_ref[...] = jnp.zeros_like(acco_ref)

        hv, dys = h_ref[...], dys_ref[...]
        for o, n in _chunks(FH, BWD_W_CHUNK):
            g = gg_ref[:, o:o + n].astype(F32)
            act = (g * _sigmoid(g) * gu_ref[:, o:o + n].astype(F32)).astype(BF16)
            accw_ref[0, o:o + n, :] += _dot_tn(dg_ref[:, o:o + n], hv)
            accw_ref[1, o:o + n, :] += _dot_tn(du_ref[:, o:o + n], hv)
            acco_ref[o:o + n, :] += _dot_tn(act, dys)

        @pl.when(k == nk - 1)
        def _():
            dw_ref[...] = accw_ref[...].astype(BF16)
            dwo_ref[...] = acco_ref[...].astype(BF16)

    tile = pl.BlockSpec((tk, D), lambda j, k: (k, 0))
    gate = pl.BlockSpec((tk, FH), lambda j, k: (k, j))
    up = pl.BlockSpec((tk, FH), lambda j, k: (k, j + 2))
    return _pcall(
        body, (h, dys, gu, gu, dgu, dgu), grid=(2, nk),
        in_specs=[tile, tile, gate, up, gate, up],
        out_specs=[pl.BlockSpec((2, FH, D), lambda j, k: (0, j, 0), pipeline_mode=pl.Buffered(1)),
                   pl.BlockSpec((FH, D), lambda j, k: (j, 0), pipeline_mode=pl.Buffered(1))],
        out_shape=[_sds((2, F, D), BF16), _sds((F, D), BF16)],
        scratch_shapes=[pltpu.VMEM((2, FH, D), F32), pltpu.VMEM((FH, D), F32)],
        sem=("parallel", "arbitrary"), name="ffn_bwd_w", comm=comm)


def _mixin_fwd(x, gain, wext, tm, comm=None):
    S, D = x.shape
    PW = wext.shape[0]

    def body(x_ref, g_ref, w_ref, p_ref):
        xh, _ = _rms(x_ref[...])
        p_ref[...] = _dot_nt((xh * g_ref[...]).astype(BF16), w_ref[...]).astype(BF16)

    return _pcall(
        body, (x, gain, wext), grid=(S // tm,),
        in_specs=[pl.BlockSpec((tm, D), lambda i: (i, 0)), _resident((1, D)), _resident((PW, D))],
        out_specs=[pl.BlockSpec((tm, PW), lambda i: (i, 0))],
        out_shape=[_sds((S, PW), BF16)],
        sem=("parallel",), name="mixin_fwd", comm=comm)


def _mixin_bwd(x, gain, dxo, dq, dkv, dag, wext, tm, comm=None):
    S, D = x.shape
    PW = wext.shape[0]
    QW = dq.shape[1]
    o1, o2 = QW, QW + 2 * KV_DUP

    def body(x_ref, g_ref, dxo_ref, dq_ref, dkv_ref, dag_ref, w_ref, dx_ref, dgain_ref, dw_ref):
        i = pl.program_id(0)
        xh, r = _rms(x_ref[...])
        h = (xh * g_ref[...]).astype(BF16)
        dqv, dkvv, dagv = dq_ref[...], dkv_ref[...], dag_ref[...]
        dh = _dot(dqv, w_ref[0:o1, :]) + _dot(dkvv, w_ref[o1:o2, :]) + _dot(dagv, w_ref[o2:PW, :])
        dxn, dgn = _rms_bwd(dh, xh, r, g_ref[...])
        dx_ref[...] = dxo_ref[...] + dxn

        @pl.when(i == 0)
        def _():
            dgain_ref[...] = dgn
            dw_ref[0:o1, :] = _dot_tn(dqv, h)
            dw_ref[o1:o2, :] = _dot_tn(dkvv, h)
            dw_ref[o2:PW, :] = _dot_tn(dagv, h)

        @pl.when(i > 0)
        def _():
            dgain_ref[...] += dgn
            dw_ref[0:o1, :] += _dot_tn(dqv, h)
            dw_ref[o1:o2, :] += _dot_tn(dkvv, h)
            dw_ref[o2:PW, :] += _dot_tn(dagv, h)

    return _pcall(
        body, (x, gain, dxo, dq, dkv, dag, wext), grid=(S // tm,),
        in_specs=[pl.BlockSpec((tm, D), lambda i: (i, 0)), _resident((1, D)),
                  pl.BlockSpec((tm, D), lambda i: (i, 0)),
                  pl.BlockSpec((tm, QW), lambda i: (i, 0)),
                  pl.BlockSpec((tm, 2 * KV_DUP), lambda i: (i, 0)),
                  pl.BlockSpec((tm, PW - o2), lambda i: (i, 0)),
                  _resident((PW, D))],
        out_specs=[pl.BlockSpec((tm, D), lambda i: (i, 0)),
                   pl.BlockSpec((1, D), lambda i: (0, 0)),
                   pl.BlockSpec((PW, D), lambda i: (0, 0))],
        out_shape=[_sds((S, D), F32), _sds((1, D), F32), _sds((PW, D), F32)],
        sem=("arbitrary",), name="mixin_bwd", comm=comm)


def _attn_bias(n_heads):
    group = n_heads // N_KV_HEADS
    rows = group * WINDOW
    r = jnp.arange(rows)[:, None]
    s = jnp.arange(2 * WINDOW)[None, :]
    dist = (r % WINDOW) + WINDOW - s
    window = (dist >= 0) & (dist < WINDOW)
    out = []
    for first in (True, False):
        valid = window & (s >= WINDOW) if first else window
        tiles = []
        for kh in range(N_KV_HEADS):
            slope = jnp.asarray([2.0 ** (-8.0 * (kh * group + i + 1) / n_heads) for i in range(group)], F32)
            bias = -slope[r // WINDOW] * dist.astype(F32)
            tiles.append(jnp.where(valid, bias, NEG_INF))
        out.append(jnp.stack(tiles))
    return jnp.stack(out)


def _sink_column(group, sinks):
    rows = group * WINDOW
    seg = lax.shift_right_logical(lax.broadcasted_iota(jnp.int32, (rows, 1), 0), WINDOW.bit_length() - 1)
    sink = jnp.zeros((rows, 1), F32)
    for i in range(group):
        sink = jnp.where(seg == i, sinks[i], sink)
    return sink


def _lane_halves():
    lo = lax.broadcasted_iota(jnp.int32, (WINDOW, LANES), 1) < HEAD_DIM
    return lo, [jnp.where(lo, 1.0, 0.0).astype(BF16), jnp.where(lo, 0.0, 1.0).astype(BF16)]


def _stack_heads(ref, first_tile, n_tiles, halves, row0=0):
    parts = []
    for t in range(first_tile, first_tile + n_tiles):
        tile = ref[row0:row0 + WINDOW, LANES * t:LANES * (t + 1)]
        parts += [tile * halves[0], tile * halves[1]]
    return jnp.concatenate(parts, axis=0)


def _unstack_heads(ref, first_tile, n_tiles, lo, stacked, row0=0):
    for i in range(n_tiles):
        a = stacked[2 * i * WINDOW:(2 * i + 1) * WINDOW]
        b = stacked[(2 * i + 1) * WINDOW:(2 * i + 2) * WINDOW]
        t = first_tile + i
        ref[row0:row0 + WINDOW, LANES * t:LANES * (t + 1)] = jnp.where(lo, a, b).astype(ref.dtype)


def _softmax_sink(sc, sink):
    m = jnp.maximum(jnp.max(sc, axis=-1, keepdims=True), sink)
    p = jnp.exp(sc - m)
    es = jnp.exp(sink - m)
    inv = 1.0 / (jnp.sum(p, axis=-1, keepdims=True) + es)
    return p * inv, es * inv


def _attn_fwd(proj, sinks, bias, n_heads, comm=None):
    S = proj.shape[0]
    nb = S // WINDOW
    group = n_heads // N_KV_HEADS
    scale = 1.0 / math.sqrt(HEAD_DIM)
    QW = n_heads * HEAD_DIM
    kblk, vblk = QW // KV_DUP, QW // KV_DUP + 1

    def body(sink_ref, bias_ref, q_ref, kc_ref, kp_ref, vc_ref, vp_ref, o_ref):
        n = pl.program_id(0)
        lo, halves = _lane_halves()
        tiles = group // 2
        chains = [(j, kh) for j in range(2) for kh in range(N_KV_HEADS)]
        sink = [_sink_column(group, [sink_ref[h] for h in range(kh * group, (kh + 1) * group)]) for kh in range(N_KV_HEADS)]

        def keys(cur_ref, prev_ref, j, kh):
            lanes = slice(LANES * kh, LANES * (kh + 1))
            if j == 1:
                return cur_ref[:, lanes]
            return jnp.concatenate([prev_ref[:, lanes], cur_ref[0:WINDOW, lanes]], axis=0)

        kds = [keys(kc_ref, kp_ref, j, kh) * scale for j, kh in chains]
        vd = [keys(vc_ref, vp_ref, j, kh) for j, kh in chains]
        qs = [_stack_heads(q_ref, kh * tiles, tiles, halves, j * WINDOW) for j, kh in chains]
        bias_of = lambda j, kh: bias_ref[jnp.minimum(n, 1), kh] if j == 0 else bias_ref[1, kh]
        sc = [_dot_nt(qs[c], kds[c]) + bias_of(j, kh) for c, (j, kh) in enumerate(chains)]
        pn = [_softmax_sink(sc[c], sink[kh])[0] for c, (j, kh) in enumerate(chains)]
        out = [_dot(pn[c].astype(BF16), vd[c]) for c in range(len(chains))]
        for c, (j, kh) in enumerate(chains):
            _unstack_heads(o_ref, kh * tiles, tiles, lo, out[c], j * WINDOW)

    pair = lambda b: pl.BlockSpec((2 * WINDOW, KV_DUP), lambda n: (n, b))
    before = lambda b: pl.BlockSpec((WINDOW, KV_DUP), lambda n: (jnp.maximum(2 * n - 1, 0), b))
    return _pcall(
        body, (sinks, bias, proj, proj, proj, proj, proj), grid=(nb // 2,),
        in_specs=[pl.BlockSpec(memory_space=pltpu.SMEM), _resident(bias.shape),
                  pl.BlockSpec((2 * WINDOW, QW), lambda n: (n, 0)), pair(kblk), before(kblk), pair(vblk), before(vblk)],
        out_specs=[pl.BlockSpec((2 * WINDOW, QW), lambda n: (n, 0))],
        out_shape=[_sds((S, QW), BF16)],
        sem=("parallel",), name="attn_fwd", comm=comm)


def _attn_bwd(proj, dmix, sinks, bias, n_heads, comm=None):
    S = proj.shape[0]
    nb = S // WINDOW
    npair = nb // 2
    group = n_heads // N_KV_HEADS
    scale = 1.0 / math.sqrt(HEAD_DIM)
    QW = n_heads * HEAD_DIM
    kblk, vblk = QW // KV_DUP, QW // KV_DUP + 1

    def body(sink_ref, bias_ref, q_ref, kc_ref, kp_ref, vc_ref, vp_ref, do_ref, dq_ref, dkv_ref, dsink_ref, carry_ref):
        n = pl.program_id(0)

        @pl.when(n == 0)
        def _():
            carry_ref[...] = jnp.zeros_like(carry_ref)
            dsink_ref[...] = jnp.zeros_like(dsink_ref)

        @pl.when(n < npair)
        def _():
            lo, halves = _lane_halves()
            lane1 = lax.broadcasted_iota(jnp.int32, (1, LANES), 1)
            tiles = group // 2
            chains = [(j, kh) for j in range(2) for kh in range(N_KV_HEADS)]
            cs = range(len(chains))
            sink = [_sink_column(group, [sink_ref[h] for h in range(kh * group, (kh + 1) * group)])
                    for kh in range(N_KV_HEADS)]

            def keys(cur_ref, prev_ref, j, kh):
                lanes = slice(LANES * kh, LANES * (kh + 1))
                if j == 1:
                    return cur_ref[:, lanes]
                return jnp.concatenate([prev_ref[:, lanes], cur_ref[0:WINDOW, lanes]], axis=0)

            kds = [keys(kc_ref, kp_ref, j, kh) * scale for j, kh in chains]
            vd = [keys(vc_ref, vp_ref, j, kh) for j, kh in chains]
            qs = [_stack_heads(q_ref, kh * tiles, tiles, halves, j * WINDOW) for j, kh in chains]
            dos = [_stack_heads(do_ref, kh * tiles, tiles, halves, j * WINDOW) for j, kh in chains]
            bias_of = lambda j, kh: bias_ref[jnp.minimum(n, 1), kh] if j == 0 else bias_ref[1, kh]
            sc = [_dot_nt(qs[c], kds[c]) + bias_of(*chains[c]) for c in cs]
            dp = [_dot_nt(dos[c], vd[c]) for c in cs]
            probs = [_softmax_sink(sc[c], sink[chains[c][1]]) for c in cs]
            pn = [p[0] for p in probs]
            delta = [jnp.sum(pn[c] * dp[c], axis=-1, keepdims=True) for c in cs]
            dsb = [(pn[c] * (dp[c] - delta[c])).astype(BF16) for c in cs]
            dqs = [_dot(dsb[c], kds[c]) for c in cs]
            dkd = [_dot_tn(dsb[c], qs[c]) * scale for c in cs]
            dvd = [_dot_tn(pn[c].astype(BF16), dos[c]) for c in cs]
            dsink = jnp.zeros((1, LANES), F32)
            for c, (j, kh) in enumerate(chains):
                sd = probs[c][1] * delta[c]
                for i in range(group):
                    dsink = dsink - jnp.where(lane1 == kh * group + i, jnp.sum(sd[i * WINDOW:(i + 1) * WINDOW]), 0.0)
                _unstack_heads(dq_ref, kh * tiles, tiles, lo, dqs[c], j * WINDOW)
            dsink_ref[...] += dsink
            both = [jnp.concatenate([dkd[c] for c in cs if chains[c][0] == j] + [dvd[c] for c in cs if chains[c][0] == j],
                                    axis=1) for j in range(2)]
            dkv_ref[0:WINDOW, :] = carry_ref[0].astype(BF16)
            dkv_ref[WINDOW:2 * WINDOW, :] = (carry_ref[1] + both[0][0:WINDOW]).astype(BF16)
            carry_ref[0] = both[0][WINDOW:2 * WINDOW] + both[1][0:WINDOW]
            carry_ref[1] = both[1][WINDOW:2 * WINDOW]

        @pl.when(n == npair)
        def _():
            dkv_ref[0:WINDOW, :] = carry_ref[0].astype(BF16)
            dkv_ref[WINDOW:2 * WINDOW, :] = carry_ref[1].astype(BF16)

    last = npair - 1
    cur = lambda n: jnp.minimum(n, last)
    pair = lambda b: pl.BlockSpec((2 * WINDOW, KV_DUP), lambda n: (cur(n), b))
    before = lambda b: pl.BlockSpec((WINDOW, KV_DUP), lambda n: (jnp.clip(2 * n - 1, 0, nb - 1), b))
    wide = pl.BlockSpec((2 * WINDOW, QW), lambda n: (cur(n), 0))
    return _pcall(
        body, (sinks, bias, proj, proj, proj, proj, proj, dmix), grid=(npair + 1,),
        in_specs=[pl.BlockSpec(memory_space=pltpu.SMEM), _resident(bias.shape), wide,
                  pair(kblk), before(kblk), pair(vblk), before(vblk), wide],
        out_specs=[wide,
                   pl.BlockSpec((2 * WINDOW, 2 * KV_DUP), lambda n: (jnp.clip(n - 1, 0, last), 0)),
                   pl.BlockSpec((1, LANES), lambda n: (0, 0))],
        out_shape=[_sds((S, QW), BF16), _sds((S, 2 * KV_DUP), BF16), _sds((1, LANES), F32)],
        scratch_shapes=[pltpu.VMEM((2, WINDOW, 2 * KV_DUP), F32)],
        sem=("arbitrary",), name="attn_bwd", comm=comm)


def _glu_window(a_ref, g_ref, ap_ref, gp_ref, win_ref, first):
    tm = a_ref.shape[0]
    zp = ap_ref[...].astype(F32) * _sigmoid(gp_ref[...].astype(F32))
    win_ref[0:HALO, :] = jnp.where(first, jnp.zeros_like(zp), zp)
    win_ref[HALO:HALO + tm, :] = a_ref[...].astype(F32) * _sigmoid(g_ref[...].astype(F32))


def _preshift(win_ref, sh_ref):
    n = win_ref.shape[0] - SUBLANES
    for s in range(1, SUBLANES):
        sh_ref[s - 1, 0:n, :] = win_ref[s:s + n, :]


def _window(win_ref, sh_ref, start):
    s = start % SUBLANES
    if s == 0:
        return win_ref[start:start + ROWS, :]
    return sh_ref[s - 1, start - s:start - s + ROWS, :]


def _conv_fwd(proj, wdw, bdw, lng, lnb, n_heads, tm, comm=None):
    S = proj.shape[0]
    taps, C = wdw.shape
    ablk = (n_heads * HEAD_DIM + 2 * KV_DUP) // C
    hb = tm // HALO
    off = HALO - (taps - 1)

    def body(a_ref, g_ref, ap_ref, gp_ref, w_ref, b_ref, lg_ref, lb_ref, o_ref, y_ref, win_ref, sh_ref):
        _glu_window(a_ref, g_ref, ap_ref, gp_ref, win_ref, pl.program_id(0) == 0)
        _preshift(win_ref, sh_ref)
        for c in range(tm // ROWS):
            r0 = c * ROWS
            acc = jnp.zeros((ROWS, C), F32) + b_ref[...]
            for k in range(taps):
                acc = acc + w_ref[k:k + 1, :] * _window(win_ref, sh_ref, r0 + off + k)
            y_ref[r0:r0 + ROWS, :] = acc
        y = y_ref[...]
        mu = jnp.mean(y, axis=-1, keepdims=True)
        yc = y - mu
        yn = yc * lax.rsqrt(jnp.mean(yc * yc, axis=-1, keepdims=True) + RMS_EPS) * lg_ref[...] + lb_ref[...]
        o_ref[...] = (yn * _sigmoid(yn)).astype(BF16)

    vec = pl.BlockSpec((1, C), lambda i: (0, 0))
    halo = lambda b: pl.BlockSpec((HALO, C), lambda i: (jnp.maximum(i * hb - 1, 0), b))
    return _pcall(
        body, (proj, proj, proj, proj, wdw, bdw, lng, lnb,), grid=(S // tm,),
        in_specs=[pl.BlockSpec((tm, C), lambda i: (i, ablk)), pl.BlockSpec((tm, C), lambda i: (i, ablk + 1)),
                  halo(ablk), halo(ablk + 1),
                  pl.BlockSpec((taps, C), lambda i: (0, 0)), vec, vec, vec],
        out_specs=[pl.BlockSpec((tm, C), lambda i: (i, 0)), pl.BlockSpec((tm, C), lambda i: (i, 0))],
        out_shape=[_sds((S, C), BF16), _sds((S, C), F32)],
        scratch_shapes=[pltpu.VMEM((tm + HALO, C), F32), pltpu.VMEM((SUBLANES - 1, tm + HALO, C), F32)],
        sem=("parallel",), name="conv_fwd", comm=comm)


def _conv_bwd(proj, dmix, ysave, wdw, lng, lnb, n_heads, tm, comm=None):
    S = proj.shape[0]
    taps, C = wdw.shape
    QW = n_heads * HEAD_DIM
    ablk = (QW + 2 * KV_DUP) // C
    cblk = QW // C
    hb = tm // HALO
    nt = S // tm
    off = HALO - (taps - 1)

    def ln_bwd(dc, y, lg, lb):
        mu = jnp.mean(y, axis=-1, keepdims=True)
        yc = y - mu
        r = lax.rsqrt(jnp.mean(yc * yc, axis=-1, keepdims=True) + RMS_EPS)
        yh = yc * r
        yn = yh * lg + lb
        sg = _sigmoid(yn)
        dyn = dc * (sg * (1.0 + yn * (1.0 - sg)))
        dyh = dyn * lg
        dy = r * (dyh - jnp.mean(dyh, axis=-1, keepdims=True) - yh * jnp.mean(dyh * yh, axis=-1, keepdims=True))
        return dy, dyn, yh

    def body(dc_ref, dcn_ref, y_ref, yn_ref, a_ref, g_ref, ap_ref, gp_ref, w_ref, lg_ref, lb_ref,
             dag_ref, dw_ref, dvec_ref, zwin_ref, dyw_ref, dwacc_ref, zsh_ref, dysh_ref):
        i = pl.program_id(0)

        @pl.when(i == 0)
        def _():
            dwacc_ref[...] = jnp.zeros_like(dwacc_ref)
            dvec_ref[...] = jnp.zeros_like(dvec_ref)

        lg, lb = lg_ref[...], lb_ref[...]
        dy, dyn, yh = ln_bwd(dc_ref[...].astype(F32), y_ref[...], lg, lb)
        dy_next, _, _ = ln_bwd(dcn_ref[...].astype(F32), yn_ref[...], lg, lb)
        dyw_ref[0:tm, :] = dy
        dyw_ref[tm:tm + HALO, :] = jnp.where(i == nt - 1, jnp.zeros_like(dy_next), dy_next)
        dvec_ref[0:1, :] += jnp.sum(dy, axis=0, keepdims=True)
        dvec_ref[1:2, :] += jnp.sum(dyn * yh, axis=0, keepdims=True)
        dvec_ref[2:3, :] += jnp.sum(dyn, axis=0, keepdims=True)
        _glu_window(a_ref, g_ref, ap_ref, gp_ref, zwin_ref, i == 0)
        _preshift(zwin_ref, zsh_ref)
        _preshift(dyw_ref, dysh_ref)

        for c in range(tm // ROWS):
            r0 = c * ROWS
            dz = jnp.zeros((ROWS, C), F32)
            dyc = dyw_ref[r0:r0 + ROWS, :]
            for k in range(taps):
                dz = dz + w_ref[k:k + 1, :] * _window(dyw_ref, dysh_ref, r0 + taps - 1 - k)
                prod = dyc * _window(zwin_ref, zsh_ref, r0 + off + k)
                dwacc_ref[k] += jnp.sum(prod.reshape(ROWS // SUBLANES, SUBLANES, C), axis=0)
            a = a_ref[r0:r0 + ROWS, :].astype(F32)
            s = _sigmoid(g_ref[r0:r0 + ROWS, :].astype(F32))
            dag_ref[r0:r0 + ROWS, 0:C] = (dz * s).astype(BF16)
            dag_ref[r0:r0 + ROWS, C:2 * C] = (dz * a * s * (1.0 - s)).astype(BF16)

        @pl.when(i == nt - 1)
        def _():
            dw_ref[...] = jnp.zeros_like(dw_ref)
            for k in range(taps):
                dw_ref[k:k + 1, :] = jnp.sum(dwacc_ref[k], axis=0, keepdims=True)

    vec = pl.BlockSpec((1, C), lambda i: (0, 0))
    tile = lambda b: pl.BlockSpec((tm, C), lambda i: (i, b))
    prev = lambda b: pl.BlockSpec((HALO, C), lambda i: (jnp.maximum(i * hb - 1, 0), b))
    nxt = lambda b: pl.BlockSpec((HALO, C), lambda i: (jnp.minimum((i + 1) * hb, S // HALO - 1), b))
    return _pcall(
        body, (dmix, dmix, ysave, ysave, proj, proj, proj, proj, wdw, lng, lnb,), grid=(nt,),
        in_specs=[tile(cblk), nxt(cblk), tile(0), nxt(0), tile(ablk), tile(ablk + 1), prev(ablk), prev(ablk + 1),
                  pl.BlockSpec((taps, C), lambda i: (0, 0)), vec, vec],
        out_specs=[pl.BlockSpec((tm, 2 * C), lambda i: (i, 0)),
                   pl.BlockSpec((HALO, C), lambda i: (0, 0)),
                   pl.BlockSpec((8, C), lambda i: (0, 0))],
        out_shape=[_sds((S, 2 * C), BF16), _sds((HALO, C), F32), _sds((8, C), F32)],
        scratch_shapes=[pltpu.VMEM((tm + HALO, C), F32), pltpu.VMEM((tm + HALO, C), F32),
                        pltpu.VMEM((taps, SUBLANES, C), F32),
                        pltpu.VMEM((SUBLANES - 1, tm + HALO, C), F32), pltpu.VMEM((SUBLANES - 1, tm + HALO, C), F32)],
        sem=("arbitrary",), name="conv_bwd", comm=comm)


def _mixout_fwd(x, attn, conv, wo, tm, comm=None):
    S, D = x.shape
    QW, C = attn.shape[1], conv.shape[1]

    def body(x_ref, a_ref, c_ref, w_ref, o_ref):
        o_ref[...] = x_ref[...] + _dot(a_ref[...], w_ref[0:QW, :]) + _dot(c_ref[...], w_ref[QW:QW + C, :])

    return _pcall(
        body, (x, attn, conv, wo,), grid=(S // tm,),
        in_specs=[pl.BlockSpec((tm, D), lambda i: (i, 0)),
                  pl.BlockSpec((tm, QW), lambda i: (i, 0)),
                  pl.BlockSpec((tm, C), lambda i: (i, 0)),
                  pl.BlockSpec((QW + C, D), lambda i: (0, 0))],
        out_specs=[pl.BlockSpec((tm, D), lambda i: (i, 0))],
        out_shape=[_sds((S, D), F32)],
        sem=("parallel",), name="mixout_fwd", comm=comm)


def _mixout_bwd(dxo, attn, conv, wo, tm, comm=None):
    S, D = dxo.shape
    QW, C = attn.shape[1], conv.shape[1]
    nt = S // tm

    def body(dx_ref, a_ref, c_ref, w_ref, dm_ref, dw_ref, acc_ref):
        i = pl.program_id(0)
        dxb = dx_ref[...].astype(BF16)
        dm_ref[...] = _dot_nt(dxb, w_ref[...]).astype(BF16)

        @pl.when(i == 0)
        def _():
            acc_ref[...] = jnp.zeros_like(acc_ref)

        acc_ref[0:QW, :] += _dot_tn(a_ref[...], dxb)
        acc_ref[QW:QW + C, :] += _dot_tn(c_ref[...], dxb)

        @pl.when(i == nt - 1)
        def _():
            dw_ref[...] = acc_ref[...].astype(BF16)

    return _pcall(
        body, (dxo, attn, conv, wo,), grid=(nt,),
        in_specs=[pl.BlockSpec((tm, D), lambda i: (i, 0)),
                  pl.BlockSpec((tm, QW), lambda i: (i, 0)),
                  pl.BlockSpec((tm, C), lambda i: (i, 0)),
                  pl.BlockSpec((QW + C, D), lambda i: (0, 0))],
        out_specs=[pl.BlockSpec((tm, QW + C), lambda i: (i, 0)),
                   pl.BlockSpec((QW + C, D), lambda i: (0, 0))],
        out_shape=[_sds((S, QW + C), BF16), _sds((QW + C, D), BF16)],
        scratch_shapes=[pltpu.VMEM((QW + C, D), F32)],
        sem=("arbitrary",), name="mixout_bwd", comm=comm)


def _loss_head(x, gain, target, tm, comm=None):
    S, D = x.shape
    nt = S // tm

    def body(x_ref, g_ref, t_ref, dx_ref, loss_ref, dgain_ref):
        i = pl.program_id(0)
        xh, r = _rms(x_ref[...])
        e = xh * g_ref[...] - t_ref[...]
        loss_ref[...] = jnp.zeros((1, LANES), F32) + 0.5 * jnp.sum(jnp.mean(e * e, axis=-1, keepdims=True))
        dxn, dgn = _rms_bwd(e * (1.0 / D), xh, r, g_ref[...])
        dx_ref[...] = dxn

        @pl.when(i == 0)
        def _():
            dgain_ref[...] = dgn

        @pl.when(i > 0)
        def _():
            dgain_ref[...] += dgn

    return _pcall(
        body, (x, gain, target,), grid=(nt,),
        in_specs=[pl.BlockSpec((tm, D), lambda i: (i, 0)),
                  pl.BlockSpec((1, D), lambda i: (0, 0)),
                  pl.BlockSpec((tm, D), lambda i: (i, 0))],
        out_specs=[pl.BlockSpec((tm, D), lambda i: (i, 0)),
                   pl.BlockSpec((None, 1, LANES), lambda i: (i, 0, 0)),
                   pl.BlockSpec((1, D), lambda i: (0, 0))],
        out_shape=[_sds((S, D), F32), _sds((nt, 1, LANES), F32), _sds((1, D), F32)],
        sem=("arbitrary",), name="loss_head", comm=comm)


def _adam(w, m, v, parts, name, comm=None):
    L, R, C = w.shape
    P = parts[0].shape[0]
    br = _row_block(R, 256)
    c1 = 1.0 - ADAM_B1 ** ADAM_STEP
    c2 = 1.0 - ADAM_B2 ** ADAM_STEP

    def body(w_ref, m_ref, v_ref, *rest):
        p_refs, (g_ref, d_ref, mo_ref, vo_ref) = rest[:L], rest[L:]
        layer = pl.program_id(0)

        def update(p_ref):
            g = p_ref[0].astype(F32)
            for k in range(1, P):
                g = g + p_ref[k].astype(F32)
            mn = ADAM_B1 * m_ref[...] + (1.0 - ADAM_B1) * g
            vn = ADAM_B2 * v_ref[...] + (1.0 - ADAM_B2) * (g * g)
            g_ref[...] = g
            mo_ref[...] = mn
            vo_ref[...] = vn
            d_ref[...] = -ADAM_LR * ((mn / c1) / (jnp.sqrt(vn / c2) + ADAM_EPS) + ADAM_WD * w_ref[...])

        for k in range(L):
            pl.when(layer == k)(functools.partial(update, p_refs[k]))

    blk = pl.BlockSpec((None, br, C), lambda l, i: (l, i, 0))
    part = lambda k: pl.BlockSpec((P, br, C), lambda l, i: (0, jnp.where(l == k, i, 0), 0))
    return _pcall(
        body, (w, m, v, *parts), grid=(L, R // br),
        in_specs=[blk, blk, blk] + [part(k) for k in range(L)],
        out_specs=[blk, blk, blk, blk],
        out_shape=[_sds((L, R, C), F32)] * 4,
        sem=("parallel", "parallel"), name=name, comm=comm)


def _sum_parts(parts):
    P, R, C = parts.shape

    def body(p_ref, o_ref):
        g = p_ref[0]
        for k in range(1, P):
            g = g + p_ref[k]
        o_ref[...] = g

    vmem = pl.BlockSpec(memory_space=pltpu.VMEM)
    return _pcall(body, (parts,), in_specs=[vmem], out_specs=[vmem], out_shape=[_sds((R, C), F32)],
                  name="sum_parts")[0]


def _place():
    x, y, c = lax.axis_index("x"), lax.axis_index("y"), lax.axis_index("c")
    return x, y, c, [(1 - x, y), (x, 1 - y), (1 - x, 1 - y)]


def _dev(px, py, pc):
    return 4 * px + 2 * py + pc


def _gather_comm(shards, fulls, slot_of):
    n = len(shards)

    def copies(srcs, outs, send_sems, recv_sems):
        x, y, c, chips = _place()

        def copy(a, k, block, to, from_shard=False):
            dst = slot_of[a](outs[a], _dev(*block))
            return pltpu.make_async_remote_copy(
                src_ref=srcs[a] if from_shard else dst, dst_ref=dst,
                send_sem=send_sems.at[a, k], recv_sem=recv_sems.at[a, k], device_id=to, device_id_type=MESH)

        return copy, (x, y, c), (x, y, 1 - c), chips

    def local(srcs, outs, local_sems):
        x, y, c, _ = _place()
        return [pltpu.make_async_copy(srcs[a], slot_of[a](outs[a], _dev(x, y, c)), local_sems.at[a])
                for a in range(n)]

    def first_copies(copy, me, sibling, chips):
        out = []
        for a in range(n):
            out.append(copy(a, 0, me, sibling, True))
            out += [copy(a, 1 + j, me, (*chip, me[2]), True) for j, chip in enumerate(chips)]
        return out

    def start(srcs, outs, sems):
        send_sems, recv_sems, local_sems = sems
        copy, me, sibling, chips = copies(srcs, outs, send_sems, recv_sems)
        for cp in local(srcs, outs, local_sems):
            cp.start()
        for cp in first_copies(copy, me, sibling, chips):
            cp.start()

    def forwards(copy, me, sibling, chips):
        return [copy(a, 4 + j, (*chip, me[2]), sibling) for j, chip in enumerate(chips) for a in range(n)]

    def mid(srcs, outs, sems):
        send_sems, recv_sems, local_sems = sems
        copy, me, sibling, chips = copies(srcs, outs, send_sems, recv_sems)
        for j, chip in enumerate(chips):
            for a in range(n):
                copy(a, 1 + j, (*chip, me[2]), me).wait_recv()
                copy(a, 4 + j, (*chip, me[2]), sibling).start()

    def finish(srcs, outs, sems):
        send_sems, recv_sems, local_sems = sems
        copy, me, sibling, chips = copies(srcs, outs, send_sems, recv_sems)
        c = me[2]
        for a in range(n):
            copy(a, 0, sibling, me).wait_recv()
            for j, chip in enumerate(chips):
                copy(a, 4 + j, (*chip, 1 - c), me).wait_recv()
        for cp in first_copies(copy, me, sibling, chips) + forwards(copy, me, sibling, chips):
            cp.wait_send()
        for cp in local(srcs, outs, local_sems):
            cp.wait()

    sems = [pltpu.SemaphoreType.DMA((n, 7)), pltpu.SemaphoreType.DMA((n, 7)), pltpu.SemaphoreType.DMA((n,))]
    return _Comm(shards, fulls, sems, start, finish, mid)


def _swap_comm(grads):
    n = len(grads)

    def copies(srcs, outs, sems):
        x, y, c, _ = _place()
        return [pltpu.make_async_remote_copy(
            src_ref=srcs[a].at[:, pl.ds(1 - c, 1)], dst_ref=outs[a],
            send_sem=sems[0].at[a], recv_sem=sems[1].at[a], device_id=(x, y, 1 - c), device_id_type=MESH)
            for a in range(n)]

    def start(srcs, outs, sems):
        for cp in copies(srcs, outs, sems):
            cp.start()

    def finish(srcs, outs, sems):
        for cp in copies(srcs, outs, sems):
            cp.wait()

    return _Comm(grads, [_sds((N_CHIP, 1) + g.shape[2:], g.dtype) for g in grads],
                 [pltpu.SemaphoreType.DMA((n,)), pltpu.SemaphoreType.DMA((n,))], start, finish)


def _exchange_comm(parts):
    n = len(parts)

    def copies(srcs, outs, sems):
        x, y, c, chips = _place()
        mine = 2 * x + y
        loc = [pltpu.make_async_copy(srcs[a].at[pl.ds(mine, 1)], outs[a].at[pl.ds(mine, 1)], sems[2].at[a])
               for a in range(n)]
        rem = [pltpu.make_async_remote_copy(
            src_ref=srcs[a].at[pl.ds(2 * px + py, 1)], dst_ref=outs[a].at[pl.ds(mine, 1)],
            send_sem=sems[0].at[a, j], recv_sem=sems[1].at[a, j], device_id=(px, py, c), device_id_type=MESH)
            for a in range(n) for j, (px, py) in enumerate(chips)]
        return loc + rem

    def start(srcs, outs, sems):
        for cp in copies(srcs, outs, sems):
            cp.start()

    def finish(srcs, outs, sems):
        for cp in copies(srcs, outs, sems):
            cp.wait()

    return _Comm(parts, [_sds(p.shape, p.dtype) for p in parts],
                 [pltpu.SemaphoreType.DMA((n, 3)), pltpu.SemaphoreType.DMA((n, 3)), pltpu.SemaphoreType.DMA((n,))],
                 start, finish)


FWD_KERNELS = [("ffn1", ("a1", "b1"), 26), ("mixin", ("wi",), 7), ("attn", (), 7), ("conv", (), 15), ("mixout", ("wo",), 6),
               ("ffn2", ("a2", "b2"), 26)]
GATHER_SHARE = 70
FIRST_GATHERED = ("a1", "b1", "wi", "wo")


def _plan_gathers(n_layers, shard_bytes):
    per_layer = sum(shard_bytes.values())
    cost = {n: GATHER_SHARE * b / per_layer for n, b in shard_bytes.items()}
    room = {len(FWD_KERNELS) * l + i: k[2] for l in range(n_layers) for i, k in enumerate(FWD_KERNELS)}
    first, plan = [], {}
    for l in range(n_layers):
        for i, (_, needs, _) in enumerate(FWD_KERNELS):
            due = len(FWD_KERNELS) * l + i
            for name in needs:
                if l == 0 and name in FIRST_GATHERED:
                    first.append((name, l))
                    continue
                fits = [k for k in range(due) if room[k] >= cost[name]]
                k = fits[0] if fits else max(range(due), key=lambda k: room[k])
                room[k] -= cost[name]
                plan.setdefault(k, []).append((name, l))
    return first, plan


def _comm_only(comm, name):
    return _pcall(lambda: None, (), in_specs=[], out_specs=[], out_shape=[], name=name, comm=comm)


def _gather_small(v):
    R, C = v.shape

    def body(x_ref, out_ref, send_sems, recv_sems, local_sem):
        x, y, c, chips = _place()
        me, sibling = (x, y, c), (x, y, 1 - c)

        def copy(k, block, to, from_shard=False):
            dst = out_ref.at[_dev(*block)]
            return pltpu.make_async_remote_copy(
                src_ref=x_ref if from_shard else dst, dst_ref=dst,
                send_sem=send_sems.at[k], recv_sem=recv_sems.at[k], device_id=to, device_id_type=MESH)

        mine = pltpu.make_async_copy(x_ref, out_ref.at[_dev(*me)], local_sem)
        mine.start()
        first = [copy(0, me, sibling, True)] + [copy(1 + j, me, (*chip, c), True) for j, chip in enumerate(chips)]
        for cp in first:
            cp.start()
        passed = [copy(4 + j, (*chip, c), sibling) for j, chip in enumerate(chips)]
        for j, chip in enumerate(chips):
            copy(1 + j, (*chip, c), me).wait_recv()
            passed[j].start()
        copy(0, sibling, me).wait_recv()
        for j, chip in enumerate(chips):
            copy(4 + j, (*chip, 1 - c), me).wait_recv()
        for cp in first + passed:
            cp.wait_send()
        mine.wait()

    vmem = pl.BlockSpec(memory_space=pltpu.VMEM)
    return _pcall(
        body, (v,), in_specs=[vmem], out_specs=[vmem], out_shape=[_sds((N_DEV, R, C), F32)],
        scratch_shapes=[pltpu.SemaphoreType.DMA((7,)), pltpu.SemaphoreType.DMA((7,)), pltpu.SemaphoreType.DMA],
        name="gather_small")[0]


def _add_sibling(core, g, r):
    _, _, R, C = g.shape
    br = _row_block(R)

    def body(c_ref, g_ref, r_ref, o_ref):
        o_ref[...] = (g_ref[...].astype(F32) + r_ref[...].astype(F32)).astype(BF16)

    return _pcall(
        body, (core, g, r),
        grid_spec=pltpu.PrefetchScalarGridSpec(
            num_scalar_prefetch=1, grid=(N_CHIP, R // br),
            in_specs=[pl.BlockSpec((None, None, br, C), lambda k, i, c_ref: (k, c_ref[0], i, 0)),
                      pl.BlockSpec((None, None, br, C), lambda k, i, c_ref: (k, 0, i, 0))],
            out_specs=pl.BlockSpec((None, br, C), lambda k, i, c_ref: (k, i, 0))),
        out_shape=_sds((N_CHIP, R, C), BF16), sem=("parallel", "parallel"), name="rs_add_sibling")


def _by_chip(g):
    return g.reshape((N_CHIP, 2) + g.shape[1:])


def _pack_rows(vecs):
    flat = jnp.concatenate([v.reshape(-1).astype(F32) for v in vecs])
    rows = -(-flat.shape[0] // (8 * LANES)) * 8
    return jnp.pad(flat, (0, rows * LANES - flat.shape[0])).reshape(rows, LANES)


def _unpack_rows(rows, shapes):
    flat = rows.reshape(-1)
    out, o = [], 0
    for s in shapes:
        n = math.prod(s)
        out.append(flat[o:o + n].reshape(s))
        o += n
    return out


def _adam_any(w, m, v, g, name):
    shape = w.shape
    one = lambda t: t.reshape(1, -1, shape[-1])
    return tuple(t.reshape(shape) for t in _adam(one(w), one(m), one(v), [one(g)], name))


def kernel(x, norm_ffn1, w_ffn1_in, w_ffn1_out, norm_mix, w_in, sinks, w_dw, b_dw, conv_ln_g, conv_ln_b, w_out, norm_ffn2, w_ffn2_in, w_ffn2_out, final_norm, loss_target, m_norm_ffn1, m_w_ffn1_in, m_w_ffn1_out, m_norm_mix, m_w_in, m_sinks, m_w_dw, m_b_dw, m_conv_ln_g, m_conv_ln_b, m_w_out, m_norm_ffn2, m_w_ffn2_in, m_w_ffn2_out, m_final_norm, v_norm_ffn1, v_w_ffn1_in, v_w_ffn1_out, v_norm_mix, v_w_in, v_sinks, v_w_dw, v_b_dw, v_conv_ln_g, v_conv_ln_b, v_w_out, v_norm_ffn2, v_w_ffn2_in, v_w_ffn2_out, v_final_norm):
    _, S, D = x.shape
    L = norm_ffn1.shape[0]
    NF = w_ffn1_in.shape[2]
    RF = w_ffn1_out.shape[1]
    NW = w_in.shape[2]
    RO = w_out.shape[1]
    taps, CD = w_dw.shape[1], w_dw.shape[2]
    H = sinks.shape[1]
    C = N_DEV * CD
    QW = H * HEAD_DIM
    KVW = N_KV_HEADS * HEAD_DIM
    assert 2 * RF == NF and QW + C == N_DEV * RO and N_DEV * NW == QW + 2 * KVW + 2 * C
    tm = min(512, S)
    tp = min(1024, S)
    ta = min(256, S)
    tc = min(256, S)
    core = lax.axis_index("c").astype(jnp.int32).reshape(1)

    x0 = x[0]
    target = loss_target[0]
    attn_bias = _attn_bias(H)

    wdw_all = _gather_small(_pack_rows([w_dw]))
    n_dw = L * taps * CD
    wdw_full = jnp.stack([wdw_all[d].reshape(-1)[:n_dw].reshape(L, taps, CD) for d in range(N_DEV)],
                         axis=2).reshape(L, taps, C)

    def shard(name, l):
        return {"a1": lambda: w_ffn1_in[l].T.astype(BF16), "a2": lambda: w_ffn2_in[l].T.astype(BF16),
                "b1": lambda: w_ffn1_out[l].astype(BF16), "b2": lambda: w_ffn2_out[l].astype(BF16),
                "wi": lambda: w_in[l].T.astype(BF16), "wo": lambda: w_out[l].astype(BF16)}[name]()

    rows_of = {"a1": NF, "a2": NF, "b1": RF, "b2": RF, "wi": NW, "wo": RO}
    full_of = {n: _sds((N_DEV * r, D), BF16) for n, r in rows_of.items()}
    slot_of = {n: (lambda ref, b, r=r: ref.at[pl.ds(b * r, r)]) for n, r in rows_of.items()}
    first, plan = _plan_gathers(L, {n: r * D * 2 for n, r in rows_of.items()})
    got = {}

    def gather(items):
        if not items:
            return None
        return _gather_comm([shard(n, l) for n, l in items], [full_of[n] for n, _ in items],
                            [slot_of[n] for n, _ in items])

    def carrying(k, call, n_own):
        items = plan.get(k, [])
        res = call(gather(items))
        got.update(zip(items, res[n_own:]))
        return res[:n_own]

    def wext_of(wi):
        q, k, v, u = wi[:QW], wi[QW:QW + KVW], wi[QW + KVW:QW + 2 * KVW], wi[QW + 2 * KVW:]
        dup = lambda t: jnp.concatenate(
            [t[HEAD_DIM * (i // 2):HEAD_DIM * (i // 2 + 1)] for i in range(2 * N_KV_HEADS)], axis=0)
        return jnp.concatenate([q, dup(k), dup(v), u], axis=0)

    got.update(zip(first, _comm_only(gather(first), "ag_first")))

    saved = []
    xc = x0
    for l in range(L):
        k0 = len(FWD_KERNELS) * l
        g1, gm, g2 = norm_ffn1[l][None], norm_mix[l][None], norm_ffn2[l][None]
        x1, gu1 = carrying(k0, lambda c: _ffn_fwd(xc, g1, got["a1", l], got["b1", l], tm, c), 2)
        wext = wext_of(got["wi", l])
        proj, = carrying(k0 + 1, lambda c: _mixin_fwd(x1, gm, wext, tp, c), 1)
        attn, = carrying(k0 + 2, lambda c: _attn_fwd(proj, sinks[l], attn_bias, H, c), 1)
        conv, ysave = carrying(k0 + 3, lambda c: _conv_fwd(proj, wdw_full[l], b_dw[l][None], conv_ln_g[l][None],
                                                           conv_ln_b[l][None], H, tc, c), 2)
        x2, = carrying(k0 + 4, lambda c: _mixout_fwd(x1, attn, conv, got["wo", l], tp, c), 1)
        x3, gu2 = carrying(k0 + 5, lambda c: _ffn_fwd(x2, g2, got["a2", l], got["b2", l], tm, c), 2)
        W = dict(a1=got["a1", l], b1=got["b1", l], a2=got["a2", l], b2=got["b2", l], wo=got["wo", l], wext=wext)
        saved.append(dict(W=W, x0=xc, x1=x1, x2=x2, gu1=gu1, gu2=gu2, proj=proj, attn=attn, conv=conv, ysave=ysave))
        xc = x3

    dx, loss_parts, dfinal = _loss_head(xc, final_norm[None], target, tm)
    loss = lax.psum(jnp.sum(loss_parts[:, 0, 0]), ("x", "y", "c"))

    def swap(gs):
        return _swap_comm([_by_chip(g) for g in gs]) if gs else None

    def added(gs, gots):
        return [_add_sibling(core, _by_chip(g), r) for g, r in zip(gs, gots)]

    small = [None] * L
    big = [dict() for _ in range(L)]
    carry = None
    for l in reversed(range(L)):
        sv = saved[l]
        W = sv["W"]
        g1, gm, g2 = norm_ffn1[l][None], norm_mix[l][None], norm_ffn2[l][None]
        fold = lambda t: jnp.concatenate(
            [t[2 * HEAD_DIM * i:2 * HEAD_DIM * i + HEAD_DIM] + t[2 * HEAD_DIM * i + HEAD_DIM:2 * HEAD_DIM * (i + 1)]
             for i in range(N_KV_HEADS)], axis=0)

        r = _ffn_bwd_a(sv["x2"], g2, dx, sv["gu2"], W["a2"], W["b2"], ta, swap(carry[1]) if carry else None)
        dx2, dgu2, dg2, h2, dys2 = r[:5]
        p_a1, p_b1 = added(carry[1], r[5:]) if carry else (None, None)
        r = _ffn_bwd_w(h2, dys2, sv["gu2"], dgu2, tm, _exchange_comm([p_a1]) if carry else None)
        da2, db2 = r[0].reshape(N_DEV, NF, D), r[1].reshape(N_DEV, RF, D)
        if carry:
            big[carry[0]]["a1"], = r[2:]
        dmix, dwo = _mixout_bwd(dx2, sv["attn"], sv["conv"], W["wo"], tp)
        dwo = dwo.reshape(N_DEV, RO, D)
        r = _conv_bwd(sv["proj"], dmix, sv["ysave"], wdw_full[l], conv_ln_g[l][None], conv_ln_b[l][None], H, tc,
                      _join(_exchange_comm([p_b1]) if carry else None, swap([da2, db2, dwo])))
        dag, dwdw, dvec = r[:3]
        r = r[3:]
        if carry:
            big[carry[0]]["b1"], r = r[0], r[1:]
        p_a2, p_b2, p_wo = added([da2, db2, dwo], r)
        r = _attn_bwd(sv["proj"], dmix, sinks[l], attn_bias, H, _exchange_comm([p_b2]))
        dq, dkv, dsink = r[:3]
        big[l]["b2"] = r[3]
        r = _mixin_bwd(sv["x1"], gm, dx2, dq, dkv, dag, W["wext"], tm, _exchange_comm([p_wo]))
        dx1, dgm, dwext = r[:3]
        big[l]["wo"] = r[3]
        dwi = jnp.concatenate([dwext[:QW], fold(dwext[QW:QW + KV_DUP]),
                               fold(dwext[QW + KV_DUP:QW + 2 * KV_DUP]), dwext[QW + 2 * KV_DUP:]], axis=0)
        dwi = dwi.astype(BF16).reshape(N_DEV, NW, D)
        r = _ffn_bwd_a(sv["x0"], g1, dx1, sv["gu1"], W["a1"], W["b1"], ta, _join(_exchange_comm([p_a2]), swap([dwi])))
        dx0, dgu1, dg1, h1, dys1 = r[:5]
        big[l]["a2"] = r[5]
        p_wi, = added([dwi], r[6:])
        r = _ffn_bwd_w(h1, dys1, sv["gu1"], dgu1, tm, _exchange_comm([p_wi]))
        carry = (l, [r[0].reshape(N_DEV, NF, D), r[1].reshape(N_DEV, RF, D)])
        big[l]["wi"] = r[2]
        dx = dx0
        small[l] = [dg1[0], dgm[0], dsink[0, :H], dwdw[:taps], dvec[0], dvec[1], dvec[2], dg2[0]]

    p_carry = added(carry[1], _comm_only(swap(carry[1]), "rs_swap_last"))
    big[carry[0]]["a1"], big[carry[0]]["b1"] = _comm_only(_exchange_comm(p_carry), "rs_exchange_last")
    grad_x = dx[None]

    small_shapes = [(D,), (D,), (H,), (taps, C), (C,), (C,), (C,), (D,)]
    packed = _pack_rows([t for l in range(L) for t in small[l]] + [dfinal[0]])
    total = _sum_parts(_gather_small(packed))
    flat = _unpack_rows(total, small_shapes * L + [(D,)])
    per = [jnp.stack([flat[l * len(small_shapes) + i] for l in range(L)]) for i in range(len(small_shapes))]
    g_nf1, g_nmix, g_sinks, g_wdw_full, g_bdw, g_lng, g_lnb, g_nf2 = per
    g_final = flat[-1]
    dev = _dev(lax.axis_index("x"), lax.axis_index("y"), lax.axis_index("c"))
    g_wdw = lax.dynamic_slice_in_dim(g_wdw_full, dev * CD, CD, axis=2)

    res = {}
    res["norm_ffn1"] = _adam_any(norm_ffn1, m_norm_ffn1, v_norm_ffn1, g_nf1, "adam_small")
    res["norm_mix"] = _adam_any(norm_mix, m_norm_mix, v_norm_mix, g_nmix, "adam_small")
    res["sinks"] = _adam_any(sinks, m_sinks, v_sinks, g_sinks, "adam_small")
    res["w_dw"] = _adam_any(w_dw, m_w_dw, v_w_dw, g_wdw, "adam_small")
    res["b_dw"] = _adam_any(b_dw, m_b_dw, v_b_dw, g_bdw, "adam_small")
    res["conv_ln_g"] = _adam_any(conv_ln_g, m_conv_ln_g, v_conv_ln_g, g_lng, "adam_small")
    res["conv_ln_b"] = _adam_any(conv_ln_b, m_conv_ln_b, v_conv_ln_b, g_lnb, "adam_small")
    res["norm_ffn2"] = _adam_any(norm_ffn2, m_norm_ffn2, v_norm_ffn2, g_nf2, "adam_small")
    res["final_norm"] = tuple(t[0] for t in _adam_any(final_norm[None], m_final_norm[None], v_final_norm[None],
                                                      g_final[None], "adam_small"))

    def adam_big(key, w, m, v, name, transposed=False, comm=None):
        t = (lambda a: a.transpose(0, 2, 1)) if transposed else (lambda a: a)
        r = _adam(t(w), t(m), t(v), [big[l][key] for l in range(L)], name, comm)
        return tuple(t(o) for o in r[:4]), r[4:]

    res["w_ffn1_in"], _ = adam_big("a1", w_ffn1_in, m_w_ffn1_in, v_w_ffn1_in, "adam_ffn_in", True)
    res["w_ffn1_out"], _ = adam_big("b1", w_ffn1_out, m_w_ffn1_out, v_w_ffn1_out, "adam_ffn_out")
    res["w_ffn2_in"], _ = adam_big("a2", w_ffn2_in, m_w_ffn2_in, v_w_ffn2_in, "adam_ffn_in", True)
    res["w_ffn2_out"], _ = adam_big("b2", w_ffn2_out, m_w_ffn2_out, v_w_ffn2_out, "adam_ffn_out")
    res["w_in"], _ = adam_big("wi", w_in, m_w_in, v_w_in, "adam_w_in", True)
    res["w_out"], _ = adam_big("wo", w_out, m_w_out, v_w_out, "adam_w_out")

    order = ["norm_ffn1", "w_ffn1_in", "w_ffn1_out", "norm_mix", "w_in", "sinks", "w_dw", "b_dw", "conv_ln_g",
             "conv_ln_b", "w_out", "norm_ffn2", "w_ffn2_in", "w_ffn2_out", "final_norm"]
    return (loss, grad_x, *[res[n][0] for n in order], *[res[n][1] for n in order],
            *[res[n][2] for n in order], *[res[n][3] for n in order])
```

```python
import functools
import math

import jax
import jax.numpy as jnp
from jax import lax
from jax.experimental import pallas as pl
from jax.experimental.pallas import tpu as pltpu

F32 = jnp.float32
BF16 = jnp.bfloat16
MESH = pl.DeviceIdType.MESH

N_DEV = 8
N_CHIP = 4
HEAD_DIM = 64
N_KV_HEADS = 2
WINDOW = 128
KV_DUP = 2 * HEAD_DIM * N_KV_HEADS
RMS_EPS = 1e-6
NEG_INF = -1e30
FFN_RES = 0.5
HALO = 32
ROWS = 32
FFN_CHUNK = 512
BWD_W_CHUNK = 384
LANES = 128
SUBLANES = 8
V7X_VMEM_LIMIT = 56 * 1024 * 1024

ADAM_LR = 0.001
ADAM_B1 = 0.9
ADAM_B2 = 0.999
ADAM_EPS = 1e-08
ADAM_WD = 0.01
ADAM_STEP = 10


def _raw_call(body, **kw):
    return pl.pallas_call(body, **kw)


class _Comm:
    def __init__(self, ins, outs, sems, start, finish, mid=None):
        self.ins, self.outs, self.sems, self.start, self.finish = list(ins), list(outs), list(sems), start, finish
        self.mid = mid or (lambda ins, outs, sems: None)


def _join(*comms):
    comms = [c for c in comms if c is not None]
    if not comms:
        return None

    def split(refs, attr):
        out, o = [], 0
        for c in comms:
            n = len(getattr(c, attr))
            out.append(refs[o:o + n])
            o += n
        return out

    def run(which):
        def go(ins, outs, sems):
            for c, i, o, m in zip(comms, split(ins, "ins"), split(outs, "outs"), split(sems, "sems")):
                getattr(c, which)(i, o, m)
        return go

    return _Comm(sum((c.ins for c in comms), []), sum((c.outs for c in comms), []),
                 sum((c.sems for c in comms), []), run("start"), run("finish"), run("mid"))


def _pcall(body, args, *, name, out_shape, grid=(), in_specs=None, out_specs=None, scratch_shapes=(), sem=(),
           comm=None, grid_spec=None):
    if grid_spec is not None:
        return _raw_call(body, grid_spec=grid_spec, out_shape=out_shape, name=name,
                         compiler_params=_params(*sem))(*args)
    if comm is None:
        return _raw_call(body, grid=grid, in_specs=in_specs, out_specs=out_specs, out_shape=out_shape,
                         scratch_shapes=list(scratch_shapes), name=name, compiler_params=_params(*sem))(*args)
    n_in, n_out, n_scr = len(in_specs), len(out_shape), len(scratch_shapes)
    ci, co = len(comm.ins), len(comm.outs)

    def fused(*refs):
        cuts = [n_in, ci, n_out, co, n_scr]
        parts, o = [], 0
        for n in cuts:
            parts.append(refs[o:o + n])
            o += n
        ins, cins, outs, couts, scr = parts
        csems = refs[o:]
        steps = math.prod(grid)
        if steps < 3:
            comm.start(cins, couts, csems)
            body(*ins, *outs, *scr)
            comm.mid(cins, couts, csems)
            comm.finish(cins, couts, csems)
            return
        step = functools.reduce(lambda acc, a: acc * grid[a] + pl.program_id(a), range(len(grid)), 0)

        @pl.when(step == 0)
        def _():
            comm.start(cins, couts, csems)

        @pl.when(step == steps - 2)
        def _():
            comm.mid(cins, couts, csems)

        body(*ins, *outs, *scr)

        @pl.when(step == steps - 1)
        def _():
            comm.finish(cins, couts, csems)

    return _raw_call(
        fused, grid=grid, in_specs=list(in_specs) + [ANY] * ci, out_specs=list(out_specs) + [ANY] * co,
        out_shape=list(out_shape) + comm.outs, scratch_shapes=list(scratch_shapes) + comm.sems, name=name,
        compiler_params=_params(*(["arbitrary"] * len(grid))))(*args, *comm.ins)


ANY = pl.BlockSpec(memory_space=pl.ANY)


def _params(*sem):
    return pltpu.CompilerParams(dimension_semantics=sem, vmem_limit_bytes=V7X_VMEM_LIMIT)


def _dot(a, b):
    return jnp.dot(a, b, preferred_element_type=F32)


def _dot_nt(a, b):
    return lax.dot_general(a, b, (((1,), (1,)), ((), ())), preferred_element_type=F32)


def _dot_tn(a, b):
    return lax.dot_general(a, b, (((0,), (0,)), ((), ())), preferred_element_type=F32)


def _sigmoid(x):
    return 1.0 / (1.0 + jnp.exp(-x))


def _rms(x):
    r = lax.rsqrt(jnp.mean(x * x, axis=-1, keepdims=True) + RMS_EPS)
    return x * r, r


def _rms_bwd(dh, xh, r, g):
    dxh = dh * g
    dx = r * (dxh - xh * jnp.mean(dxh * xh, axis=-1, keepdims=True))
    return dx, jnp.sum(dh * xh, axis=0, keepdims=True)


def _sds(shape, dtype):
    return jax.ShapeDtypeStruct(shape, dtype)


def _row_block(rows, limit=512):
    fits = [d for d in range(16, min(rows, limit) + 1, 16) if rows % d == 0]
    return fits[-1] if fits else rows


def _chunks(n, step):
    return [(o, min(step, n - o)) for o in range(0, n, step)]


def _resident(shape):
    return pl.BlockSpec(shape, lambda *_: (0,) * len(shape), pipeline_mode=pl.Buffered(1))


def _ffn_fwd(x, gain, wint, wout, tm, comm=None):
    S, D = x.shape
    F = wout.shape[0]

    def body(x_ref, g_ref, w_ref, wo_ref, xo_ref, gu_ref, a_ref):
        xh, _ = _rms(x_ref[...])
        h = (xh * g_ref[...]).astype(BF16)
        for o, n in _chunks(F, FFN_CHUNK):
            gb = _dot_nt(h, w_ref[o:o + n, :]).astype(BF16)
            ub = _dot_nt(h, w_ref[F + o:F + o + n, :]).astype(BF16)
            gu_ref[:, o:o + n] = gb
            gu_ref[:, F + o:F + o + n] = ub
            g = gb.astype(F32)
            a_ref[:, o:o + n] = (g * _sigmoid(g) * ub.astype(F32)).astype(BF16)
        xo_ref[...] = x_ref[...] + FFN_RES * _dot(a_ref[...], wo_ref[...])

    return _pcall(
        body, (x, gain, wint, wout), grid=(S // tm,),
        in_specs=[pl.BlockSpec((tm, D), lambda i: (i, 0)), _resident((1, D)),
                  _resident((2 * F, D)), _resident((F, D))],
        out_specs=[pl.BlockSpec((tm, D), lambda i: (i, 0)), pl.BlockSpec((tm, 2 * F), lambda i: (i, 0))],
        out_shape=[_sds((S, D), F32), _sds((S, 2 * F), BF16)],
        scratch_shapes=[pltpu.VMEM((tm, F), BF16)],
        sem=("parallel",), name="ffn_fwd", comm=comm)


def _ffn_bwd_a(x, gain, dxo, gu, wint, wout, tm, comm=None):
    S, D = x.shape
    F = wout.shape[0]

    def body(x_ref, g_ref, dxo_ref, gu_ref, w_ref, wo_ref, dx_ref, dgu_ref, dgain_ref, h_ref, dys_ref):
        i = pl.program_id(0)
        dys = (FFN_RES * dxo_ref[...]).astype(BF16)
        dys_ref[...] = dys
        for o, n in _chunks(F, FFN_CHUNK):
            dact = _dot_nt(dys, wo_ref[o:o + n, :])
            g = gu_ref[:, o:o + n].astype(F32)
            u = gu_ref[:, F + o:F + o + n].astype(F32)
            s = _sigmoid(g)
            dgu_ref[:, o:o + n] = (dact * u * (s * (1.0 + g * (1.0 - s)))).astype(BF16)
            dgu_ref[:, F + o:F + o + n] = (dact * (g * s)).astype(BF16)
        dh = _dot(dgu_ref[...], w_ref[...])
        xh, r = _rms(x_ref[...])
        h_ref[...] = (xh * g_ref[...]).astype(BF16)
        dxn, dgn = _rms_bwd(dh, xh, r, g_ref[...])
        dx_ref[...] = dxo_ref[...] + dxn

        @pl.when(i == 0)
        def _():
            dgain_ref[...] = dgn

        @pl.when(i > 0)
        def _():
            dgain_ref[...] += dgn

    tile = pl.BlockSpec((tm, D), lambda i: (i, 0))
    wide = pl.BlockSpec((tm, 2 * F), lambda i: (i, 0))
    return _pcall(
        body, (x, gain, dxo, gu, wint, wout), grid=(S // tm,),
        in_specs=[tile, _resident((1, D)), tile, wide, _resident((2 * F, D)), _resident((F, D))],
        out_specs=[tile, wide, pl.BlockSpec((1, D), lambda i: (0, 0)), tile, tile],
        out_shape=[_sds((S, D), F32), _sds((S, 2 * F), BF16), _sds((1, D), F32), _sds((S, D), BF16), _sds((S, D), BF16)],
        sem=("arbitrary",), name="ffn_bwd_a", comm=comm)


def _ffn_bwd_w(h, dys, gu, dgu, tk, comm=None):
    S, D = h.shape
    F = gu.shape[1] // 2
    FH = F // 2
    nk = S // tk

    def body(h_ref, dys_ref, gg_ref, gu_ref, dg_ref, du_ref, dw_ref, dwo_ref, accw_ref, acco_ref):
        k = pl.program_id(1)

        @pl.when(k == 0)
        def _():
            accw_ref[...] = jnp.zeros_like(accw_ref)
            acco_ref[...] = jnp.zeros_like(acco_ref)

        hv, dys = h_ref[...], dys_ref[...]
        for o, n in _chunks(FH, BWD_W_CHUNK):
            g = gg_ref[:, o:o + n].astype(F32)
            act = (g * _sigmoid(g) * gu_ref[:, o:o + n].astype(F32)).astype(BF16)
            accw_ref[0, o:o + n, :] += _dot_tn(dg_ref[:, o:o + n], hv)
            accw_ref[1, o:o + n, :] += _dot_tn(du_ref[:, o:o + n], hv)
            acco_ref[o:o + n, :] += _dot_tn(act, dys)

        @pl.when(k == nk - 1)
        def _():
            dw_ref[...] = accw_ref[...].astype(BF16)
            dwo_ref[...] = acco_ref[...].astype(BF16)

    tile = pl.BlockSpec((tk, D), lambda j, k: (k, 0))
    gate = pl.BlockSpec((tk, FH), lambda j, k: (k, j))
    up = pl.BlockSpec((tk, FH), lambda j, k: (k, j + 2))
    return _pcall(
        body, (h, dys, gu, gu, dgu, dgu), grid=(2, nk),
        in_specs=[tile, tile, gate, up, gate, up],
        out_specs=[pl.BlockSpec((2, FH, D), lambda j, k: (0, j, 0), pipeline_mode=pl.Buffered(1)),
                   pl.BlockSpec((FH, D), lambda j, k: (j, 0), pipeline_mode=pl.Buffered(1))],
        out_shape=[_sds((2, F, D), BF16), _sds((F, D), BF16)],
        scratch_shapes=[pltpu.VMEM((2, FH, D), F32), pltpu.VMEM((FH, D), F32)],
        sem=("parallel", "arbitrary"), name="ffn_bwd_w", comm=comm)


def _mixin_fwd(x, gain, wext, tm, comm=None):
    S, D = x.shape
    PW = wext.shape[0]

    def body(x_ref, g_ref, w_ref, p_ref):
        xh, _ = _rms(x_ref[...])
        p_ref[...] = _dot_nt((xh * g_ref[...]).astype(BF16), w_ref[...]).astype(BF16)

    return _pcall(
        body, (x, gain, wext), grid=(S // tm,),
        in_specs=[pl.BlockSpec((tm, D), lambda i: (i, 0)), _resident((1, D)), _resident((PW, D))],
        out_specs=[pl.BlockSpec((tm, PW), lambda i: (i, 0))],
        out_shape=[_sds((S, PW), BF16)],
        sem=("parallel",), name="mixin_fwd", comm=comm)


def _mixin_bwd(x, gain, dxo, dq, dkv, dag, wext, tm, comm=None):
    S, D = x.shape
    PW = wext.shape[0]
    QW = dq.shape[1]
    o1, o2 = QW, QW + 2 * KV_DUP

    def body(x_ref, g_ref, dxo_ref, dq_ref, dkv_ref, dag_ref, w_ref, dx_ref, dgain_ref, dw_ref):
        i = pl.program_id(0)
        xh, r = _rms(x_ref[...])
        h = (xh * g_ref[...]).astype(BF16)
        dqv, dkvv, dagv = dq_ref[...], dkv_ref[...], dag_ref[...]
        dh = _dot(dqv, w_ref[0:o1, :]) + _dot(dkvv, w_ref[o1:o2, :]) + _dot(dagv, w_ref[o2:PW, :])
        dxn, dgn = _rms_bwd(dh, xh, r, g_ref[...])
        dx_ref[...] = dxo_ref[...] + dxn

        @pl.when(i == 0)
        def _():
            dgain_ref[...] = dgn
            dw_ref[0:o1, :] = _dot_tn(dqv, h)
            dw_ref[o1:o2, :] = _dot_tn(dkvv, h)
            dw_ref[o2:PW, :] = _dot_tn(dagv, h)

        @pl.when(i > 0)
        def _():
            dgain_ref[...] += dgn
            dw_ref[0:o1, :] += _dot_tn(dqv, h)
            dw_ref[o1:o2, :] += _dot_tn(dkvv, h)
            dw_ref[o2:PW, :] += _dot_tn(dagv, h)

    return _pcall(
        body, (x, gain, dxo, dq, dkv, dag, wext), grid=(S // tm,),
        in_specs=[pl.BlockSpec((tm, D), lambda i: (i, 0)), _resident((1, D)),
                  pl.BlockSpec((tm, D), lambda i: (i, 0)),
                  pl.BlockSpec((tm, QW), lambda i: (i, 0)),
                  pl.BlockSpec((tm, 2 * KV_DUP), lambda i: (i, 0)),
                  pl.BlockSpec((tm, PW - o2), lambda i: (i, 0)),
                  _resident((PW, D))],
        out_specs=[pl.BlockSpec((tm, D), lambda i: (i, 0)),
                   pl.BlockSpec((1, D), lambda i: (0, 0)),
                   pl.BlockSpec((PW, D), lambda i: (0, 0))],
        out_shape=[_sds((S, D), F32), _sds((1, D), F32), _sds((PW, D), F32)],
        sem=("arbitrary",), name="mixin_bwd", comm=comm)


def _attn_bias(n_heads):
    group = n_heads // N_KV_HEADS
    rows = group * WINDOW
    r = jnp.arange(rows)[:, None]
    s = jnp.arange(2 * WINDOW)[None, :]
    dist = (r % WINDOW) + WINDOW - s
    window = (dist >= 0) & (dist < WINDOW)
    out = []
    for first in (True, False):
        valid = window & (s >= WINDOW) if first else window
        tiles = []
        for kh in range(N_KV_HEADS):
            slope = jnp.asarray([2.0 ** (-8.0 * (kh * group + i + 1) / n_heads) for i in range(group)], F32)
            bias = -slope[r // WINDOW] * dist.astype(F32)
            tiles.append(jnp.where(valid, bias, NEG_INF))
        out.append(jnp.stack(tiles))
    return jnp.stack(out)


def _sink_column(group, sinks):
    rows = group * WINDOW
    seg = lax.shift_right_logical(lax.broadcasted_iota(jnp.int32, (rows, 1), 0), WINDOW.bit_length() - 1)
    sink = jnp.zeros((rows, 1), F32)
    for i in range(group):
        sink = jnp.where(seg == i, sinks[i], sink)
    return sink


def _lane_halves():
    lo = lax.broadcasted_iota(jnp.int32, (WINDOW, LANES), 1) < HEAD_DIM
    return lo, [jnp.where(lo, 1.0, 0.0).astype(BF16), jnp.where(lo, 0.0, 1.0).astype(BF16)]


def _stack_heads(ref, first_tile, n_tiles, halves, row0=0):
    parts = []
    for t in range(first_tile, first_tile + n_tiles):
        tile = ref[row0:row0 + WINDOW, LANES * t:LANES * (t + 1)]
        parts += [tile * halves[0], tile * halves[1]]
    return jnp.concatenate(parts, axis=0)


def _unstack_heads(ref, first_tile, n_tiles, lo, stacked, row0=0):
    for i in range(n_tiles):
        a = stacked[2 * i * WINDOW:(2 * i + 1) * WINDOW]
        b = stacked[(2 * i + 1) * WINDOW:(2 * i + 2) * WINDOW]
        t = first_tile + i
        ref[row0:row0 + WINDOW, LANES * t:LANES * (t + 1)] = jnp.where(lo, a, b).astype(ref.dtype)


def _softmax_sink(sc, sink):
    m = jnp.maximum(jnp.max(sc, axis=-1, keepdims=True), sink)
    p = jnp.exp(sc - m)
    es = jnp.exp(sink - m)
    inv = 1.0 / (jnp.sum(p, axis=-1, keepdims=True) + es)
    return p * inv, es * inv


def _attn_fwd(proj, sinks, bias, n_heads, comm=None):
    S = proj.shape[0]
    nb = S // WINDOW
    group = n_heads // N_KV_HEADS
    scale = 1.0 / math.sqrt(HEAD_DIM)
    QW = n_heads * HEAD_DIM
    kblk, vblk = QW // KV_DUP, QW // KV_DUP + 1

    def body(sink_ref, bias_ref, q_ref, kc_ref, kp_ref, vc_ref, vp_ref, o_ref):
        n = pl.program_id(0)
        lo, halves = _lane_halves()
        tiles = group // 2
        chains = [(j, kh) for j in range(2) for kh in range(N_KV_HEADS)]
        sink = [_sink_column(group, [sink_ref[h] for h in range(kh * group, (kh + 1) * group)]) for kh in range(N_KV_HEADS)]

        def keys(cur_ref, prev_ref, j, kh):
            lanes = slice(LANES * kh, LANES * (kh + 1))
            if j == 1:
                return cur_ref[:, lanes]
            return jnp.concatenate([prev_ref[:, lanes], cur_ref[0:WINDOW, lanes]], axis=0)

        kds = [keys(kc_ref, kp_ref, j, kh) * scale for j, kh in chains]
        vd = [keys(vc_ref, vp_ref, j, kh) for j, kh in chains]
        qs = [_stack_heads(q_ref, kh * tiles, tiles, halves, j * WINDOW) for j, kh in chains]
        bias_of = lambda j, kh: bias_ref[jnp.minimum(n, 1), kh] if j == 0 else bias_ref[1, kh]
        sc = [_dot_nt(qs[c], kds[c]) + bias_of(j, kh) for c, (j, kh) in enumerate(chains)]
        pn = [_softmax_sink(sc[c], sink[kh])[0] for c, (j, kh) in enumerate(chains)]
        out = [_dot(pn[c].astype(BF16), vd[c]) for c in range(len(chains))]
        for c, (j, kh) in enumerate(chains):
            _unstack_heads(o_ref, kh * tiles, tiles, lo, out[c], j * WINDOW)

    pair = lambda b: pl.BlockSpec((2 * WINDOW, KV_DUP), lambda n: (n, b))
    before = lambda b: pl.BlockSpec((WINDOW, KV_DUP), lambda n: (jnp.maximum(2 * n - 1, 0), b))
    return _pcall(
        body, (sinks, bias, proj, proj, proj, proj, proj), grid=(nb // 2,),
        in_specs=[pl.BlockSpec(memory_space=pltpu.SMEM), _resident(bias.shape),
                  pl.BlockSpec((2 * WINDOW, QW), lambda n: (n, 0)), pair(kblk), before(kblk), pair(vblk), before(vblk)],
        out_specs=[pl.BlockSpec((2 * WINDOW, QW), lambda n: (n, 0))],
        out_shape=[_sds((S, QW), BF16)],
        sem=("parallel",), name="attn_fwd", comm=comm)


def _attn_bwd(proj, dmix, sinks, bias, n_heads, comm=None):
    S = proj.shape[0]
    nb = S // WINDOW
    npair = nb // 2
    group = n_heads // N_KV_HEADS
    scale = 1.0 / math.sqrt(HEAD_DIM)
    QW = n_heads * HEAD_DIM
    kblk, vblk = QW // KV_DUP, QW // KV_DUP + 1

    def body(sink_ref, bias_ref, q_ref, kc_ref, kp_ref, vc_ref, vp_ref, do_ref, dq_ref, dkv_ref, dsink_ref, carry_ref):
        n = pl.program_id(0)

        @pl.when(n == 0)
        def _():
            carry_ref[...] = jnp.zeros_like(carry_ref)
            dsink_ref[...] = jnp.zeros_like(dsink_ref)

        @pl.when(n < npair)
        def _():
            lo, halves = _lane_halves()
            lane1 = lax.broadcasted_iota(jnp.int32, (1, LANES), 1)
            tiles = group // 2
            chains = [(j, kh) for j in range(2) for kh in range(N_KV_HEADS)]
            cs = range(len(chains))
            sink = [_sink_column(group, [sink_ref[h] for h in range(kh * group, (kh + 1) * group)])
                    for kh in range(N_KV_HEADS)]

            def keys(cur_ref, prev_ref, j, kh):
                lanes = slice(LANES * kh, LANES * (kh + 1))
                if j == 1:
                    return cur_ref[:, lanes]
                return jnp.concatenate([prev_ref[:, lanes], cur_ref[0:WINDOW, lanes]], axis=0)

            kds = [keys(kc_ref, kp_ref, j, kh) * scale for j, kh in chains]
            vd = [keys(vc_ref, vp_ref, j, kh) for j, kh in chains]
            qs = [_stack_heads(q_ref, kh * tiles, tiles, halves, j * WINDOW) for j, kh in chains]
            dos = [_stack_heads(do_ref, kh * tiles, tiles, halves, j * WINDOW) for j, kh in chains]
            bias_of = lambda j, kh: bias_ref[jnp.minimum(n, 1), kh] if j == 0 else bias_ref[1, kh]
            sc = [_dot_nt(qs[c], kds[c]) + bias_of(*chains[c]) for c in cs]
            dp = [_dot_nt(dos[c], vd[c]) for c in cs]
            probs = [_softmax_sink(sc[c], sink[chains[c][1]]) for c in cs]
            pn = [p[0] for p in probs]
            delta = [jnp.sum(pn[c] * dp[c], axis=-1, keepdims=True) for c in cs]
            dsb = [(pn[c] * (dp[c] - delta[c])).astype(BF16) for c in cs]
            dqs = [_dot(dsb[c], kds[c]) for c in cs]
            dkd = [_dot_tn(dsb[c], qs[c]) * scale for c in cs]
            dvd = [_dot_tn(pn[c].astype(BF16), dos[c]) for c in cs]
            dsink = jnp.zeros((1, LANES), F32)
            for c, (j, kh) in enumerate(chains):
                sd = probs[c][1] * delta[c]
                for i in range(group):
                    dsink = dsink - jnp.where(lane1 == kh * group + i, jnp.sum(sd[i * WINDOW:(i + 1) * WINDOW]), 0.0)
                _unstack_heads(dq_ref, kh * tiles, tiles, lo, dqs[c], j * WINDOW)
            dsink_ref[...] += dsink
            both = [jnp.concatenate([dkd[c] for c in cs if chains[c][0] == j] + [dvd[c] for c in cs if chains[c][0] == j],
                                    axis=1) for j in range(2)]
            dkv_ref[0:WINDOW, :] = carry_ref[0].astype(BF16)
            dkv_ref[WINDOW:2 * WINDOW, :] = (carry_ref[1] + both[0][0:WINDOW]).astype(BF16)
            carry_ref[0] = both[0][WINDOW:2 * WINDOW] + both[1][0:WINDOW]
            carry_ref[1] = both[1][WINDOW:2 * WINDOW]

        @pl.when(n == npair)
        def _():
            dkv_ref[0:WINDOW, :] = carry_ref[0].astype(BF16)
            dkv_ref[WINDOW:2 * WINDOW, :] = carry_ref[1].astype(BF16)

    last = npair - 1
    cur = lambda n: jnp.minimum(n, last)
    pair = lambda b: pl.BlockSpec((2 * WINDOW, KV_DUP), lambda n: (cur(n), b))
    before = lambda b: pl.BlockSpec((WINDOW, KV_DUP), lambda n: (jnp.clip(2 * n - 1, 0, nb - 1), b))
    wide = pl.BlockSpec((2 * WINDOW, QW), lambda n: (cur(n), 0))
    return _pcall(
        body, (sinks, bias, proj, proj, proj, proj, proj, dmix), grid=(npair + 1,),
        in_specs=[pl.BlockSpec(memory_space=pltpu.SMEM), _resident(bias.shape), wide,
                  pair(kblk), before(kblk), pair(vblk), before(vblk), wide],
        out_specs=[wide,
                   pl.BlockSpec((2 * WINDOW, 2 * KV_DUP), lambda n: (jnp.clip(n - 1, 0, last), 0)),
                   pl.BlockSpec((1, LANES), lambda n: (0, 0))],
        out_shape=[_sds((S, QW), BF16), _sds((S, 2 * KV_DUP), BF16), _sds((1, LANES), F32)],
        scratch_shapes=[pltpu.VMEM((2, WINDOW, 2 * KV_DUP), F32)],
        sem=("arbitrary",), name="attn_bwd", comm=comm)


def _glu_window(a_ref, g_ref, ap_ref, gp_ref, win_ref, first):
    tm = a_ref.shape[0]
    zp = ap_ref[...].astype(F32) * _sigmoid(gp_ref[...].astype(F32))
    win_ref[0:HALO, :] = jnp.where(first, jnp.zeros_like(zp), zp)
    win_ref[HALO:HALO + tm, :] = a_ref[...].astype(F32) * _sigmoid(g_ref[...].astype(F32))


def _preshift(win_ref, sh_ref):
    n = win_ref.shape[0] - SUBLANES
    for s in range(1, SUBLANES):
        sh_ref[s - 1, 0:n, :] = win_ref[s:s + n, :]


def _window(win_ref, sh_ref, start):
    s = start % SUBLANES
    if s == 0:
        return win_ref[start:start + ROWS, :]
    return sh_ref[s - 1, start - s:start - s + ROWS, :]


def _conv_fwd(proj, wdw, bdw, lng, lnb, n_heads, tm, comm=None):
    S = proj.shape[0]
    taps, C = wdw.shape
    ablk = (n_heads * HEAD_DIM + 2 * KV_DUP) // C
    hb = tm // HALO
    off = HALO - (taps - 1)

    def body(a_ref, g_ref, ap_ref, gp_ref, w_ref, b_ref, lg_ref, lb_ref, o_ref, y_ref, win_ref, sh_ref):
        _glu_window(a_ref, g_ref, ap_ref, gp_ref, win_ref, pl.program_id(0) == 0)
        _preshift(win_ref, sh_ref)
        for c in range(tm // ROWS):
            r0 = c * ROWS
            acc = jnp.zeros((ROWS, C), F32) + b_ref[...]
            for k in range(taps):
                acc = acc + w_ref[k:k + 1, :] * _window(win_ref, sh_ref, r0 + off + k)
            y_ref[r0:r0 + ROWS, :] = acc
        y = y_ref[...]
        mu = jnp.mean(y, axis=-1, keepdims=True)
        yc = y - mu
        yn = yc * lax.rsqrt(jnp.mean(yc * yc, axis=-1, keepdims=True) + RMS_EPS) * lg_ref[...] + lb_ref[...]
        o_ref[...] = (yn * _sigmoid(yn)).astype(BF16)

    vec = pl.BlockSpec((1, C), lambda i: (0, 0))
    halo = lambda b: pl.BlockSpec((HALO, C), lambda i: (jnp.maximum(i * hb - 1, 0), b))
    return _pcall(
        body, (proj, proj, proj, proj, wdw, bdw, lng, lnb,), grid=(S // tm,),
        in_specs=[pl.BlockSpec((tm, C), lambda i: (i, ablk)), pl.BlockSpec((tm, C), lambda i: (i, ablk + 1)),
                  halo(ablk), halo(ablk + 1),
                  pl.BlockSpec((taps, C), lambda i: (0, 0)), vec, vec, vec],
        out_specs=[pl.BlockSpec((tm, C), lambda i: (i, 0)), pl.BlockSpec((tm, C), lambda i: (i, 0))],
        out_shape=[_sds((S, C), BF16), _sds((S, C), F32)],
        scratch_shapes=[pltpu.VMEM((tm + HALO, C), F32), pltpu.VMEM((SUBLANES - 1, tm + HALO, C), F32)],
        sem=("parallel",), name="conv_fwd", comm=comm)


def _conv_bwd(proj, dmix, ysave, wdw, lng, lnb, n_heads, tm, comm=None):
    S = proj.shape[0]
    taps, C = wdw.shape
    QW = n_heads * HEAD_DIM
    ablk = (QW + 2 * KV_DUP) // C
    cblk = QW // C
    hb = tm // HALO
    nt = S // tm
    off = HALO - (taps - 1)

    def ln_bwd(dc, y, lg, lb):
        mu = jnp.mean(y, axis=-1, keepdims=True)
        yc = y - mu
        r = lax.rsqrt(jnp.mean(yc * yc, axis=-1, keepdims=True) + RMS_EPS)
        yh = yc * r
        yn = yh * lg + lb
        sg = _sigmoid(yn)
        dyn = dc * (sg * (1.0 + yn * (1.0 - sg)))
        dyh = dyn * lg
        dy = r * (dyh - jnp.mean(dyh, axis=-1, keepdims=True) - yh * jnp.mean(dyh * yh, axis=-1, keepdims=True))
        return dy, dyn, yh

    def body(dc_ref, dcn_ref, y_ref, yn_ref, a_ref, g_ref, ap_ref, gp_ref, w_ref, lg_ref, lb_ref,
             dag_ref, dw_ref, dvec_ref, zwin_ref, dyw_ref, dwacc_ref, zsh_ref, dysh_ref):
        i = pl.program_id(0)

        @pl.when(i == 0)
        def _():
            dwacc_ref[...] = jnp.zeros_like(dwacc_ref)
            dvec_ref[...] = jnp.zeros_like(dvec_ref)

        lg, lb = lg_ref[...], lb_ref[...]
        dy, dyn, yh = ln_bwd(dc_ref[...].astype(F32), y_ref[...], lg, lb)
        dy_next, _, _ = ln_bwd(dcn_ref[...].astype(F32), yn_ref[...], lg, lb)
        dyw_ref[0:tm, :] = dy
        dyw_ref[tm:tm + HALO, :] = jnp.where(i == nt - 1, jnp.zeros_like(dy_next), dy_next)
        dvec_ref[0:1, :] += jnp.sum(dy, axis=0, keepdims=True)
        dvec_ref[1:2, :] += jnp.sum(dyn * yh, axis=0, keepdims=True)
        dvec_ref[2:3, :] += jnp.sum(dyn, axis=0, keepdims=True)
        _glu_window(a_ref, g_ref, ap_ref, gp_ref, zwin_ref, i == 0)
        _preshift(zwin_ref, zsh_ref)
        _preshift(dyw_ref, dysh_ref)

        for c in range(tm // ROWS):
            r0 = c * ROWS
            dz = jnp.zeros((ROWS, C), F32)
            dyc = dyw_ref[r0:r0 + ROWS, :]
            for k in range(taps):
                dz = dz + w_ref[k:k + 1, :] * _window(dyw_ref, dysh_ref, r0 + taps - 1 - k)
                prod = dyc * _window(zwin_ref, zsh_ref, r0 + off + k)
                dwacc_ref[k] += jnp.sum(prod.reshape(ROWS // SUBLANES, SUBLANES, C), axis=0)
            a = a_ref[r0:r0 + ROWS, :].astype(F32)
            s = _sigmoid(g_ref[r0:r0 + ROWS, :].astype(F32))
            dag_ref[r0:r0 + ROWS, 0:C] = (dz * s).astype(BF16)
            dag_ref[r0:r0 + ROWS, C:2 * C] = (dz * a * s * (1.0 - s)).astype(BF16)

        @pl.when(i == nt - 1)
        def _():
            dw_ref[...] = jnp.zeros_like(dw_ref)
            for k in range(taps):
                dw_ref[k:k + 1, :] = jnp.sum(dwacc_ref[k], axis=0, keepdims=True)

    vec = pl.BlockSpec((1, C), lambda i: (0, 0))
    tile = lambda b: pl.BlockSpec((tm, C), lambda i: (i, b))
    prev = lambda b: pl.BlockSpec((HALO, C), lambda i: (jnp.maximum(i * hb - 1, 0), b))
    nxt = lambda b: pl.BlockSpec((HALO, C), lambda i: (jnp.minimum((i + 1) * hb, S // HALO - 1), b))
    return _pcall(
        body, (dmix, dmix, ysave, ysave, proj, proj, proj, proj, wdw, lng, lnb,), grid=(nt,),
        in_specs=[tile(cblk), nxt(cblk), tile(0), nxt(0), tile(ablk), tile(ablk + 1), prev(ablk), prev(ablk + 1),
                  pl.BlockSpec((taps, C), lambda i: (0, 0)), vec, vec],
        out_specs=[pl.BlockSpec((tm, 2 * C), lambda i: (i, 0)),
                   pl.BlockSpec((HALO, C), lambda i: (0, 0)),
                   pl.BlockSpec((8, C), lambda i: (0, 0))],
        out_shape=[_sds((S, 2 * C), BF16), _sds((HALO, C), F32), _sds((8, C), F32)],
        scratch_shapes=[pltpu.VMEM((tm + HALO, C), F32), pltpu.VMEM((tm + HALO, C), F32),
                        pltpu.VMEM((taps, SUBLANES, C), F32),
                        pltpu.VMEM((SUBLANES - 1, tm + HALO, C), F32), pltpu.VMEM((SUBLANES - 1, tm + HALO, C), F32)],
        sem=("arbitrary",), name="conv_bwd", comm=comm)


def _mixout_fwd(x, attn, conv, wo, tm, comm=None):
    S, D = x.shape
    QW, C = attn.shape[1], conv.shape[1]

    def body(x_ref, a_ref, c_ref, w_ref, o_ref):
        o_ref[...] = x_ref[...] + _dot(a_ref[...], w_ref[0:QW, :]) + _dot(c_ref[...], w_ref[QW:QW + C, :])

    return _pcall(
        body, (x, attn, conv, wo,), grid=(S // tm,),
        in_specs=[pl.BlockSpec((tm, D), lambda i: (i, 0)),
                  pl.BlockSpec((tm, QW), lambda i: (i, 0)),
                  pl.BlockSpec((tm, C), lambda i: (i, 0)),
                  pl.BlockSpec((QW + C, D), lambda i: (0, 0))],
        out_specs=[pl.BlockSpec((tm, D), lambda i: (i, 0))],
        out_shape=[_sds((S, D), F32)],
        sem=("parallel",), name="mixout_fwd", comm=comm)


def _mixout_bwd(dxo, attn, conv, wo, tm, comm=None):
    S, D = dxo.shape
    QW, C = attn.shape[1], conv.shape[1]
    nt = S // tm

    def body(dx_ref, a_ref, c_ref, w_ref, dm_ref, dw_ref, acc_ref):
        i = pl.program_id(0)
        dxb = dx_ref[...].astype(BF16)
        dm_ref[...] = _dot_nt(dxb, w_ref[...]).astype(BF16)

        @pl.when(i == 0)
        def _():
            acc_ref[...] = jnp.zeros_like(acc_ref)

        acc_ref[0:QW, :] += _dot_tn(a_ref[...], dxb)
        acc_ref[QW:QW + C, :] += _dot_tn(c_ref[...], dxb)

        @pl.when(i == nt - 1)
        def _():
            dw_ref[...] = acc_ref[...].astype(BF16)

    return _pcall(
        body, (dxo, attn, conv, wo,), grid=(nt,),
        in_specs=[pl.BlockSpec((tm, D), lambda i: (i, 0)),
                  pl.BlockSpec((tm, QW), lambda i: (i, 0)),
                  pl.BlockSpec((tm, C), lambda i: (i, 0)),
                  pl.BlockSpec((QW + C, D), lambda i: (0, 0))],
        out_specs=[pl.BlockSpec((tm, QW + C), lambda i: (i, 0)),
                   pl.BlockSpec((QW + C, D), lambda i: (0, 0))],
        out_shape=[_sds((S, QW + C), BF16), _sds((QW + C, D), BF16)],
        scratch_shapes=[pltpu.VMEM((QW + C, D), F32)],
        sem=("arbitrary",), name="mixout_bwd", comm=comm)


def _loss_head(x, gain, target, tm, comm=None):
    S, D = x.shape
    nt = S // tm

    def body(x_ref, g_ref, t_ref, dx_ref, loss_ref, dgain_ref):
        i = pl.program_id(0)
        xh, r = _rms(x_ref[...])
        e = xh * g_ref[...] - t_ref[...]
        loss_ref[...] = jnp.zeros((1, LANES), F32) + 0.5 * jnp.sum(jnp.mean(e * e, axis=-1, keepdims=True))
        dxn, dgn = _rms_bwd(e * (1.0 / D), xh, r, g_ref[...])
        dx_ref[...] = dxn

        @pl.when(i == 0)
        def _():
            dgain_ref[...] = dgn

        @pl.when(i > 0)
        def _():
            dgain_ref[...] += dgn

    return _pcall(
        body, (x, gain, target,), grid=(nt,),
        in_specs=[pl.BlockSpec((tm, D), lambda i: (i, 0)),
                  pl.BlockSpec((1, D), lambda i: (0, 0)),
                  pl.BlockSpec((tm, D), lambda i: (i, 0))],
        out_specs=[pl.BlockSpec((tm, D), lambda i: (i, 0)),
                   pl.BlockSpec((None, 1, LANES), lambda i: (i, 0, 0)),
                   pl.BlockSpec((1, D), lambda i: (0, 0))],
        out_shape=[_sds((S, D), F32), _sds((nt, 1, LANES), F32), _sds((1, D), F32)],
        sem=("arbitrary",), name="loss_head", comm=comm)


def _adam(w, m, v, parts, name, comm=None):
    L, R, C = w.shape
    P = parts[0].shape[0]
    br = _row_block(R, 256)
    c1 = 1.0 - ADAM_B1 ** ADAM_STEP
    c2 = 1.0 - ADAM_B2 ** ADAM_STEP

    def body(w_ref, m_ref, v_ref, *rest):
        p_refs, (g_ref, d_ref, mo_ref, vo_ref) = rest[:L], rest[L:]
        layer = pl.program_id(0)

        def update(p_ref):
            g = p_ref[0].astype(F32)
            for k in range(1, P):
                g = g + p_ref[k].astype(F32)
            mn = ADAM_B1 * m_ref[...] + (1.0 - ADAM_B1) * g
            vn = ADAM_B2 * v_ref[...] + (1.0 - ADAM_B2) * (g * g)
            g_ref[...] = g
            mo_ref[...] = mn
            vo_ref[...] = vn
            d_ref[...] = -ADAM_LR * ((mn / c1) / (jnp.sqrt(vn / c2) + ADAM_EPS) + ADAM_WD * w_ref[...])

        for k in range(L):
            pl.when(layer == k)(functools.partial(update, p_refs[k]))

    blk = pl.BlockSpec((None, br, C), lambda l, i: (l, i, 0))
    part = lambda k: pl.BlockSpec((P, br, C), lambda l, i: (0, jnp.where(l == k, i, 0), 0))
    return _pcall(
        body, (w, m, v, *parts), grid=(L, R // br),
        in_specs=[blk, blk, blk] + [part(k) for k in range(L)],
        out_specs=[blk, blk, blk, blk],
        out_shape=[_sds((L, R, C), F32)] * 4,
        sem=("parallel", "parallel"), name=name, comm=comm)


def _sum_parts(parts):
    P, R, C = parts.shape

    def body(p_ref, o_ref):
        g = p_ref[0]
        for k in range(1, P):
            g = g + p_ref[k]
        o_ref[...] = g

    vmem = pl.BlockSpec(memory_space=pltpu.VMEM)
    return _pcall(body, (parts,), in_specs=[vmem], out_specs=[vmem], out_shape=[_sds((R, C), F32)],
                  name="sum_parts")[0]


def _place():
    x, y, c = lax.axis_index("x"), lax.axis_index("y"), lax.axis_index("c")
    return x, y, c, [(1 - x, y), (x, 1 - y), (1 - x, 1 - y)]


def _dev(px, py, pc):
    return 4 * px + 2 * py + pc


def _gather_comm(shards, fulls, slot_of):
    n = len(shards)

    def copies(srcs, outs, send_sems, recv_sems):
        x, y, c, chips = _place()

        def copy(a, k, block, to, from_shard=False):
            dst = slot_of[a](outs[a], _dev(*block))
            return pltpu.make_async_remote_copy(
                src_ref=srcs[a] if from_shard else dst, dst_ref=dst,
                send_sem=send_sems.at[a, k], recv_sem=recv_sems.at[a, k], device_id=to, device_id_type=MESH)

        return copy, (x, y, c), (x, y, 1 - c), chips

    def local(srcs, outs, local_sems):
        x, y, c, _ = _place()
        return [pltpu.make_async_copy(srcs[a], slot_of[a](outs[a], _dev(x, y, c)), local_sems.at[a])
                for a in range(n)]

    def first_copies(copy, me, sibling, chips):
        out = []
        for a in range(n):
            out.append(copy(a, 0, me, sibling, True))
            out += [copy(a, 1 + j, me, (*chip, me[2]), True) for j, chip in enumerate(chips)]
        return out

    def start(srcs, outs, sems):
        send_sems, recv_sems, local_sems = sems
        copy, me, sibling, chips = copies(srcs, outs, send_sems, recv_sems)
        for cp in local(srcs, outs, local_sems):
            cp.start()
        for cp in first_copies(copy, me, sibling, chips):
            cp.start()

    def forwards(copy, me, sibling, chips):
        return [copy(a, 4 + j, (*chip, me[2]), sibling) for j, chip in enumerate(chips) for a in range(n)]

    def mid(srcs, outs, sems):
        send_sems, recv_sems, local_sems = sems
        copy, me, sibling, chips = copies(srcs, outs, send_sems, recv_sems)
        for j, chip in enumerate(chips):
            for a in range(n):
                copy(a, 1 + j, (*chip, me[2]), me).wait_recv()
                copy(a, 4 + j, (*chip, me[2]), sibling).start()

    def finish(srcs, outs, sems):
        send_sems, recv_sems, local_sems = sems
        copy, me, sibling, chips = copies(srcs, outs, send_sems, recv_sems)
        c = me[2]
        for a in range(n):
            copy(a, 0, sibling, me).wait_recv()
            for j, chip in enumerate(chips):
                copy(a, 4 + j, (*chip, 1 - c), me).wait_recv()
        for cp in first_copies(copy, me, sibling, chips) + forwards(copy, me, sibling, chips):
            cp.wait_send()
        for cp in local(srcs, outs, local_sems):
            cp.wait()

    sems = [pltpu.SemaphoreType.DMA((n, 7)), pltpu.SemaphoreType.DMA((n, 7)), pltpu.SemaphoreType.DMA((n,))]
    return _Comm(shards, fulls, sems, start, finish, mid)


def _swap_comm(grads):
    n = len(grads)

    def copies(srcs, outs, sems):
        x, y, c, _ = _place()
        return [pltpu.make_async_remote_copy(
            src_ref=srcs[a].at[:, pl.ds(1 - c, 1)], dst_ref=outs[a],
            send_sem=sems[0].at[a], recv_sem=sems[1].at[a], device_id=(x, y, 1 - c), device_id_type=MESH)
            for a in range(n)]

    def start(srcs, outs, sems):
        for cp in copies(srcs, outs, sems):
            cp.start()

    def finish(srcs, outs, sems):
        for cp in copies(srcs, outs, sems):
            cp.wait()

    return _Comm(grads, [_sds((N_CHIP, 1) + g.shape[2:], g.dtype) for g in grads],
                 [pltpu.SemaphoreType.DMA((n,)), pltpu.SemaphoreType.DMA((n,))], start, finish)


def _exchange_comm(parts):
    n = len(parts)

    def copies(srcs, outs, sems):
        x, y, c, chips = _place()
        mine = 2 * x + y
        loc = [pltpu.make_async_copy(srcs[a].at[pl.ds(mine, 1)], outs[a].at[pl.ds(mine, 1)], sems[2].at[a])
               for a in range(n)]
        rem = [pltpu.make_async_remote_copy(
            src_ref=srcs[a].at[pl.ds(2 * px + py, 1)], dst_ref=outs[a].at[pl.ds(mine, 1)],
            send_sem=sems[0].at[a, j], recv_sem=sems[1].at[a, j], device_id=(px, py, c), device_id_type=MESH)
            for a in range(n) for j, (px, py) in enumerate(chips)]
        return loc + rem

    def start(srcs, outs, sems):
        for cp in copies(srcs, outs, sems):
            cp.start()

    def finish(srcs, outs, sems):
        for cp in copies(srcs, outs, sems):
            cp.wait()

    return _Comm(parts, [_sds(p.shape, p.dtype) for p in parts],
                 [pltpu.SemaphoreType.DMA((n, 3)), pltpu.SemaphoreType.DMA((n, 3)), pltpu.SemaphoreType.DMA((n,))],
                 start, finish)


FWD_KERNELS = [("ffn1", ("a1", "b1"), 26), ("mixin", ("wi",), 7), ("attn", (), 7), ("conv", (), 15), ("mixout", ("wo",), 6),
               ("ffn2", ("a2", "b2"), 26)]
GATHER_SHARE = 70
FIRST_GATHERED = ("a1", "b1", "wi", "wo")


def _plan_gathers(n_layers, shard_bytes):
    per_layer = sum(shard_bytes.values())
    cost = {n: GATHER_SHARE * b / per_layer for n, b in shard_bytes.items()}
    room = {len(FWD_KERNELS) * l + i: k[2] for l in range(n_layers) for i, k in enumerate(FWD_KERNELS)}
    first, plan = [], {}
    for l in range(n_layers):
        for i, (_, needs, _) in enumerate(FWD_KERNELS):
            due = len(FWD_KERNELS) * l + i
            for name in needs:
                if l == 0 and name in FIRST_GATHERED:
                    first.append((name, l))
                    continue
                fits = [k for k in range(due) if room[k] >= cost[name]]
                k = fits[0] if fits else max(range(due), key=lambda k: room[k])
                room[k] -= cost[name]
                plan.setdefault(k, []).append((name, l))
    return first, plan


def _comm_only(comm, name):
    return _pcall(lambda: None, (), in_specs=[], out_specs=[], out_shape=[], name=name, comm=comm)


def _gather_small(v):
    R, C = v.shape

    def body(x_ref, out_ref, send_sems, recv_sems, local_sem):
        x, y, c, chips = _place()
        me, sibling = (x, y, c), (x, y, 1 - c)

        def copy(k, block, to, from_shard=False):
            dst = out_ref.at[_dev(*block)]
            return pltpu.make_async_remote_copy(
                src_ref=x_ref if from_shard else dst, dst_ref=dst,
                send_sem=send_sems.at[k], recv_sem=recv_sems.at[k], device_id=to, device_id_type=MESH)

        mine = pltpu.make_async_copy(x_ref, out_ref.at[_dev(*me)], local_sem)
        mine.start()
        first = [copy(0, me, sibling, True)] + [copy(1 + j, me, (*chip, c), True) for j, chip in enumerate(chips)]
        for cp in first:
            cp.start()
        passed = [copy(4 + j, (*chip, c), sibling) for j, chip in enumerate(chips)]
        for j, chip in enumerate(chips):
            copy(1 + j, (*chip, c), me).wait_recv()
            passed[j].start()
        copy(0, sibling, me).wait_recv()
        for j, chip in enumerate(chips):
            copy(4 + j, (*chip, 1 - c), me).wait_recv()
        for cp in first + passed:
            cp.wait_send()
        mine.wait()

    vmem = pl.BlockSpec(memory_space=pltpu.VMEM)
    return _pcall(
        body, (v,), in_specs=[vmem], out_specs=[vmem], out_shape=[_sds((N_DEV, R, C), F32)],
        scratch_shapes=[pltpu.SemaphoreType.DMA((7,)), pltpu.SemaphoreType.DMA((7,)), pltpu.SemaphoreType.DMA],
        name="gather_small")[0]


def _add_sibling(core, g, r):
    _, _, R, C = g.shape
    br = _row_block(R)

    def body(c_ref, g_ref, r_ref, o_ref):
        o_ref[...] = (g_ref[...].astype(F32) + r_ref[...].astype(F32)).astype(BF16)

    return _pcall(
        body, (core, g, r),
        grid_spec=pltpu.PrefetchScalarGridSpec(
            num_scalar_prefetch=1, grid=(N_CHIP, R // br),
            in_specs=[pl.BlockSpec((None, None, br, C), lambda k, i, c_ref: (k, c_ref[0], i, 0)),
                      pl.BlockSpec((None, None, br, C), lambda k, i, c_ref: (k, 0, i, 0))],
            out_specs=pl.BlockSpec((None, br, C), lambda k, i, c_ref: (k, i, 0))),
        out_shape=_sds((N_CHIP, R, C), BF16), sem=("parallel", "parallel"), name="rs_add_sibling")


def _by_chip(g):
    return g.reshape((N_CHIP, 2) + g.shape[1:])


def _pack_rows(vecs):
    flat = jnp.concatenate([v.reshape(-1).astype(F32) for v in vecs])
    rows = -(-flat.shape[0] // (8 * LANES)) * 8
    return jnp.pad(flat, (0, rows * LANES - flat.shape[0])).reshape(rows, LANES)


def _unpack_rows(rows, shapes):
    flat = rows.reshape(-1)
    out, o = [], 0
    for s in shapes:
        n = math.prod(s)
        out.append(flat[o:o + n].reshape(s))
        o += n
    return out


def _adam_any(w, m, v, g, name):
    shape = w.shape
    one = lambda t: t.reshape(1, -1, shape[-1])
    return tuple(t.reshape(shape) for t in _adam(one(w), one(m), one(v), [one(g)], name))


def kernel(x, norm_ffn1, w_ffn1_in, w_ffn1_out, norm_mix, w_in, sinks, w_dw, b_dw, conv_ln_g, conv_ln_b, w_out, norm_ffn2, w_ffn2_in, w_ffn2_out, final_norm, loss_target, m_norm_ffn1, m_w_ffn1_in, m_w_ffn1_out, m_norm_mix, m_w_in, m_sinks, m_w_dw, m_b_dw, m_conv_ln_g, m_conv_ln_b, m_w_out, m_norm_ffn2, m_w_ffn2_in, m_w_ffn2_out, m_final_norm, v_norm_ffn1, v_w_ffn1_in, v_w_ffn1_out, v_norm_mix, v_w_in, v_sinks, v_w_dw, v_b_dw, v_conv_ln_g, v_conv_ln_b, v_w_out, v_norm_ffn2, v_w_ffn2_in, v_w_ffn2_out, v_final_norm):
    _, S, D = x.shape
    L = norm_ffn1.shape[0]
    NF = w_ffn1_in.shape[2]
    RF = w_ffn1_out.shape[1]
    NW = w_in.shape[2]
    RO = w_out.shape[1]
    taps, CD = w_dw.shape[1], w_dw.shape[2]
    H = sinks.shape[1]
    C = N_DEV * CD
    QW = H * HEAD_DIM
    KVW = N_KV_HEADS * HEAD_DIM
    assert 2 * RF == NF and QW + C == N_DEV * RO and N_DEV * NW == QW + 2 * KVW + 2 * C
    tm = min(512, S)
    ta = min(256, S)
    tc = min(512, S)
    core = lax.axis_index("c").astype(jnp.int32).reshape(1)

    x0 = x[0]
    target = loss_target[0]
    attn_bias = _attn_bias(H)

    wdw_all = _gather_small(_pack_rows([w_dw]))
    n_dw = L * taps * CD
    wdw_full = jnp.stack([wdw_all[d].reshape(-1)[:n_dw].reshape(L, taps, CD) for d in range(N_DEV)],
                         axis=2).reshape(L, taps, C)

    def shard(name, l):
        return {"a1": lambda: w_ffn1_in[l].T.astype(BF16), "a2": lambda: w_ffn2_in[l].T.astype(BF16),
                "b1": lambda: w_ffn1_out[l].astype(BF16), "b2": lambda: w_ffn2_out[l].astype(BF16),
                "wi": lambda: w_in[l].T.astype(BF16), "wo": lambda: w_out[l].astype(BF16)}[name]()

    rows_of = {"a1": NF, "a2": NF, "b1": RF, "b2": RF, "wi": NW, "wo": RO}
    full_of = {n: _sds((N_DEV * r, D), BF16) for n, r in rows_of.items()}
    slot_of = {n: (lambda ref, b, r=r: ref.at[pl.ds(b * r, r)]) for n, r in rows_of.items()}
    first, plan = _plan_gathers(L, {n: r * D * 2 for n, r in rows_of.items()})
    got = {}

    def gather(items):
        if not items:
            return None
        return _gather_comm([shard(n, l) for n, l in items], [full_of[n] for n, _ in items],
                            [slot_of[n] for n, _ in items])

    def carrying(k, call, n_own):
        items = plan.get(k, [])
        res = call(gather(items))
        got.update(zip(items, res[n_own:]))
        return res[:n_own]

    def wext_of(wi):
        q, k, v, u = wi[:QW], wi[QW:QW + KVW], wi[QW + KVW:QW + 2 * KVW], wi[QW + 2 * KVW:]
        dup = lambda t: jnp.concatenate(
            [t[HEAD_DIM * (i // 2):HEAD_DIM * (i // 2 + 1)] for i in range(2 * N_KV_HEADS)], axis=0)
        return jnp.concatenate([q, dup(k), dup(v), u], axis=0)

    got.update(zip(first, _comm_only(gather(first), "ag_first")))

    saved = []
    xc = x0
    for l in range(L):
        k0 = len(FWD_KERNELS) * l
        g1, gm, g2 = norm_ffn1[l][None], norm_mix[l][None], norm_ffn2[l][None]
        x1, gu1 = carrying(k0, lambda c: _ffn_fwd(xc, g1, got["a1", l], got["b1", l], tm, c), 2)
        wext = wext_of(got["wi", l])
        proj, = carrying(k0 + 1, lambda c: _mixin_fwd(x1, gm, wext, tm, c), 1)
        attn, = carrying(k0 + 2, lambda c: _attn_fwd(proj, sinks[l], attn_bias, H, c), 1)
        conv, ysave = carrying(k0 + 3, lambda c: _conv_fwd(proj, wdw_full[l], b_dw[l][None], conv_ln_g[l][None],
                                                           conv_ln_b[l][None], H, tc, c), 2)
        x2, = carrying(k0 + 4, lambda c: _mixout_fwd(x1, attn, conv, got["wo", l], tm, c), 1)
        x3, gu2 = carrying(k0 + 5, lambda c: _ffn_fwd(x2, g2, got["a2", l], got["b2", l], tm, c), 2)
        W = dict(a1=got["a1", l], b1=got["b1", l], a2=got["a2", l], b2=got["b2", l], wo=got["wo", l], wext=wext)
        saved.append(dict(W=W, x0=xc, x1=x1, x2=x2, gu1=gu1, gu2=gu2, proj=proj, attn=attn, conv=conv, ysave=ysave))
        xc = x3

    dx, loss_parts, dfinal = _loss_head(xc, final_norm[None], target, tm)
    loss = lax.psum(jnp.sum(loss_parts[:, 0, 0]), ("x", "y", "c"))

    def swap(gs):
        return _swap_comm([_by_chip(g) for g in gs]) if gs else None

    def added(gs, gots):
        return [_add_sibling(core, _by_chip(g), r) for g, r in zip(gs, gots)]

    small = [None] * L
    big = [dict() for _ in range(L)]
    carry = None
    for l in reversed(range(L)):
        sv = saved[l]
        W = sv["W"]
        g1, gm, g2 = norm_ffn1[l][None], norm_mix[l][None], norm_ffn2[l][None]
        fold = lambda t: jnp.concatenate(
            [t[2 * HEAD_DIM * i:2 * HEAD_DIM * i + HEAD_DIM] + t[2 * HEAD_DIM * i + HEAD_DIM:2 * HEAD_DIM * (i + 1)]
             for i in range(N_KV_HEADS)], axis=0)

        r = _ffn_bwd_a(sv["x2"], g2, dx, sv["gu2"], W["a2"], W["b2"], ta, swap(carry[1]) if carry else None)
        dx2, dgu2, dg2, h2, dys2 = r[:5]
        p_a1, p_b1 = added(carry[1], r[5:]) if carry else (None, None)
        r = _ffn_bwd_w(h2, dys2, sv["gu2"], dgu2, tm, _exchange_comm([p_a1]) if carry else None)
        da2, db2 = r[0].reshape(N_DEV, NF, D), r[1].reshape(N_DEV, RF, D)
        if carry:
            big[carry[0]]["a1"], = r[2:]
        dmix, dwo = _mixout_bwd(dx2, sv["attn"], sv["conv"], W["wo"], tm)
        dwo = dwo.reshape(N_DEV, RO, D)
        r = _conv_bwd(sv["proj"], dmix, sv["ysave"], wdw_full[l], conv_ln_g[l][None], conv_ln_b[l][None], H, tc,
                      _join(_exchange_comm([p_b1]) if carry else None, swap([da2, db2, dwo])))
        dag, dwdw, dvec = r[:3]
        r = r[3:]
        if carry:
            big[carry[0]]["b1"], r = r[0], r[1:]
        p_a2, p_b2, p_wo = added([da2, db2, dwo], r)
        r = _attn_bwd(sv["proj"], dmix, sinks[l], attn_bias, H, _exchange_comm([p_b2]))
        dq, dkv, dsink = r[:3]
        big[l]["b2"] = r[3]
        r = _mixin_bwd(sv["x1"], gm, dx2, dq, dkv, dag, W["wext"], tm, _exchange_comm([p_wo]))
        dx1, dgm, dwext = r[:3]
        big[l]["wo"] = r[3]
        dwi = jnp.concatenate([dwext[:QW], fold(dwext[QW:QW + KV_DUP]),
                               fold(dwext[QW + KV_DUP:QW + 2 * KV_DUP]), dwext[QW + 2 * KV_DUP:]], axis=0)
        dwi = dwi.astype(BF16).reshape(N_DEV, NW, D)
        r = _ffn_bwd_a(sv["x0"], g1, dx1, sv["gu1"], W["a1"], W["b1"], ta, _join(_exchange_comm([p_a2]), swap([dwi])))
        dx0, dgu1, dg1, h1, dys1 = r[:5]
        big[l]["a2"] = r[5]
        p_wi, = added([dwi], r[6:])
        r = _ffn_bwd_w(h1, dys1, sv["gu1"], dgu1, tm, _exchange_comm([p_wi]))
        carry = (l, [r[0].reshape(N_DEV, NF, D), r[1].reshape(N_DEV, RF, D)])
        big[l]["wi"] = r[2]
        dx = dx0
        small[l] = [dg1[0], dgm[0], dsink[0, :H], dwdw[:taps], dvec[0], dvec[1], dvec[2], dg2[0]]

    p_carry = added(carry[1], _comm_only(swap(carry[1]), "rs_swap_last"))
    big[carry[0]]["a1"], big[carry[0]]["b1"] = _comm_only(_exchange_comm(p_carry), "rs_exchange_last")
    grad_x = dx[None]

    small_shapes = [(D,), (D,), (H,), (taps, C), (C,), (C,), (C,), (D,)]
    packed = _pack_rows([t for l in range(L) for t in small[l]] + [dfinal[0]])
    total = _sum_parts(_gather_small(packed))
    flat = _unpack_rows(total, small_shapes * L + [(D,)])
    per = [jnp.stack([flat[l * len(small_shapes) + i] for l in range(L)]) for i in range(len(small_shapes))]
    g_nf1, g_nmix, g_sinks, g_wdw_full, g_bdw, g_lng, g_lnb, g_nf2 = per
    g_final = flat[-1]
    dev = _dev(lax.axis_index("x"), lax.axis_index("y"), lax.axis_index("c"))
    g_wdw = lax.dynamic_slice_in_dim(g_wdw_full, dev * CD, CD, axis=2)

    res = {}
    res["norm_ffn1"] = _adam_any(norm_ffn1, m_norm_ffn1, v_norm_ffn1, g_nf1, "adam_small")
    res["norm_mix"] = _adam_any(norm_mix, m_norm_mix, v_norm_mix, g_nmix, "adam_small")
    res["sinks"] = _adam_any(sinks, m_sinks, v_sinks, g_sinks, "adam_small")
    res["w_dw"] = _adam_any(w_dw, m_w_dw, v_w_dw, g_wdw, "adam_small")
    res["b_dw"] = _adam_any(b_dw, m_b_dw, v_b_dw, g_bdw, "adam_small")
    res["conv_ln_g"] = _adam_any(conv_ln_g, m_conv_ln_g, v_conv_ln_g, g_lng, "adam_small")
    res["conv_ln_b"] = _adam_any(conv_ln_b, m_conv_ln_b, v_conv_ln_b, g_lnb, "adam_small")
    res["norm_ffn2"] = _adam_any(norm_ffn2, m_norm_ffn2, v_norm_ffn2, g_nf2, "adam_small")
    res["final_norm"] = tuple(t[0] for t in _adam_any(final_norm[None], m_final_norm[None], v_final_norm[None],
                                                      g_final[None], "adam_small"))

    def adam_big(key, w, m, v, name, transposed=False, comm=None):
        t = (lambda a: a.transpose(0, 2, 1)) if transposed else (lambda a: a)
        r = _adam(t(w), t(m), t(v), [big[l][key] for l in range(L)], name, comm)
        return tuple(t(o) for o in r[:4]), r[4:]

    res["w_ffn1_in"], _ = adam_big("a1", w_ffn1_in, m_w_ffn1_in, v_w_ffn1_in, "adam_ffn_in", True)
    res["w_ffn1_out"], _ = adam_big("b1", w_ffn1_out, m_w_ffn1_out, v_w_ffn1_out, "adam_ffn_out")
    res["w_ffn2_in"], _ = adam_big("a2", w_ffn2_in, m_w_ffn2_in, v_w_ffn2_in, "adam_ffn_in", True)
    res["w_ffn2_out"], _ = adam_big("b2", w_ffn2_out, m_w_ffn2_out, v_w_ffn2_out, "adam_ffn_out")
    res["w_in"], _ = adam_big("wi", w_in, m_w_in, v_w_in, "adam_w_in", True)
    res["w_out"], _ = adam_big("wo", w_out, m_w_out, v_w_out, "adam_w_out")

    order = ["norm_ffn1", "w_ffn1_in", "w_ffn1_out", "norm_mix", "w_in", "sinks", "w_dw", "b_dw", "conv_ln_g",
             "conv_ln_b", "w_out", "norm_ffn2", "w_ffn2_in", "w_ffn2_out", "final_norm"]
    return (loss, grad_x, *[res[n][0] for n in order], *[res[n][1] for n in order],
            *[res[n][2] for n in order], *[res[n][3] for n in order])
```
